```python
import math
import jax
import jax.numpy as jnp
from jax import lax
import numpy as np

D_MODEL = 2048
BATCH = 4
SEQ = 2048
DEPTH = 4
DEC_BATCH = 128
DEC_SEQ = 4
PAST_LEN = 16384
PAGE_SIZE = 128

N_BRANCH = 4
BRANCH = D_MODEL // 4
CONV_W = 4
LRU_BLOCKS = 4
LRU_BD = BRANCH // LRU_BLOCKS
LRU_C = 8.0
GLA_H = 4
GLA_DV = BRANCH // GLA_H
GLA_DK = GLA_DV // 2
GLA_RANK = 16
GLA_TAU = 16.0
RET_H = 4
RET_DV = BRANCH // RET_H
RET_DK = RET_DV // 2
ROPE_BASE = 10000.0
DN_H = 4
DN_D = BRANCH // DN_H
CHUNK = 64
D_FF = 5632
EPS = 1e-6

IN_SIZES = (BRANCH, BRANCH,
            GLA_H * GLA_DK, GLA_H * GLA_DK, BRANCH, BRANCH, GLA_RANK,
            RET_H * RET_DK, RET_H * RET_DK, BRANCH, BRANCH,
            BRANCH, BRANCH, BRANCH, BRANCH, DN_H, DN_H)
N_IN = sum(IN_SIZES)

kernel_name = 'hybrid_rglru_gla_retnet_gdn_macaron_step'

F32 = jnp.float32


def split_points():
    return [int(i) for i in np.cumsum(np.array(IN_SIZES))[:-1]]


def rmsnorm(x, g):
    xf = x.astype(F32)
    y = xf * lax.rsqrt(jnp.mean(xf * xf, axis=-1, keepdims=True) + EPS)
    return (y * g.astype(F32)).astype(x.dtype)


def groupnorm(x, g):
    xf = x.astype(F32)
    xc = xf - jnp.mean(xf, axis=-1, keepdims=True)
    return xc * lax.rsqrt(jnp.mean(xc * xc, axis=-1, keepdims=True) + EPS) * g.astype(F32)


def l2norm(x):
    xf = x.astype(F32)
    return xf * lax.rsqrt(jnp.sum(xf * xf, axis=-1, keepdims=True) + EPS)


def split_heads(x, n):
    B, T, _ = x.shape
    return x.reshape(B, T, n, -1).transpose(0, 2, 1, 3)


def merge_heads(x):
    B, H, T, d = x.shape
    return x.transpose(0, 2, 1, 3).reshape(B, T, H * d)


def swiglu(h, w_in, w_out):
    gate, up = jnp.split(h @ w_in, 2, axis=-1)
    return (jax.nn.silu(gate) * up) @ w_out


def causal_conv(x, buf, w):
    T = x.shape[1]
    xx = jnp.concatenate([buf.astype(x.dtype), x], axis=1)
    y = xx[:, 0:T] * w[0]
    for j in range(1, CONV_W):
        y = y + xx[:, j:j + T] * w[j]
    return y, xx[:, T:]


def rope(x, pos):
    half = x.shape[-1] // 2
    inv = ROPE_BASE ** (-jnp.arange(half, dtype=F32) / half)
    ang = pos.astype(F32)[:, None] * inv[None, :]
    cos, sin = jnp.cos(ang), jnp.sin(ang)
    x1, x2 = x[..., :half].astype(F32), x[..., half:].astype(F32)
    return jnp.concatenate([x1 * cos - x2 * sin, x2 * cos + x1 * sin], axis=-1)


def chunk_len(T):
    return CHUNK if T % CHUNK == 0 else T


def to_chunks(x, c):
    B, H, T = x.shape[:3]
    return jnp.moveaxis(x.astype(F32).reshape((B, H, T // c, c) + x.shape[3:]), 2, 0)


def from_chunks(y):
    N, B, H, c, d = y.shape
    return jnp.moveaxis(y, 0, 2).reshape(B, H, N * c, d)


def rglru(x, h0, wa, ba, wx, bx, lam, reset):
    B, T, C = x.shape
    xf = x.astype(F32)
    xb = xf.reshape(B, T, LRU_BLOCKS, LRU_BD)
    r = jax.nn.sigmoid(jnp.einsum('btnc,ncd->btnd', xb, wa.astype(F32)).reshape(B, T, C) + ba)
    i = jax.nn.sigmoid(jnp.einsum('btnc,ncd->btnd', xb, wx.astype(F32)).reshape(B, T, C) + bx)
    log_a = -LRU_C * r * jax.nn.softplus(-lam.astype(F32))
    a = jnp.exp(log_a)
    mult = jnp.where(reset[None, :, None], 1.0, jnp.sqrt(-jnp.expm1(2.0 * log_a)))
    u = mult * i * xf

    def combine(left, right):
        a_l, b_l = left
        a_r, b_r = right
        return a_l * a_r, a_r * b_l + b_r

    a_cum, h = lax.associative_scan(combine, (a, u), axis=1)
    h = h + a_cum * h0.astype(F32)[:, None, :]
    return h, h[:, -1]


def gla_chunked(q, k, v, log_g, S0):
    c = chunk_len(q.shape[2])
    mask = jnp.tril(jnp.ones((c, c), bool))

    def step(S, inp):
        qi, ki, vi, gi = inp
        b = jnp.cumsum(gi, axis=-2)
        b_last = b[..., -1, :]
        q_t = qi * jnp.exp(b)
        k_t = ki * jnp.exp(-b)
        att = jnp.where(mask, jnp.einsum('bhtd,bhsd->bhts', q_t, k_t), 0.0)
        o = jnp.einsum('bhts,bhsv->bhtv', att, vi) + jnp.einsum('bhtd,bhdv->bhtv', q_t, S)
        S = jnp.exp(b_last)[..., None] * S + jnp.einsum('bhsd,bhsv->bhdv', ki * jnp.exp(b_last[..., None, :] - b), vi)
        return S, o

    S, o = lax.scan(step, S0.astype(F32), (to_chunks(q, c), to_chunks(k, c), to_chunks(v, c), to_chunks(log_g, c)))
    return from_chunks(o), S


def retention_chunked(q, k, v, S0):
    c = chunk_len(q.shape[2])
    log_gamma = jnp.log(1.0 - 2.0 ** (-5.0 - jnp.arange(RET_H, dtype=F32)))
    idx = jnp.arange(c, dtype=F32)
    diff = idx[:, None] - idx[None, :]
    dmat = jnp.where(diff >= 0, jnp.exp(jnp.maximum(diff, 0.0)[None] * log_gamma[:, None, None]), 0.0)
    q_decay = jnp.exp((idx + 1.0)[None, :, None] * log_gamma[:, None, None])
    k_decay = jnp.exp((c - 1.0 - idx)[None, :, None] * log_gamma[:, None, None])
    chunk_decay = jnp.exp(c * log_gamma)[:, None, None]

    def step(S, inp):
        qi, ki, vi = inp
        att = jnp.einsum('bhtd,bhsd->bhts', qi, ki) * dmat
        o = jnp.einsum('bhts,bhsv->bhtv', att, vi) + jnp.einsum('bhtd,bhdv->bhtv', qi * q_decay, S)
        S = chunk_decay * S + jnp.einsum('bhsd,bhsv->bhdv', ki * k_decay, vi)
        return S, o

    S, o = lax.scan(step, S0.astype(F32), (to_chunks(q, c), to_chunks(k, c), to_chunks(v, c)))
    return from_chunks(o), S


def gated_delta_chunked(q, k, v, g, beta, S0):
    c = chunk_len(q.shape[2])
    strict = jnp.tril(jnp.ones((c, c), bool), -1)
    incl = jnp.tril(jnp.ones((c, c), bool))
    eye = jnp.eye(c, dtype=F32)

    def step(S, inp):
        qi, ki, vi, gi, bi = inp
        G = jnp.cumsum(gi, axis=-1)
        decay = jnp.exp(jnp.where(incl, G[..., :, None] - G[..., None, :], -jnp.inf))
        kb = ki * bi[..., None]
        M = jnp.where(strict, jnp.einsum('bhtd,bhsd->bhts', kb, ki) * decay, 0.0)
        Tm = lax.linalg.triangular_solve(eye + M, jnp.broadcast_to(eye, M.shape), left_side=True, lower=True, unit_diagonal=True)
        u = Tm @ (vi * bi[..., None])
        w = Tm @ (kb * jnp.exp(G)[..., None])
        v_new = u - w @ S
        att = jnp.einsum('bhtd,bhsd->bhts', qi, ki) * decay
        o = (qi * jnp.exp(G)[..., None]) @ S + att @ v_new
        S = jnp.exp(G[..., -1])[..., None, None] * S + jnp.einsum('bhsd,bhsv->bhdv', ki * jnp.exp(G[..., -1:] - G)[..., None], v_new)
        return S, o

    S, o = lax.scan(step, S0.astype(F32), (to_chunks(q, c), to_chunks(k, c), to_chunks(v, c), to_chunks(g, c), to_chunks(beta, c)))
    return from_chunks(o), S


def setup_inputs(seed: int = 0) -> dict:
    key = jax.random.key(seed)
    ks = iter(jax.random.split(key, 48))

    def nrm(shape, scale):
        return jax.random.normal(next(ks), shape, F32) * scale

    def uni(shape, lo, hi):
        return jax.random.uniform(next(ks), shape, F32, lo, hi)

    a_base = uni((DEPTH, BRANCH), 0.9, 0.999) ** (1.0 / LRU_C)
    dt = jnp.exp(uni((DEPTH, DN_H), math.log(1e-3), math.log(1e-1)))
    return {
        'x_prompt': nrm((BATCH, SEQ, D_MODEL), 1.0),
        'x_sample': nrm((DEC_BATCH, DEC_SEQ, D_MODEL), 1.0),
        'state_lru_conv': nrm((DEPTH, DEC_BATCH, CONV_W - 1, BRANCH), 1.0),
        'state_lru_h': nrm((DEPTH, DEC_BATCH, BRANCH), 1.0),
        'state_gla': nrm((DEPTH, DEC_BATCH, GLA_H, GLA_DK, GLA_DV), 1.0),
        'state_ret': nrm((DEPTH, DEC_BATCH, RET_H, RET_DK, RET_DV), 1.0),
        'state_dn_conv': nrm((DEPTH, DEC_BATCH, CONV_W - 1, 3 * BRANCH), 1.0),
        'state_dn': nrm((DEPTH, DEC_BATCH, DN_H, DN_D, DN_D), 0.1),
        'norm_g': 1.0 + nrm((DEPTH, 3, D_MODEL), 0.01),
        'final_norm_g': 1.0 + nrm((D_MODEL,), 0.01),
        'w_ff_in': nrm((DEPTH, 2, D_MODEL, 2 * D_FF), D_MODEL ** -0.5),
        'w_ff_out': nrm((DEPTH, 2, D_FF, D_MODEL), D_FF ** -0.5),
        'w_in': nrm((DEPTH, D_MODEL, N_IN), D_MODEL ** -0.5),
        'w_gate': nrm((DEPTH, N_BRANCH, D_MODEL, D_MODEL), D_MODEL ** -0.5),
        'w_branch': nrm((DEPTH, N_BRANCH, BRANCH, D_MODEL), BRANCH ** -0.5),
        'w_out': nrm((DEPTH, D_MODEL, D_MODEL), D_MODEL ** -0.5),
        'lru_conv_w': nrm((DEPTH, CONV_W, BRANCH), CONV_W ** -0.5),
        'lru_conv_b': nrm((DEPTH, BRANCH), 0.01),
        'lru_wa': nrm((DEPTH, LRU_BLOCKS, LRU_BD, LRU_BD), LRU_BD ** -0.5),
        'lru_ba': nrm((DEPTH, BRANCH), 0.01),
        'lru_wx': nrm((DEPTH, LRU_BLOCKS, LRU_BD, LRU_BD), LRU_BD ** -0.5),
        'lru_bx': nrm((DEPTH, BRANCH), 0.01),
        'lru_lambda': jnp.log(a_base) - jnp.log1p(-a_base),
        'gla_wg': nrm((DEPTH, GLA_RANK, GLA_H * GLA_DK), GLA_RANK ** -0.5),
        'gla_bg': uni((DEPTH, GLA_H * GLA_DK), 1.0, 3.0),
        'gla_norm_g': 1.0 + nrm((DEPTH, GLA_DV), 0.01),
        'ret_norm_g': 1.0 + nrm((DEPTH, RET_DV), 0.01),
        'dn_conv_w': nrm((DEPTH, CONV_W, 3 * BRANCH), CONV_W ** -0.5),
        'dn_a_log': jnp.log(uni((DEPTH, DN_H), 1.0, 16.0)),
        'dn_dt_bias': dt + jnp.log(-jnp.expm1(-dt)),
        'dn_norm_g': 1.0 + nrm((DEPTH, DN_D), 0.01),
    }


def reference(x_prompt, x_sample, state_lru_conv, state_lru_h, state_gla, state_ret, state_dn_conv, state_dn,
              norm_g, final_norm_g, w_ff_in, w_ff_out, w_in, w_gate, w_branch, w_out,
              lru_conv_w, lru_conv_b, lru_wa, lru_ba, lru_wx, lru_bx, lru_lambda,
              gla_wg, gla_bg, gla_norm_g, ret_norm_g, dn_conv_w, dn_a_log, dn_dt_bias, dn_norm_g):
    points = split_points()

    def mixer(h, st, pos, l):
        c_lru, h_lru, S_gla, S_ret, c_dn, S_dn = st
        (lx, ly, gq, gk, gv, gr, glr, rq, rk, rv, rg,
         dq, dk, dv, dz, db, da) = jnp.split(h @ w_in[l], points, axis=-1)

        xc, c_lru_new = causal_conv(lx, c_lru, lru_conv_w[l])
        hr, h_lru_new = rglru(xc + lru_conv_b[l], h_lru, lru_wa[l], lru_ba[l], lru_wx[l], lru_bx[l], lru_lambda[l], pos == 0)
        o_lru = hr * jax.nn.gelu(ly.astype(F32))

        q = split_heads(gq, GLA_H).astype(F32) * GLA_DK ** -0.5
        k = split_heads(gk, GLA_H)
        v = split_heads(gv, GLA_H)
        lg = split_heads(jax.nn.log_sigmoid((glr @ gla_wg[l] + gla_bg[l]).astype(F32)) / GLA_TAU, GLA_H)
        o, S_gla_new = gla_chunked(q, k, v, lg, S_gla)
        o_gla = merge_heads(rmsnorm(o, gla_norm_g[l])) * jax.nn.silu(gr.astype(F32))

        q = rope(split_heads(rq, RET_H), pos)
        k = rope(split_heads(rk, RET_H), pos) * RET_DK ** -0.5
        v = split_heads(rv, RET_H)
        o, S_ret_new = retention_chunked(q, k, v, S_ret)
        o_ret = merge_heads(groupnorm(o, ret_norm_g[l])) * jax.nn.silu(rg.astype(F32))

        qkv, c_dn_new = causal_conv(jnp.concatenate([dq, dk, dv], axis=-1), c_dn, dn_conv_w[l])
        q, k, v = jnp.split(jax.nn.silu(qkv), 3, axis=-1)
        q = l2norm(split_heads(q, DN_H)) * DN_D ** -0.5
        k = l2norm(split_heads(k, DN_H))
        v = split_heads(v, DN_H)
        beta = jax.nn.sigmoid(db.astype(F32)).transpose(0, 2, 1)
        g = (-jnp.exp(dn_a_log[l]) * jax.nn.softplus(da + dn_dt_bias[l])).astype(F32).transpose(0, 2, 1)
        o, S_dn_new = gated_delta_chunked(q, k, v, g, beta, S_dn)
        o_dn = merge_heads(rmsnorm(o, dn_norm_g[l])) * jax.nn.silu(dz.astype(F32))

        branches = (o_lru, o_gla, o_ret, o_dn)
        m = jax.nn.sigmoid(h @ w_gate[l, 0]) * (branches[0].astype(h.dtype) @ w_branch[l, 0])
        for n in range(1, N_BRANCH):
            m = m + jax.nn.sigmoid(h @ w_gate[l, n]) * (branches[n].astype(h.dtype) @ w_branch[l, n])
        return (m @ w_out[l]).astype(h.dtype), (c_lru_new, h_lru_new, S_gla_new, S_ret_new, c_dn_new, S_dn_new)

    def run(x, init_states, start):
        pos = start + jnp.arange(x.shape[1])
        collected = []
        for l in range(DEPTH):
            x = x + 0.5 * swiglu(rmsnorm(x, norm_g[l, 0]), w_ff_in[l, 0], w_ff_out[l, 0])
            mix, new = mixer(rmsnorm(x, norm_g[l, 1]), [s[l] for s in init_states], pos, l)
            x = x + mix
            x = x + 0.5 * swiglu(rmsnorm(x, norm_g[l, 2]), w_ff_in[l, 1], w_ff_out[l, 1])
            collected.append(new)
        y = rmsnorm(x, final_norm_g)
        stacked = [jnp.stack([c[i] for c in collected]) for i in range(6)]
        return y, stacked

    Bp = x_prompt.shape[0]
    zero_states = (
        jnp.zeros((DEPTH, Bp, CONV_W - 1, BRANCH), x_prompt.dtype),
        jnp.zeros((DEPTH, Bp, BRANCH), F32),
        jnp.zeros((DEPTH, Bp, GLA_H, GLA_DK, GLA_DV), F32),
        jnp.zeros((DEPTH, Bp, RET_H, RET_DK, RET_DV), F32),
        jnp.zeros((DEPTH, Bp, CONV_W - 1, 3 * BRANCH), x_prompt.dtype),
        jnp.zeros((DEPTH, Bp, DN_H, DN_D, DN_D), F32),
    )
    y_prompt, p_states = run(x_prompt, zero_states, 0)
    y_sample, s_states = run(x_sample, (state_lru_conv, state_lru_h, state_gla, state_ret, state_dn_conv, state_dn), PAST_LEN)
    p_lru_conv, p_lru_h, p_gla, p_ret, p_dn_conv, p_dn = p_states
    s_lru_conv, s_lru_h, s_gla, s_ret, s_dn_conv, s_dn = s_states
    return (y_prompt, y_sample, p_lru_conv, p_lru_h, p_gla, p_ret, p_dn_conv, p_dn,
            s_lru_conv, s_lru_h, s_gla, s_ret, s_dn_conv, s_dn)
```

```python
import functools
import math

import numpy as np
import jax
import jax.numpy as jnp
from jax import lax
from jax.experimental import pallas as pl
from jax.experimental.pallas import tpu as pltpu

F32 = jnp.float32
BF = jnp.bfloat16
EPS = 1e-6

D_MODEL = 2048
BRANCH = 512
CONV_W = 4
HEADS = 4
DK = 64
DV = 128
GLA_RANK = 16
GLA_TAU = 16.0
LRU_C = 8.0
LRU_BD = 128
ROPE_BASE = 10000.0
CHUNK = 64
PAST_LEN = 16384

V7X_VMEM_BYTES = 64 * 1024 * 1024
VMEM_LIMIT = V7X_VMEM_BYTES - 8 * 1024 * 1024
SUBLANES = 8
SAMPLE_PAD = 8

COL_LX, COL_LY = 0, 512
COL_GQ, COL_GK, COL_GV, COL_GR = 1024, 1280, 1536, 2048
COL_RQ, COL_RK, COL_RQS, COL_RKS, COL_RV, COL_RG = 2560, 2816, 3072, 3328, 3584, 4096
COL_DQKV, COL_DZ = 4608, 6144
COL_SM = 6656
SM_GLR, SM_DB, SM_DA = 0, 16, 20
N_PACK = 7168


def _mm(a, b):
    return jnp.dot(a.astype(BF), b.astype(BF), preferred_element_type=F32)


def _mm_nt(a, b):
    return lax.dot_general(a.astype(BF), b.astype(BF), (((1,), (1,)), ((), ())), preferred_element_type=F32)


def _mm_tn(a, b):
    return lax.dot_general(a.astype(BF), b.astype(BF), (((0,), (0,)), ((), ())), preferred_element_type=F32)


def _split3(x):
    hi = x.astype(BF)
    r = x - hi.astype(F32)
    mid = r.astype(BF)
    lo = (r - mid.astype(F32)).astype(BF)
    return hi, mid, lo


def _split2(x):
    hi = x.astype(BF)
    return hi, (x - hi.astype(F32)).astype(BF)


def _mm_hp(a, b):
    ah, al = _split2(a)
    bh, bl = _split2(b)
    d = functools.partial(jnp.dot, preferred_element_type=F32)
    return d(ah, bh) + d(ah, bl) + d(al, bh)


def _cumsum_rows(x, lower_ones):
    d = functools.partial(jnp.dot, preferred_element_type=F32)
    hi, mid, lo = _split3(x)
    return d(lower_ones, hi) + d(lower_ones, mid) + d(lower_ones, lo)


def _softplus(x):
    return jnp.maximum(x, 0.0) + jnp.log1p(jnp.exp(-jnp.abs(x)))


def _silu(x):
    return x * jax.nn.sigmoid(x)


def _gelu_tanh(x):
    return x * (0.5 * (1.0 + jnp.tanh(math.sqrt(2.0 / math.pi) * (x + 0.044715 * (x * x * x)))))


def _rms(x, g):
    return x * lax.rsqrt(jnp.mean(x * x, axis=-1, keepdims=True) + EPS) * g


def _pick_tile(n, target, align):
    best = None
    for t in range(align, min(n, target) + 1, align):
        if n % t == 0:
            best = t
    assert best is not None, (n, target, align)
    return best


def _cparams(sem):
    return pltpu.CompilerParams(dimension_semantics=sem, vmem_limit_bytes=VMEM_LIMIT)


def _ffn_kernel(x_ref, g_ref, wg_ref, wu_ref, wo_ref, o_ref, hn_ref):
    j = pl.program_id(1)
    nj = pl.num_programs(1)

    @pl.when(j == 0)
    def _():
        hn_ref[...] = _rms(x_ref[...], g_ref[...]).astype(BF)

    hn = hn_ref[...]
    gate = jnp.dot(hn, wg_ref[...], preferred_element_type=F32)
    up = jnp.dot(hn, wu_ref[...], preferred_element_type=F32)
    act = (_silu(gate) * up).astype(BF)
    part = jnp.dot(act, wo_ref[...], preferred_element_type=F32)

    @pl.when(j == 0)
    def _():
        o_ref[...] = part

    @pl.when(j > 0)
    def _():
        o_ref[...] += part

    @pl.when(j == nj - 1)
    def _():
        o_ref[...] = x_ref[...] + 0.5 * o_ref[...]


def _ffn(x, g, w_in, w_out, l, s):
    n, d = x.shape
    f = w_out.shape[2]
    tm = _pick_tile(n, 544, 16)
    tf = _pick_tile(f, 512, 128)
    nf = f // tf
    return pl.pallas_call(
        _ffn_kernel,
        grid=(n // tm, nf),
        in_specs=[
            pl.BlockSpec((tm, d), lambda i, j: (i, 0)),
            pl.BlockSpec((1, d), lambda i, j: (0, 0)),
            pl.BlockSpec((None, None, d, tf), lambda i, j: (l, s, 0, j)),
            pl.BlockSpec((None, None, d, tf), lambda i, j: (l, s, 0, j + nf)),
            pl.BlockSpec((None, None, tf, d), lambda i, j: (l, s, j, 0)),
        ],
        out_specs=pl.BlockSpec((tm, d), lambda i, j: (i, 0)),
        out_shape=jax.ShapeDtypeStruct((n, d), F32),
        scratch_shapes=[pltpu.VMEM((tm, d), BF)],
        compiler_params=_cparams(("parallel", "arbitrary")),
        name="ffn",
    )(x, g, w_in, w_in, w_out)


def _inproj_kernel(x_ref, g_ref, w_ref, o_ref, hn_ref):
    @pl.when(pl.program_id(1) == 0)
    def _():
        hn_ref[...] = _rms(x_ref[...], g_ref[...]).astype(BF)

    o_ref[...] = jnp.dot(hn_ref[...], w_ref[...], preferred_element_type=F32)


def _inproj(x, g, w_pack, l):
    n, d = x.shape
    npk = w_pack.shape[2]
    tm = _pick_tile(n, 544, 16)
    tn = _pick_tile(npk, 1024, 128)
    return pl.pallas_call(
        _inproj_kernel,
        grid=(n // tm, npk // tn),
        in_specs=[
            pl.BlockSpec((tm, d), lambda i, j: (i, 0)),
            pl.BlockSpec((1, d), lambda i, j: (0, 0)),
            pl.BlockSpec((None, d, tn), lambda i, j: (l, 0, j)),
        ],
        out_specs=pl.BlockSpec((tm, tn), lambda i, j: (i, j)),
        out_shape=jax.ShapeDtypeStruct((n, npk), F32),
        scratch_shapes=[pltpu.VMEM((tm, d), BF)],
        compiler_params=_cparams(("parallel", "arbitrary")),
        name="inproj",
    )(x, g, w_pack)


def _merge_kernel(x_ref, g_ref, ob_ref, wgate_ref, wbr_ref, wo_ref, o_ref, hn_ref):
    j = pl.program_id(1)
    nj = pl.num_programs(1)

    @pl.when(j == 0)
    def _():
        hn_ref[...] = _rms(x_ref[...], g_ref[...]).astype(BF)

    hn = hn_ref[...]
    m = None
    for n in range(HEADS):
        gate = jax.nn.sigmoid(jnp.dot(hn, wgate_ref[n], preferred_element_type=F32))
        br = jnp.dot(ob_ref[:, n * BRANCH:(n + 1) * BRANCH].astype(BF), wbr_ref[n], preferred_element_type=F32)
        m = gate * br if m is None else m + gate * br
    part = jnp.dot(m.astype(BF), wo_ref[...], preferred_element_type=F32)

    @pl.when(j == 0)
    def _():
        o_ref[...] = part

    @pl.when(j > 0)
    def _():
        o_ref[...] += part

    @pl.when(j == nj - 1)
    def _():
        o_ref[...] = x_ref[...] + o_ref[...]


def _merge(x, g, ob, w_gate, w_branch, w_out, l):
    n, d = x.shape
    tm = _pick_tile(n, 544, 16)
    tn = 256
    return pl.pallas_call(
        _merge_kernel,
        grid=(n // tm, d // tn),
        in_specs=[
            pl.BlockSpec((tm, d), lambda i, j: (i, 0)),
            pl.BlockSpec((1, d), lambda i, j: (0, 0)),
            pl.BlockSpec((tm, HEADS * BRANCH), lambda i, j: (i, 0)),
            pl.BlockSpec((None, HEADS, d, tn), lambda i, j: (l, 0, 0, j)),
            pl.BlockSpec((None, HEADS, BRANCH, tn), lambda i, j: (l, 0, 0, j)),
            pl.BlockSpec((None, tn, d), lambda i, j: (l, j, 0)),
        ],
        out_specs=pl.BlockSpec((tm, d), lambda i, j: (i, 0)),
        out_shape=jax.ShapeDtypeStruct((n, d), F32),
        scratch_shapes=[pltpu.VMEM((tm, d), BF)],
        compiler_params=_cparams(("parallel", "arbitrary")),
        name="merge",
    )(x, g, ob, w_gate, w_branch, w_out)


def _final_norm_kernel(x_ref, g_ref, o_ref):
    o_ref[...] = _rms(x_ref[...], g_ref[...])


def _final_norm(x, g):
    n, d = x.shape
    tm = _pick_tile(n, 544, 8)
    return pl.pallas_call(
        _final_norm_kernel,
        grid=(n // tm,),
        in_specs=[pl.BlockSpec((tm, d), lambda i: (i, 0)), pl.BlockSpec((1, d), lambda i: (0, 0))],
        out_specs=pl.BlockSpec((tm, d), lambda i: (i, 0)),
        out_shape=jax.ShapeDtypeStruct((n, d), F32),
        compiler_params=_cparams(("parallel",)),
        name="final_norm",
    )(x, g)


def _conv4(xs_ref, w, tp):
    y = None
    for j in range(CONV_W):
        term = xs_ref[pl.ds(SUBLANES - CONV_W + 1 + j, tp), :] * w[j:j + 1, :]
        y = term if y is None else y + term
    return y


def _make_lru_kernel(nb, tp, t_real, is_prompt):
    def kern(lx_ref, ly_ref, c0_ref, h0_ref, cw_ref, cb_ref, wa_ref, ba_ref, wx_ref, bx_ref, lam_ref,
             o_ref, cn_ref, hn_ref, xs_scr, a_scr, u_scr, cc_scr, ch_scr):
        tb = pl.program_id(1)

        @pl.when(tb == 0)
        def _():
            cc_scr[...] = c0_ref[...]
            ch_scr[...] = h0_ref[...]

        cw = cw_ref[...]
        neg_sp = -LRU_C * _softplus(-lam_ref[...])
        row = lax.broadcasted_iota(jnp.int32, (tp, 1), 0)

        def per_seq(bi, carry):
            r0 = pl.multiple_of(bi * tp, SUBLANES)
            xs_scr[0:SUBLANES, :] = cc_scr[bi]
            xs_scr[SUBLANES:SUBLANES + tp, :] = lx_ref[pl.ds(r0, tp), :]
            xc = _conv4(xs_scr, cw, tp) + cb_ref[...]
            cc_scr[bi] = xs_scr[pl.ds(t_real, SUBLANES), :]
            xcb = xc.astype(BF)
            r_parts, i_parts = [], []
            for n in range(BRANCH // LRU_BD):
                blk = xcb[:, n * LRU_BD:(n + 1) * LRU_BD]
                r_parts.append(jnp.dot(blk, wa_ref[n], preferred_element_type=F32))
                i_parts.append(jnp.dot(blk, wx_ref[n], preferred_element_type=F32))
            r = jax.nn.sigmoid(jnp.concatenate(r_parts, axis=1) + ba_ref[...])
            ig = jax.nn.sigmoid(jnp.concatenate(i_parts, axis=1) + bx_ref[...])
            log_a = r * neg_sp
            a = jnp.exp(log_a)
            mult = jnp.sqrt(-jnp.tanh(log_a) * (a * a + 1.0))
            if is_prompt:
                mult = jnp.where(jnp.logical_and(row == 0, tb == 0), 1.0, mult)
            a_scr[...] = a
            u_scr[...] = mult * ig * xc

            def step(t, h):
                h = a_scr[pl.ds(t, 1), :] * h + u_scr[pl.ds(t, 1), :]
                a_scr[pl.ds(t, 1), :] = h
                return h

            h_last = lax.fori_loop(0, t_real, step, ch_scr[bi])
            ch_scr[bi] = h_last
            o_ref[pl.ds(r0, tp), :] = a_scr[...] * _gelu_tanh(ly_ref[pl.ds(r0, tp), :])
            return carry

        lax.fori_loop(0, nb, per_seq, 0)
        cn_ref[...] = cc_scr[...]
        hn_ref[...] = ch_scr[...]

    return kern


def _lru(p, c0, h0, lw, l, nseq, tseq, nb, tp, t_real, is_prompt):
    nt = tseq // tp
    c = BRANCH
    rows = nb * tp
    wspec2 = lambda shape: pl.BlockSpec((None,) + shape, lambda b, t: (l,) + (0,) * len(shape))
    return pl.pallas_call(
        _make_lru_kernel(nb, tp, t_real, is_prompt),
        grid=(nseq // nb, nt),
        in_specs=[
            pl.BlockSpec((rows, c), lambda b, t: (b * nt + t, COL_LX // c)),
            pl.BlockSpec((rows, c), lambda b, t: (b * nt + t, COL_LY // c)),
            pl.BlockSpec((nb, SUBLANES, c), lambda b, t: (b, 0, 0)),
            pl.BlockSpec((nb, 1, c), lambda b, t: (b, 0, 0)),
            wspec2((CONV_W, c)), wspec2((1, c)),
            wspec2((c // LRU_BD, LRU_BD, LRU_BD)), wspec2((1, c)),
            wspec2((c // LRU_BD, LRU_BD, LRU_BD)), wspec2((1, c)),
            wspec2((1, c)),
        ],
        out_specs=[
            pl.BlockSpec((rows, c), lambda b, t: (b * nt + t, 0)),
            pl.BlockSpec((nb, SUBLANES, c), lambda b, t: (b, 0, 0)),
            pl.BlockSpec((nb, 1, c), lambda b, t: (b, 0, 0)),
        ],
        out_shape=[
            jax.ShapeDtypeStruct((nseq * tseq, c), F32),
            jax.ShapeDtypeStruct((nseq, SUBLANES, c), F32),
            jax.ShapeDtypeStruct((nseq, 1, c), F32),
        ],
        scratch_shapes=[
            pltpu.VMEM((tp + SUBLANES, c), F32), pltpu.VMEM((tp, c), F32), pltpu.VMEM((tp, c), F32),
            pltpu.VMEM((nb, SUBLANES, c), F32), pltpu.VMEM((nb, 1, c), F32),
        ],
        compiler_params=_cparams(("parallel", "arbitrary")),
        name="lru_prompt" if is_prompt else "lru_sample",
    )(p, p, c0, h0, lw["conv_w"], lw["conv_b"], lw["wa"], lw["ba"], lw["wx"], lw["bx"], lw["lam"])


def _chunk_consts(c, cl):
    row = lax.broadcasted_iota(jnp.int32, (c, c), 0)
    col = lax.broadcasted_iota(jnp.int32, (c, c), 1)
    tril = row >= col
    lower_ones = jnp.where(tril, 1.0, 0.0).astype(BF)
    real = lax.broadcasted_iota(jnp.int32, (c, 1), 0) < cl
    return row, col, tril, lower_ones, real


def _make_gla_kernel(nb, tp, c, t_real):
    nc = tp // c
    cl = min(c, t_real)

    def kern(q_ref, k_ref, v_ref, gr_ref, sm_ref, s0_ref, wg_ref, bg_ref, ng_ref, o_ref, sn_ref, s_scr):
        @pl.when(pl.program_id(1) == 0)
        def _():
            s_scr[...] = s0_ref[...]

        _, _, tril, lower_ones, real = _chunk_consts(c, cl)
        ones_real = jnp.where(real, 1.0, 0.0).astype(BF) * jnp.ones((c, DV), BF)
        ng = ng_ref[...]

        def per_chunk(idx, carry):
            bi = idx // nc
            r0 = pl.multiple_of(idx * c, SUBLANES)
            rows = pl.ds(r0, c)
            q = q_ref[rows, :] * (DK ** -0.5)
            k = k_ref[rows, :]
            v = v_ref[rows, :]
            gr = gr_ref[rows, :]
            lg = -_softplus(-(_mm(sm_ref[rows, :], wg_ref[...]) + bg_ref[...])) * (1.0 / GLA_TAU)
            b = _cumsum_rows(lg, lower_ones)
            hi, mid, lo = _split3(lg)
            d_tn = lambda x: lax.dot_general(x, ones_real, (((0,), (0,)), ((), ())), preferred_element_type=F32)
            b_last_col = d_tn(hi) + d_tn(mid) + d_tn(lo)
            b_last_row = b[cl - 1:cl, :]
            q_t = q * jnp.exp(b)
            k_t = jnp.where(real, k * jnp.exp(-b), 0.0)
            k_d = jnp.where(real, k * jnp.exp(b_last_row - b), 0.0)
            for h in range(HEADS):
                ks = slice(h * DK, (h + 1) * DK)
                vs = slice(h * DV, (h + 1) * DV)
                att = jnp.where(tril, _mm_nt(q_t[:, ks], k_t[:, ks]), 0.0)
                s = s_scr[bi, h]
                o = _mm(att, v[:, vs]) + _mm(q_t[:, ks], s)
                s_scr[bi, h] = jnp.exp(b_last_col[ks, :]) * s + _mm_tn(k_d[:, ks], v[:, vs])
                o_ref[rows, vs] = _rms(o, ng) * _silu(gr[:, vs])
            return carry

        lax.fori_loop(0, nb * nc, per_chunk, 0)
        sn_ref[...] = s_scr[...]

    return kern


def _gla(p, s0, lw, l, nseq, tseq, nb, tp, c, t_real, name):
    nt = tseq // tp
    rows = nb * tp
    qk = HEADS * DK
    vw = HEADS * DV
    wspec = lambda shape: pl.BlockSpec((None,) + shape, lambda b, t: (l,) + (0,) * len(shape))
    pcol = lambda w, off: pl.BlockSpec((rows, w), lambda b, t: (b * nt + t, off // w))
    return pl.pallas_call(
        _make_gla_kernel(nb, tp, c, t_real),
        grid=(nseq // nb, nt),
        in_specs=[
            pcol(qk, COL_GQ), pcol(qk, COL_GK), pcol(vw, COL_GV), pcol(vw, COL_GR), pcol(128, COL_SM),
            pl.BlockSpec((nb, HEADS, DK, DV), lambda b, t: (b, 0, 0, 0)),
            wspec((128, qk)), wspec((1, qk)), wspec((1, DV)),
        ],
        out_specs=[
            pl.BlockSpec((rows, vw), lambda b, t: (b * nt + t, 0)),
            pl.BlockSpec((nb, HEADS, DK, DV), lambda b, t: (b, 0, 0, 0)),
        ],
        out_shape=[
            jax.ShapeDtypeStruct((nseq * tseq, vw), F32),
            jax.ShapeDtypeStruct((nseq, HEADS, DK, DV), F32),
        ],
        scratch_shapes=[pltpu.VMEM((nb, HEADS, DK, DV), F32)],
        compiler_params=_cparams(("parallel", "arbitrary")),
        name=name,
    )(p, p, p, p, p, s0, lw["gla_wg"], lw["gla_bg"], lw["gla_ng"])


def _make_ret_kernel(nb, tp, c, t_real):
    nc = tp // c
    cl = min(c, t_real)
    log_gamma = [float(np.log(np.float32(1.0) - np.float32(2.0) ** np.float32(-5.0 - h))) for h in range(HEADS)]

    def kern(q_ref, k_ref, qs_ref, ks_ref, v_ref, g_ref, cos_ref, sin_ref, s0_ref, ng_ref, o_ref, sn_ref, s_scr):
        @pl.when(pl.program_id(1) == 0)
        def _():
            s_scr[...] = s0_ref[...]

        row, col, tril, _, real = _chunk_consts(c, cl)
        diff = (row - col).astype(F32)
        ridx = lax.broadcasted_iota(jnp.int32, (c, 1), 0).astype(F32)
        ng = ng_ref[...]

        def per_chunk(idx, carry):
            bi = idx // nc
            ci = idx - bi * nc
            r0 = pl.multiple_of(idx * c, SUBLANES)
            rows = pl.ds(r0, c)
            trow = pl.ds(pl.multiple_of(ci * c, SUBLANES), c)
            cos = cos_ref[trow, :]
            sin = sin_ref[trow, :]
            q = q_ref[rows, :] * cos + qs_ref[rows, :] * sin
            k = jnp.where(real, (k_ref[rows, :] * cos + ks_ref[rows, :] * sin) * (DK ** -0.5), 0.0)
            v = v_ref[rows, :]
            g = g_ref[rows, :]
            for h in range(HEADS):
                ks = slice(h * DK, (h + 1) * DK)
                vs = slice(h * DV, (h + 1) * DV)
                lgm = log_gamma[h]
                dmat = jnp.where(tril, jnp.exp(jnp.maximum(diff, 0.0) * lgm), 0.0)
                q_decay = jnp.exp((ridx + 1.0) * lgm)
                k_decay = jnp.exp((cl - 1.0 - ridx) * lgm)
                att = _mm_nt(q[:, ks], k[:, ks]) * dmat
                s = s_scr[bi, h]
                o = _mm(att, v[:, vs]) + _mm(q[:, ks] * q_decay, s)
                s_scr[bi, h] = math.exp(cl * lgm) * s + _mm_tn(k[:, ks] * k_decay, v[:, vs])
                oc = o - jnp.mean(o, axis=-1, keepdims=True)
                on = oc * lax.rsqrt(jnp.mean(oc * oc, axis=-1, keepdims=True) + EPS) * ng
                o_ref[rows, vs] = on * _silu(g[:, vs])
            return carry

        lax.fori_loop(0, nb * nc, per_chunk, 0)
        sn_ref[...] = s_scr[...]

    return kern


def _ret(p, s0, cos_t, sin_t, lw, l, nseq, tseq, nb, tp, c, t_real, name):
    nt = tseq // tp
    rows = nb * tp
    qk = HEADS * DK
    vw = HEADS * DV
    wspec = lambda shape: pl.BlockSpec((None,) + shape, lambda b, t: (l,) + (0,) * len(shape))
    pcol = lambda w, off: pl.BlockSpec((rows, w), lambda b, t: (b * nt + t, off // w))
    return pl.pallas_call(
        _make_ret_kernel(nb, tp, c, t_real),
        grid=(nseq // nb, nt),
        in_specs=[
            pcol(qk, COL_RQ), pcol(qk, COL_RK), pcol(qk, COL_RQS), pcol(qk, COL_RKS), pcol(vw, COL_RV),
            pcol(vw, COL_RG),
            pl.BlockSpec((tp, qk), lambda b, t: (t, 0)),
            pl.BlockSpec((tp, qk), lambda b, t: (t, 0)),
            pl.BlockSpec((nb, HEADS, DK, DV), lambda b, t: (b, 0, 0, 0)),
            wspec((1, DV)),
        ],
        out_specs=[
            pl.BlockSpec((rows, vw), lambda b, t: (b * nt + t, 0)),
            pl.BlockSpec((nb, HEADS, DK, DV), lambda b, t: (b, 0, 0, 0)),
        ],
        out_shape=[
            jax.ShapeDtypeStruct((nseq * tseq, vw), F32),
            jax.ShapeDtypeStruct((nseq, HEADS, DK, DV), F32),
        ],
        scratch_shapes=[pltpu.VMEM((nb, HEADS, DK, DV), F32)],
        compiler_params=_cparams(("parallel", "arbitrary")),
        name=name,
    )(p, p, p, p, p, p, cos_t, sin_t, s0, lw["ret_ng"])


def _unit_lower_inverse(m, c):
    row = lax.broadcasted_iota(jnp.int32, (c, c), 0)
    col = lax.broadcasted_iota(jnp.int32, (c, c), 1)
    eye = jnp.where(row == col, 1.0, 0.0)

    def same_block(bits):
        return lax.shift_right_logical(row, bits) == lax.shift_right_logical(col, bits)

    n1 = jnp.where(same_block(3), m, 0.0)
    n2 = _mm_hp(n1, n1)
    n4 = _mm_hp(n2, n2)
    d = eye - n1
    d = d + _mm_hp(d, n2)
    d = d + _mm_hp(d, n4)
    bits = 3
    while (1 << bits) < c:
        lm = jnp.where(jnp.logical_and(same_block(bits + 1), jnp.logical_not(same_block(bits))), m, 0.0)
        d = d - _mm_hp(d, _mm_hp(lm, d))
        bits += 1
    return d


def _make_dn_kernel(nb, tp, c, t_real):
    nc = tp // c
    cl = min(c, t_real)
    cw3 = 3 * BRANCH

    def kern(x_ref, z_ref, sm_ref, c0_ref, s0_ref, cw_ref, alog_ref, dtb_ref, ng_ref,
             o_ref, cn_ref, sn_ref, xs_scr, qkv_scr, cc_scr, s_scr):
        @pl.when(pl.program_id(1) == 0)
        def _():
            cc_scr[...] = c0_ref[...]
            s_scr[...] = s0_ref[...]

        row, col, tril, lower_ones, real = _chunk_consts(c, cl)
        strict = row > col
        lane = lax.broadcasted_iota(jnp.int32, (c, 128), 1)
        ng = ng_ref[...]
        cw = cw_ref[...]
        neg_a = -jnp.exp(alog_ref[...])

        def per_seq(bi, carry):
            s0r = pl.multiple_of(bi * tp, SUBLANES)
            xs_scr[0:SUBLANES, :] = cc_scr[bi]
            xs_scr[SUBLANES:SUBLANES + tp, :] = x_ref[pl.ds(s0r, tp), :]
            qkv_scr[...] = _silu(_conv4(xs_scr, cw, tp))
            cc_scr[bi] = xs_scr[pl.ds(t_real, SUBLANES), :]

            def per_chunk(ci, carry2):
                t0 = pl.multiple_of(ci * c, SUBLANES)
                rows = pl.ds(pl.multiple_of(s0r + t0, SUBLANES), c)
                sm = sm_ref[rows, :]
                z = z_ref[rows, :]
                gfull = neg_a * _softplus(sm + dtb_ref[...])
                gcum = _cumsum_rows(gfull, lower_ones)
                g3 = _split3(gcum)
                beta_all = jnp.where(real, jax.nn.sigmoid(sm), 0.0)
                for h in range(HEADS):
                    hs = slice(h * DV, (h + 1) * DV)
                    qh = qkv_scr[pl.ds(t0, c), h * DV:(h + 1) * DV]
                    kh = qkv_scr[pl.ds(t0, c), BRANCH + h * DV:BRANCH + (h + 1) * DV]
                    vh = qkv_scr[pl.ds(t0, c), 2 * BRANCH + h * DV:2 * BRANCH + (h + 1) * DV]
                    qh = qh * lax.rsqrt(jnp.sum(qh * qh, axis=-1, keepdims=True) + EPS) * (DV ** -0.5)
                    kh = kh * lax.rsqrt(jnp.sum(kh * kh, axis=-1, keepdims=True) + EPS)
                    kh = jnp.where(real, kh, 0.0)
                    beta = beta_all[:, SM_DB + h:SM_DB + h + 1]
                    gcol = gcum[:, SM_DA + h:SM_DA + h + 1]
                    sel = jnp.where(lane == SM_DA + h, 1.0, 0.0).astype(BF)
                    d_nt = lambda x: lax.dot_general(sel, x, (((1,), (1,)), ((), ())), preferred_element_type=F32)
                    grow = d_nt(g3[0]) + d_nt(g3[1]) + d_nt(g3[2])
                    decay = jnp.where(tril, jnp.exp(jnp.where(tril, gcol - grow, 0.0)), 0.0)
                    g_last = gcol[cl - 1:cl, :]
                    e_g = jnp.exp(gcol)
                    kb = kh * beta
                    kk = _mm_nt(kb, kh) * decay
                    tm = _unit_lower_inverse(jnp.where(strict, kk, 0.0), c)
                    u = _mm(tm, vh * beta)
                    w = _mm(tm, kb * e_g)
                    s = s_scr[bi, h]
                    v_new = u - _mm(w, s)
                    att = _mm_nt(qh, kh) * decay
                    o = _mm(qh * e_g, s) + _mm(att, v_new)
                    s_scr[bi, h] = jnp.exp(g_last) * s + _mm_tn(kh * jnp.exp(g_last - gcol), v_new)
                    o_ref[rows, hs] = _rms(o, ng) * _silu(z[:, hs])
                return carry2

            lax.fori_loop(0, nc, per_chunk, 0)
            return carry

        lax.fori_loop(0, nb, per_seq, 0)
        cn_ref[...] = cc_scr[...]
        sn_ref[...] = s_scr[...]

    return kern


def _dn(p, c0, s0, lw, l, nseq, tseq, nb, tp, c, t_real, name):
    nt = tseq // tp
    rows = nb * tp
    cw3 = 3 * BRANCH
    wspec = lambda shape: pl.BlockSpec((None,) + shape, lambda b, t: (l,) + (0,) * len(shape))
    pcol = lambda w, off: pl.BlockSpec((rows, w), lambda b, t: (b * nt + t, off // w))
    return pl.pallas_call(
        _make_dn_kernel(nb, tp, c, t_real),
        grid=(nseq // nb, nt),
        in_specs=[
            pcol(cw3, COL_DQKV), pcol(BRANCH, COL_DZ), pcol(128, COL_SM),
            pl.BlockSpec((nb, SUBLANES, cw3), lambda b, t: (b, 0, 0)),
            pl.BlockSpec((nb, HEADS, DV, DV), lambda b, t: (b, 0, 0, 0)),
            wspec((CONV_W, cw3)), wspec((1, 128)), wspec((1, 128)), wspec((1, DV)),
        ],
        out_specs=[
            pl.BlockSpec((rows, BRANCH), lambda b, t: (b * nt + t, 0)),
            pl.BlockSpec((nb, SUBLANES, cw3), lambda b, t: (b, 0, 0)),
            pl.BlockSpec((nb, HEADS, DV, DV), lambda b, t: (b, 0, 0, 0)),
        ],
        out_shape=[
            jax.ShapeDtypeStruct((nseq * tseq, BRANCH), F32),
            jax.ShapeDtypeStruct((nseq, SUBLANES, cw3), F32),
            jax.ShapeDtypeStruct((nseq, HEADS, DV, DV), F32),
        ],
        scratch_shapes=[
            pltpu.VMEM((tp + SUBLANES, cw3), F32), pltpu.VMEM((tp, cw3), F32),
            pltpu.VMEM((nb, SUBLANES, cw3), F32), pltpu.VMEM((nb, HEADS, DV, DV), F32),
        ],
        compiler_params=_cparams(("parallel", "arbitrary")),
        name=name,
    )(p, p, p, c0, s0, lw["dn_conv_w"], lw["dn_alog"], lw["dn_dtb"], lw["dn_ng"])


def _pack_w_in(w_in):
    sizes = (BRANCH, BRANCH, HEADS * DK, HEADS * DK, BRANCH, BRANCH, GLA_RANK,
             HEADS * DK, HEADS * DK, BRANCH, BRANCH, BRANCH, BRANCH, BRANCH, BRANCH, HEADS, HEADS)
    offs = np.concatenate([[0], np.cumsum(sizes)])
    seg = lambda i: w_in[:, :, int(offs[i]):int(offs[i + 1])]
    (lx, ly, gq, gk, gv, gr, glr, rq, rk, rv, rg, dq, dk, dv, dz, db, da) = [seg(i) for i in range(len(sizes))]

    def swap_halves(w):
        parts = []
        for h in range(HEADS):
            parts += [w[:, :, h * DK + DK // 2:(h + 1) * DK], w[:, :, h * DK:h * DK + DK // 2]]
        return jnp.concatenate(parts, axis=2)

    lead = w_in.shape[:2]
    small_pad = jnp.zeros(lead + (128 - GLA_RANK - 2 * HEADS,), w_in.dtype)
    used = COL_SM + 128
    tail = jnp.zeros(lead + (N_PACK - used,), w_in.dtype)
    packed = jnp.concatenate([lx, ly, gq, gk, gv, gr, rq, rk, swap_halves(rq), swap_halves(rk), rv, rg,
                              dq, dk, dv, dz, glr, db, da, small_pad, tail], axis=2)
    return packed.astype(BF)


def _rope_tables(pos):
    half = DK // 2
    inv = ROPE_BASE ** (-jnp.arange(half, dtype=F32) / half)
    ang = pos.astype(F32)[:, None] * inv[None, :]
    cos, sin = jnp.cos(ang), jnp.sin(ang)
    cos_t = jnp.tile(jnp.concatenate([cos, cos], axis=1), (1, HEADS))
    sin_t = jnp.tile(jnp.concatenate([-sin, sin], axis=1), (1, HEADS))
    return cos_t, sin_t


def _lane_row(vals, off):
    depth = vals.shape[0]
    return jnp.zeros((depth, 1, 128), F32).at[:, 0, off:off + HEADS].set(vals.astype(F32))


def kernel(x_prompt, x_sample, state_lru_conv, state_lru_h, state_gla, state_ret, state_dn_conv, state_dn,
           norm_g, final_norm_g, w_ff_in, w_ff_out, w_in, w_gate, w_branch, w_out,
           lru_conv_w, lru_conv_b, lru_wa, lru_ba, lru_wx, lru_bx, lru_lambda,
           gla_wg, gla_bg, gla_norm_g, ret_norm_g, dn_conv_w, dn_a_log, dn_dt_bias, dn_norm_g):
    bp, tpr, d = x_prompt.shape
    bs, ts, _ = x_sample.shape
    depth = w_in.shape[0]
    n_p, n_s = bp * tpr, bs * ts
    assert d == D_MODEL and ts <= SAMPLE_PAD and tpr % CHUNK == 0

    w_ff_in_b = w_ff_in.astype(BF)
    w_ff_out_b = w_ff_out.astype(BF)
    w_pack = _pack_w_in(w_in)
    w_gate_b = w_gate.astype(BF)
    w_branch_b = w_branch.astype(BF)
    w_out_b = w_out.astype(BF)
    wg_pad = jnp.zeros((depth, 128, HEADS * DK), F32).at[:, SM_GLR:SM_GLR + GLA_RANK, :].set(gla_wg).astype(BF)
    r3 = lambda a: a.reshape(a.shape[0], 1, a.shape[1])
    lw = {
        "conv_w": lru_conv_w, "conv_b": r3(lru_conv_b), "wa": lru_wa.astype(BF), "ba": r3(lru_ba),
        "wx": lru_wx.astype(BF), "bx": r3(lru_bx), "lam": r3(lru_lambda),
        "gla_wg": wg_pad, "gla_bg": r3(gla_bg), "gla_ng": r3(gla_norm_g), "ret_ng": r3(ret_norm_g),
        "dn_conv_w": dn_conv_w, "dn_alog": _lane_row(dn_a_log, SM_DA), "dn_dtb": _lane_row(dn_dt_bias, SM_DA),
        "dn_ng": r3(dn_norm_g),
    }
    norm_g4 = norm_g.reshape(depth, 3, 1, d)

    cos_p, sin_p = _rope_tables(jnp.arange(tpr))
    pos_s = jnp.where(jnp.arange(SAMPLE_PAD) < ts, PAST_LEN + jnp.arange(SAMPLE_PAD), 0)
    cos_s, sin_s = _rope_tables(pos_s)

    pad_conv = lambda a: jnp.pad(a, ((0, 0), (SUBLANES - (CONV_W - 1), 0), (0, 0)))
    zeros = lambda *s: jnp.zeros(s, F32)

    tp_p = _pick_tile(tpr, 512, CHUNK)
    nb_s = _pick_tile(bs, 16, 1)

    x = jnp.concatenate([x_prompt.reshape(n_p, d), x_sample.reshape(n_s, d)], axis=0)
    new_p = [[] for _ in range(6)]
    new_s = [[] for _ in range(6)]
    for l in range(depth):
        x = _ffn(x, norm_g4[l, 0], w_ff_in_b, w_ff_out_b, l, 0)
        p = _inproj(x, norm_g4[l, 1], w_pack, l)
        p_s = jnp.pad(p[n_p:].reshape(bs, ts, N_PACK), ((0, 0), (0, SAMPLE_PAD - ts), (0, 0)))
        p_s = p_s.reshape(bs * SAMPLE_PAD, N_PACK)

        a_p = dict(nseq=bp, tseq=tpr, nb=1, tp=tp_p)
        o_lru_p, c_lru_p, h_lru_p = _lru(p, zeros(bp, SUBLANES, BRANCH), zeros(bp, 1, BRANCH), lw, l,
                                         t_real=tp_p, is_prompt=True, **a_p)
        o_gla_p, s_gla_p = _gla(p, zeros(bp, HEADS, DK, DV), lw, l, c=CHUNK, t_real=tp_p, name="gla_prompt", **a_p)
        o_ret_p, s_ret_p = _ret(p, zeros(bp, HEADS, DK, DV), cos_p, sin_p, lw, l, c=CHUNK, t_real=tp_p,
                                name="ret_prompt", **a_p)
        o_dn_p, c_dn_p, s_dn_p = _dn(p, zeros(bp, SUBLANES, 3 * BRANCH), zeros(bp, HEADS, DV, DV), lw, l,
                                     c=CHUNK, t_real=tp_p, name="dn_prompt", **a_p)

        a_s = dict(nseq=bs, tseq=SAMPLE_PAD, nb=nb_s, tp=SAMPLE_PAD)
        o_lru_s, c_lru_s, h_lru_s = _lru(p_s, pad_conv(state_lru_conv[l]), state_lru_h[l][:, None, :], lw, l,
                                         t_real=ts, is_prompt=False, **a_s)
        o_gla_s, s_gla_s = _gla(p_s, state_gla[l], lw, l, c=SAMPLE_PAD, t_real=ts, name="gla_sample", **a_s)
        o_ret_s, s_ret_s = _ret(p_s, state_ret[l], cos_s, sin_s, lw, l, c=SAMPLE_PAD, t_real=ts,
                                name="ret_sample", **a_s)
        o_dn_s, c_dn_s, s_dn_s = _dn(p_s, pad_conv(state_dn_conv[l]), state_dn[l], lw, l, c=SAMPLE_PAD, t_real=ts,
                                     name="dn_sample", **a_s)

        unpad = lambda o: o.reshape(bs, SAMPLE_PAD, BRANCH)[:, :ts].reshape(n_s, BRANCH)
        ob = jnp.concatenate([
            jnp.concatenate([o_lru_p, o_gla_p, o_ret_p, o_dn_p], axis=1),
            jnp.concatenate([unpad(o_lru_s), unpad(o_gla_s), unpad(o_ret_s), unpad(o_dn_s)], axis=1)], axis=0)
        x = _merge(x, norm_g4[l, 1], ob, w_gate_b, w_branch_b, w_out_b, l)
        x = _ffn(x, norm_g4[l, 2], w_ff_in_b, w_ff_out_b, l, 1)

        tail3 = lambda cwin: cwin[:, SUBLANES - (CONV_W - 1):, :]
        for lst, vals in ((new_p, (tail3(c_lru_p), h_lru_p[:, 0], s_gla_p, s_ret_p, tail3(c_dn_p), s_dn_p)),
                          (new_s, (tail3(c_lru_s), h_lru_s[:, 0], s_gla_s, s_ret_s, tail3(c_dn_s), s_dn_s))):
            for i, v in enumerate(vals):
                lst[i].append(v)

    y = _final_norm(x, final_norm_g.reshape(1, d))
    y_prompt = y[:n_p].reshape(bp, tpr, d)
    y_sample = y[n_p:].reshape(bs, ts, d)
    stack = lambda lst: [jnp.stack(v) for v in lst]
    return tuple([y_prompt, y_sample] + stack(new_p) + stack(new_s))
```

```python
import functools
import math

import numpy as np
import jax
import jax.numpy as jnp
from jax import lax
from jax.experimental import pallas as pl
from jax.experimental.pallas import tpu as pltpu

F32 = jnp.float32
BF = jnp.bfloat16
EPS = 1e-6

D_MODEL = 2048
BRANCH = 512
CONV_W = 4
HEADS = 4
DK = 64
DV = 128
GLA_RANK = 16
GLA_TAU = 16.0
LRU_C = 8.0
LRU_BD = 128
ROPE_BASE = 10000.0
CHUNK = 64
PAST_LEN = 16384

V7X_VMEM_BYTES = 64 * 1024 * 1024
VMEM_LIMIT = V7X_VMEM_BYTES - 8 * 1024 * 1024
SUBLANES = 8
SAMPLE_PAD = 8

COL_LX, COL_LY = 0, 512
COL_GQ, COL_GK, COL_GV, COL_GR = 1024, 1280, 1536, 2048
COL_RQ, COL_RK, COL_RQS, COL_RKS, COL_RV, COL_RG = 2560, 2816, 3072, 3328, 3584, 4096
COL_DQKV, COL_DZ = 4608, 6144
COL_SM = 6656
SM_GLR, SM_DB, SM_DA = 0, 16, 20
N_PACK = 7168


def _dot(a, b):
    return jnp.dot(a, b, preferred_element_type=F32)


def _mm(a, b):
    return _dot(a.astype(BF), b.astype(BF))


def _mm_nt(a, b):
    return lax.dot_general(a.astype(BF), b.astype(BF), (((1,), (1,)), ((), ())), preferred_element_type=F32)


def _mm_tn(a, b):
    return lax.dot_general(a.astype(BF), b.astype(BF), (((0,), (0,)), ((), ())), preferred_element_type=F32)


def _split3(x):
    hi = x.astype(BF)
    r = x - hi.astype(F32)
    mid = r.astype(BF)
    lo = (r - mid.astype(F32)).astype(BF)
    return hi, mid, lo


def _split2(x):
    hi = x.astype(BF)
    return hi, (x - hi.astype(F32)).astype(BF)


def _mm_hp(a, b):
    ah, al = _split2(a)
    bh, bl = _split2(b)
    return _dot(ah, bh) + _dot(ah, bl) + _dot(al, bh)


def _cumsum_rows(x, lower_ones):
    hi, mid, lo = _split3(x)
    return _dot(lower_ones, hi) + _dot(lower_ones, mid) + _dot(lower_ones, lo)


def _softplus(x):
    return jnp.maximum(x, 0.0) + jnp.log1p(jnp.exp(-jnp.abs(x)))


def _silu(x):
    return x * jax.nn.sigmoid(x)


def _gelu_tanh(x):
    return x * (0.5 * (1.0 + jnp.tanh(math.sqrt(2.0 / math.pi) * (x + 0.044715 * (x * x * x)))))


def _rms(x, g):
    return x * lax.rsqrt(jnp.mean(x * x, axis=-1, keepdims=True) + EPS) * g


def _pick_tile(n, target, align):
    best = None
    for t in range(align, min(n, target) + 1, align):
        if n % t == 0:
            best = t
    assert best is not None, (n, target, align)
    return best


def _cparams(sem):
    return pltpu.CompilerParams(dimension_semantics=sem, vmem_limit_bytes=VMEM_LIMIT)


def _ffn_kernel(x_ref, g_ref, wg_ref, wu_ref, wo_ref, o_ref, hn_ref):
    j = pl.program_id(1)
    nj = pl.num_programs(1)

    @pl.when(j == 0)
    def _():
        hn_ref[...] = _rms(x_ref[...], g_ref[...]).astype(BF)

    hn = hn_ref[...]
    gate = _dot(hn, wg_ref[...])
    up = _dot(hn, wu_ref[...])
    act = (_silu(gate) * up).astype(BF)
    part = _dot(act, wo_ref[...])

    @pl.when(j == 0)
    def _():
        o_ref[...] = part

    @pl.when(j > 0)
    def _():
        o_ref[...] += part

    @pl.when(j == nj - 1)
    def _():
        o_ref[...] = x_ref[...] + 0.5 * o_ref[...]


def _ffn(x, g, w_in, w_out, l, s):
    n, d = x.shape
    f = w_out.shape[2]
    tm = _pick_tile(n, 512, 16)
    tf = _pick_tile(f, 512, 128)
    nf = f // tf
    return pl.pallas_call(
        _ffn_kernel,
        grid=(n // tm, nf),
        in_specs=[
            pl.BlockSpec((tm, d), lambda i, j: (i, 0)),
            pl.BlockSpec((1, d), lambda i, j: (0, 0)),
            pl.BlockSpec((None, None, d, tf), lambda i, j: (l, s, 0, j)),
            pl.BlockSpec((None, None, d, tf), lambda i, j: (l, s, 0, j + nf)),
            pl.BlockSpec((None, None, tf, d), lambda i, j: (l, s, j, 0)),
        ],
        out_specs=pl.BlockSpec((tm, d), lambda i, j: (i, 0)),
        out_shape=jax.ShapeDtypeStruct((n, d), F32),
        scratch_shapes=[pltpu.VMEM((tm, d), BF)],
        compiler_params=_cparams(("parallel", "arbitrary")),
        name="ffn",
    )(x, g, w_in, w_in, w_out)


def _inproj_kernel(x_ref, g_ref, w_ref, o_ref, hn_ref):
    @pl.when(pl.program_id(1) == 0)
    def _():
        hn_ref[...] = _rms(x_ref[...], g_ref[...]).astype(BF)

    o_ref[...] = _dot(hn_ref[...], w_ref[...])


def _inproj(x, g, w_pack, l):
    n, d = x.shape
    npk = w_pack.shape[2]
    tm = _pick_tile(n, 512, 16)
    tn = _pick_tile(npk, 1024, 128)
    return pl.pallas_call(
        _inproj_kernel,
        grid=(n // tm, npk // tn),
        in_specs=[
            pl.BlockSpec((tm, d), lambda i, j: (i, 0)),
            pl.BlockSpec((1, d), lambda i, j: (0, 0)),
            pl.BlockSpec((None, d, tn), lambda i, j: (l, 0, j)),
        ],
        out_specs=pl.BlockSpec((tm, tn), lambda i, j: (i, j)),
        out_shape=jax.ShapeDtypeStruct((n, npk), F32),
        scratch_shapes=[pltpu.VMEM((tm, d), BF)],
        compiler_params=_cparams(("parallel", "arbitrary")),
        name="inproj",
    )(x, g, w_pack)


def _merge_kernel(x_ref, g_ref, b0_ref, b1_ref, b2_ref, b3_ref, wgate_ref, wbr_ref, wo_ref, o_ref, hn_ref):
    j = pl.program_id(1)
    nj = pl.num_programs(1)

    @pl.when(j == 0)
    def _():
        hn_ref[...] = _rms(x_ref[...], g_ref[...]).astype(BF)

    hn = hn_ref[...]
    m = None
    for n, b_ref in enumerate((b0_ref, b1_ref, b2_ref, b3_ref)):
        gate = jax.nn.sigmoid(_dot(hn, wgate_ref[n]))
        br = _dot(b_ref[...].astype(BF), wbr_ref[n])
        m = gate * br if m is None else m + gate * br
    part = _dot(m.astype(BF), wo_ref[...])

    @pl.when(j == 0)
    def _():
        o_ref[...] = part

    @pl.when(j > 0)
    def _():
        o_ref[...] += part

    @pl.when(j == nj - 1)
    def _():
        o_ref[...] = x_ref[...] + o_ref[...]


def _merge(x, g, branches, w_gate, w_branch, w_out, l):
    n, d = x.shape
    tm = _pick_tile(n, 512, 16)
    tn = 256
    return pl.pallas_call(
        _merge_kernel,
        grid=(n // tm, d // tn),
        in_specs=[
            pl.BlockSpec((tm, d), lambda i, j: (i, 0)),
            pl.BlockSpec((1, d), lambda i, j: (0, 0)),
        ] + [pl.BlockSpec((tm, BRANCH), lambda i, j: (i, 0))] * HEADS + [
            pl.BlockSpec((None, HEADS, d, tn), lambda i, j: (l, 0, 0, j)),
            pl.BlockSpec((None, HEADS, BRANCH, tn), lambda i, j: (l, 0, 0, j)),
            pl.BlockSpec((None, tn, d), lambda i, j: (l, j, 0)),
        ],
        out_specs=pl.BlockSpec((tm, d), lambda i, j: (i, 0)),
        out_shape=jax.ShapeDtypeStruct((n, d), F32),
        scratch_shapes=[pltpu.VMEM((tm, d), BF)],
        compiler_params=_cparams(("parallel", "arbitrary")),
        name="merge",
    )(x, g, *branches, w_gate, w_branch, w_out)


def _final_norm_kernel(x_ref, g_ref, o_ref):
    o_ref[...] = _rms(x_ref[...], g_ref[...])


def _final_norm(x, g):
    n, d = x.shape
    tm = _pick_tile(n, 512, 8)
    return pl.pallas_call(
        _final_norm_kernel,
        grid=(n // tm,),
        in_specs=[pl.BlockSpec((tm, d), lambda i: (i, 0)), pl.BlockSpec((1, d), lambda i: (0, 0))],
        out_specs=pl.BlockSpec((tm, d), lambda i: (i, 0)),
        out_shape=jax.ShapeDtypeStruct((n, d), F32),
        compiler_params=_cparams(("parallel",)),
        name="final_norm",
    )(x, g)


def _conv4(xs_ref, j_seq, w, tp):
    y = None
    for j in range(CONV_W):
        term = xs_ref[j_seq, pl.ds(SUBLANES - CONV_W + 1 + j, tp), :] * w[j:j + 1, :]
        y = term if y is None else y + term
    return y


def _make_lru_kernel(nb, gb, tp, t_real, is_prompt):
    def kern(lx_ref, ly_ref, c0_ref, h0_ref, cw_ref, cb_ref, wa_ref, ba_ref, wx_ref, bx_ref, lam_ref,
             o_ref, cn_ref, hn_ref, xs_scr, a_scr, u_scr, cc_scr, ch_scr):
        tb = pl.program_id(1)

        @pl.when(tb == 0)
        def _():
            cc_scr[...] = c0_ref[...]
            ch_scr[...] = h0_ref[...]

        cw = cw_ref[...]
        neg_sp = -LRU_C * _softplus(-lam_ref[...])
        row = lax.broadcasted_iota(jnp.int32, (tp, 1), 0)

        def per_group(g, carry):
            seqs = [g * gb + j for j in range(gb)]
            for j, sq in enumerate(seqs):
                xs_scr[j, 0:SUBLANES, :] = cc_scr[sq]
                xs_scr[j, SUBLANES:SUBLANES + tp, :] = lx_ref[sq]
            for j, sq in enumerate(seqs):
                xc = _conv4(xs_scr, j, cw, tp) + cb_ref[...]
                cc_scr[sq] = xs_scr[j, pl.ds(t_real, SUBLANES), :]
                xcb = xc.astype(BF)
                r_parts, i_parts = [], []
                for n in range(BRANCH // LRU_BD):
                    blk = xcb[:, n * LRU_BD:(n + 1) * LRU_BD]
                    r_parts.append(_dot(blk, wa_ref[n]))
                    i_parts.append(_dot(blk, wx_ref[n]))
                r = jax.nn.sigmoid(jnp.concatenate(r_parts, axis=1) + ba_ref[...])
                ig = jax.nn.sigmoid(jnp.concatenate(i_parts, axis=1) + bx_ref[...])
                log_a = r * neg_sp
                a = jnp.exp(log_a)
                mult = jnp.sqrt(-jnp.tanh(log_a) * (a * a + 1.0))
                if is_prompt:
                    mult = jnp.where(jnp.logical_and(row == 0, tb == 0), 1.0, mult)
                a_scr[j] = a
                u_scr[j] = mult * ig * xc

            def step(t, hs):
                out = []
                for j in range(gb):
                    h = a_scr[j, pl.ds(t, 1), :] * hs[j] + u_scr[j, pl.ds(t, 1), :]
                    a_scr[j, pl.ds(t, 1), :] = h
                    out.append(h)
                return tuple(out)

            h_last = lax.fori_loop(0, t_real, step, tuple(ch_scr[sq] for sq in seqs),
                                   unroll=True if t_real <= SUBLANES else 8)
            for j, sq in enumerate(seqs):
                ch_scr[sq] = h_last[j]
                o_ref[sq] = a_scr[j] * _gelu_tanh(ly_ref[sq])
            return carry

        lax.fori_loop(0, nb // gb, per_group, 0)
        cn_ref[...] = cc_scr[...]
        hn_ref[...] = ch_scr[...]

    return kern


def _mixer_specs(nb, tp, l):
    wspec = lambda shape: pl.BlockSpec((None,) + shape, lambda b, t: (l,) + (0,) * len(shape))
    pcol = lambda w, off: pl.BlockSpec((nb, tp, w), lambda b, t: (b, t, off // w))
    state = lambda *shape: pl.BlockSpec((nb,) + shape, lambda b, t: (b,) + (0,) * len(shape))
    return wspec, pcol, state


def _lru(p, c0, h0, lw, l, nb, gb, tp, t_real, is_prompt):
    nseq, tseq, _ = p.shape
    c = BRANCH
    wspec, pcol, state = _mixer_specs(nb, tp, l)
    return pl.pallas_call(
        _make_lru_kernel(nb, gb, tp, t_real, is_prompt),
        grid=(nseq // nb, tseq // tp),
        in_specs=[
            pcol(c, COL_LX), pcol(c, COL_LY), state(SUBLANES, c), state(1, c),
            wspec((CONV_W, c)), wspec((1, c)),
            wspec((c // LRU_BD, LRU_BD, LRU_BD)), wspec((1, c)),
            wspec((c // LRU_BD, LRU_BD, LRU_BD)), wspec((1, c)),
            wspec((1, c)),
        ],
        out_specs=[pcol(c, 0), state(SUBLANES, c), state(1, c)],
        out_shape=[
            jax.ShapeDtypeStruct((nseq, tseq, c), F32),
            jax.ShapeDtypeStruct((nseq, SUBLANES, c), F32),
            jax.ShapeDtypeStruct((nseq, 1, c), F32),
        ],
        scratch_shapes=[
            pltpu.VMEM((gb, tp + SUBLANES, c), F32), pltpu.VMEM((gb, tp, c), F32), pltpu.VMEM((gb, tp, c), F32),
            pltpu.VMEM((nb, SUBLANES, c), F32), pltpu.VMEM((nb, 1, c), F32),
        ],
        compiler_params=_cparams(("parallel", "arbitrary")),
        name="lru_prompt" if is_prompt else "lru_sample",
    )(p, p, c0, h0, lw["conv_w"], lw["conv_b"], lw["wa"], lw["ba"], lw["wx"], lw["bx"], lw["lam"])


def _chunk_consts(c, cl):
    row = lax.broadcasted_iota(jnp.int32, (c, c), 0)
    col = lax.broadcasted_iota(jnp.int32, (c, c), 1)
    tril = row >= col
    lower_ones = jnp.where(tril, 1.0, 0.0).astype(BF)
    real = lax.broadcasted_iota(jnp.int32, (c, 1), 0) < cl
    return row, col, tril, lower_ones, real


def _ks(h):
    return slice(h * DK, (h + 1) * DK)


def _vs(h):
    return slice(h * DV, (h + 1) * DV)


def _make_gla_kernel(nb, gb, tp, c, t_real):
    nc = tp // c
    cl = min(c, t_real)
    G, H = range(gb), range(HEADS)

    def kern(q_ref, k_ref, v_ref, gr_ref, sm_ref, s0_ref, wg_ref, bg_ref, ng_ref, o_ref, sn_ref, s_scr):
        @pl.when(pl.program_id(1) == 0)
        def _():
            s_scr[...] = s0_ref[...]

        _, _, tril, lower_ones, real = _chunk_consts(c, cl)
        ones_real = jnp.where(real, 1.0, 0.0).astype(BF) * jnp.ones((c, DV), BF)
        d_tn = lambda x: lax.dot_general(x, ones_real, (((0,), (0,)), ((), ())), preferred_element_type=F32)
        ng = ng_ref[...]
        wg = wg_ref[...]
        bg = bg_ref[...]

        def per_chunk(idx, carry):
            g = idx // nc
            ci = idx - g * nc
            rows = pl.ds(pl.multiple_of(ci * c, SUBLANES), c)
            seqs = [g * gb + j for j in G]
            lg = [-_softplus(-(_mm(sm_ref[sq, rows, :], wg) + bg)) * (1.0 / GLA_TAU) for sq in seqs]
            sp = [_split3(x) for x in lg]
            b = [_dot(lower_ones, s[0]) + _dot(lower_ones, s[1]) + _dot(lower_ones, s[2]) for s in sp]
            b_last_col = [d_tn(s[0]) + d_tn(s[1]) + d_tn(s[2]) for s in sp]
            q_t = [q_ref[sq, rows, :] * (DK ** -0.5) * jnp.exp(b[j]) for j, sq in enumerate(seqs)]
            k = [k_ref[sq, rows, :] for sq in seqs]
            k_t = [jnp.where(real, k[j] * jnp.exp(-b[j]), 0.0) for j in G]
            k_d = [jnp.where(real, k[j] * jnp.exp(b[j][cl - 1:cl, :] - b[j]), 0.0) for j in G]
            v = [v_ref[sq, rows, :] for sq in seqs]
            att = [[jnp.where(tril, _mm_nt(q_t[j][:, _ks(h)], k_t[j][:, _ks(h)]), 0.0) for h in H] for j in G]
            s = [[s_scr[seqs[j], h] for h in H] for j in G]
            kv = [[_mm_tn(k_d[j][:, _ks(h)], v[j][:, _vs(h)]) for h in H] for j in G]
            o = [[_mm(att[j][h], v[j][:, _vs(h)]) + _mm(q_t[j][:, _ks(h)], s[j][h]) for h in H] for j in G]
            for j in G:
                for h in H:
                    s_scr[seqs[j], h] = jnp.exp(b_last_col[j][_ks(h), :]) * s[j][h] + kv[j][h]
            for j in G:
                gr = gr_ref[seqs[j], rows, :]
                for h in H:
                    o_ref[seqs[j], rows, _vs(h)] = _rms(o[j][h], ng) * _silu(gr[:, _vs(h)])
            return carry

        lax.fori_loop(0, (nb // gb) * nc, per_chunk, 0)
        sn_ref[...] = s_scr[...]

    return kern


def _gla(p, s0, lw, l, nb, gb, tp, c, t_real, name):
    nseq, tseq, _ = p.shape
    qk = HEADS * DK
    vw = HEADS * DV
    wspec, pcol, state = _mixer_specs(nb, tp, l)
    return pl.pallas_call(
        _make_gla_kernel(nb, gb, tp, c, t_real),
        grid=(nseq // nb, tseq // tp),
        in_specs=[
            pcol(qk, COL_GQ), pcol(qk, COL_GK), pcol(vw, COL_GV), pcol(vw, COL_GR), pcol(128, COL_SM),
            state(HEADS, DK, DV), wspec((128, qk)), wspec((1, qk)), wspec((1, DV)),
        ],
        out_specs=[pcol(vw, 0), state(HEADS, DK, DV)],
        out_shape=[
            jax.ShapeDtypeStruct((nseq, tseq, vw), F32),
            jax.ShapeDtypeStruct((nseq, HEADS, DK, DV), F32),
        ],
        scratch_shapes=[pltpu.VMEM((nb, HEADS, DK, DV), F32)],
        compiler_params=_cparams(("parallel", "arbitrary")),
        name=name,
    )(p, p, p, p, p, s0, lw["gla_wg"], lw["gla_bg"], lw["gla_ng"])


def _make_ret_kernel(nb, gb, tp, c, t_real):
    nc = tp // c
    cl = min(c, t_real)
    G, H = range(gb), range(HEADS)
    log_gamma = [float(np.log(np.float32(1.0) - np.float32(2.0) ** np.float32(-5.0 - h))) for h in range(HEADS)]

    def kern(q_ref, k_ref, qs_ref, ks_ref, v_ref, g_ref, cos_ref, sin_ref, s0_ref, ng_ref, o_ref, sn_ref, s_scr):
        @pl.when(pl.program_id(1) == 0)
        def _():
            s_scr[...] = s0_ref[...]

        row, col, tril, _, real = _chunk_consts(c, cl)
        diff = (row - col).astype(F32)
        ridx = lax.broadcasted_iota(jnp.int32, (c, 1), 0).astype(F32)
        ng = ng_ref[...]
        dmat = [jnp.where(tril, jnp.exp(jnp.maximum(diff, 0.0) * lgm), 0.0) for lgm in log_gamma]
        q_decay = [jnp.exp((ridx + 1.0) * lgm) for lgm in log_gamma]
        k_decay = [jnp.exp((cl - 1.0 - ridx) * lgm) for lgm in log_gamma]

        def per_chunk(idx, carry):
            g = idx // nc
            ci = idx - g * nc
            rows = pl.ds(pl.multiple_of(ci * c, SUBLANES), c)
            seqs = [g * gb + j for j in G]
            cos = cos_ref[rows, :]
            sin = sin_ref[rows, :]
            q = [q_ref[sq, rows, :] * cos + qs_ref[sq, rows, :] * sin for sq in seqs]
            k = [jnp.where(real, (k_ref[sq, rows, :] * cos + ks_ref[sq, rows, :] * sin) * (DK ** -0.5), 0.0)
                 for sq in seqs]
            v = [v_ref[sq, rows, :] for sq in seqs]
            att = [[_mm_nt(q[j][:, _ks(h)], k[j][:, _ks(h)]) * dmat[h] for h in H] for j in G]
            s = [[s_scr[seqs[j], h] for h in H] for j in G]
            kv = [[_mm_tn(k[j][:, _ks(h)] * k_decay[h], v[j][:, _vs(h)]) for h in H] for j in G]
            o = [[_mm(att[j][h], v[j][:, _vs(h)]) + _mm(q[j][:, _ks(h)] * q_decay[h], s[j][h]) for h in H]
                 for j in G]
            for j in G:
                for h in H:
                    s_scr[seqs[j], h] = math.exp(cl * log_gamma[h]) * s[j][h] + kv[j][h]
            for j in G:
                gate = g_ref[seqs[j], rows, :]
                for h in H:
                    oc = o[j][h] - jnp.mean(o[j][h], axis=-1, keepdims=True)
                    on = oc * lax.rsqrt(jnp.mean(oc * oc, axis=-1, keepdims=True) + EPS) * ng
                    o_ref[seqs[j], rows, _vs(h)] = on * _silu(gate[:, _vs(h)])
            return carry

        lax.fori_loop(0, (nb // gb) * nc, per_chunk, 0)
        sn_ref[...] = s_scr[...]

    return kern


def _ret(p, s0, cos_t, sin_t, lw, l, nb, gb, tp, c, t_real, name):
    nseq, tseq, _ = p.shape
    qk = HEADS * DK
    vw = HEADS * DV
    wspec, pcol, state = _mixer_specs(nb, tp, l)
    return pl.pallas_call(
        _make_ret_kernel(nb, gb, tp, c, t_real),
        grid=(nseq // nb, tseq // tp),
        in_specs=[
            pcol(qk, COL_RQ), pcol(qk, COL_RK), pcol(qk, COL_RQS), pcol(qk, COL_RKS), pcol(vw, COL_RV),
            pcol(vw, COL_RG),
            pl.BlockSpec((tp, qk), lambda b, t: (t, 0)),
            pl.BlockSpec((tp, qk), lambda b, t: (t, 0)),
            state(HEADS, DK, DV), wspec((1, DV)),
        ],
        out_specs=[pcol(vw, 0), state(HEADS, DK, DV)],
        out_shape=[
            jax.ShapeDtypeStruct((nseq, tseq, vw), F32),
            jax.ShapeDtypeStruct((nseq, HEADS, DK, DV), F32),
        ],
        scratch_shapes=[pltpu.VMEM((nb, HEADS, DK, DV), F32)],
        compiler_params=_cparams(("parallel", "arbitrary")),
        name=name,
    )(p, p, p, p, p, p, cos_t, sin_t, s0, lw["ret_ng"])


def _unit_lower_inverse_many(ms, c):
    row = lax.broadcasted_iota(jnp.int32, (c, c), 0)
    col = lax.broadcasted_iota(jnp.int32, (c, c), 1)
    eye = jnp.where(row == col, 1.0, 0.0)

    def same_block(bits):
        return lax.shift_right_logical(row, bits) == lax.shift_right_logical(col, bits)

    in8 = same_block(3)
    n1 = [jnp.where(in8, m, 0.0) for m in ms]
    n2 = [_mm_hp(x, x) for x in n1]
    n4 = [_mm_hp(x, x) for x in n2]
    d = [eye - x for x in n1]
    d = [x + _mm_hp(x, y) for x, y in zip(d, n2)]
    d = [x + _mm_hp(x, y) for x, y in zip(d, n4)]
    bits = 3
    while (1 << bits) < c:
        sel = jnp.logical_and(same_block(bits + 1), jnp.logical_not(same_block(bits)))
        ld = [_mm_hp(jnp.where(sel, m, 0.0), x) for m, x in zip(ms, d)]
        d = [x - _mm_hp(x, y) for x, y in zip(d, ld)]
        bits += 1
    return d


def _make_dn_kernel(nb, gb, tp, c, t_real):
    nc = tp // c
    cl = min(c, t_real)
    cw3 = 3 * BRANCH
    G, H = range(gb), range(HEADS)
    GH = [(j, h) for j in G for h in H]

    def kern(x_ref, z_ref, sm_ref, c0_ref, s0_ref, cw_ref, alog_ref, dtb_ref, ng_ref,
             o_ref, cn_ref, sn_ref, xs_scr, qkv_scr, cc_scr, s_scr):
        @pl.when(pl.program_id(1) == 0)
        def _():
            cc_scr[...] = c0_ref[...]
            s_scr[...] = s0_ref[...]

        row, col, tril, lower_ones, real = _chunk_consts(c, cl)
        strict = row > col
        lane = lax.broadcasted_iota(jnp.int32, (c, 128), 1)
        sel = [jnp.where(lane == SM_DA + h, 1.0, 0.0).astype(BF) for h in H]
        d_nt = lambda a, x: lax.dot_general(a, x, (((1,), (1,)), ((), ())), preferred_element_type=F32)
        ng = ng_ref[...]
        cw = cw_ref[...]
        neg_a = -jnp.exp(alog_ref[...])
        dtb = dtb_ref[...]

        def per_group(g, carry):
            seqs = [g * gb + j for j in G]
            for j, sq in enumerate(seqs):
                xs_scr[j, 0:SUBLANES, :] = cc_scr[sq]
                xs_scr[j, SUBLANES:SUBLANES + tp, :] = x_ref[sq]
            for j, sq in enumerate(seqs):
                qkv_scr[j] = _silu(_conv4(xs_scr, j, cw, tp))
                cc_scr[sq] = xs_scr[j, pl.ds(t_real, SUBLANES), :]

            def per_chunk(ci, carry2):
                rows = pl.ds(pl.multiple_of(ci * c, SUBLANES), c)
                sm = [sm_ref[sq, rows, :] for sq in seqs]
                gcum = [_cumsum_rows(neg_a * _softplus(x + dtb), lower_ones) for x in sm]
                g3 = [_split3(x) for x in gcum]
                beta_all = [jnp.where(real, jax.nn.sigmoid(x), 0.0) for x in sm]

                def head_in(j, h, part):
                    x = qkv_scr[j, rows, part * BRANCH + h * DV:part * BRANCH + (h + 1) * DV]
                    return x

                qh = {jh: head_in(*jh, 0) for jh in GH}
                kh = {jh: head_in(*jh, 1) for jh in GH}
                vh = {jh: head_in(*jh, 2) for jh in GH}
                qh = {jh: x * lax.rsqrt(jnp.sum(x * x, axis=-1, keepdims=True) + EPS) * (DV ** -0.5)
                      for jh, x in qh.items()}
                kh = {jh: jnp.where(real, x * lax.rsqrt(jnp.sum(x * x, axis=-1, keepdims=True) + EPS), 0.0)
                      for jh, x in kh.items()}
                beta = {(j, h): beta_all[j][:, SM_DB + h:SM_DB + h + 1] for j, h in GH}
                gcol = {(j, h): gcum[j][:, SM_DA + h:SM_DA + h + 1] for j, h in GH}
                grow = {(j, h): d_nt(sel[h], g3[j][0]) + d_nt(sel[h], g3[j][1]) + d_nt(sel[h], g3[j][2])
                        for j, h in GH}
                decay = {jh: jnp.where(tril, jnp.exp(jnp.where(tril, gcol[jh] - grow[jh], 0.0)), 0.0) for jh in GH}
                e_g = {jh: jnp.exp(gcol[jh]) for jh in GH}
                kb = {jh: kh[jh] * beta[jh] for jh in GH}
                kk = {jh: _mm_nt(kb[jh], kh[jh]) for jh in GH}
                qk = {jh: _mm_nt(qh[jh], kh[jh]) for jh in GH}
                tm = _unit_lower_inverse_many([jnp.where(strict, kk[jh] * decay[jh], 0.0) for jh in GH], c)
                tm = dict(zip(GH, tm))
                u = {jh: _mm(tm[jh], vh[jh] * beta[jh]) for jh in GH}
                w = {jh: _mm(tm[jh], kb[jh] * e_g[jh]) for jh in GH}
                s = {(j, h): s_scr[seqs[j], h] for j, h in GH}
                ws = {jh: _mm(w[jh], s[jh]) for jh in GH}
                qs = {jh: _mm(qh[jh] * e_g[jh], s[jh]) for jh in GH}
                v_new = {jh: u[jh] - ws[jh] for jh in GH}
                o = {jh: qs[jh] + _mm(qk[jh] * decay[jh], v_new[jh]) for jh in GH}
                for j, h in GH:
                    g_last = gcol[(j, h)][cl - 1:cl, :]
                    k_d = kh[(j, h)] * jnp.exp(g_last - gcol[(j, h)])
                    s_scr[seqs[j], h] = jnp.exp(g_last) * s[(j, h)] + _mm_tn(k_d, v_new[(j, h)])
                for j in G:
                    z = z_ref[seqs[j], rows, :]
                    for h in H:
                        o_ref[seqs[j], rows, _vs(h)] = _rms(o[(j, h)], ng) * _silu(z[:, _vs(h)])
                return carry2

            lax.fori_loop(0, nc, per_chunk, 0)
            return carry

        lax.fori_loop(0, nb // gb, per_group, 0)
        cn_ref[...] = cc_scr[...]
        sn_ref[...] = s_scr[...]

    return kern


def _dn(p, c0, s0, lw, l, nb, gb, tp, c, t_real, name):
    nseq, tseq, _ = p.shape
    cw3 = 3 * BRANCH
    wspec, pcol, state = _mixer_specs(nb, tp, l)
    return pl.pallas_call(
        _make_dn_kernel(nb, gb, tp, c, t_real),
        grid=(nseq // nb, tseq // tp),
        in_specs=[
            pcol(cw3, COL_DQKV), pcol(BRANCH, COL_DZ), pcol(128, COL_SM),
            state(SUBLANES, cw3), state(HEADS, DV, DV),
            wspec((CONV_W, cw3)), wspec((1, 128)), wspec((1, 128)), wspec((1, DV)),
        ],
        out_specs=[pcol(BRANCH, 0), state(SUBLANES, cw3), state(HEADS, DV, DV)],
        out_shape=[
            jax.ShapeDtypeStruct((nseq, tseq, BRANCH), F32),
            jax.ShapeDtypeStruct((nseq, SUBLANES, cw3), F32),
            jax.ShapeDtypeStruct((nseq, HEADS, DV, DV), F32),
        ],
        scratch_shapes=[
            pltpu.VMEM((gb, tp + SUBLANES, cw3), F32), pltpu.VMEM((gb, tp, cw3), F32),
            pltpu.VMEM((nb, SUBLANES, cw3), F32), pltpu.VMEM((nb, HEADS, DV, DV), F32),
        ],
        compiler_params=_cparams(("parallel", "arbitrary")),
        name=name,
    )(p, p, p, c0, s0, lw["dn_conv_w"], lw["dn_alog"], lw["dn_dtb"], lw["dn_ng"])


def _pack_w_in(w_in):
    sizes = (BRANCH, BRANCH, HEADS * DK, HEADS * DK, BRANCH, BRANCH, GLA_RANK,
             HEADS * DK, HEADS * DK, BRANCH, BRANCH, BRANCH, BRANCH, BRANCH, BRANCH, HEADS, HEADS)
    offs = np.concatenate([[0], np.cumsum(sizes)])
    seg = lambda i: w_in[:, :, int(offs[i]):int(offs[i + 1])]
    (lx, ly, gq, gk, gv, gr, glr, rq, rk, rv, rg, dq, dk, dv, dz, db, da) = [seg(i) for i in range(len(sizes))]

    def swap_halves(w):
        parts = []
        for h in range(HEADS):
            parts += [w[:, :, h * DK + DK // 2:(h + 1) * DK], w[:, :, h * DK:h * DK + DK // 2]]
        return jnp.concatenate(parts, axis=2)

    lead = w_in.shape[:2]
    small_pad = jnp.zeros(lead + (128 - GLA_RANK - 2 * HEADS,), w_in.dtype)
    used = COL_SM + 128
    tail = jnp.zeros(lead + (N_PACK - used,), w_in.dtype)
    packed = jnp.concatenate([lx, ly, gq, gk, gv, gr, rq, rk, swap_halves(rq), swap_halves(rk), rv, rg,
                              dq, dk, dv, dz, glr, db, da, small_pad, tail], axis=2)
    return packed.astype(BF)


def _rope_tables(pos):
    half = DK // 2
    inv = ROPE_BASE ** (-jnp.arange(half, dtype=F32) / half)
    ang = pos.astype(F32)[:, None] * inv[None, :]
    cos, sin = jnp.cos(ang), jnp.sin(ang)
    cos_t = jnp.tile(jnp.concatenate([cos, cos], axis=1), (1, HEADS))
    sin_t = jnp.tile(jnp.concatenate([-sin, sin], axis=1), (1, HEADS))
    return cos_t, sin_t


def _lane_row(vals, off):
    depth = vals.shape[0]
    return jnp.zeros((depth, 1, 128), F32).at[:, 0, off:off + HEADS].set(vals.astype(F32))


def kernel(x_prompt, x_sample, state_lru_conv, state_lru_h, state_gla, state_ret, state_dn_conv, state_dn,
           norm_g, final_norm_g, w_ff_in, w_ff_out, w_in, w_gate, w_branch, w_out,
           lru_conv_w, lru_conv_b, lru_wa, lru_ba, lru_wx, lru_bx, lru_lambda,
           gla_wg, gla_bg, gla_norm_g, ret_norm_g, dn_conv_w, dn_a_log, dn_dt_bias, dn_norm_g):
    bp, tpr, d = x_prompt.shape
    bs, ts, _ = x_sample.shape
    depth = w_in.shape[0]
    n_p, n_s = bp * tpr, bs * ts
    assert d == D_MODEL and ts <= SAMPLE_PAD and tpr % CHUNK == 0

    w_ff_in_b = w_ff_in.astype(BF)
    w_ff_out_b = w_ff_out.astype(BF)
    w_pack = _pack_w_in(w_in)
    w_gate_b = w_gate.astype(BF)
    w_branch_b = w_branch.astype(BF)
    w_out_b = w_out.astype(BF)
    wg_pad = jnp.zeros((depth, 128, HEADS * DK), F32).at[:, SM_GLR:SM_GLR + GLA_RANK, :].set(gla_wg).astype(BF)
    r3 = lambda a: a.reshape(a.shape[0], 1, a.shape[1])
    lw = {
        "conv_w": lru_conv_w, "conv_b": r3(lru_conv_b), "wa": lru_wa.astype(BF), "ba": r3(lru_ba),
        "wx": lru_wx.astype(BF), "bx": r3(lru_bx), "lam": r3(lru_lambda),
        "gla_wg": wg_pad, "gla_bg": r3(gla_bg), "gla_ng": r3(gla_norm_g), "ret_ng": r3(ret_norm_g),
        "dn_conv_w": dn_conv_w, "dn_alog": _lane_row(dn_a_log, SM_DA), "dn_dtb": _lane_row(dn_dt_bias, SM_DA),
        "dn_ng": r3(dn_norm_g),
    }
    norm_g4 = norm_g.reshape(depth, 3, 1, d)

    cos_p, sin_p = _rope_tables(jnp.arange(tpr))
    pos_s = jnp.where(jnp.arange(SAMPLE_PAD) < ts, PAST_LEN + jnp.arange(SAMPLE_PAD), 0)
    cos_s, sin_s = _rope_tables(pos_s)

    pad_conv = lambda a: jnp.pad(a, ((0, 0), (SUBLANES - (CONV_W - 1), 0), (0, 0)))
    zeros = lambda *s: jnp.zeros(s, F32)

    tp_p = _pick_tile(tpr, 256, CHUNK)
    a_p = dict(nb=bp, gb=bp, tp=tp_p)
    nb_s = _pick_tile(bs, 16, 1)
    a_s = dict(nb=nb_s, gb=_pick_tile(nb_s, 4, 1), tp=SAMPLE_PAD)

    xp = x_prompt.reshape(n_p, d)
    xs = x_sample.reshape(n_s, d)
    new_p = [[] for _ in range(6)]
    new_s = [[] for _ in range(6)]
    for l in range(depth):
        xp = _ffn(xp, norm_g4[l, 0], w_ff_in_b, w_ff_out_b, l, 0)
        xs = _ffn(xs, norm_g4[l, 0], w_ff_in_b, w_ff_out_b, l, 0)
        p_p = _inproj(xp, norm_g4[l, 1], w_pack, l).reshape(bp, tpr, N_PACK)
        p_s = _inproj(xs, norm_g4[l, 1], w_pack, l).reshape(bs, ts, N_PACK)
        p_s = jnp.pad(p_s, ((0, 0), (0, SAMPLE_PAD - ts), (0, 0)))

        o_lru_p, c_lru_p, h_lru_p = _lru(p_p, zeros(bp, SUBLANES, BRANCH), zeros(bp, 1, BRANCH), lw, l,
                                         t_real=tp_p, is_prompt=True, **a_p)
        o_gla_p, s_gla_p = _gla(p_p, zeros(bp, HEADS, DK, DV), lw, l, c=CHUNK, t_real=tp_p, name="gla_prompt", **a_p)
        o_ret_p, s_ret_p = _ret(p_p, zeros(bp, HEADS, DK, DV), cos_p, sin_p, lw, l, c=CHUNK, t_real=tp_p,
                                name="ret_prompt", **a_p)
        o_dn_p, c_dn_p, s_dn_p = _dn(p_p, zeros(bp, SUBLANES, 3 * BRANCH), zeros(bp, HEADS, DV, DV), lw, l,
                                     c=CHUNK, t_real=tp_p, name="dn_prompt", **a_p)

        o_lru_s, c_lru_s, h_lru_s = _lru(p_s, pad_conv(state_lru_conv[l]), state_lru_h[l][:, None, :], lw, l,
                                         t_real=ts, is_prompt=False, **a_s)
        o_gla_s, s_gla_s = _gla(p_s, state_gla[l], lw, l, c=SAMPLE_PAD, t_real=ts, name="gla_sample", **a_s)
        o_ret_s, s_ret_s = _ret(p_s, state_ret[l], cos_s, sin_s, lw, l, c=SAMPLE_PAD, t_real=ts,
                                name="ret_sample", **a_s)
        o_dn_s, c_dn_s, s_dn_s = _dn(p_s, pad_conv(state_dn_conv[l]), state_dn[l], lw, l, c=SAMPLE_PAD, t_real=ts,
                                     name="dn_sample", **a_s)

        flat_p = lambda o: o.reshape(n_p, BRANCH)
        flat_s = lambda o: o[:, :ts].reshape(n_s, BRANCH)
        xp = _merge(xp, norm_g4[l, 1], [flat_p(o) for o in (o_lru_p, o_gla_p, o_ret_p, o_dn_p)],
                    w_gate_b, w_branch_b, w_out_b, l)
        xs = _merge(xs, norm_g4[l, 1], [flat_s(o) for o in (o_lru_s, o_gla_s, o_ret_s, o_dn_s)],
                    w_gate_b, w_branch_b, w_out_b, l)
        xp = _ffn(xp, norm_g4[l, 2], w_ff_in_b, w_ff_out_b, l, 1)
        xs = _ffn(xs, norm_g4[l, 2], w_ff_in_b, w_ff_out_b, l, 1)

        tail3 = lambda cwin: cwin[:, SUBLANES - (CONV_W - 1):, :]
        for lst, vals in ((new_p, (tail3(c_lru_p), h_lru_p[:, 0], s_gla_p, s_ret_p, tail3(c_dn_p), s_dn_p)),
                          (new_s, (tail3(c_lru_s), h_lru_s[:, 0], s_gla_s, s_ret_s, tail3(c_dn_s), s_dn_s))):
            for i, v in enumerate(vals):
                lst[i].append(v)

    fg = final_norm_g.reshape(1, d)
    y_prompt = _final_norm(xp, fg).reshape(bp, tpr, d)
    y_sample = _final_norm(xs, fg).reshape(bs, ts, d)
    stack = lambda lst: [jnp.stack(v) for v in lst]
    return tuple([y_prompt, y_sample] + stack(new_p) + stack(new_s))
```

```python
import functools
import math

import numpy as np
import jax
import jax.numpy as jnp
from jax import lax
from jax.experimental import pallas as pl
from jax.experimental.pallas import tpu as pltpu

F32 = jnp.float32
BF = jnp.bfloat16
EPS = 1e-6

D_MODEL = 2048
BRANCH = 512
CONV_W = 4
HEADS = 4
DK = 64
DV = 128
GLA_RANK = 16
GLA_TAU = 16.0
LRU_C = 8.0
LRU_BD = 128
ROPE_BASE = 10000.0
CHUNK = 64
PAST_LEN = 16384

V7X_VMEM_BYTES = 64 * 1024 * 1024
VMEM_LIMIT = V7X_VMEM_BYTES - 8 * 1024 * 1024
SUBLANES = 8
SAMPLE_PAD = 8

COL_LX, COL_LY = 0, 512
COL_GQ, COL_GK, COL_GV, COL_GR = 1024, 1280, 1536, 2048
COL_RQ, COL_RK, COL_RQS, COL_RKS, COL_RV, COL_RG = 2560, 2816, 3072, 3328, 3584, 4096
COL_DQKV, COL_DZ = 4608, 6144
COL_SM = 6656
SM_GLR, SM_DB, SM_DA = 0, 16, 20
N_PACK = 7168


def _dot(a, b):
    return jnp.dot(a, b, preferred_element_type=F32)


def _mm(a, b):
    return _dot(a.astype(BF), b.astype(BF))


def _mm_nt(a, b):
    return lax.dot_general(a.astype(BF), b.astype(BF), (((1,), (1,)), ((), ())), preferred_element_type=F32)


def _mm_tn(a, b):
    return lax.dot_general(a.astype(BF), b.astype(BF), (((0,), (0,)), ((), ())), preferred_element_type=F32)


def _split3(x):
    hi = x.astype(BF)
    r = x - hi.astype(F32)
    mid = r.astype(BF)
    lo = (r - mid.astype(F32)).astype(BF)
    return hi, mid, lo


def _split2(x):
    hi = x.astype(BF)
    return hi, (x - hi.astype(F32)).astype(BF)


def _mm_hp(a, b):
    ah, al = _split2(a)
    bh, bl = _split2(b)
    return _dot(ah, bh) + _dot(ah, bl) + _dot(al, bh)


def _cumsum_rows(x, lower_ones):
    hi, mid, lo = _split3(x)
    return _dot(lower_ones, hi) + _dot(lower_ones, mid) + _dot(lower_ones, lo)


def _softplus(x):
    return jnp.maximum(x, 0.0) + jnp.log1p(jnp.exp(-jnp.abs(x)))


def _silu(x):
    return x * jax.nn.sigmoid(x)


def _gelu_tanh(x):
    return x * (0.5 * (1.0 + jnp.tanh(math.sqrt(2.0 / math.pi) * (x + 0.044715 * (x * x * x)))))


def _rms(x, g):
    return x * lax.rsqrt(jnp.mean(x * x, axis=-1, keepdims=True) + EPS) * g


def _pick_tile(n, target, align):
    best = None
    for t in range(align, min(n, target) + 1, align):
        if n % t == 0:
            best = t
    assert best is not None, (n, target, align)
    return best


def _cparams(sem):
    return pltpu.CompilerParams(dimension_semantics=sem, vmem_limit_bytes=VMEM_LIMIT)


def _ffn_kernel(x_ref, g_ref, wg_ref, wu_ref, wo_ref, o_ref, hn_ref):
    @pl.when(pl.program_id(1) == 0)
    def _():
        x = x_ref[...]
        hn_ref[...] = _rms(x, g_ref[...]).astype(BF)
        o_ref[...] = x

    hn = hn_ref[...]
    gate = _dot(hn, wg_ref[...])
    up = _dot(hn, wu_ref[...])
    act = (_silu(gate) * up * 0.5).astype(BF)
    o_ref[...] += _dot(act, wo_ref[...])


def _ffn(x, g, w_in, w_out, l, s):
    n, d = x.shape
    f = w_out.shape[2]
    tm = _pick_tile(n, 512, 16)
    tf = _pick_tile(f, 512, 128)
    nf = f // tf
    return pl.pallas_call(
        _ffn_kernel,
        grid=(n // tm, nf),
        in_specs=[
            pl.BlockSpec((tm, d), lambda i, j: (i, 0)),
            pl.BlockSpec((1, d), lambda i, j: (0, 0)),
            pl.BlockSpec((None, None, d, tf), lambda i, j: (l, s, 0, j)),
            pl.BlockSpec((None, None, d, tf), lambda i, j: (l, s, 0, j + nf)),
            pl.BlockSpec((None, None, tf, d), lambda i, j: (l, s, j, 0)),
        ],
        out_specs=pl.BlockSpec((tm, d), lambda i, j: (i, 0)),
        out_shape=jax.ShapeDtypeStruct((n, d), F32),
        scratch_shapes=[pltpu.VMEM((tm, d), BF)],
        compiler_params=_cparams(("parallel", "arbitrary")),
        name="ffn",
    )(x, g, w_in, w_in, w_out)


def _inproj_kernel(x_ref, g_ref, w_ref, o_ref, hn_ref):
    @pl.when(pl.program_id(1) == 0)
    def _():
        hn_ref[...] = _rms(x_ref[...], g_ref[...]).astype(BF)

    o_ref[...] = _dot(hn_ref[...], w_ref[...])


def _inproj(x, g, w_pack, l):
    n, d = x.shape
    npk = w_pack.shape[2]
    tm = _pick_tile(n, 1024, 16)
    tn = _pick_tile(npk, 1024, 128)
    return pl.pallas_call(
        _inproj_kernel,
        grid=(n // tm, npk // tn),
        in_specs=[
            pl.BlockSpec((tm, d), lambda i, j: (i, 0)),
            pl.BlockSpec((1, d), lambda i, j: (0, 0)),
            pl.BlockSpec((None, d, tn), lambda i, j: (l, 0, j)),
        ],
        out_specs=pl.BlockSpec((tm, tn), lambda i, j: (i, j)),
        out_shape=jax.ShapeDtypeStruct((n, npk), F32),
        scratch_shapes=[pltpu.VMEM((tm, d), BF)],
        compiler_params=_cparams(("parallel", "arbitrary")),
        name="inproj",
    )(x, g, w_pack)


def _merge_kernel(x_ref, g_ref, b0_ref, b1_ref, b2_ref, b3_ref, wgate_ref, wbr_ref, wo_ref, o_ref, hn_ref):
    @pl.when(pl.program_id(1) == 0)
    def _():
        x = x_ref[...]
        hn_ref[...] = _rms(x, g_ref[...]).astype(BF)
        o_ref[...] = x

    hn = hn_ref[...]
    m = None
    for n, b_ref in enumerate((b0_ref, b1_ref, b2_ref, b3_ref)):
        gate = jax.nn.sigmoid(_dot(hn, wgate_ref[n]))
        br = _dot(b_ref[...].astype(BF), wbr_ref[n])
        m = gate * br if m is None else m + gate * br
    o_ref[...] += _dot(m.astype(BF), wo_ref[...])


def _merge(x, g, branches, w_gate, w_branch, w_out, l):
    n, d = x.shape
    tm = _pick_tile(n, 512, 16)
    tn = 256
    return pl.pallas_call(
        _merge_kernel,
        grid=(n // tm, d // tn),
        in_specs=[
            pl.BlockSpec((tm, d), lambda i, j: (i, 0)),
            pl.BlockSpec((1, d), lambda i, j: (0, 0)),
        ] + [pl.BlockSpec((tm, BRANCH), lambda i, j: (i, 0))] * HEADS + [
            pl.BlockSpec((None, HEADS, d, tn), lambda i, j: (l, 0, 0, j)),
            pl.BlockSpec((None, HEADS, BRANCH, tn), lambda i, j: (l, 0, 0, j)),
            pl.BlockSpec((None, tn, d), lambda i, j: (l, j, 0)),
        ],
        out_specs=pl.BlockSpec((tm, d), lambda i, j: (i, 0)),
        out_shape=jax.ShapeDtypeStruct((n, d), F32),
        scratch_shapes=[pltpu.VMEM((tm, d), BF)],
        compiler_params=_cparams(("parallel", "arbitrary")),
        name="merge",
    )(x, g, *branches, w_gate, w_branch, w_out)


def _final_norm_kernel(x_ref, g_ref, o_ref):
    o_ref[...] = _rms(x_ref[...], g_ref[...])


def _final_norm(x, g):
    n, d = x.shape
    tm = _pick_tile(n, 512, 8)
    return pl.pallas_call(
        _final_norm_kernel,
        grid=(n // tm,),
        in_specs=[pl.BlockSpec((tm, d), lambda i: (i, 0)), pl.BlockSpec((1, d), lambda i: (0, 0))],
        out_specs=pl.BlockSpec((tm, d), lambda i: (i, 0)),
        out_shape=jax.ShapeDtypeStruct((n, d), F32),
        compiler_params=_cparams(("parallel",)),
        name="final_norm",
    )(x, g)


def _conv4(xs_ref, j_seq, w, tp):
    y = None
    for j in range(CONV_W):
        term = xs_ref[j_seq, pl.ds(SUBLANES - CONV_W + 1 + j, tp), :] * w[j:j + 1, :]
        y = term if y is None else y + term
    return y


def _make_lru_kernel(nb, gb, tp, t_real, is_prompt):
    def kern(lx_ref, ly_ref, c0_ref, h0_ref, cw_ref, cb_ref, wa_ref, ba_ref, wx_ref, bx_ref, lam_ref,
             o_ref, cn_ref, hn_ref, xs_scr, a_scr, u_scr, cc_scr, ch_scr):
        tb = pl.program_id(1)

        @pl.when(tb == 0)
        def _():
            cc_scr[...] = c0_ref[...]
            ch_scr[...] = h0_ref[...]

        cw = cw_ref[...]
        neg_sp = -LRU_C * _softplus(-lam_ref[...])
        row = lax.broadcasted_iota(jnp.int32, (tp, 1), 0)

        def per_group(g, carry):
            seqs = [g * gb + j for j in range(gb)]
            for j, sq in enumerate(seqs):
                xs_scr[j, 0:SUBLANES, :] = cc_scr[sq]
                xs_scr[j, SUBLANES:SUBLANES + tp, :] = lx_ref[sq]
            for j, sq in enumerate(seqs):
                xc = _conv4(xs_scr, j, cw, tp) + cb_ref[...]
                cc_scr[sq] = xs_scr[j, pl.ds(t_real, SUBLANES), :]
                xcb = xc.astype(BF)
                r_parts, i_parts = [], []
                for n in range(BRANCH // LRU_BD):
                    blk = xcb[:, n * LRU_BD:(n + 1) * LRU_BD]
                    r_parts.append(_dot(blk, wa_ref[n]))
                    i_parts.append(_dot(blk, wx_ref[n]))
                r = jax.nn.sigmoid(jnp.concatenate(r_parts, axis=1) + ba_ref[...])
                ig = jax.nn.sigmoid(jnp.concatenate(i_parts, axis=1) + bx_ref[...])
                log_a = r * neg_sp
                a = jnp.exp(log_a)
                mult = jnp.sqrt(-jnp.tanh(log_a) * (a * a + 1.0))
                if is_prompt:
                    mult = jnp.where(jnp.logical_and(row == 0, tb == 0), 1.0, mult)
                a_scr[j] = a
                u_scr[j] = mult * ig * xc

            def step(t, hs):
                out = []
                for j in range(gb):
                    h = a_scr[j, pl.ds(t, 1), :] * hs[j] + u_scr[j, pl.ds(t, 1), :]
                    a_scr[j, pl.ds(t, 1), :] = h
                    out.append(h)
                return tuple(out)

            h_last = lax.fori_loop(0, t_real, step, tuple(ch_scr[sq] for sq in seqs),
                                   unroll=True if t_real <= SUBLANES else 8)
            for j, sq in enumerate(seqs):
                ch_scr[sq] = h_last[j]
                o_ref[sq] = a_scr[j] * _gelu_tanh(ly_ref[sq])
            return carry

        lax.fori_loop(0, nb // gb, per_group, 0)
        cn_ref[...] = cc_scr[...]
        hn_ref[...] = ch_scr[...]

    return kern


def _mixer_specs(nb, tp, l):
    wspec = lambda shape: pl.BlockSpec((None,) + shape, lambda b, t: (l,) + (0,) * len(shape))
    pcol = lambda w, off: pl.BlockSpec((nb, tp, w), lambda b, t: (b, t, off // w))
    state = lambda *shape: pl.BlockSpec((nb,) + shape, lambda b, t: (b,) + (0,) * len(shape))
    return wspec, pcol, state


class _StateIO:
    def __init__(self, s0, stacked, l, nb):
        self.s0, self.l, self.nb = s0, l, nb
        self.stacked = stacked is not None
        self.prev = stacked[1] if self.stacked else None
        self.depth = stacked[0] if self.stacked else None

    def spec(self):
        shape = self.s0.shape[2:] if self.stacked else self.s0.shape[1:]
        zeros = (0,) * len(shape)
        if self.stacked:
            l = self.l
            return pl.BlockSpec((None, self.nb) + shape, lambda b, t: (l, b) + zeros)
        return pl.BlockSpec((self.nb,) + shape, lambda b, t: (b,) + zeros)

    def out_shape(self):
        return jax.ShapeDtypeStruct(self.s0.shape, F32)

    def extra_inputs(self):
        return [self.prev] if self.prev is not None else []

    def extra_specs(self):
        return [pl.BlockSpec(memory_space=pl.ANY)] if self.prev is not None else []

    def aliases(self, n_in, out_idx):
        return {n_in: out_idx} if self.prev is not None else {}

    def wrap(self, kern, n_in):
        if self.prev is None:
            return kern
        return lambda *refs: kern(*refs[:n_in], *refs[n_in + 1:])


def _lru(p, c0, h0, lw, l, nb, gb, tp, t_real, is_prompt):
    nseq, tseq, _ = p.shape
    c = BRANCH
    wspec, pcol, state = _mixer_specs(nb, tp, l)
    return pl.pallas_call(
        _make_lru_kernel(nb, gb, tp, t_real, is_prompt),
        grid=(nseq // nb, tseq // tp),
        in_specs=[
            pcol(c, COL_LX), pcol(c, COL_LY), state(SUBLANES, c), state(1, c),
            wspec((CONV_W, c)), wspec((1, c)),
            wspec((c // LRU_BD, LRU_BD, LRU_BD)), wspec((1, c)),
            wspec((c // LRU_BD, LRU_BD, LRU_BD)), wspec((1, c)),
            wspec((1, c)),
        ],
        out_specs=[pcol(c, 0), state(SUBLANES, c), state(1, c)],
        out_shape=[
            jax.ShapeDtypeStruct((nseq, tseq, c), F32),
            jax.ShapeDtypeStruct((nseq, SUBLANES, c), F32),
            jax.ShapeDtypeStruct((nseq, 1, c), F32),
        ],
        scratch_shapes=[
            pltpu.VMEM((gb, tp + SUBLANES, c), F32), pltpu.VMEM((gb, tp, c), F32), pltpu.VMEM((gb, tp, c), F32),
            pltpu.VMEM((nb, SUBLANES, c), F32), pltpu.VMEM((nb, 1, c), F32),
        ],
        compiler_params=_cparams(("parallel", "arbitrary")),
        name="lru_prompt" if is_prompt else "lru_sample",
    )(p, p, c0, h0, lw["conv_w"], lw["conv_b"], lw["wa"], lw["ba"], lw["wx"], lw["bx"], lw["lam"])


def _chunk_consts(c, cl):
    row = lax.broadcasted_iota(jnp.int32, (c, c), 0)
    col = lax.broadcasted_iota(jnp.int32, (c, c), 1)
    tril = row >= col
    lower_ones = jnp.where(tril, 1.0, 0.0).astype(BF)
    real = lax.broadcasted_iota(jnp.int32, (c, 1), 0) < cl
    return row, col, tril, lower_ones, real


def _ks(h):
    return slice(h * DK, (h + 1) * DK)


def _vs(h):
    return slice(h * DV, (h + 1) * DV)


def _make_gla_kernel(nb, gb, tp, c, t_real):
    nc = tp // c
    cl = min(c, t_real)
    G, H = range(gb), range(HEADS)

    def kern(q_ref, k_ref, v_ref, gr_ref, sm_ref, s0_ref, wg_ref, bg_ref, ng_ref, o_ref, sn_ref, s_scr):
        @pl.when(pl.program_id(1) == 0)
        def _():
            s_scr[...] = s0_ref[...]

        _, _, tril, lower_ones, real = _chunk_consts(c, cl)
        ones_real = jnp.where(real, 1.0, 0.0).astype(BF) * jnp.ones((c, DV), BF)
        d_tn = lambda x: lax.dot_general(x, ones_real, (((0,), (0,)), ((), ())), preferred_element_type=F32)
        ng = ng_ref[...]
        wg = wg_ref[...]
        bg = bg_ref[...]

        def per_chunk(idx, carry):
            g = idx // nc
            ci = idx - g * nc
            rows = pl.ds(pl.multiple_of(ci * c, SUBLANES), c)
            seqs = [g * gb + j for j in G]
            lg = [-_softplus(-(_mm(sm_ref[sq, rows, :], wg) + bg)) * (1.0 / GLA_TAU) for sq in seqs]
            sp = [_split3(x) for x in lg]
            b = [_dot(lower_ones, s[0]) + _dot(lower_ones, s[1]) + _dot(lower_ones, s[2]) for s in sp]
            b_last_col = [d_tn(s[0]) + d_tn(s[1]) + d_tn(s[2]) for s in sp]
            q_t = [q_ref[sq, rows, :] * (DK ** -0.5) * jnp.exp(b[j]) for j, sq in enumerate(seqs)]
            k = [k_ref[sq, rows, :] for sq in seqs]
            k_t = [jnp.where(real, k[j] * jnp.exp(-b[j]), 0.0) for j in G]
            k_d = [jnp.where(real, k[j] * jnp.exp(b[j][cl - 1:cl, :] - b[j]), 0.0) for j in G]
            v = [v_ref[sq, rows, :] for sq in seqs]
            att = [[jnp.where(tril, _mm_nt(q_t[j][:, _ks(h)], k_t[j][:, _ks(h)]), 0.0) for h in H] for j in G]
            s = [[s_scr[seqs[j], h] for h in H] for j in G]
            kv = [[_mm_tn(k_d[j][:, _ks(h)], v[j][:, _vs(h)]) for h in H] for j in G]
            o = [[_mm(att[j][h], v[j][:, _vs(h)]) + _mm(q_t[j][:, _ks(h)], s[j][h]) for h in H] for j in G]
            for j in G:
                for h in H:
                    s_scr[seqs[j], h] = jnp.exp(b_last_col[j][_ks(h), :]) * s[j][h] + kv[j][h]
            for j in G:
                gr = gr_ref[seqs[j], rows, :]
                for h in H:
                    o_ref[seqs[j], rows, _vs(h)] = _rms(o[j][h], ng) * _silu(gr[:, _vs(h)])
            return carry

        lax.fori_loop(0, (nb // gb) * nc, per_chunk, 0)
        sn_ref[...] = s_scr[...]

    return kern


def _gla(p, s0, lw, l, nb, gb, tp, c, t_real, name, stacked=None):
    nseq, tseq, _ = p.shape
    qk = HEADS * DK
    vw = HEADS * DV
    wspec, pcol, _ = _mixer_specs(nb, tp, l)
    sio = _StateIO(s0, stacked, l, nb)
    in_specs = [
        pcol(qk, COL_GQ), pcol(qk, COL_GK), pcol(vw, COL_GV), pcol(vw, COL_GR), pcol(128, COL_SM),
        sio.spec(), wspec((128, qk)), wspec((1, qk)), wspec((1, DV)),
    ]
    n_in = len(in_specs)
    return pl.pallas_call(
        sio.wrap(_make_gla_kernel(nb, gb, tp, c, t_real), n_in),
        grid=(nseq // nb, tseq // tp),
        in_specs=in_specs + sio.extra_specs(),
        out_specs=[pcol(vw, 0), sio.spec()],
        out_shape=[jax.ShapeDtypeStruct((nseq, tseq, vw), F32), sio.out_shape()],
        scratch_shapes=[pltpu.VMEM((nb, HEADS, DK, DV), F32)],
        input_output_aliases=sio.aliases(n_in, 1),
        compiler_params=_cparams(("parallel", "arbitrary")),
        name=name,
    )(p, p, p, p, p, s0, lw["gla_wg"], lw["gla_bg"], lw["gla_ng"], *sio.extra_inputs())


def _make_ret_kernel(nb, gb, tp, c, t_real):
    nc = tp // c
    cl = min(c, t_real)
    G, H = range(gb), range(HEADS)
    log_gamma = [float(np.log(np.float32(1.0) - np.float32(2.0) ** np.float32(-5.0 - h))) for h in range(HEADS)]

    def kern(q_ref, k_ref, qs_ref, ks_ref, v_ref, g_ref, cos_ref, sin_ref, s0_ref, ng_ref, o_ref, sn_ref, s_scr):
        @pl.when(pl.program_id(1) == 0)
        def _():
            s_scr[...] = s0_ref[...]

        row, col, tril, _, real = _chunk_consts(c, cl)
        diff = (row - col).astype(F32)
        ridx = lax.broadcasted_iota(jnp.int32, (c, 1), 0).astype(F32)
        ng = ng_ref[...]
        dmat = [jnp.where(tril, jnp.exp(jnp.maximum(diff, 0.0) * lgm), 0.0) for lgm in log_gamma]
        q_decay = [jnp.exp((ridx + 1.0) * lgm) for lgm in log_gamma]
        k_decay = [jnp.exp((cl - 1.0 - ridx) * lgm) for lgm in log_gamma]

        def per_chunk(idx, carry):
            g = idx // nc
            ci = idx - g * nc
            rows = pl.ds(pl.multiple_of(ci * c, SUBLANES), c)
            seqs = [g * gb + j for j in G]
            cos = cos_ref[rows, :]
            sin = sin_ref[rows, :]
            q = [q_ref[sq, rows, :] * cos + qs_ref[sq, rows, :] * sin for sq in seqs]
            k = [jnp.where(real, (k_ref[sq, rows, :] * cos + ks_ref[sq, rows, :] * sin) * (DK ** -0.5), 0.0)
                 for sq in seqs]
            v = [v_ref[sq, rows, :] for sq in seqs]
            att = [[_mm_nt(q[j][:, _ks(h)], k[j][:, _ks(h)]) * dmat[h] for h in H] for j in G]
            s = [[s_scr[seqs[j], h] for h in H] for j in G]
            kv = [[_mm_tn(k[j][:, _ks(h)] * k_decay[h], v[j][:, _vs(h)]) for h in H] for j in G]
            o = [[_mm(att[j][h], v[j][:, _vs(h)]) + _mm(q[j][:, _ks(h)] * q_decay[h], s[j][h]) for h in H]
                 for j in G]
            for j in G:
                for h in H:
                    s_scr[seqs[j], h] = math.exp(cl * log_gamma[h]) * s[j][h] + kv[j][h]
            for j in G:
                gate = g_ref[seqs[j], rows, :]
                for h in H:
                    oc = o[j][h] - jnp.mean(o[j][h], axis=-1, keepdims=True)
                    on = oc * lax.rsqrt(jnp.mean(oc * oc, axis=-1, keepdims=True) + EPS) * ng
                    o_ref[seqs[j], rows, _vs(h)] = on * _silu(gate[:, _vs(h)])
            return carry

        lax.fori_loop(0, (nb // gb) * nc, per_chunk, 0)
        sn_ref[...] = s_scr[...]

    return kern


def _ret(p, s0, cos_t, sin_t, lw, l, nb, gb, tp, c, t_real, name, stacked=None):
    nseq, tseq, _ = p.shape
    qk = HEADS * DK
    vw = HEADS * DV
    wspec, pcol, _ = _mixer_specs(nb, tp, l)
    sio = _StateIO(s0, stacked, l, nb)
    in_specs = [
        pcol(qk, COL_RQ), pcol(qk, COL_RK), pcol(qk, COL_RQS), pcol(qk, COL_RKS), pcol(vw, COL_RV),
        pcol(vw, COL_RG),
        pl.BlockSpec((tp, qk), lambda b, t: (t, 0)),
        pl.BlockSpec((tp, qk), lambda b, t: (t, 0)),
        sio.spec(), wspec((1, DV)),
    ]
    n_in = len(in_specs)
    return pl.pallas_call(
        sio.wrap(_make_ret_kernel(nb, gb, tp, c, t_real), n_in),
        grid=(nseq // nb, tseq // tp),
        in_specs=in_specs + sio.extra_specs(),
        out_specs=[pcol(vw, 0), sio.spec()],
        out_shape=[jax.ShapeDtypeStruct((nseq, tseq, vw), F32), sio.out_shape()],
        scratch_shapes=[pltpu.VMEM((nb, HEADS, DK, DV), F32)],
        input_output_aliases=sio.aliases(n_in, 1),
        compiler_params=_cparams(("parallel", "arbitrary")),
        name=name,
    )(p, p, p, p, p, p, cos_t, sin_t, s0, lw["ret_ng"], *sio.extra_inputs())


def _unit_lower_inverse_many(ms, c):
    row = lax.broadcasted_iota(jnp.int32, (c, c), 0)
    col = lax.broadcasted_iota(jnp.int32, (c, c), 1)
    eye = jnp.where(row == col, 1.0, 0.0)

    def same_block(bits):
        return lax.shift_right_logical(row, bits) == lax.shift_right_logical(col, bits)

    in8 = same_block(3)
    n1 = [jnp.where(in8, m, 0.0) for m in ms]
    n2 = [_mm_hp(x, x) for x in n1]
    n4 = [_mm_hp(x, x) for x in n2]
    d = [eye - x for x in n1]
    d = [x + _mm_hp(x, y) for x, y in zip(d, n2)]
    d = [x + _mm_hp(x, y) for x, y in zip(d, n4)]
    bits = 3
    while (1 << bits) < c:
        sel = jnp.logical_and(same_block(bits + 1), jnp.logical_not(same_block(bits)))
        ld = [_mm_hp(jnp.where(sel, m, 0.0), x) for m, x in zip(ms, d)]
        d = [x - _mm_hp(x, y) for x, y in zip(d, ld)]
        bits += 1
    return d


def _make_dn_kernel(nb, gb, tp, c, t_real):
    nc = tp // c
    cl = min(c, t_real)
    cw3 = 3 * BRANCH
    G, H = range(gb), range(HEADS)
    GH = [(j, h) for j in G for h in H]

    def kern(x_ref, z_ref, sm_ref, c0_ref, s0_ref, cw_ref, alog_ref, dtb_ref, ng_ref,
             o_ref, cn_ref, sn_ref, xs_scr, qkv_scr, cc_scr, s_scr):
        @pl.when(pl.program_id(1) == 0)
        def _():
            cc_scr[...] = c0_ref[...]
            s_scr[...] = s0_ref[...]

        row, col, tril, lower_ones, real = _chunk_consts(c, cl)
        strict = row > col
        lane = lax.broadcasted_iota(jnp.int32, (c, 128), 1)
        sel = [jnp.where(lane == SM_DA + h, 1.0, 0.0).astype(BF) for h in H]
        d_nt = lambda a, x: lax.dot_general(a, x, (((1,), (1,)), ((), ())), preferred_element_type=F32)
        ng = ng_ref[...]
        cw = cw_ref[...]
        neg_a = -jnp.exp(alog_ref[...])
        dtb = dtb_ref[...]

        def per_group(g, carry):
            seqs = [g * gb + j for j in G]
            for j, sq in enumerate(seqs):
                xs_scr[j, 0:SUBLANES, :] = cc_scr[sq]
                xs_scr[j, SUBLANES:SUBLANES + tp, :] = x_ref[sq]
            for j, sq in enumerate(seqs):
                qkv_scr[j] = _silu(_conv4(xs_scr, j, cw, tp))
                cc_scr[sq] = xs_scr[j, pl.ds(t_real, SUBLANES), :]

            def per_chunk(ci, carry2):
                rows = pl.ds(pl.multiple_of(ci * c, SUBLANES), c)
                sm = [sm_ref[sq, rows, :] for sq in seqs]
                gcum = [_cumsum_rows(neg_a * _softplus(x + dtb), lower_ones) for x in sm]
                g3 = [_split3(x) for x in gcum]
                beta_all = [jnp.where(real, jax.nn.sigmoid(x), 0.0) for x in sm]

                def head_in(j, h, part):
                    x = qkv_scr[j, rows, part * BRANCH + h * DV:part * BRANCH + (h + 1) * DV]
                    return x

                qh = {jh: head_in(*jh, 0) for jh in GH}
                kh = {jh: head_in(*jh, 1) for jh in GH}
                vh = {jh: head_in(*jh, 2) for jh in GH}
                qh = {jh: x * lax.rsqrt(jnp.sum(x * x, axis=-1, keepdims=True) + EPS) * (DV ** -0.5)
                      for jh, x in qh.items()}
                kh = {jh: jnp.where(real, x * lax.rsqrt(jnp.sum(x * x, axis=-1, keepdims=True) + EPS), 0.0)
                      for jh, x in kh.items()}
                beta = {(j, h): beta_all[j][:, SM_DB + h:SM_DB + h + 1] for j, h in GH}
                gcol = {(j, h): gcum[j][:, SM_DA + h:SM_DA + h + 1] for j, h in GH}
                grow = {(j, h): d_nt(sel[h], g3[j][0]) + d_nt(sel[h], g3[j][1]) + d_nt(sel[h], g3[j][2])
                        for j, h in GH}
                decay = {jh: jnp.where(tril, jnp.exp(jnp.where(tril, gcol[jh] - grow[jh], 0.0)), 0.0) for jh in GH}
                e_g = {jh: jnp.exp(gcol[jh]) for jh in GH}
                kb = {jh: kh[jh] * beta[jh] for jh in GH}
                kk = {jh: _mm_nt(kb[jh], kh[jh]) for jh in GH}
                qk = {jh: _mm_nt(qh[jh], kh[jh]) for jh in GH}
                tm = _unit_lower_inverse_many([jnp.where(strict, kk[jh] * decay[jh], 0.0) for jh in GH], c)
                tm = dict(zip(GH, tm))
                u = {jh: _mm(tm[jh], vh[jh] * beta[jh]) for jh in GH}
                w = {jh: _mm(tm[jh], kb[jh] * e_g[jh]) for jh in GH}
                s = {(j, h): s_scr[seqs[j], h] for j, h in GH}
                ws = {jh: _mm(w[jh], s[jh]) for jh in GH}
                qs = {jh: _mm(qh[jh] * e_g[jh], s[jh]) for jh in GH}
                v_new = {jh: u[jh] - ws[jh] for jh in GH}
                o = {jh: qs[jh] + _mm(qk[jh] * decay[jh], v_new[jh]) for jh in GH}
                for j, h in GH:
                    g_last = gcol[(j, h)][cl - 1:cl, :]
                    k_d = kh[(j, h)] * jnp.exp(g_last - gcol[(j, h)])
                    s_scr[seqs[j], h] = jnp.exp(g_last) * s[(j, h)] + _mm_tn(k_d, v_new[(j, h)])
                for j in G:
                    z = z_ref[seqs[j], rows, :]
                    for h in H:
                        o_ref[seqs[j], rows, _vs(h)] = _rms(o[(j, h)], ng) * _silu(z[:, _vs(h)])
                return carry2

            lax.fori_loop(0, nc, per_chunk, 0)
            return carry

        lax.fori_loop(0, nb // gb, per_group, 0)
        cn_ref[...] = cc_scr[...]
        sn_ref[...] = s_scr[...]

    return kern


def _dn(p, c0, s0, lw, l, nb, gb, tp, c, t_real, name, stacked=None):
    nseq, tseq, _ = p.shape
    cw3 = 3 * BRANCH
    wspec, pcol, state = _mixer_specs(nb, tp, l)
    sio = _StateIO(s0, stacked, l, nb)
    in_specs = [
        pcol(cw3, COL_DQKV), pcol(BRANCH, COL_DZ), pcol(128, COL_SM),
        state(SUBLANES, cw3), sio.spec(),
        wspec((CONV_W, cw3)), wspec((1, 128)), wspec((1, 128)), wspec((1, DV)),
    ]
    n_in = len(in_specs)
    return pl.pallas_call(
        sio.wrap(_make_dn_kernel(nb, gb, tp, c, t_real), n_in),
        grid=(nseq // nb, tseq // tp),
        in_specs=in_specs + sio.extra_specs(),
        out_specs=[pcol(BRANCH, 0), state(SUBLANES, cw3), sio.spec()],
        out_shape=[
            jax.ShapeDtypeStruct((nseq, tseq, BRANCH), F32),
            jax.ShapeDtypeStruct((nseq, SUBLANES, cw3), F32),
            sio.out_shape(),
        ],
        scratch_shapes=[
            pltpu.VMEM((gb, tp + SUBLANES, cw3), F32), pltpu.VMEM((gb, tp, cw3), F32),
            pltpu.VMEM((nb, SUBLANES, cw3), F32), pltpu.VMEM((nb, HEADS, DV, DV), F32),
        ],
        input_output_aliases=sio.aliases(n_in, 2),
        compiler_params=_cparams(("parallel", "arbitrary")),
        name=name,
    )(p, p, p, c0, s0, lw["dn_conv_w"], lw["dn_alog"], lw["dn_dtb"], lw["dn_ng"], *sio.extra_inputs())


def _pack_w_in(w_in):
    sizes = (BRANCH, BRANCH, HEADS * DK, HEADS * DK, BRANCH, BRANCH, GLA_RANK,
             HEADS * DK, HEADS * DK, BRANCH, BRANCH, BRANCH, BRANCH, BRANCH, BRANCH, HEADS, HEADS)
    offs = np.concatenate([[0], np.cumsum(sizes)])
    seg = lambda i: w_in[:, :, int(offs[i]):int(offs[i + 1])]
    (lx, ly, gq, gk, gv, gr, glr, rq, rk, rv, rg, dq, dk, dv, dz, db, da) = [seg(i) for i in range(len(sizes))]

    def swap_halves(w):
        parts = []
        for h in range(HEADS):
            parts += [w[:, :, h * DK + DK // 2:(h + 1) * DK], w[:, :, h * DK:h * DK + DK // 2]]
        return jnp.concatenate(parts, axis=2)

    lead = w_in.shape[:2]
    small_pad = jnp.zeros(lead + (128 - GLA_RANK - 2 * HEADS,), w_in.dtype)
    used = COL_SM + 128
    tail = jnp.zeros(lead + (N_PACK - used,), w_in.dtype)
    packed = jnp.concatenate([lx, ly, gq, gk, gv, gr, rq, rk, swap_halves(rq), swap_halves(rk), rv, rg,
                              dq, dk, dv, dz, glr, db, da, small_pad, tail], axis=2)
    return packed.astype(BF)


def _rope_tables(pos):
    half = DK // 2
    inv = ROPE_BASE ** (-jnp.arange(half, dtype=F32) / half)
    ang = pos.astype(F32)[:, None] * inv[None, :]
    cos, sin = jnp.cos(ang), jnp.sin(ang)
    cos_t = jnp.tile(jnp.concatenate([cos, cos], axis=1), (1, HEADS))
    sin_t = jnp.tile(jnp.concatenate([-sin, sin], axis=1), (1, HEADS))
    return cos_t, sin_t


def _lane_row(vals, off):
    depth = vals.shape[0]
    return jnp.zeros((depth, 1, 128), F32).at[:, 0, off:off + HEADS].set(vals.astype(F32))


def kernel(x_prompt, x_sample, state_lru_conv, state_lru_h, state_gla, state_ret, state_dn_conv, state_dn,
           norm_g, final_norm_g, w_ff_in, w_ff_out, w_in, w_gate, w_branch, w_out,
           lru_conv_w, lru_conv_b, lru_wa, lru_ba, lru_wx, lru_bx, lru_lambda,
           gla_wg, gla_bg, gla_norm_g, ret_norm_g, dn_conv_w, dn_a_log, dn_dt_bias, dn_norm_g):
    bp, tpr, d = x_prompt.shape
    bs, ts, _ = x_sample.shape
    depth = w_in.shape[0]
    n_p, n_s = bp * tpr, bs * ts
    assert d == D_MODEL and ts <= SAMPLE_PAD and tpr % CHUNK == 0

    w_ff_in_b = w_ff_in.astype(BF)
    w_ff_out_b = w_ff_out.astype(BF)
    w_pack = _pack_w_in(w_in)
    w_gate_b = w_gate.astype(BF)
    w_branch_b = w_branch.astype(BF)
    w_out_b = w_out.astype(BF)
    wg_pad = jnp.zeros((depth, 128, HEADS * DK), F32).at[:, SM_GLR:SM_GLR + GLA_RANK, :].set(gla_wg).astype(BF)
    r3 = lambda a: a.reshape(a.shape[0], 1, a.shape[1])
    lw = {
        "conv_w": lru_conv_w, "conv_b": r3(lru_conv_b), "wa": lru_wa.astype(BF), "ba": r3(lru_ba),
        "wx": lru_wx.astype(BF), "bx": r3(lru_bx), "lam": r3(lru_lambda),
        "gla_wg": wg_pad, "gla_bg": r3(gla_bg), "gla_ng": r3(gla_norm_g), "ret_ng": r3(ret_norm_g),
        "dn_conv_w": dn_conv_w, "dn_alog": _lane_row(dn_a_log, SM_DA), "dn_dtb": _lane_row(dn_dt_bias, SM_DA),
        "dn_ng": r3(dn_norm_g),
    }
    norm_g4 = norm_g.reshape(depth, 3, 1, d)

    cos_p, sin_p = _rope_tables(jnp.arange(tpr))
    pos_s = jnp.where(jnp.arange(SAMPLE_PAD) < ts, PAST_LEN + jnp.arange(SAMPLE_PAD), 0)
    cos_s, sin_s = _rope_tables(pos_s)

    pad_conv = lambda a: jnp.pad(a, ((0, 0), (SUBLANES - (CONV_W - 1), 0), (0, 0)))
    zeros = lambda *s: jnp.zeros(s, F32)

    tp_p = _pick_tile(tpr, 256, CHUNK)
    a_p = dict(nb=bp, gb=bp, tp=tp_p)
    nb_s = _pick_tile(bs, 16, 1)
    a_s = dict(nb=nb_s, gb=_pick_tile(nb_s, 4, 1), tp=SAMPLE_PAD)

    xp = x_prompt.reshape(n_p, d)
    xs = x_sample.reshape(n_s, d)
    new_p = [[] for _ in range(6)]
    new_s = [[] for _ in range(6)]
    s_gla_s = s_ret_s = s_dn_s = None
    for l in range(depth):
        xp = _ffn(xp, norm_g4[l, 0], w_ff_in_b, w_ff_out_b, l, 0)
        xs = _ffn(xs, norm_g4[l, 0], w_ff_in_b, w_ff_out_b, l, 0)
        p_p = _inproj(xp, norm_g4[l, 1], w_pack, l).reshape(bp, tpr, N_PACK)
        p_s = _inproj(xs, norm_g4[l, 1], w_pack, l).reshape(bs, ts, N_PACK)
        p_s = jnp.pad(p_s, ((0, 0), (0, SAMPLE_PAD - ts), (0, 0)))

        o_lru_p, c_lru_p, h_lru_p = _lru(p_p, zeros(bp, SUBLANES, BRANCH), zeros(bp, 1, BRANCH), lw, l,
                                         t_real=tp_p, is_prompt=True, **a_p)
        o_gla_p, s_gla_p = _gla(p_p, zeros(bp, HEADS, DK, DV), lw, l, c=CHUNK, t_real=tp_p, name="gla_prompt", **a_p)
        o_ret_p, s_ret_p = _ret(p_p, zeros(bp, HEADS, DK, DV), cos_p, sin_p, lw, l, c=CHUNK, t_real=tp_p,
                                name="ret_prompt", **a_p)
        o_dn_p, c_dn_p, s_dn_p = _dn(p_p, zeros(bp, SUBLANES, 3 * BRANCH), zeros(bp, HEADS, DV, DV), lw, l,
                                     c=CHUNK, t_real=tp_p, name="dn_prompt", **a_p)

        o_lru_s, c_lru_s, h_lru_s = _lru(p_s, pad_conv(state_lru_conv[l]), state_lru_h[l][:, None, :], lw, l,
                                         t_real=ts, is_prompt=False, **a_s)
        o_gla_s, s_gla_s = _gla(p_s, state_gla, lw, l, c=SAMPLE_PAD, t_real=ts, name="gla_sample",
                                stacked=(depth, s_gla_s), **a_s)
        o_ret_s, s_ret_s = _ret(p_s, state_ret, cos_s, sin_s, lw, l, c=SAMPLE_PAD, t_real=ts, name="ret_sample",
                                stacked=(depth, s_ret_s), **a_s)
        o_dn_s, c_dn_s, s_dn_s = _dn(p_s, pad_conv(state_dn_conv[l]), state_dn, lw, l, c=SAMPLE_PAD, t_real=ts,
                                     name="dn_sample", stacked=(depth, s_dn_s), **a_s)

        flat_p = lambda o: o.reshape(n_p, BRANCH)
        flat_s = lambda o: o[:, :ts].reshape(n_s, BRANCH)
        xp = _merge(xp, norm_g4[l, 1], [flat_p(o) for o in (o_lru_p, o_gla_p, o_ret_p, o_dn_p)],
                    w_gate_b, w_branch_b, w_out_b, l)
        xs = _merge(xs, norm_g4[l, 1], [flat_s(o) for o in (o_lru_s, o_gla_s, o_ret_s, o_dn_s)],
                    w_gate_b, w_branch_b, w_out_b, l)
        xp = _ffn(xp, norm_g4[l, 2], w_ff_in_b, w_ff_out_b, l, 1)
        xs = _ffn(xs, norm_g4[l, 2], w_ff_in_b, w_ff_out_b, l, 1)

        tail3 = lambda cwin: cwin[:, SUBLANES - (CONV_W - 1):, :]
        for i, v in enumerate((tail3(c_lru_p), h_lru_p[:, 0], s_gla_p, s_ret_p, tail3(c_dn_p), s_dn_p)):
            new_p[i].append(v)
        for i, v in ((0, tail3(c_lru_s)), (1, h_lru_s[:, 0]), (4, tail3(c_dn_s))):
            new_s[i].append(v)

    fg = final_norm_g.reshape(1, d)
    y_prompt = _final_norm(xp, fg).reshape(bp, tpr, d)
    y_sample = _final_norm(xs, fg).reshape(bs, ts, d)
    sp = [jnp.stack(v) for v in new_p]
    s_small = {i: jnp.stack(new_s[i]) for i in (0, 1, 4)}
    return tuple([y_prompt, y_sample] + sp + [s_small[0], s_small[1], s_gla_s, s_ret_s, s_small[4], s_dn_s])
```

```python
import functools
import math

import numpy as np
import jax
import jax.numpy as jnp
from jax import lax
from jax.experimental import pallas as pl
from jax.experimental.pallas import tpu as pltpu

F32 = jnp.float32
BF = jnp.bfloat16
EPS = 1e-6

D_MODEL = 2048
BRANCH = 512
CONV_W = 4
HEADS = 4
DK = 64
DV = 128
GLA_RANK = 16
GLA_TAU = 16.0
LRU_C = 8.0
LRU_BD = 128
ROPE_BASE = 10000.0
CHUNK = 64
PAST_LEN = 16384

V7X_VMEM_BYTES = 64 * 1024 * 1024
VMEM_LIMIT = V7X_VMEM_BYTES - 8 * 1024 * 1024
SUBLANES = 8
SAMPLE_PAD = 8

COL_LX, COL_LY = 0, 512
COL_GQ, COL_GK, COL_GV, COL_GR = 1024, 1280, 1536, 2048
COL_RQ, COL_RK, COL_RQS, COL_RKS, COL_RV, COL_RG = 2560, 2816, 3072, 3328, 3584, 4096
COL_DQKV, COL_DZ = 4608, 6144
COL_SM = 6656
SM_GLR, SM_DB, SM_DA = 0, 16, 20
N_PACK = 7168


def _dot(a, b):
    return jnp.dot(a, b, preferred_element_type=F32)


def _mm(a, b):
    return _dot(a.astype(BF), b.astype(BF))


def _mm_nt(a, b):
    return lax.dot_general(a.astype(BF), b.astype(BF), (((1,), (1,)), ((), ())), preferred_element_type=F32)


def _mm_tn(a, b):
    return lax.dot_general(a.astype(BF), b.astype(BF), (((0,), (0,)), ((), ())), preferred_element_type=F32)


def _split3(x):
    hi = x.astype(BF)
    r = x - hi.astype(F32)
    mid = r.astype(BF)
    lo = (r - mid.astype(F32)).astype(BF)
    return hi, mid, lo


def _split2(x):
    hi = x.astype(BF)
    return hi, (x - hi.astype(F32)).astype(BF)


def _mm_hp(a, b):
    ah, al = _split2(a)
    bh, bl = _split2(b)
    return _dot(ah, bh) + _dot(ah, bl) + _dot(al, bh)


def _cumsum_rows(x, lower_ones):
    hi, mid, lo = _split3(x)
    return _dot(lower_ones, hi) + _dot(lower_ones, mid) + _dot(lower_ones, lo)


def _softplus(x):
    return jnp.maximum(x, 0.0) + jnp.log1p(jnp.exp(-jnp.abs(x)))


def _silu(x):
    return x * jax.nn.sigmoid(x)


def _gelu_tanh(x):
    return x * (0.5 * (1.0 + jnp.tanh(math.sqrt(2.0 / math.pi) * (x + 0.044715 * (x * x * x)))))


def _rms(x, g):
    return x * lax.rsqrt(jnp.mean(x * x, axis=-1, keepdims=True) + EPS) * g


def _pick_tile(n, target, align):
    best = None
    for t in range(align, min(n, target) + 1, align):
        if n % t == 0:
            best = t
    assert best is not None, (n, target, align)
    return best


def _cparams(sem):
    return pltpu.CompilerParams(dimension_semantics=sem, vmem_limit_bytes=VMEM_LIMIT)


def _ffn_kernel(x_ref, g_ref, wg_ref, wu_ref, wo_ref, o_ref, hn_ref):
    @pl.when(pl.program_id(1) == 0)
    def _():
        x = x_ref[...]
        hn_ref[...] = _rms(x, g_ref[...]).astype(BF)
        o_ref[...] = x

    hn = hn_ref[...]
    gate = _dot(hn, wg_ref[...])
    up = _dot(hn, wu_ref[...])
    act = (_silu(gate) * up * 0.5).astype(BF)
    o_ref[...] += _dot(act, wo_ref[...])


def _col_tiled(w, tn):
    *lead, k, n = w.shape
    nl = len(lead)
    return w.reshape(*lead, k, n // tn, tn).transpose(*range(nl), nl + 1, nl, nl + 2)


def _ffn_tile(f):
    return _pick_tile(f, 512, 128)


def _ffn(x, g, w_in, w_out, l, s):
    n, d = x.shape
    f = w_out.shape[2]
    tm = _pick_tile(n, 512, 16)
    tf = _ffn_tile(f)
    nf = f // tf
    assert w_in.shape[2:] == (2 * nf, d, tf)
    return pl.pallas_call(
        _ffn_kernel,
        grid=(n // tm, nf),
        in_specs=[
            pl.BlockSpec((tm, d), lambda i, j: (i, 0)),
            pl.BlockSpec((1, d), lambda i, j: (0, 0)),
            pl.BlockSpec((None, None, None, d, tf), lambda i, j: (l, s, j, 0, 0)),
            pl.BlockSpec((None, None, None, d, tf), lambda i, j: (l, s, j + nf, 0, 0)),
            pl.BlockSpec((None, None, tf, d), lambda i, j: (l, s, j, 0)),
        ],
        out_specs=pl.BlockSpec((tm, d), lambda i, j: (i, 0)),
        out_shape=jax.ShapeDtypeStruct((n, d), F32),
        scratch_shapes=[pltpu.VMEM((tm, d), BF)],
        compiler_params=_cparams(("parallel", "arbitrary")),
        name="ffn",
    )(x, g, w_in, w_in, w_out)


def _inproj_kernel(x_ref, g_ref, w_ref, o_ref, hn_ref):
    @pl.when(pl.program_id(1) == 0)
    def _():
        hn_ref[...] = _rms(x_ref[...], g_ref[...]).astype(BF)

    o_ref[...] = _dot(hn_ref[...], w_ref[...])


INPROJ_TN = 1024
MERGE_TN = 256


def _inproj(x, g, w_pack, l):
    n, d = x.shape
    _, nt, _, tn = w_pack.shape
    npk = nt * tn
    tm = _pick_tile(n, 1024, 16)
    return pl.pallas_call(
        _inproj_kernel,
        grid=(n // tm, nt),
        in_specs=[
            pl.BlockSpec((tm, d), lambda i, j: (i, 0)),
            pl.BlockSpec((1, d), lambda i, j: (0, 0)),
            pl.BlockSpec((None, None, d, tn), lambda i, j: (l, j, 0, 0)),
        ],
        out_specs=pl.BlockSpec((tm, tn), lambda i, j: (i, j)),
        out_shape=jax.ShapeDtypeStruct((n, npk), F32),
        scratch_shapes=[pltpu.VMEM((tm, d), BF)],
        compiler_params=_cparams(("parallel", "arbitrary")),
        name="inproj",
    )(x, g, w_pack)


def _merge_kernel(x_ref, g_ref, b0_ref, b1_ref, b2_ref, b3_ref, wgate_ref, wbr_ref, wo_ref, o_ref, hn_ref):
    @pl.when(pl.program_id(1) == 0)
    def _():
        x = x_ref[...]
        hn_ref[...] = _rms(x, g_ref[...]).astype(BF)
        o_ref[...] = x

    hn = hn_ref[...]
    m = None
    for n, b_ref in enumerate((b0_ref, b1_ref, b2_ref, b3_ref)):
        gate = jax.nn.sigmoid(_dot(hn, wgate_ref[n]))
        br = _dot(b_ref[...].astype(BF), wbr_ref[n])
        m = gate * br if m is None else m + gate * br
    o_ref[...] += _dot(m.astype(BF), wo_ref[...])


def _merge(x, g, branches, w_gate, w_branch, w_out, l):
    n, d = x.shape
    tm = _pick_tile(n, 512, 16)
    tn = MERGE_TN
    assert w_gate.shape[2:] == (d // tn, d, tn) and w_branch.shape[2:] == (d // tn, BRANCH, tn)
    return pl.pallas_call(
        _merge_kernel,
        grid=(n // tm, d // tn),
        in_specs=[
            pl.BlockSpec((tm, d), lambda i, j: (i, 0)),
            pl.BlockSpec((1, d), lambda i, j: (0, 0)),
        ] + [pl.BlockSpec((tm, BRANCH), lambda i, j: (i, 0))] * HEADS + [
            pl.BlockSpec((None, HEADS, None, d, tn), lambda i, j: (l, 0, j, 0, 0)),
            pl.BlockSpec((None, HEADS, None, BRANCH, tn), lambda i, j: (l, 0, j, 0, 0)),
            pl.BlockSpec((None, tn, d), lambda i, j: (l, j, 0)),
        ],
        out_specs=pl.BlockSpec((tm, d), lambda i, j: (i, 0)),
        out_shape=jax.ShapeDtypeStruct((n, d), F32),
        scratch_shapes=[pltpu.VMEM((tm, d), BF)],
        compiler_params=_cparams(("parallel", "arbitrary")),
        name="merge",
    )(x, g, *branches, w_gate, w_branch, w_out)


def _final_norm_kernel(x_ref, g_ref, o_ref):
    o_ref[...] = _rms(x_ref[...], g_ref[...])


def _final_norm(x, g):
    n, d = x.shape
    tm = _pick_tile(n, 512, 8)
    return pl.pallas_call(
        _final_norm_kernel,
        grid=(n // tm,),
        in_specs=[pl.BlockSpec((tm, d), lambda i: (i, 0)), pl.BlockSpec((1, d), lambda i: (0, 0))],
        out_specs=pl.BlockSpec((tm, d), lambda i: (i, 0)),
        out_shape=jax.ShapeDtypeStruct((n, d), F32),
        compiler_params=_cparams(("parallel",)),
        name="final_norm",
    )(x, g)


def _conv4(xs_ref, j_seq, w, tp):
    y = None
    for j in range(CONV_W):
        term = xs_ref[j_seq, pl.ds(SUBLANES - CONV_W + 1 + j, tp), :] * w[j:j + 1, :]
        y = term if y is None else y + term
    return y


def _make_lru_kernel(nb, gb, tp, t_real, is_prompt):
    def kern(lx_ref, ly_ref, c0_ref, h0_ref, cw_ref, cb_ref, wa_ref, ba_ref, wx_ref, bx_ref, lam_ref,
             o_ref, cn_ref, hn_ref, xs_scr, a_scr, u_scr, cc_scr, ch_scr):
        tb = pl.program_id(1)

        @pl.when(tb == 0)
        def _():
            cc_scr[...] = c0_ref[...]
            ch_scr[...] = h0_ref[...]

        cw = cw_ref[...]
        neg_sp = -LRU_C * _softplus(-lam_ref[...])
        row = lax.broadcasted_iota(jnp.int32, (tp, 1), 0)

        def per_group(g, carry):
            seqs = [g * gb + j for j in range(gb)]
            for j, sq in enumerate(seqs):
                xs_scr[j, 0:SUBLANES, :] = cc_scr[sq]
                xs_scr[j, SUBLANES:SUBLANES + tp, :] = lx_ref[sq]
            for j, sq in enumerate(seqs):
                xc = _conv4(xs_scr, j, cw, tp) + cb_ref[...]
                cc_scr[sq] = xs_scr[j, pl.ds(t_real, SUBLANES), :]
                xcb = xc.astype(BF)
                r_parts, i_parts = [], []
                for n in range(BRANCH // LRU_BD):
                    blk = xcb[:, n * LRU_BD:(n + 1) * LRU_BD]
                    r_parts.append(_dot(blk, wa_ref[n]))
                    i_parts.append(_dot(blk, wx_ref[n]))
                r = jax.nn.sigmoid(jnp.concatenate(r_parts, axis=1) + ba_ref[...])
                ig = jax.nn.sigmoid(jnp.concatenate(i_parts, axis=1) + bx_ref[...])
                log_a = r * neg_sp
                a = jnp.exp(log_a)
                mult = jnp.sqrt(-jnp.tanh(log_a) * (a * a + 1.0))
                if is_prompt:
                    mult = jnp.where(jnp.logical_and(row == 0, tb == 0), 1.0, mult)
                a_scr[j] = a
                u_scr[j] = mult * ig * xc

            def step(t, hs):
                out = []
                for j in range(gb):
                    h = a_scr[j, pl.ds(t, 1), :] * hs[j] + u_scr[j, pl.ds(t, 1), :]
                    a_scr[j, pl.ds(t, 1), :] = h
                    out.append(h)
                return tuple(out)

            h_last = lax.fori_loop(0, t_real, step, tuple(ch_scr[sq] for sq in seqs),
                                   unroll=True if t_real <= SUBLANES else 8)
            for j, sq in enumerate(seqs):
                ch_scr[sq] = h_last[j]
                o_ref[sq] = a_scr[j] * _gelu_tanh(ly_ref[sq])
            return carry

        lax.fori_loop(0, nb // gb, per_group, 0)
        cn_ref[...] = cc_scr[...]
        hn_ref[...] = ch_scr[...]

    return kern


def _mixer_specs(nb, tp, l):
    wspec = lambda shape: pl.BlockSpec((None,) + shape, lambda b, t: (l,) + (0,) * len(shape))
    pcol = lambda w, off: pl.BlockSpec((nb, tp, w), lambda b, t: (b, t, off // w))
    state = lambda *shape: pl.BlockSpec((nb,) + shape, lambda b, t: (b,) + (0,) * len(shape))
    return wspec, pcol, state


class _StateIO:
    def __init__(self, s0, stacked, l, nb):
        self.s0, self.l, self.nb = s0, l, nb
        self.stacked = stacked is not None
        self.prev = stacked[1] if self.stacked else None
        self.depth = stacked[0] if self.stacked else None

    def spec(self):
        shape = self.s0.shape[2:] if self.stacked else self.s0.shape[1:]
        zeros = (0,) * len(shape)
        if self.stacked:
            l = self.l
            return pl.BlockSpec((None, self.nb) + shape, lambda b, t: (l, b) + zeros)
        return pl.BlockSpec((self.nb,) + shape, lambda b, t: (b,) + zeros)

    def out_shape(self):
        return jax.ShapeDtypeStruct(self.s0.shape, F32)

    def extra_inputs(self):
        return [self.prev] if self.prev is not None else []

    def extra_specs(self):
        return [pl.BlockSpec(memory_space=pl.ANY)] if self.prev is not None else []

    def aliases(self, n_in, out_idx):
        return {n_in: out_idx} if self.prev is not None else {}

    def wrap(self, kern, n_in):
        if self.prev is None:
            return kern
        return lambda *refs: kern(*refs[:n_in], *refs[n_in + 1:])


def _lru(p, c0, h0, lw, l, nb, gb, tp, t_real, is_prompt):
    nseq, tseq, _ = p.shape
    c = BRANCH
    wspec, pcol, state = _mixer_specs(nb, tp, l)
    return pl.pallas_call(
        _make_lru_kernel(nb, gb, tp, t_real, is_prompt),
        grid=(nseq // nb, tseq // tp),
        in_specs=[
            pcol(c, COL_LX), pcol(c, COL_LY), state(SUBLANES, c), state(1, c),
            wspec((CONV_W, c)), wspec((1, c)),
            wspec((c // LRU_BD, LRU_BD, LRU_BD)), wspec((1, c)),
            wspec((c // LRU_BD, LRU_BD, LRU_BD)), wspec((1, c)),
            wspec((1, c)),
        ],
        out_specs=[pcol(c, 0), state(SUBLANES, c), state(1, c)],
        out_shape=[
            jax.ShapeDtypeStruct((nseq, tseq, c), F32),
            jax.ShapeDtypeStruct((nseq, SUBLANES, c), F32),
            jax.ShapeDtypeStruct((nseq, 1, c), F32),
        ],
        scratch_shapes=[
            pltpu.VMEM((gb, tp + SUBLANES, c), F32), pltpu.VMEM((gb, tp, c), F32), pltpu.VMEM((gb, tp, c), F32),
            pltpu.VMEM((nb, SUBLANES, c), F32), pltpu.VMEM((nb, 1, c), F32),
        ],
        compiler_params=_cparams(("parallel", "arbitrary")),
        name="lru_prompt" if is_prompt else "lru_sample",
    )(p, p, c0, h0, lw["conv_w"], lw["conv_b"], lw["wa"], lw["ba"], lw["wx"], lw["bx"], lw["lam"])


def _chunk_consts(c, cl):
    row = lax.broadcasted_iota(jnp.int32, (c, c), 0)
    col = lax.broadcasted_iota(jnp.int32, (c, c), 1)
    tril = row >= col
    lower_ones = jnp.where(tril, 1.0, 0.0).astype(BF)
    real = lax.broadcasted_iota(jnp.int32, (c, 1), 0) < cl
    return row, col, tril, lower_ones, real


def _ks(h):
    return slice(h * DK, (h + 1) * DK)


def _vs(h):
    return slice(h * DV, (h + 1) * DV)


def _make_gla_kernel(nb, gb, tp, c, t_real):
    nc = tp // c
    cl = min(c, t_real)
    G, H = range(gb), range(HEADS)

    def kern(q_ref, k_ref, v_ref, gr_ref, sm_ref, s0_ref, wg_ref, bg_ref, ng_ref, o_ref, sn_ref, s_scr):
        @pl.when(pl.program_id(1) == 0)
        def _():
            s_scr[...] = s0_ref[...]

        _, _, tril, lower_ones, real = _chunk_consts(c, cl)
        ones_real = jnp.where(real, 1.0, 0.0).astype(BF) * jnp.ones((c, DV), BF)
        d_tn = lambda x: lax.dot_general(x, ones_real, (((0,), (0,)), ((), ())), preferred_element_type=F32)
        ng = ng_ref[...]
        wg = wg_ref[...]
        bg = bg_ref[...]

        def per_chunk(idx, carry):
            g = idx // nc
            ci = idx - g * nc
            rows = pl.ds(pl.multiple_of(ci * c, SUBLANES), c)
            seqs = [g * gb + j for j in G]
            lg = [-_softplus(-(_mm(sm_ref[sq, rows, :], wg) + bg)) * (1.0 / GLA_TAU) for sq in seqs]
            sp = [_split3(x) for x in lg]
            b = [_dot(lower_ones, s[0]) + _dot(lower_ones, s[1]) + _dot(lower_ones, s[2]) for s in sp]
            b_last_col = [d_tn(s[0]) + d_tn(s[1]) + d_tn(s[2]) for s in sp]
            q_t = [q_ref[sq, rows, :] * (DK ** -0.5) * jnp.exp(b[j]) for j, sq in enumerate(seqs)]
            k = [k_ref[sq, rows, :] for sq in seqs]
            k_t = [jnp.where(real, k[j] * jnp.exp(-b[j]), 0.0) for j in G]
            k_d = [jnp.where(real, k[j] * jnp.exp(b[j][cl - 1:cl, :] - b[j]), 0.0) for j in G]
            v = [v_ref[sq, rows, :] for sq in seqs]
            att = [[jnp.where(tril, _mm_nt(q_t[j][:, _ks(h)], k_t[j][:, _ks(h)]), 0.0) for h in H] for j in G]
            s = [[s_scr[seqs[j], h] for h in H] for j in G]
            kv = [[_mm_tn(k_d[j][:, _ks(h)], v[j][:, _vs(h)]) for h in H] for j in G]
            o = [[_mm(att[j][h], v[j][:, _vs(h)]) + _mm(q_t[j][:, _ks(h)], s[j][h]) for h in H] for j in G]
            for j in G:
                for h in H:
                    s_scr[seqs[j], h] = jnp.exp(b_last_col[j][_ks(h), :]) * s[j][h] + kv[j][h]
            for j in G:
                gr = gr_ref[seqs[j], rows, :]
                for h in H:
                    o_ref[seqs[j], rows, _vs(h)] = _rms(o[j][h], ng) * _silu(gr[:, _vs(h)])
            return carry

        lax.fori_loop(0, (nb // gb) * nc, per_chunk, 0)
        sn_ref[...] = s_scr[...]

    return kern


def _gla(p, s0, lw, l, nb, gb, tp, c, t_real, name, stacked=None):
    nseq, tseq, _ = p.shape
    qk = HEADS * DK
    vw = HEADS * DV
    wspec, pcol, _ = _mixer_specs(nb, tp, l)
    sio = _StateIO(s0, stacked, l, nb)
    in_specs = [
        pcol(qk, COL_GQ), pcol(qk, COL_GK), pcol(vw, COL_GV), pcol(vw, COL_GR), pcol(128, COL_SM),
        sio.spec(), wspec((128, qk)), wspec((1, qk)), wspec((1, DV)),
    ]
    n_in = len(in_specs)
    return pl.pallas_call(
        sio.wrap(_make_gla_kernel(nb, gb, tp, c, t_real), n_in),
        grid=(nseq // nb, tseq // tp),
        in_specs=in_specs + sio.extra_specs(),
        out_specs=[pcol(vw, 0), sio.spec()],
        out_shape=[jax.ShapeDtypeStruct((nseq, tseq, vw), F32), sio.out_shape()],
        scratch_shapes=[pltpu.VMEM((nb, HEADS, DK, DV), F32)],
        input_output_aliases=sio.aliases(n_in, 1),
        compiler_params=_cparams(("parallel", "arbitrary")),
        name=name,
    )(p, p, p, p, p, s0, lw["gla_wg"], lw["gla_bg"], lw["gla_ng"], *sio.extra_inputs())


def _make_ret_kernel(nb, gb, tp, c, t_real):
    nc = tp // c
    cl = min(c, t_real)
    G, H = range(gb), range(HEADS)
    log_gamma = [float(np.log(np.float32(1.0) - np.float32(2.0) ** np.float32(-5.0 - h))) for h in range(HEADS)]

    def kern(q_ref, k_ref, qs_ref, ks_ref, v_ref, g_ref, cos_ref, sin_ref, s0_ref, ng_ref, o_ref, sn_ref, s_scr):
        @pl.when(pl.program_id(1) == 0)
        def _():
            s_scr[...] = s0_ref[...]

        row, col, tril, _, real = _chunk_consts(c, cl)
        diff = (row - col).astype(F32)
        ridx = lax.broadcasted_iota(jnp.int32, (c, 1), 0).astype(F32)
        ng = ng_ref[...]
        dmat = [jnp.where(tril, jnp.exp(jnp.maximum(diff, 0.0) * lgm), 0.0) for lgm in log_gamma]
        q_decay = [jnp.exp((ridx + 1.0) * lgm) for lgm in log_gamma]
        k_decay = [jnp.exp((cl - 1.0 - ridx) * lgm) for lgm in log_gamma]

        def per_chunk(idx, carry):
            g = idx // nc
            ci = idx - g * nc
            rows = pl.ds(pl.multiple_of(ci * c, SUBLANES), c)
            seqs = [g * gb + j for j in G]
            cos = cos_ref[rows, :]
            sin = sin_ref[rows, :]
            q = [q_ref[sq, rows, :] * cos + qs_ref[sq, rows, :] * sin for sq in seqs]
            k = [jnp.where(real, (k_ref[sq, rows, :] * cos + ks_ref[sq, rows, :] * sin) * (DK ** -0.5), 0.0)
                 for sq in seqs]
            v = [v_ref[sq, rows, :] for sq in seqs]
            att = [[_mm_nt(q[j][:, _ks(h)], k[j][:, _ks(h)]) * dmat[h] for h in H] for j in G]
            s = [[s_scr[seqs[j], h] for h in H] for j in G]
            kv = [[_mm_tn(k[j][:, _ks(h)] * k_decay[h], v[j][:, _vs(h)]) for h in H] for j in G]
            o = [[_mm(att[j][h], v[j][:, _vs(h)]) + _mm(q[j][:, _ks(h)] * q_decay[h], s[j][h]) for h in H]
                 for j in G]
            for j in G:
                for h in H:
                    s_scr[seqs[j], h] = math.exp(cl * log_gamma[h]) * s[j][h] + kv[j][h]
            for j in G:
                gate = g_ref[seqs[j], rows, :]
                for h in H:
                    oc = o[j][h] - jnp.mean(o[j][h], axis=-1, keepdims=True)
                    on = oc * lax.rsqrt(jnp.mean(oc * oc, axis=-1, keepdims=True) + EPS) * ng
                    o_ref[seqs[j], rows, _vs(h)] = on * _silu(gate[:, _vs(h)])
            return carry

        lax.fori_loop(0, (nb // gb) * nc, per_chunk, 0)
        sn_ref[...] = s_scr[...]

    return kern


def _ret(p, s0, cos_t, sin_t, lw, l, nb, gb, tp, c, t_real, name, stacked=None):
    nseq, tseq, _ = p.shape
    qk = HEADS * DK
    vw = HEADS * DV
    wspec, pcol, _ = _mixer_specs(nb, tp, l)
    sio = _StateIO(s0, stacked, l, nb)
    in_specs = [
        pcol(qk, COL_RQ), pcol(qk, COL_RK), pcol(qk, COL_RQS), pcol(qk, COL_RKS), pcol(vw, COL_RV),
        pcol(vw, COL_RG),
        pl.BlockSpec((tp, qk), lambda b, t: (t, 0)),
        pl.BlockSpec((tp, qk), lambda b, t: (t, 0)),
        sio.spec(), wspec((1, DV)),
    ]
    n_in = len(in_specs)
    return pl.pallas_call(
        sio.wrap(_make_ret_kernel(nb, gb, tp, c, t_real), n_in),
        grid=(nseq // nb, tseq // tp),
        in_specs=in_specs + sio.extra_specs(),
        out_specs=[pcol(vw, 0), sio.spec()],
        out_shape=[jax.ShapeDtypeStruct((nseq, tseq, vw), F32), sio.out_shape()],
        scratch_shapes=[pltpu.VMEM((nb, HEADS, DK, DV), F32)],
        input_output_aliases=sio.aliases(n_in, 1),
        compiler_params=_cparams(("parallel", "arbitrary")),
        name=name,
    )(p, p, p, p, p, p, cos_t, sin_t, s0, lw["ret_ng"], *sio.extra_inputs())


def _unit_lower_inverse_many(ms, c):
    row = lax.broadcasted_iota(jnp.int32, (c, c), 0)
    col = lax.broadcasted_iota(jnp.int32, (c, c), 1)
    eye = jnp.where(row == col, 1.0, 0.0)

    def same_block(bits):
        return lax.shift_right_logical(row, bits) == lax.shift_right_logical(col, bits)

    in8 = same_block(3)
    n1 = [jnp.where(in8, m, 0.0) for m in ms]
    n2 = [_mm_hp(x, x) for x in n1]
    n4 = [_mm_hp(x, x) for x in n2]
    d = [eye - x for x in n1]
    d = [x + _mm_hp(x, y) for x, y in zip(d, n2)]
    d = [x + _mm_hp(x, y) for x, y in zip(d, n4)]
    bits = 3
    while (1 << bits) < c:
        sel = jnp.logical_and(same_block(bits + 1), jnp.logical_not(same_block(bits)))
        ld = [_mm_hp(jnp.where(sel, m, 0.0), x) for m, x in zip(ms, d)]
        d = [x - _mm_hp(x, y) for x, y in zip(d, ld)]
        bits += 1
    return d


def _make_dn_kernel(nb, gb, tp, c, t_real):
    nc = tp // c
    cl = min(c, t_real)
    cw3 = 3 * BRANCH
    G, H = range(gb), range(HEADS)
    GH = [(j, h) for j in G for h in H]

    def kern(x_ref, z_ref, sm_ref, c0_ref, s0_ref, cw_ref, alog_ref, dtb_ref, ng_ref,
             o_ref, cn_ref, sn_ref, xs_scr, qkv_scr, cc_scr, s_scr):
        @pl.when(pl.program_id(1) == 0)
        def _():
            cc_scr[...] = c0_ref[...]
            s_scr[...] = s0_ref[...]

        row, col, tril, lower_ones, real = _chunk_consts(c, cl)
        strict = row > col
        lane = lax.broadcasted_iota(jnp.int32, (c, 128), 1)
        sel = [jnp.where(lane == SM_DA + h, 1.0, 0.0).astype(BF) for h in H]
        d_nt = lambda a, x: lax.dot_general(a, x, (((1,), (1,)), ((), ())), preferred_element_type=F32)
        ng = ng_ref[...]
        cw = cw_ref[...]
        neg_a = -jnp.exp(alog_ref[...])
        dtb = dtb_ref[...]

        def per_group(g, carry):
            seqs = [g * gb + j for j in G]
            for j, sq in enumerate(seqs):
                xs_scr[j, 0:SUBLANES, :] = cc_scr[sq]
                xs_scr[j, SUBLANES:SUBLANES + tp, :] = x_ref[sq]
            for j, sq in enumerate(seqs):
                qkv_scr[j] = _silu(_conv4(xs_scr, j, cw, tp))
                cc_scr[sq] = xs_scr[j, pl.ds(t_real, SUBLANES), :]

            def per_chunk(ci, carry2):
                rows = pl.ds(pl.multiple_of(ci * c, SUBLANES), c)
                sm = [sm_ref[sq, rows, :] for sq in seqs]
                gcum = [_cumsum_rows(neg_a * _softplus(x + dtb), lower_ones) for x in sm]
                g3 = [_split3(x) for x in gcum]
                beta_all = [jnp.where(real, jax.nn.sigmoid(x), 0.0) for x in sm]

                def head_in(j, h, part):
                    x = qkv_scr[j, rows, part * BRANCH + h * DV:part * BRANCH + (h + 1) * DV]
                    return x

                qh = {jh: head_in(*jh, 0) for jh in GH}
                kh = {jh: head_in(*jh, 1) for jh in GH}
                vh = {jh: head_in(*jh, 2) for jh in GH}
                qh = {jh: x * lax.rsqrt(jnp.sum(x * x, axis=-1, keepdims=True) + EPS) * (DV ** -0.5)
                      for jh, x in qh.items()}
                kh = {jh: jnp.where(real, x * lax.rsqrt(jnp.sum(x * x, axis=-1, keepdims=True) + EPS), 0.0)
                      for jh, x in kh.items()}
                beta = {(j, h): beta_all[j][:, SM_DB + h:SM_DB + h + 1] for j, h in GH}
                gcol = {(j, h): gcum[j][:, SM_DA + h:SM_DA + h + 1] for j, h in GH}
                grow = {(j, h): d_nt(sel[h], g3[j][0]) + d_nt(sel[h], g3[j][1]) + d_nt(sel[h], g3[j][2])
                        for j, h in GH}
                decay = {jh: jnp.where(tril, jnp.exp(jnp.where(tril, gcol[jh] - grow[jh], 0.0)), 0.0) for jh in GH}
                e_g = {jh: jnp.exp(gcol[jh]) for jh in GH}
                kb = {jh: kh[jh] * beta[jh] for jh in GH}
                kk = {jh: _mm_nt(kb[jh], kh[jh]) for jh in GH}
                qk = {jh: _mm_nt(qh[jh], kh[jh]) for jh in GH}
                tm = _unit_lower_inverse_many([jnp.where(strict, kk[jh] * decay[jh], 0.0) for jh in GH], c)
                tm = dict(zip(GH, tm))
                u = {jh: _mm(tm[jh], vh[jh] * beta[jh]) for jh in GH}
                w = {jh: _mm(tm[jh], kb[jh] * e_g[jh]) for jh in GH}
                s = {(j, h): s_scr[seqs[j], h] for j, h in GH}
                ws = {jh: _mm(w[jh], s[jh]) for jh in GH}
                qs = {jh: _mm(qh[jh] * e_g[jh], s[jh]) for jh in GH}
                v_new = {jh: u[jh] - ws[jh] for jh in GH}
                o = {jh: qs[jh] + _mm(qk[jh] * decay[jh], v_new[jh]) for jh in GH}
                for j, h in GH:
                    g_last = gcol[(j, h)][cl - 1:cl, :]
                    k_d = kh[(j, h)] * jnp.exp(g_last - gcol[(j, h)])
                    s_scr[seqs[j], h] = jnp.exp(g_last) * s[(j, h)] + _mm_tn(k_d, v_new[(j, h)])
                for j in G:
                    z = z_ref[seqs[j], rows, :]
                    for h in H:
                        o_ref[seqs[j], rows, _vs(h)] = _rms(o[(j, h)], ng) * _silu(z[:, _vs(h)])
                return carry2

            lax.fori_loop(0, nc, per_chunk, 0)
            return carry

        lax.fori_loop(0, nb // gb, per_group, 0)
        cn_ref[...] = cc_scr[...]
        sn_ref[...] = s_scr[...]

    return kern


def _dn(p, c0, s0, lw, l, nb, gb, tp, c, t_real, name, stacked=None):
    nseq, tseq, _ = p.shape
    cw3 = 3 * BRANCH
    wspec, pcol, state = _mixer_specs(nb, tp, l)
    sio = _StateIO(s0, stacked, l, nb)
    in_specs = [
        pcol(cw3, COL_DQKV), pcol(BRANCH, COL_DZ), pcol(128, COL_SM),
        state(SUBLANES, cw3), sio.spec(),
        wspec((CONV_W, cw3)), wspec((1, 128)), wspec((1, 128)), wspec((1, DV)),
    ]
    n_in = len(in_specs)
    return pl.pallas_call(
        sio.wrap(_make_dn_kernel(nb, gb, tp, c, t_real), n_in),
        grid=(nseq // nb, tseq // tp),
        in_specs=in_specs + sio.extra_specs(),
        out_specs=[pcol(BRANCH, 0), state(SUBLANES, cw3), sio.spec()],
        out_shape=[
            jax.ShapeDtypeStruct((nseq, tseq, BRANCH), F32),
            jax.ShapeDtypeStruct((nseq, SUBLANES, cw3), F32),
            sio.out_shape(),
        ],
        scratch_shapes=[
            pltpu.VMEM((gb, tp + SUBLANES, cw3), F32), pltpu.VMEM((gb, tp, cw3), F32),
            pltpu.VMEM((nb, SUBLANES, cw3), F32), pltpu.VMEM((nb, HEADS, DV, DV), F32),
        ],
        input_output_aliases=sio.aliases(n_in, 2),
        compiler_params=_cparams(("parallel", "arbitrary")),
        name=name,
    )(p, p, p, c0, s0, lw["dn_conv_w"], lw["dn_alog"], lw["dn_dtb"], lw["dn_ng"], *sio.extra_inputs())


def _pack_w_in(w_in):
    sizes = (BRANCH, BRANCH, HEADS * DK, HEADS * DK, BRANCH, BRANCH, GLA_RANK,
             HEADS * DK, HEADS * DK, BRANCH, BRANCH, BRANCH, BRANCH, BRANCH, BRANCH, HEADS, HEADS)
    offs = np.concatenate([[0], np.cumsum(sizes)])
    seg = lambda i: w_in[:, :, int(offs[i]):int(offs[i + 1])]
    (lx, ly, gq, gk, gv, gr, glr, rq, rk, rv, rg, dq, dk, dv, dz, db, da) = [seg(i) for i in range(len(sizes))]

    def swap_halves(w):
        halves = w.reshape(w.shape[:2] + (HEADS, 2, DK // 2))
        return halves[:, :, :, ::-1, :].reshape(w.shape)

    lead = w_in.shape[:2]
    small_pad = jnp.zeros(lead + (128 - GLA_RANK - 2 * HEADS,), w_in.dtype)
    used = COL_SM + 128
    tail = jnp.zeros(lead + (N_PACK - used,), w_in.dtype)
    packed = jnp.concatenate([lx, ly, gq, gk, gv, gr, rq, rk, swap_halves(rq), swap_halves(rk), rv, rg,
                              dq, dk, dv, dz, glr, db, da, small_pad, tail], axis=2)
    return packed.astype(BF)


def _rope_tables(pos):
    half = DK // 2
    inv = ROPE_BASE ** (-jnp.arange(half, dtype=F32) / half)
    ang = pos.astype(F32)[:, None] * inv[None, :]
    cos, sin = jnp.cos(ang), jnp.sin(ang)
    cos_t = jnp.tile(jnp.concatenate([cos, cos], axis=1), (1, HEADS))
    sin_t = jnp.tile(jnp.concatenate([-sin, sin], axis=1), (1, HEADS))
    return cos_t, sin_t


def _lane_row(vals, off):
    depth = vals.shape[0]
    return jnp.zeros((depth, 1, 128), F32).at[:, 0, off:off + HEADS].set(vals.astype(F32))


def kernel(x_prompt, x_sample, state_lru_conv, state_lru_h, state_gla, state_ret, state_dn_conv, state_dn,
           norm_g, final_norm_g, w_ff_in, w_ff_out, w_in, w_gate, w_branch, w_out,
           lru_conv_w, lru_conv_b, lru_wa, lru_ba, lru_wx, lru_bx, lru_lambda,
           gla_wg, gla_bg, gla_norm_g, ret_norm_g, dn_conv_w, dn_a_log, dn_dt_bias, dn_norm_g):
    bp, tpr, d = x_prompt.shape
    bs, ts, _ = x_sample.shape
    depth = w_in.shape[0]
    n_p, n_s = bp * tpr, bs * ts
    assert d == D_MODEL and ts <= SAMPLE_PAD and tpr % CHUNK == 0

    w_ff_in_b = _col_tiled(w_ff_in.astype(BF), _ffn_tile(w_ff_out.shape[2]))
    w_ff_out_b = w_ff_out.astype(BF)
    w_pack = _col_tiled(_pack_w_in(w_in), INPROJ_TN)
    w_gate_b = _col_tiled(w_gate.astype(BF), MERGE_TN)
    w_branch_b = _col_tiled(w_branch.astype(BF), MERGE_TN)
    w_out_b = w_out.astype(BF)
    wg_pad = jnp.zeros((depth, 128, HEADS * DK), F32).at[:, SM_GLR:SM_GLR + GLA_RANK, :].set(gla_wg).astype(BF)
    r3 = lambda a: a.reshape(a.shape[0], 1, a.shape[1])
    lw = {
        "conv_w": lru_conv_w, "conv_b": r3(lru_conv_b), "wa": lru_wa.astype(BF), "ba": r3(lru_ba),
        "wx": lru_wx.astype(BF), "bx": r3(lru_bx), "lam": r3(lru_lambda),
        "gla_wg": wg_pad, "gla_bg": r3(gla_bg), "gla_ng": r3(gla_norm_g), "ret_ng": r3(ret_norm_g),
        "dn_conv_w": dn_conv_w, "dn_alog": _lane_row(dn_a_log, SM_DA), "dn_dtb": _lane_row(dn_dt_bias, SM_DA),
        "dn_ng": r3(dn_norm_g),
    }
    norm_g4 = norm_g.reshape(depth, 3, 1, d)

    cos_p, sin_p = _rope_tables(jnp.arange(tpr))
    pos_s = jnp.where(jnp.arange(SAMPLE_PAD) < ts, PAST_LEN + jnp.arange(SAMPLE_PAD), 0)
    cos_s, sin_s = _rope_tables(pos_s)

    pad_conv = lambda a: jnp.pad(a, ((0, 0), (SUBLANES - (CONV_W - 1), 0), (0, 0)))
    zeros = lambda *s: jnp.zeros(s, F32)

    tp_p = _pick_tile(tpr, 256, CHUNK)
    a_p = dict(nb=bp, gb=bp, tp=tp_p)
    nb_s = _pick_tile(bs, 16, 1)
    a_s = dict(nb=nb_s, gb=_pick_tile(nb_s, 4, 1), tp=SAMPLE_PAD)

    xp = x_prompt.reshape(n_p, d)
    xs = x_sample.reshape(n_s, d)
    new_p = [[] for _ in range(6)]
    new_s = [[] for _ in range(6)]
    s_gla_s = s_ret_s = s_dn_s = None
    for l in range(depth):
        xp = _ffn(xp, norm_g4[l, 0], w_ff_in_b, w_ff_out_b, l, 0)
        xs = _ffn(xs, norm_g4[l, 0], w_ff_in_b, w_ff_out_b, l, 0)
        p_p = _inproj(xp, norm_g4[l, 1], w_pack, l).reshape(bp, tpr, N_PACK)
        p_s = _inproj(xs, norm_g4[l, 1], w_pack, l).reshape(bs, ts, N_PACK)
        p_s = jnp.pad(p_s, ((0, 0), (0, SAMPLE_PAD - ts), (0, 0)))

        o_lru_p, c_lru_p, h_lru_p = _lru(p_p, zeros(bp, SUBLANES, BRANCH), zeros(bp, 1, BRANCH), lw, l,
                                         t_real=tp_p, is_prompt=True, **a_p)
        o_gla_p, s_gla_p = _gla(p_p, zeros(bp, HEADS, DK, DV), lw, l, c=CHUNK, t_real=tp_p, name="gla_prompt", **a_p)
        o_ret_p, s_ret_p = _ret(p_p, zeros(bp, HEADS, DK, DV), cos_p, sin_p, lw, l, c=CHUNK, t_real=tp_p,
                                name="ret_prompt", **a_p)
        o_dn_p, c_dn_p, s_dn_p = _dn(p_p, zeros(bp, SUBLANES, 3 * BRANCH), zeros(bp, HEADS, DV, DV), lw, l,
                                     c=CHUNK, t_real=tp_p, name="dn_prompt", **a_p)

        o_lru_s, c_lru_s, h_lru_s = _lru(p_s, pad_conv(state_lru_conv[l]), state_lru_h[l][:, None, :], lw, l,
                                         t_real=ts, is_prompt=False, **a_s)
        o_gla_s, s_gla_s = _gla(p_s, state_gla, lw, l, c=SAMPLE_PAD, t_real=ts, name="gla_sample",
                                stacked=(depth, s_gla_s), **a_s)
        o_ret_s, s_ret_s = _ret(p_s, state_ret, cos_s, sin_s, lw, l, c=SAMPLE_PAD, t_real=ts, name="ret_sample",
                                stacked=(depth, s_ret_s), **a_s)
        o_dn_s, c_dn_s, s_dn_s = _dn(p_s, pad_conv(state_dn_conv[l]), state_dn, lw, l, c=SAMPLE_PAD, t_real=ts,
                                     name="dn_sample", stacked=(depth, s_dn_s), **a_s)

        flat_p = lambda o: o.reshape(n_p, BRANCH)
        flat_s = lambda o: o[:, :ts].reshape(n_s, BRANCH)
        xp = _merge(xp, norm_g4[l, 1], [flat_p(o) for o in (o_lru_p, o_gla_p, o_ret_p, o_dn_p)],
                    w_gate_b, w_branch_b, w_out_b, l)
        xs = _merge(xs, norm_g4[l, 1], [flat_s(o) for o in (o_lru_s, o_gla_s, o_ret_s, o_dn_s)],
                    w_gate_b, w_branch_b, w_out_b, l)
        xp = _ffn(xp, norm_g4[l, 2], w_ff_in_b, w_ff_out_b, l, 1)
        xs = _ffn(xs, norm_g4[l, 2], w_ff_in_b, w_ff_out_b, l, 1)

        tail3 = lambda cwin: cwin[:, SUBLANES - (CONV_W - 1):, :]
        for i, v in enumerate((tail3(c_lru_p), h_lru_p[:, 0], s_gla_p, s_ret_p, tail3(c_dn_p), s_dn_p)):
            new_p[i].append(v)
        for i, v in ((0, tail3(c_lru_s)), (1, h_lru_s[:, 0]), (4, tail3(c_dn_s))):
            new_s[i].append(v)

    fg = final_norm_g.reshape(1, d)
    y_prompt = _final_norm(xp, fg).reshape(bp, tpr, d)
    y_sample = _final_norm(xs, fg).reshape(bs, ts, d)
    sp = [jnp.stack(v) for v in new_p]
    s_small = {i: jnp.stack(new_s[i]) for i in (0, 1, 4)}
    return tuple([y_prompt, y_sample] + sp + [s_small[0], s_small[1], s_gla_s, s_ret_s, s_small[4], s_dn_s])
```

```python
import functools
import math

import numpy as np
import jax
import jax.numpy as jnp
from jax import lax
from jax.experimental import pallas as pl
from jax.experimental.pallas import tpu as pltpu

F32 = jnp.float32
BF = jnp.bfloat16
EPS = 1e-6

D_MODEL = 2048
BRANCH = 512
CONV_W = 4
HEADS = 4
DK = 64
DV = 128
GLA_RANK = 16
GLA_TAU = 16.0
LRU_C = 8.0
LRU_BD = 128
ROPE_BASE = 10000.0
CHUNK = 64
PAST_LEN = 16384

V7X_VMEM_BYTES = 64 * 1024 * 1024
VMEM_LIMIT = V7X_VMEM_BYTES - 8 * 1024 * 1024
SUBLANES = 8
SAMPLE_PAD = 8

COL_LX, COL_LY = 0, 512
COL_GQ, COL_GK, COL_GV, COL_GR = 1024, 1280, 1536, 2048
COL_RQ, COL_RK, COL_RQS, COL_RKS, COL_RV, COL_RG = 2560, 2816, 3072, 3328, 3584, 4096
COL_DQKV, COL_DZ = 4608, 6144
COL_SM = 6656
SM_GLR, SM_DB, SM_DA = 0, 16, 20
N_PACK = 6912


def _dot(a, b):
    return jnp.dot(a, b, preferred_element_type=F32)


def _mm(a, b):
    return _dot(a.astype(BF), b.astype(BF))


def _mm_nt(a, b):
    return lax.dot_general(a.astype(BF), b.astype(BF), (((1,), (1,)), ((), ())), preferred_element_type=F32)


def _mm_tn(a, b):
    return lax.dot_general(a.astype(BF), b.astype(BF), (((0,), (0,)), ((), ())), preferred_element_type=F32)


def _split3(x):
    hi = x.astype(BF)
    r = x - hi.astype(F32)
    mid = r.astype(BF)
    lo = (r - mid.astype(F32)).astype(BF)
    return hi, mid, lo


def _split2(x):
    hi = x.astype(BF)
    return hi, (x - hi.astype(F32)).astype(BF)


def _mm_hp(a, b):
    ah, al = _split2(a)
    bh, bl = _split2(b)
    return _dot(ah, bh) + _dot(ah, bl) + _dot(al, bh)


def _cumsum_rows(x, lower_ones):
    hi, mid, lo = _split3(x)
    return _dot(lower_ones, hi) + _dot(lower_ones, mid) + _dot(lower_ones, lo)


def _softplus(x):
    return jnp.maximum(x, 0.0) + jnp.log1p(jnp.exp(-jnp.abs(x)))


def _silu(x):
    return x * jax.nn.sigmoid(x)


def _gelu_tanh(x):
    return x * (0.5 * (1.0 + jnp.tanh(math.sqrt(2.0 / math.pi) * (x + 0.044715 * (x * x * x)))))


def _rms(x, g):
    return x * lax.rsqrt(jnp.mean(x * x, axis=-1, keepdims=True) + EPS) * g


def _pick_tile(n, target, align):
    best = None
    for t in range(align, min(n, target) + 1, align):
        if n % t == 0:
            best = t
    assert best is not None, (n, target, align)
    return best


def _cparams(sem):
    return pltpu.CompilerParams(dimension_semantics=sem, vmem_limit_bytes=VMEM_LIMIT)


def _ffn_kernel(x_ref, g_ref, wg_ref, wu_ref, wo_ref, o_ref, hn_ref):
    @pl.when(pl.program_id(1) == 0)
    def _():
        x = x_ref[...]
        hn_ref[...] = _rms(x, g_ref[...]).astype(BF)
        o_ref[...] = x

    hn = hn_ref[...]
    gate = _dot(hn, wg_ref[...])
    up = _dot(hn, wu_ref[...])
    act = (_silu(gate) * up * 0.5).astype(BF)
    o_ref[...] += _dot(act, wo_ref[...])


def _ffn(x, g, wg, wu, wo):
    n, d = x.shape
    f = wo.shape[0]
    tm = _pick_tile(n, 512, 16)
    tf = _pick_tile(f, 512, 128)
    return pl.pallas_call(
        _ffn_kernel,
        grid=(n // tm, f // tf),
        in_specs=[
            pl.BlockSpec((tm, d), lambda i, j: (i, 0)),
            pl.BlockSpec((1, d), lambda i, j: (0, 0)),
            pl.BlockSpec((d, tf), lambda i, j: (0, j)),
            pl.BlockSpec((d, tf), lambda i, j: (0, j)),
            pl.BlockSpec((tf, d), lambda i, j: (j, 0)),
        ],
        out_specs=pl.BlockSpec((tm, d), lambda i, j: (i, 0)),
        out_shape=jax.ShapeDtypeStruct((n, d), F32),
        scratch_shapes=[pltpu.VMEM((tm, d), BF)],
        compiler_params=_cparams(("parallel", "arbitrary")),
        name="ffn",
    )(x, g, wg, wu, wo)


def _ffn_cast_kernel(x_ref, g_ref, wg32_ref, wu32_ref, wo32_ref, o_ref, wg_ref, wu_ref, wo_ref, hn_ref):
    @pl.when(pl.program_id(1) == 0)
    def _():
        x = x_ref[...]
        hn_ref[...] = _rms(x, g_ref[...]).astype(BF)
        o_ref[...] = x

    wg_ref[...] = wg32_ref[...].astype(BF)
    wu_ref[...] = wu32_ref[...].astype(BF)
    wo_ref[...] = wo32_ref[...].astype(BF)
    hn = hn_ref[...]
    act = (_silu(_dot(hn, wg_ref[...])) * _dot(hn, wu_ref[...]) * 0.5).astype(BF)
    o_ref[...] += _dot(act, wo_ref[...])


def _ffn_cast(x, g, w_in, w_out, l, s):
    n, d = x.shape
    f = w_out.shape[2]
    tf = _pick_tile(f, 256, 128)
    nf = f // tf
    return pl.pallas_call(
        _ffn_cast_kernel,
        grid=(1, nf),
        in_specs=[
            pl.BlockSpec((n, d), lambda i, j: (0, 0)),
            pl.BlockSpec((1, d), lambda i, j: (0, 0)),
            pl.BlockSpec((None, None, d, tf), lambda i, j: (l, s, 0, j)),
            pl.BlockSpec((None, None, d, tf), lambda i, j: (l, s, 0, j + nf)),
            pl.BlockSpec((None, None, tf, d), lambda i, j: (l, s, j, 0)),
        ],
        out_specs=[
            pl.BlockSpec((n, d), lambda i, j: (0, 0)),
            pl.BlockSpec((d, tf), lambda i, j: (0, j)),
            pl.BlockSpec((d, tf), lambda i, j: (0, j)),
            pl.BlockSpec((tf, d), lambda i, j: (j, 0)),
        ],
        out_shape=[
            jax.ShapeDtypeStruct((n, d), F32),
            jax.ShapeDtypeStruct((d, f), BF), jax.ShapeDtypeStruct((d, f), BF), jax.ShapeDtypeStruct((f, d), BF),
        ],
        scratch_shapes=[pltpu.VMEM((n, d), BF)],
        compiler_params=_cparams(("arbitrary", "arbitrary")),
        name="ffn_cast",
    )(x, g, w_in, w_in, w_out)


def _inproj_kernel(x_ref, g_ref, w_ref, o_ref, hn_ref):
    @pl.when(pl.program_id(1) == 0)
    def _():
        hn_ref[...] = _rms(x_ref[...], g_ref[...]).astype(BF)

    o_ref[...] = _dot(hn_ref[...], w_ref[...])


def _inproj(x, g, w_pack, l):
    n, d = x.shape
    npk = w_pack.shape[2]
    tm = _pick_tile(n, 1024, 16)
    tn = _pick_tile(npk, 1024, 128)
    return pl.pallas_call(
        _inproj_kernel,
        grid=(n // tm, npk // tn),
        in_specs=[
            pl.BlockSpec((tm, d), lambda i, j: (i, 0)),
            pl.BlockSpec((1, d), lambda i, j: (0, 0)),
            pl.BlockSpec((None, d, tn), lambda i, j: (l, 0, j)),
        ],
        out_specs=pl.BlockSpec((tm, tn), lambda i, j: (i, j)),
        out_shape=jax.ShapeDtypeStruct((n, npk), F32),
        scratch_shapes=[pltpu.VMEM((tm, d), BF)],
        compiler_params=_cparams(("parallel", "arbitrary")),
        name="inproj",
    )(x, g, w_pack)


def _merge_kernel(x_ref, g_ref, b0_ref, b1_ref, b2_ref, b3_ref, wgate_ref, wbr_ref, wo_ref, o_ref, hn_ref):
    @pl.when(pl.program_id(1) == 0)
    def _():
        x = x_ref[...]
        hn_ref[...] = _rms(x, g_ref[...]).astype(BF)
        o_ref[...] = x

    hn = hn_ref[...]
    m = None
    for n, b_ref in enumerate((b0_ref, b1_ref, b2_ref, b3_ref)):
        gate = jax.nn.sigmoid(_dot(hn, wgate_ref[n]))
        br = _dot(b_ref[...].astype(BF), wbr_ref[n])
        m = gate * br if m is None else m + gate * br
    o_ref[...] += _dot(m.astype(BF), wo_ref[...])


def _merge(x, g, branches, w_gate_l, w_branch, w_out, l):
    n, d = x.shape
    tm = _pick_tile(n, 512, 16)
    tn = 256
    return pl.pallas_call(
        _merge_kernel,
        grid=(n // tm, d // tn),
        in_specs=[
            pl.BlockSpec((tm, d), lambda i, j: (i, 0)),
            pl.BlockSpec((1, d), lambda i, j: (0, 0)),
        ] + [pl.BlockSpec((tm, BRANCH), lambda i, j: (i, 0))] * HEADS + [
            pl.BlockSpec((HEADS, d, tn), lambda i, j: (0, 0, j)),
            pl.BlockSpec((None, HEADS, BRANCH, tn), lambda i, j: (l, 0, 0, j)),
            pl.BlockSpec((None, tn, d), lambda i, j: (l, j, 0)),
        ],
        out_specs=pl.BlockSpec((tm, d), lambda i, j: (i, 0)),
        out_shape=jax.ShapeDtypeStruct((n, d), F32),
        scratch_shapes=[pltpu.VMEM((tm, d), BF)],
        compiler_params=_cparams(("parallel", "arbitrary")),
        name="merge",
    )(x, g, *branches, w_gate_l, w_branch, w_out)


def _merge_cast_kernel(x_ref, g_ref, b0_ref, b1_ref, b2_ref, b3_ref, wgate32_ref, wbr_ref, wo_ref,
                       o_ref, wgate_ref, hn_ref):
    wgate_ref[...] = wgate32_ref[...].astype(BF)
    _merge_kernel(x_ref, g_ref, b0_ref, b1_ref, b2_ref, b3_ref, wgate_ref, wbr_ref, wo_ref, o_ref, hn_ref)


def _merge_cast(x, g, branches, w_gate, w_branch, w_out, l):
    n, d = x.shape
    tn = 256
    once = dict(pipeline_mode=pl.Buffered(1))
    return pl.pallas_call(
        _merge_cast_kernel,
        grid=(1, d // tn),
        in_specs=[
            pl.BlockSpec((n, d), lambda i, j: (0, 0), **once),
            pl.BlockSpec((1, d), lambda i, j: (0, 0)),
        ] + [pl.BlockSpec((n, BRANCH), lambda i, j: (0, 0), **once)] * HEADS + [
            pl.BlockSpec((None, HEADS, d, tn), lambda i, j: (l, 0, 0, j)),
            pl.BlockSpec((None, HEADS, BRANCH, tn), lambda i, j: (l, 0, 0, j)),
            pl.BlockSpec((None, tn, d), lambda i, j: (l, j, 0)),
        ],
        out_specs=[
            pl.BlockSpec((n, d), lambda i, j: (0, 0), **once),
            pl.BlockSpec((HEADS, d, tn), lambda i, j: (0, 0, j)),
        ],
        out_shape=[jax.ShapeDtypeStruct((n, d), F32), jax.ShapeDtypeStruct((HEADS, d, d), BF)],
        scratch_shapes=[pltpu.VMEM((n, d), BF)],
        compiler_params=_cparams(("arbitrary", "arbitrary")),
        name="merge_cast",
    )(x, g, *branches, w_gate, w_branch, w_out)


def _final_norm_kernel(x_ref, g_ref, o_ref):
    o_ref[...] = _rms(x_ref[...], g_ref[...])


def _final_norm(x, g):
    n, d = x.shape
    tm = _pick_tile(n, 512, 8)
    return pl.pallas_call(
        _final_norm_kernel,
        grid=(n // tm,),
        in_specs=[pl.BlockSpec((tm, d), lambda i: (i, 0)), pl.BlockSpec((1, d), lambda i: (0, 0))],
        out_specs=pl.BlockSpec((tm, d), lambda i: (i, 0)),
        out_shape=jax.ShapeDtypeStruct((n, d), F32),
        compiler_params=_cparams(("parallel",)),
        name="final_norm",
    )(x, g)


def _conv4(xs_ref, j_seq, w, tp):
    y = None
    for j in range(CONV_W):
        term = xs_ref[j_seq, pl.ds(SUBLANES - CONV_W + 1 + j, tp), :] * w[j:j + 1, :]
        y = term if y is None else y + term
    return y


def _make_lru_kernel(nb, gb, tp, t_real, is_prompt):
    def kern(lx_ref, ly_ref, c0_ref, h0_ref, cw_ref, cb_ref, wa_ref, ba_ref, wx_ref, bx_ref, lam_ref,
             o_ref, cn_ref, hn_ref, xs_scr, a_scr, u_scr, cc_scr, ch_scr):
        tb = pl.program_id(1)

        @pl.when(tb == 0)
        def _():
            cc_scr[...] = c0_ref[...]
            ch_scr[...] = h0_ref[...]

        cw = cw_ref[...]
        neg_sp = -LRU_C * _softplus(-lam_ref[...])
        row = lax.broadcasted_iota(jnp.int32, (tp, 1), 0)

        def per_group(g, carry):
            seqs = [g * gb + j for j in range(gb)]
            for j, sq in enumerate(seqs):
                xs_scr[j, 0:SUBLANES, :] = cc_scr[sq]
                xs_scr[j, SUBLANES:SUBLANES + tp, :] = lx_ref[sq]
            for j, sq in enumerate(seqs):
                xc = _conv4(xs_scr, j, cw, tp) + cb_ref[...]
                cc_scr[sq] = xs_scr[j, pl.ds(t_real, SUBLANES), :]
                xcb = xc.astype(BF)
                r_parts, i_parts = [], []
                for n in range(BRANCH // LRU_BD):
                    blk = xcb[:, n * LRU_BD:(n + 1) * LRU_BD]
                    r_parts.append(_dot(blk, wa_ref[n]))
                    i_parts.append(_dot(blk, wx_ref[n]))
                r = jax.nn.sigmoid(jnp.concatenate(r_parts, axis=1) + ba_ref[...])
                ig = jax.nn.sigmoid(jnp.concatenate(i_parts, axis=1) + bx_ref[...])
                log_a = r * neg_sp
                a = jnp.exp(log_a)
                mult = jnp.sqrt(-jnp.tanh(log_a) * (a * a + 1.0))
                if is_prompt:
                    mult = jnp.where(jnp.logical_and(row == 0, tb == 0), 1.0, mult)
                a_scr[j] = a
                u_scr[j] = mult * ig * xc

            def step(t, hs):
                out = []
                for j in range(gb):
                    h = a_scr[j, pl.ds(t, 1), :] * hs[j] + u_scr[j, pl.ds(t, 1), :]
                    a_scr[j, pl.ds(t, 1), :] = h
                    out.append(h)
                return tuple(out)

            h_last = lax.fori_loop(0, t_real, step, tuple(ch_scr[sq] for sq in seqs),
                                   unroll=True if t_real <= SUBLANES else 8)
            for j, sq in enumerate(seqs):
                ch_scr[sq] = h_last[j]
                o_ref[sq] = a_scr[j] * _gelu_tanh(ly_ref[sq])
            return carry

        lax.fori_loop(0, nb // gb, per_group, 0)
        cn_ref[...] = cc_scr[...]
        hn_ref[...] = ch_scr[...]

    return kern


def _mixer_specs(nb, tp, l):
    wspec = lambda shape: pl.BlockSpec((None,) + shape, lambda b, t: (l,) + (0,) * len(shape))
    pcol = lambda w, off: pl.BlockSpec((nb, tp, w), lambda b, t: (b, t, off // w))
    state = lambda *shape: pl.BlockSpec((nb,) + shape, lambda b, t: (b,) + (0,) * len(shape))
    return wspec, pcol, state


class _StateIO:
    def __init__(self, s0, stacked, l, nb):
        self.s0, self.l, self.nb = s0, l, nb
        self.stacked = stacked is not None
        self.prev = stacked[1] if self.stacked else None
        self.depth = stacked[0] if self.stacked else None

    def spec(self):
        shape = self.s0.shape[2:] if self.stacked else self.s0.shape[1:]
        zeros = (0,) * len(shape)
        if self.stacked:
            l = self.l
            return pl.BlockSpec((None, self.nb) + shape, lambda b, t: (l, b) + zeros)
        return pl.BlockSpec((self.nb,) + shape, lambda b, t: (b,) + zeros)

    def out_shape(self):
        return jax.ShapeDtypeStruct(self.s0.shape, F32)

    def extra_inputs(self):
        return [self.prev] if self.prev is not None else []

    def extra_specs(self):
        return [pl.BlockSpec(memory_space=pl.ANY)] if self.prev is not None else []

    def aliases(self, n_in, out_idx):
        return {n_in: out_idx} if self.prev is not None else {}

    def wrap(self, kern, n_in):
        if self.prev is None:
            return kern
        return lambda *refs: kern(*refs[:n_in], *refs[n_in + 1:])


def _lru(p, c0, h0, lw, l, nb, gb, tp, t_real, is_prompt):
    nseq, tseq, _ = p.shape
    c = BRANCH
    wspec, pcol, state = _mixer_specs(nb, tp, l)
    return pl.pallas_call(
        _make_lru_kernel(nb, gb, tp, t_real, is_prompt),
        grid=(nseq // nb, tseq // tp),
        in_specs=[
            pcol(c, COL_LX), pcol(c, COL_LY), state(SUBLANES, c), state(1, c),
            wspec((CONV_W, c)), wspec((1, c)),
            wspec((c // LRU_BD, LRU_BD, LRU_BD)), wspec((1, c)),
            wspec((c // LRU_BD, LRU_BD, LRU_BD)), wspec((1, c)),
            wspec((1, c)),
        ],
        out_specs=[pcol(c, 0), state(SUBLANES, c), state(1, c)],
        out_shape=[
            jax.ShapeDtypeStruct((nseq, tseq, c), F32),
            jax.ShapeDtypeStruct((nseq, SUBLANES, c), F32),
            jax.ShapeDtypeStruct((nseq, 1, c), F32),
        ],
        scratch_shapes=[
            pltpu.VMEM((gb, tp + SUBLANES, c), F32), pltpu.VMEM((gb, tp, c), F32), pltpu.VMEM((gb, tp, c), F32),
            pltpu.VMEM((nb, SUBLANES, c), F32), pltpu.VMEM((nb, 1, c), F32),
        ],
        compiler_params=_cparams(("parallel", "arbitrary")),
        name="lru_prompt" if is_prompt else "lru_sample",
    )(p, p, c0, h0, lw["conv_w"], lw["conv_b"], lw["wa"], lw["ba"], lw["wx"], lw["bx"], lw["lam"])


def _chunk_consts(c, cl):
    row = lax.broadcasted_iota(jnp.int32, (c, c), 0)
    col = lax.broadcasted_iota(jnp.int32, (c, c), 1)
    tril = row >= col
    lower_ones = jnp.where(tril, 1.0, 0.0).astype(BF)
    real = lax.broadcasted_iota(jnp.int32, (c, 1), 0) < cl
    return row, col, tril, lower_ones, real


def _ks(h):
    return slice(h * DK, (h + 1) * DK)


def _vs(h):
    return slice(h * DV, (h + 1) * DV)


def _make_gla_kernel(nb, gb, tp, c, t_real):
    nc = tp // c
    cl = min(c, t_real)
    G, H = range(gb), range(HEADS)

    def kern(q_ref, k_ref, v_ref, gr_ref, sm_ref, s0_ref, wg_ref, bg_ref, ng_ref, o_ref, sn_ref, s_scr):
        @pl.when(pl.program_id(1) == 0)
        def _():
            s_scr[...] = s0_ref[...]

        _, _, tril, lower_ones, real = _chunk_consts(c, cl)
        ones_real = jnp.where(real, 1.0, 0.0).astype(BF) * jnp.ones((c, DV), BF)
        d_tn = lambda x: lax.dot_general(x, ones_real, (((0,), (0,)), ((), ())), preferred_element_type=F32)
        ng = ng_ref[...]
        wg = wg_ref[...]
        bg = bg_ref[...]

        def per_chunk(idx, carry):
            g = idx // nc
            ci = idx - g * nc
            rows = pl.ds(pl.multiple_of(ci * c, SUBLANES), c)
            seqs = [g * gb + j for j in G]
            lg = [-_softplus(-(_mm(sm_ref[sq, rows, :], wg) + bg)) * (1.0 / GLA_TAU) for sq in seqs]
            sp = [_split3(x) for x in lg]
            b = [_dot(lower_ones, s[0]) + _dot(lower_ones, s[1]) + _dot(lower_ones, s[2]) for s in sp]
            b_last_col = [d_tn(s[0]) + d_tn(s[1]) + d_tn(s[2]) for s in sp]
            q_t = [q_ref[sq, rows, :] * (DK ** -0.5) * jnp.exp(b[j]) for j, sq in enumerate(seqs)]
            k = [k_ref[sq, rows, :] for sq in seqs]
            k_t = [jnp.where(real, k[j] * jnp.exp(-b[j]), 0.0) for j in G]
            k_d = [jnp.where(real, k[j] * jnp.exp(b[j][cl - 1:cl, :] - b[j]), 0.0) for j in G]
            v = [v_ref[sq, rows, :] for sq in seqs]
            att = [[jnp.where(tril, _mm_nt(q_t[j][:, _ks(h)], k_t[j][:, _ks(h)]), 0.0) for h in H] for j in G]
            s = [[s_scr[seqs[j], h] for h in H] for j in G]
            kv = [[_mm_tn(k_d[j][:, _ks(h)], v[j][:, _vs(h)]) for h in H] for j in G]
            o = [[_mm(att[j][h], v[j][:, _vs(h)]) + _mm(q_t[j][:, _ks(h)], s[j][h]) for h in H] for j in G]
            for j in G:
                for h in H:
                    s_scr[seqs[j], h] = jnp.exp(b_last_col[j][_ks(h), :]) * s[j][h] + kv[j][h]
            for j in G:
                gr = gr_ref[seqs[j], rows, :]
                for h in H:
                    o_ref[seqs[j], rows, _vs(h)] = _rms(o[j][h], ng) * _silu(gr[:, _vs(h)])
            return carry

        lax.fori_loop(0, (nb // gb) * nc, per_chunk, 0)
        sn_ref[...] = s_scr[...]

    return kern


def _gla(p, s0, lw, l, nb, gb, tp, c, t_real, name, stacked=None):
    nseq, tseq, _ = p.shape
    qk = HEADS * DK
    vw = HEADS * DV
    wspec, pcol, _ = _mixer_specs(nb, tp, l)
    sio = _StateIO(s0, stacked, l, nb)
    in_specs = [
        pcol(qk, COL_GQ), pcol(qk, COL_GK), pcol(vw, COL_GV), pcol(vw, COL_GR), pcol(128, COL_SM),
        sio.spec(), wspec((128, qk)), wspec((1, qk)), wspec((1, DV)),
    ]
    n_in = len(in_specs)
    return pl.pallas_call(
        sio.wrap(_make_gla_kernel(nb, gb, tp, c, t_real), n_in),
        grid=(nseq // nb, tseq // tp),
        in_specs=in_specs + sio.extra_specs(),
        out_specs=[pcol(vw, 0), sio.spec()],
        out_shape=[jax.ShapeDtypeStruct((nseq, tseq, vw), F32), sio.out_shape()],
        scratch_shapes=[pltpu.VMEM((nb, HEADS, DK, DV), F32)],
        input_output_aliases=sio.aliases(n_in, 1),
        compiler_params=_cparams(("parallel", "arbitrary")),
        name=name,
    )(p, p, p, p, p, s0, lw["gla_wg"], lw["gla_bg"], lw["gla_ng"], *sio.extra_inputs())


def _make_ret_kernel(nb, gb, tp, c, t_real):
    nc = tp // c
    cl = min(c, t_real)
    G, H = range(gb), range(HEADS)
    log_gamma = [float(np.log(np.float32(1.0) - np.float32(2.0) ** np.float32(-5.0 - h))) for h in range(HEADS)]

    def kern(q_ref, k_ref, qs_ref, ks_ref, v_ref, g_ref, cos_ref, sin_ref, s0_ref, ng_ref, o_ref, sn_ref, s_scr):
        @pl.when(pl.program_id(1) == 0)
        def _():
            s_scr[...] = s0_ref[...]

        row, col, tril, _, real = _chunk_consts(c, cl)
        diff = (row - col).astype(F32)
        ridx = lax.broadcasted_iota(jnp.int32, (c, 1), 0).astype(F32)
        ng = ng_ref[...]
        dmat = [jnp.where(tril, jnp.exp(jnp.maximum(diff, 0.0) * lgm), 0.0) for lgm in log_gamma]
        q_decay = [jnp.exp((ridx + 1.0) * lgm) for lgm in log_gamma]
        k_decay = [jnp.exp((cl - 1.0 - ridx) * lgm) for lgm in log_gamma]

        def per_chunk(idx, carry):
            g = idx // nc
            ci = idx - g * nc
            rows = pl.ds(pl.multiple_of(ci * c, SUBLANES), c)
            seqs = [g * gb + j for j in G]
            cos = cos_ref[rows, :]
            sin = sin_ref[rows, :]
            q = [q_ref[sq, rows, :] * cos + qs_ref[sq, rows, :] * sin for sq in seqs]
            k = [jnp.where(real, (k_ref[sq, rows, :] * cos + ks_ref[sq, rows, :] * sin) * (DK ** -0.5), 0.0)
                 for sq in seqs]
            v = [v_ref[sq, rows, :] for sq in seqs]
            att = [[_mm_nt(q[j][:, _ks(h)], k[j][:, _ks(h)]) * dmat[h] for h in H] for j in G]
            s = [[s_scr[seqs[j], h] for h in H] for j in G]
            kv = [[_mm_tn(k[j][:, _ks(h)] * k_decay[h], v[j][:, _vs(h)]) for h in H] for j in G]
            o = [[_mm(att[j][h], v[j][:, _vs(h)]) + _mm(q[j][:, _ks(h)] * q_decay[h], s[j][h]) for h in H]
                 for j in G]
            for j in G:
                for h in H:
                    s_scr[seqs[j], h] = math.exp(cl * log_gamma[h]) * s[j][h] + kv[j][h]
            for j in G:
                gate = g_ref[seqs[j], rows, :]
                for h in H:
                    oc = o[j][h] - jnp.mean(o[j][h], axis=-1, keepdims=True)
                    on = oc * lax.rsqrt(jnp.mean(oc * oc, axis=-1, keepdims=True) + EPS) * ng
                    o_ref[seqs[j], rows, _vs(h)] = on * _silu(gate[:, _vs(h)])
            return carry

        lax.fori_loop(0, (nb // gb) * nc, per_chunk, 0)
        sn_ref[...] = s_scr[...]

    return kern


def _ret(p, s0, cos_t, sin_t, lw, l, nb, gb, tp, c, t_real, name, stacked=None):
    nseq, tseq, _ = p.shape
    qk = HEADS * DK
    vw = HEADS * DV
    wspec, pcol, _ = _mixer_specs(nb, tp, l)
    sio = _StateIO(s0, stacked, l, nb)
    in_specs = [
        pcol(qk, COL_RQ), pcol(qk, COL_RK), pcol(qk, COL_RQS), pcol(qk, COL_RKS), pcol(vw, COL_RV),
        pcol(vw, COL_RG),
        pl.BlockSpec((tp, qk), lambda b, t: (t, 0)),
        pl.BlockSpec((tp, qk), lambda b, t: (t, 0)),
        sio.spec(), wspec((1, DV)),
    ]
    n_in = len(in_specs)
    return pl.pallas_call(
        sio.wrap(_make_ret_kernel(nb, gb, tp, c, t_real), n_in),
        grid=(nseq // nb, tseq // tp),
        in_specs=in_specs + sio.extra_specs(),
        out_specs=[pcol(vw, 0), sio.spec()],
        out_shape=[jax.ShapeDtypeStruct((nseq, tseq, vw), F32), sio.out_shape()],
        scratch_shapes=[pltpu.VMEM((nb, HEADS, DK, DV), F32)],
        input_output_aliases=sio.aliases(n_in, 1),
        compiler_params=_cparams(("parallel", "arbitrary")),
        name=name,
    )(p, p, p, p, p, p, cos_t, sin_t, s0, lw["ret_ng"], *sio.extra_inputs())


def _unit_lower_inverse_many(ms, c):
    row = lax.broadcasted_iota(jnp.int32, (c, c), 0)
    col = lax.broadcasted_iota(jnp.int32, (c, c), 1)
    eye = jnp.where(row == col, 1.0, 0.0)

    def same_block(bits):
        return lax.shift_right_logical(row, bits) == lax.shift_right_logical(col, bits)

    in8 = same_block(3)
    n1 = [jnp.where(in8, m, 0.0) for m in ms]
    n2 = [_mm_hp(x, x) for x in n1]
    n4 = [_mm_hp(x, x) for x in n2]
    d = [eye - x for x in n1]
    d = [x + _mm_hp(x, y) for x, y in zip(d, n2)]
    d = [x + _mm_hp(x, y) for x, y in zip(d, n4)]
    bits = 3
    while (1 << bits) < c:
        sel = jnp.logical_and(same_block(bits + 1), jnp.logical_not(same_block(bits)))
        ld = [_mm_hp(jnp.where(sel, m, 0.0), x) for m, x in zip(ms, d)]
        d = [x - _mm_hp(x, y) for x, y in zip(d, ld)]
        bits += 1
    return d


def _make_dn_kernel(nb, gb, tp, c, t_real):
    nc = tp // c
    cl = min(c, t_real)
    cw3 = 3 * BRANCH
    G, H = range(gb), range(HEADS)
    GH = [(j, h) for j in G for h in H]

    def kern(x_ref, z_ref, sm_ref, c0_ref, s0_ref, cw_ref, alog_ref, dtb_ref, ng_ref,
             o_ref, cn_ref, sn_ref, xs_scr, qkv_scr, cc_scr, s_scr):
        @pl.when(pl.program_id(1) == 0)
        def _():
            cc_scr[...] = c0_ref[...]
            s_scr[...] = s0_ref[...]

        row, col, tril, lower_ones, real = _chunk_consts(c, cl)
        strict = row > col
        lane = lax.broadcasted_iota(jnp.int32, (c, 128), 1)
        sel = [jnp.where(lane == SM_DA + h, 1.0, 0.0).astype(BF) for h in H]
        d_nt = lambda a, x: lax.dot_general(a, x, (((1,), (1,)), ((), ())), preferred_element_type=F32)
        ng = ng_ref[...]
        cw = cw_ref[...]
        neg_a = -jnp.exp(alog_ref[...])
        dtb = dtb_ref[...]

        def per_group(g, carry):
            seqs = [g * gb + j for j in G]
            for j, sq in enumerate(seqs):
                xs_scr[j, 0:SUBLANES, :] = cc_scr[sq]
                xs_scr[j, SUBLANES:SUBLANES + tp, :] = x_ref[sq]
            for j, sq in enumerate(seqs):
                qkv_scr[j] = _silu(_conv4(xs_scr, j, cw, tp))
                cc_scr[sq] = xs_scr[j, pl.ds(t_real, SUBLANES), :]

            def per_chunk(ci, carry2):
                rows = pl.ds(pl.multiple_of(ci * c, SUBLANES), c)
                sm = [sm_ref[sq, rows, :] for sq in seqs]
                gcum = [_cumsum_rows(neg_a * _softplus(x + dtb), lower_ones) for x in sm]
                g3 = [_split3(x) for x in gcum]
                beta_all = [jnp.where(real, jax.nn.sigmoid(x), 0.0) for x in sm]

                def head_in(j, h, part):
                    x = qkv_scr[j, rows, part * BRANCH + h * DV:part * BRANCH + (h + 1) * DV]
                    return x

                qh = {jh: head_in(*jh, 0) for jh in GH}
                kh = {jh: head_in(*jh, 1) for jh in GH}
                vh = {jh: head_in(*jh, 2) for jh in GH}
                qh = {jh: x * lax.rsqrt(jnp.sum(x * x, axis=-1, keepdims=True) + EPS) * (DV ** -0.5)
                      for jh, x in qh.items()}
                kh = {jh: jnp.where(real, x * lax.rsqrt(jnp.sum(x * x, axis=-1, keepdims=True) + EPS), 0.0)
                      for jh, x in kh.items()}
                beta = {(j, h): beta_all[j][:, SM_DB + h:SM_DB + h + 1] for j, h in GH}
                gcol = {(j, h): gcum[j][:, SM_DA + h:SM_DA + h + 1] for j, h in GH}
                grow = {(j, h): d_nt(sel[h], g3[j][0]) + d_nt(sel[h], g3[j][1]) + d_nt(sel[h], g3[j][2])
                        for j, h in GH}
                decay = {jh: jnp.where(tril, jnp.exp(jnp.where(tril, gcol[jh] - grow[jh], 0.0)), 0.0) for jh in GH}
                e_g = {jh: jnp.exp(gcol[jh]) for jh in GH}
                kb = {jh: kh[jh] * beta[jh] for jh in GH}
                kk = {jh: _mm_nt(kb[jh], kh[jh]) for jh in GH}
                qk = {jh: _mm_nt(qh[jh], kh[jh]) for jh in GH}
                tm = _unit_lower_inverse_many([jnp.where(strict, kk[jh] * decay[jh], 0.0) for jh in GH], c)
                tm = dict(zip(GH, tm))
                u = {jh: _mm(tm[jh], vh[jh] * beta[jh]) for jh in GH}
                w = {jh: _mm(tm[jh], kb[jh] * e_g[jh]) for jh in GH}
                s = {(j, h): s_scr[seqs[j], h] for j, h in GH}
                ws = {jh: _mm(w[jh], s[jh]) for jh in GH}
                qs = {jh: _mm(qh[jh] * e_g[jh], s[jh]) for jh in GH}
                v_new = {jh: u[jh] - ws[jh] for jh in GH}
                o = {jh: qs[jh] + _mm(qk[jh] * decay[jh], v_new[jh]) for jh in GH}
                for j, h in GH:
                    g_last = gcol[(j, h)][cl - 1:cl, :]
                    k_d = kh[(j, h)] * jnp.exp(g_last - gcol[(j, h)])
                    s_scr[seqs[j], h] = jnp.exp(g_last) * s[(j, h)] + _mm_tn(k_d, v_new[(j, h)])
                for j in G:
                    z = z_ref[seqs[j], rows, :]
                    for h in H:
                        o_ref[seqs[j], rows, _vs(h)] = _rms(o[(j, h)], ng) * _silu(z[:, _vs(h)])
                return carry2

            lax.fori_loop(0, nc, per_chunk, 0)
            return carry

        lax.fori_loop(0, nb // gb, per_group, 0)
        cn_ref[...] = cc_scr[...]
        sn_ref[...] = s_scr[...]

    return kern


def _dn(p, c0, s0, lw, l, nb, gb, tp, c, t_real, name, stacked=None):
    nseq, tseq, _ = p.shape
    cw3 = 3 * BRANCH
    wspec, pcol, state = _mixer_specs(nb, tp, l)
    sio = _StateIO(s0, stacked, l, nb)
    in_specs = [
        pcol(cw3, COL_DQKV), pcol(BRANCH, COL_DZ), pcol(128, COL_SM),
        state(SUBLANES, cw3), sio.spec(),
        wspec((CONV_W, cw3)), wspec((1, 128)), wspec((1, 128)), wspec((1, DV)),
    ]
    n_in = len(in_specs)
    return pl.pallas_call(
        sio.wrap(_make_dn_kernel(nb, gb, tp, c, t_real), n_in),
        grid=(nseq // nb, tseq // tp),
        in_specs=in_specs + sio.extra_specs(),
        out_specs=[pcol(BRANCH, 0), state(SUBLANES, cw3), sio.spec()],
        out_shape=[
            jax.ShapeDtypeStruct((nseq, tseq, BRANCH), F32),
            jax.ShapeDtypeStruct((nseq, SUBLANES, cw3), F32),
            sio.out_shape(),
        ],
        scratch_shapes=[
            pltpu.VMEM((gb, tp + SUBLANES, cw3), F32), pltpu.VMEM((gb, tp, cw3), F32),
            pltpu.VMEM((nb, SUBLANES, cw3), F32), pltpu.VMEM((nb, HEADS, DV, DV), F32),
        ],
        input_output_aliases=sio.aliases(n_in, 2),
        compiler_params=_cparams(("parallel", "arbitrary")),
        name=name,
    )(p, p, p, c0, s0, lw["dn_conv_w"], lw["dn_alog"], lw["dn_dtb"], lw["dn_ng"], *sio.extra_inputs())


IN_SIZES = (BRANCH, BRANCH, HEADS * DK, HEADS * DK, BRANCH, BRANCH, GLA_RANK,
            HEADS * DK, HEADS * DK, BRANCH, BRANCH, BRANCH, BRANCH, BRANCH, BRANCH, HEADS, HEADS)


def _pack_kernel(w_ref, o_ref):
    w = w_ref[...]
    offs = np.concatenate([[0], np.cumsum(IN_SIZES)])
    seg = lambda i: w[:, int(offs[i]):int(offs[i + 1])]
    (lx, ly, gq, gk, gv, gr, glr, rq, rk, rv, rg, dq, dk, dv, dz, db, da) = [seg(i) for i in range(len(IN_SIZES))]

    def swap_halves(x):
        parts = []
        for h in range(HEADS):
            parts += [x[:, h * DK + DK // 2:(h + 1) * DK], x[:, h * DK:h * DK + DK // 2]]
        return jnp.concatenate(parts, axis=1)

    used = COL_SM + GLA_RANK + 2 * HEADS
    tail = jnp.zeros((w.shape[0], N_PACK - used), w.dtype)
    packed = jnp.concatenate([lx, ly, gq, gk, gv, gr, rq, rk, swap_halves(rq), swap_halves(rk), rv, rg,
                              dq, dk, dv, dz, glr, db, da, tail], axis=1)
    o_ref[...] = packed.astype(BF)


def _pack_w_in(w_in):
    depth, d, n_in = w_in.shape
    assert n_in == sum(IN_SIZES)
    td = _pick_tile(d, 256, 16)
    return pl.pallas_call(
        _pack_kernel,
        grid=(depth, d // td),
        in_specs=[pl.BlockSpec((None, td, n_in), lambda l, i: (l, i, 0))],
        out_specs=pl.BlockSpec((None, td, N_PACK), lambda l, i: (l, i, 0)),
        out_shape=jax.ShapeDtypeStruct((depth, d, N_PACK), BF),
        compiler_params=_cparams(("parallel", "parallel")),
        name="pack_w_in",
    )(w_in)


def _rope_tables(pos):
    half = DK // 2
    inv = ROPE_BASE ** (-jnp.arange(half, dtype=F32) / half)
    ang = pos.astype(F32)[:, None] * inv[None, :]
    cos, sin = jnp.cos(ang), jnp.sin(ang)
    cos_t = jnp.tile(jnp.concatenate([cos, cos], axis=1), (1, HEADS))
    sin_t = jnp.tile(jnp.concatenate([-sin, sin], axis=1), (1, HEADS))
    return cos_t, sin_t


def _lane_row(vals, off):
    depth = vals.shape[0]
    return jnp.zeros((depth, 1, 128), F32).at[:, 0, off:off + HEADS].set(vals.astype(F32))


def kernel(x_prompt, x_sample, state_lru_conv, state_lru_h, state_gla, state_ret, state_dn_conv, state_dn,
           norm_g, final_norm_g, w_ff_in, w_ff_out, w_in, w_gate, w_branch, w_out,
           lru_conv_w, lru_conv_b, lru_wa, lru_ba, lru_wx, lru_bx, lru_lambda,
           gla_wg, gla_bg, gla_norm_g, ret_norm_g, dn_conv_w, dn_a_log, dn_dt_bias, dn_norm_g):
    bp, tpr, d = x_prompt.shape
    bs, ts, _ = x_sample.shape
    depth = w_in.shape[0]
    n_p, n_s = bp * tpr, bs * ts
    assert d == D_MODEL and ts <= SAMPLE_PAD and tpr % CHUNK == 0

    w_pack = _pack_w_in(w_in)
    w_branch_b = w_branch.astype(BF)
    w_out_b = w_out.astype(BF)
    wg_pad = jnp.zeros((depth, 128, HEADS * DK), F32).at[:, SM_GLR:SM_GLR + GLA_RANK, :].set(gla_wg).astype(BF)
    r3 = lambda a: a.reshape(a.shape[0], 1, a.shape[1])
    lw = {
        "conv_w": lru_conv_w, "conv_b": r3(lru_conv_b), "wa": lru_wa.astype(BF), "ba": r3(lru_ba),
        "wx": lru_wx.astype(BF), "bx": r3(lru_bx), "lam": r3(lru_lambda),
        "gla_wg": wg_pad, "gla_bg": r3(gla_bg), "gla_ng": r3(gla_norm_g), "ret_ng": r3(ret_norm_g),
        "dn_conv_w": dn_conv_w, "dn_alog": _lane_row(dn_a_log, SM_DA), "dn_dtb": _lane_row(dn_dt_bias, SM_DA),
        "dn_ng": r3(dn_norm_g),
    }
    norm_g4 = norm_g.reshape(depth, 3, 1, d)

    cos_p, sin_p = _rope_tables(jnp.arange(tpr))
    pos_s = jnp.where(jnp.arange(SAMPLE_PAD) < ts, PAST_LEN + jnp.arange(SAMPLE_PAD), 0)
    cos_s, sin_s = _rope_tables(pos_s)

    pad_conv = lambda a: jnp.pad(a, ((0, 0), (SUBLANES - (CONV_W - 1), 0), (0, 0)))
    zeros = lambda *s: jnp.zeros(s, F32)

    tp_p = _pick_tile(tpr, 256, CHUNK)
    a_p = dict(nb=bp, gb=bp, tp=tp_p)
    nb_s = _pick_tile(bs, 16, 1)
    a_s = dict(nb=nb_s, gb=_pick_tile(nb_s, 4, 1), tp=SAMPLE_PAD)

    xp = x_prompt.reshape(n_p, d)
    xs = x_sample.reshape(n_s, d)
    new_p = [[] for _ in range(6)]
    new_s = [[] for _ in range(6)]
    s_gla_s = s_ret_s = s_dn_s = None
    for l in range(depth):
        xs, *w_ffn = _ffn_cast(xs, norm_g4[l, 0], w_ff_in, w_ff_out, l, 0)
        xp = _ffn(xp, norm_g4[l, 0], *w_ffn)
        p_p = _inproj(xp, norm_g4[l, 1], w_pack, l).reshape(bp, tpr, N_PACK)
        p_s = _inproj(xs, norm_g4[l, 1], w_pack, l).reshape(bs, ts, N_PACK)
        p_s = jnp.pad(p_s, ((0, 0), (0, SAMPLE_PAD - ts), (0, 0)))

        o_lru_p, c_lru_p, h_lru_p = _lru(p_p, zeros(bp, SUBLANES, BRANCH), zeros(bp, 1, BRANCH), lw, l,
                                         t_real=tp_p, is_prompt=True, **a_p)
        o_gla_p, s_gla_p = _gla(p_p, zeros(bp, HEADS, DK, DV), lw, l, c=CHUNK, t_real=tp_p, name="gla_prompt", **a_p)
        o_ret_p, s_ret_p = _ret(p_p, zeros(bp, HEADS, DK, DV), cos_p, sin_p, lw, l, c=CHUNK, t_real=tp_p,
                                name="ret_prompt", **a_p)
        o_dn_p, c_dn_p, s_dn_p = _dn(p_p, zeros(bp, SUBLANES, 3 * BRANCH), zeros(bp, HEADS, DV, DV), lw, l,
                                     c=CHUNK, t_real=tp_p, name="dn_prompt", **a_p)

        o_lru_s, c_lru_s, h_lru_s = _lru(p_s, pad_conv(state_lru_conv[l]), state_lru_h[l][:, None, :], lw, l,
                                         t_real=ts, is_prompt=False, **a_s)
        o_gla_s, s_gla_s = _gla(p_s, state_gla, lw, l, c=SAMPLE_PAD, t_real=ts, name="gla_sample",
                                stacked=(depth, s_gla_s), **a_s)
        o_ret_s, s_ret_s = _ret(p_s, state_ret, cos_s, sin_s, lw, l, c=SAMPLE_PAD, t_real=ts, name="ret_sample",
                                stacked=(depth, s_ret_s), **a_s)
        o_dn_s, c_dn_s, s_dn_s = _dn(p_s, pad_conv(state_dn_conv[l]), state_dn, lw, l, c=SAMPLE_PAD, t_real=ts,
                                     name="dn_sample", stacked=(depth, s_dn_s), **a_s)

        flat_p = lambda o: o.reshape(n_p, BRANCH)
        flat_s = lambda o: o[:, :ts].reshape(n_s, BRANCH)
        xs, w_gate_l = _merge_cast(xs, norm_g4[l, 1], [flat_s(o) for o in (o_lru_s, o_gla_s, o_ret_s, o_dn_s)],
                                   w_gate, w_branch_b, w_out_b, l)
        xp = _merge(xp, norm_g4[l, 1], [flat_p(o) for o in (o_lru_p, o_gla_p, o_ret_p, o_dn_p)],
                    w_gate_l, w_branch_b, w_out_b, l)
        xs, *w_ffn = _ffn_cast(xs, norm_g4[l, 2], w_ff_in, w_ff_out, l, 1)
        xp = _ffn(xp, norm_g4[l, 2], *w_ffn)

        tail3 = lambda cwin: cwin[:, SUBLANES - (CONV_W - 1):, :]
        for i, v in enumerate((tail3(c_lru_p), h_lru_p[:, 0], s_gla_p, s_ret_p, tail3(c_dn_p), s_dn_p)):
            new_p[i].append(v)
        for i, v in ((0, tail3(c_lru_s)), (1, h_lru_s[:, 0]), (4, tail3(c_dn_s))):
            new_s[i].append(v)

    fg = final_norm_g.reshape(1, d)
    y_prompt = _final_norm(xp, fg).reshape(bp, tpr, d)
    y_sample = _final_norm(xs, fg).reshape(bs, ts, d)
    sp = [jnp.stack(v) for v in new_p]
    s_small = {i: jnp.stack(new_s[i]) for i in (0, 1, 4)}
    return tuple([y_prompt, y_sample] + sp + [s_small[0], s_small[1], s_gla_s, s_ret_s, s_small[4], s_dn_s])
```

```python
import functools
import math

import numpy as np
import jax
import jax.numpy as jnp
from jax import lax
from jax.experimental import pallas as pl
from jax.experimental.pallas import tpu as pltpu

F32 = jnp.float32
BF = jnp.bfloat16
EPS = 1e-6

D_MODEL = 2048
BRANCH = 512
CONV_W = 4
HEADS = 4
DK = 64
DV = 128
GLA_RANK = 16
GLA_TAU = 16.0
LRU_C = 8.0
LRU_BD = 128
ROPE_BASE = 10000.0
CHUNK = 64
PAST_LEN = 16384

V7X_VMEM_BYTES = 64 * 1024 * 1024
VMEM_LIMIT = V7X_VMEM_BYTES - 8 * 1024 * 1024
SUBLANES = 8
SAMPLE_PAD = 8

COL_LX, COL_LY = 0, 512
COL_GQ, COL_GK, COL_GV, COL_GR = 1024, 1280, 1536, 2048
COL_RQ, COL_RK, COL_RQS, COL_RKS, COL_RV, COL_RG = 2560, 2816, 3072, 3328, 3584, 4096
COL_DQKV, COL_DZ = 4608, 6144
COL_SM = 6656
SM_GLR, SM_DB, SM_DA = 0, 16, 20
N_PACK = 6912


def _dot(a, b):
    return jnp.dot(a, b, preferred_element_type=F32)


def _mm(a, b):
    return _dot(a.astype(BF), b.astype(BF))


def _mm_nt(a, b):
    return lax.dot_general(a.astype(BF), b.astype(BF), (((1,), (1,)), ((), ())), preferred_element_type=F32)


def _mm_tn(a, b):
    return lax.dot_general(a.astype(BF), b.astype(BF), (((0,), (0,)), ((), ())), preferred_element_type=F32)


def _split3(x):
    hi = x.astype(BF)
    r = x - hi.astype(F32)
    mid = r.astype(BF)
    lo = (r - mid.astype(F32)).astype(BF)
    return hi, mid, lo


def _split2(x):
    hi = x.astype(BF)
    return hi, (x - hi.astype(F32)).astype(BF)


def _mm_hp(a, b):
    ah, al = _split2(a)
    bh, bl = _split2(b)
    return _dot(ah, bh) + _dot(ah, bl) + _dot(al, bh)


def _cumsum_rows(x, lower_ones):
    hi, mid, lo = _split3(x)
    return _dot(lower_ones, hi) + _dot(lower_ones, mid) + _dot(lower_ones, lo)


def _softplus(x):
    return jnp.maximum(x, 0.0) + jnp.log1p(jnp.exp(-jnp.abs(x)))


def _silu(x):
    return x * jax.nn.sigmoid(x)


def _gelu_tanh(x):
    return x * (0.5 * (1.0 + jnp.tanh(math.sqrt(2.0 / math.pi) * (x + 0.044715 * (x * x * x)))))


def _rms(x, g):
    return x * lax.rsqrt(jnp.mean(x * x, axis=-1, keepdims=True) + EPS) * g


def _pick_tile(n, target, align):
    best = None
    for t in range(align, min(n, target) + 1, align):
        if n % t == 0:
            best = t
    assert best is not None, (n, target, align)
    return best


def _cparams(sem):
    return pltpu.CompilerParams(dimension_semantics=sem, vmem_limit_bytes=VMEM_LIMIT)


def _ffn_kernel(x_ref, g_ref, wg_ref, wu_ref, wo_ref, o_ref, hn_ref):
    @pl.when(pl.program_id(1) == 0)
    def _():
        x = x_ref[...]
        hn_ref[...] = _rms(x, g_ref[...]).astype(BF)
        o_ref[...] = x

    hn = hn_ref[...]
    gate = _dot(hn, wg_ref[...])
    up = _dot(hn, wu_ref[...])
    act = (_silu(gate) * up * 0.5).astype(BF)
    o_ref[...] += _dot(act, wo_ref[...])


def _ffn(x, g, wg, wu, wo):
    n, d = x.shape
    f = wo.shape[0]
    tm = _pick_tile(n, 1024, 16)
    tf = _pick_tile(f, 512, 128)
    return pl.pallas_call(
        _ffn_kernel,
        grid=(n // tm, f // tf),
        in_specs=[
            pl.BlockSpec((tm, d), lambda i, j: (i, 0)),
            pl.BlockSpec((1, d), lambda i, j: (0, 0)),
            pl.BlockSpec((d, tf), lambda i, j: (0, j)),
            pl.BlockSpec((d, tf), lambda i, j: (0, j)),
            pl.BlockSpec((tf, d), lambda i, j: (j, 0)),
        ],
        out_specs=pl.BlockSpec((tm, d), lambda i, j: (i, 0)),
        out_shape=jax.ShapeDtypeStruct((n, d), F32),
        scratch_shapes=[pltpu.VMEM((tm, d), BF)],
        compiler_params=_cparams(("parallel", "arbitrary")),
        name="ffn",
    )(x, g, wg, wu, wo)


def _ffn_cast_kernel(x_ref, g_ref, wg32_ref, wu32_ref, wo32_ref, o_ref, wg_ref, wu_ref, wo_ref, hn_ref):
    @pl.when(pl.program_id(1) == 0)
    def _():
        x = x_ref[...]
        hn_ref[...] = _rms(x, g_ref[...]).astype(BF)
        o_ref[...] = x

    wg_ref[...] = wg32_ref[...].astype(BF)
    wu_ref[...] = wu32_ref[...].astype(BF)
    wo_ref[...] = wo32_ref[...].astype(BF)
    hn = hn_ref[...]
    act = (_silu(_dot(hn, wg_ref[...])) * _dot(hn, wu_ref[...]) * 0.5).astype(BF)
    o_ref[...] += _dot(act, wo_ref[...])


def _ffn_cast(x, g, w_in, w_out, l, s):
    n, d = x.shape
    f = w_out.shape[2]
    tf = _pick_tile(f, 512, 128)
    nf = f // tf
    once = dict(pipeline_mode=pl.Buffered(1))
    return pl.pallas_call(
        _ffn_cast_kernel,
        grid=(1, nf),
        in_specs=[
            pl.BlockSpec((n, d), lambda i, j: (0, 0), **once),
            pl.BlockSpec((1, d), lambda i, j: (0, 0)),
            pl.BlockSpec((None, None, d, tf), lambda i, j: (l, s, 0, j)),
            pl.BlockSpec((None, None, d, tf), lambda i, j: (l, s, 0, j + nf)),
            pl.BlockSpec((None, None, tf, d), lambda i, j: (l, s, j, 0)),
        ],
        out_specs=[
            pl.BlockSpec((n, d), lambda i, j: (0, 0), **once),
            pl.BlockSpec((d, tf), lambda i, j: (0, j)),
            pl.BlockSpec((d, tf), lambda i, j: (0, j)),
            pl.BlockSpec((tf, d), lambda i, j: (j, 0)),
        ],
        out_shape=[
            jax.ShapeDtypeStruct((n, d), F32),
            jax.ShapeDtypeStruct((d, f), BF), jax.ShapeDtypeStruct((d, f), BF), jax.ShapeDtypeStruct((f, d), BF),
        ],
        scratch_shapes=[pltpu.VMEM((n, d), BF)],
        compiler_params=_cparams(("arbitrary", "arbitrary")),
        name="ffn_cast",
    )(x, g, w_in, w_in, w_out)


def _inproj_kernel(x_ref, g_ref, w_ref, o_ref, hn_ref):
    @pl.when(pl.program_id(1) == 0)
    def _():
        hn_ref[...] = _rms(x_ref[...], g_ref[...]).astype(BF)

    o_ref[...] = _dot(hn_ref[...], w_ref[...])


def _inproj(x, g, w_pack, l):
    n, d = x.shape
    npk = w_pack.shape[2]
    tm = _pick_tile(n, 1024, 16)
    tn = _pick_tile(npk, 1024, 128)
    return pl.pallas_call(
        _inproj_kernel,
        grid=(n // tm, npk // tn),
        in_specs=[
            pl.BlockSpec((tm, d), lambda i, j: (i, 0)),
            pl.BlockSpec((1, d), lambda i, j: (0, 0)),
            pl.BlockSpec((None, d, tn), lambda i, j: (l, 0, j)),
        ],
        out_specs=pl.BlockSpec((tm, tn), lambda i, j: (i, j)),
        out_shape=jax.ShapeDtypeStruct((n, npk), F32),
        scratch_shapes=[pltpu.VMEM((tm, d), BF)],
        compiler_params=_cparams(("parallel", "arbitrary")),
        name="inproj",
    )(x, g, w_pack)


def _merge_kernel(x_ref, g_ref, b0_ref, b1_ref, b2_ref, b3_ref, wgate_ref, wbr_ref, wo_ref, o_ref, hn_ref):
    @pl.when(pl.program_id(1) == 0)
    def _():
        x = x_ref[...]
        hn_ref[...] = _rms(x, g_ref[...]).astype(BF)
        o_ref[...] = x

    hn = hn_ref[...]
    m = None
    for n, b_ref in enumerate((b0_ref, b1_ref, b2_ref, b3_ref)):
        gate = jax.nn.sigmoid(_dot(hn, wgate_ref[n]))
        br = _dot(b_ref[...].astype(BF), wbr_ref[n])
        m = gate * br if m is None else m + gate * br
    o_ref[...] += _dot(m.astype(BF), wo_ref[...])


def _merge(x, g, branches, w_gate_l, w_branch, w_out, l):
    n, d = x.shape
    tm = _pick_tile(n, 512, 16)
    tn = 256
    return pl.pallas_call(
        _merge_kernel,
        grid=(n // tm, d // tn),
        in_specs=[
            pl.BlockSpec((tm, d), lambda i, j: (i, 0)),
            pl.BlockSpec((1, d), lambda i, j: (0, 0)),
        ] + [pl.BlockSpec((tm, BRANCH), lambda i, j: (i, 0))] * HEADS + [
            pl.BlockSpec((HEADS, d, tn), lambda i, j: (0, 0, j)),
            pl.BlockSpec((None, HEADS, BRANCH, tn), lambda i, j: (l, 0, 0, j)),
            pl.BlockSpec((None, tn, d), lambda i, j: (l, j, 0)),
        ],
        out_specs=pl.BlockSpec((tm, d), lambda i, j: (i, 0)),
        out_shape=jax.ShapeDtypeStruct((n, d), F32),
        scratch_shapes=[pltpu.VMEM((tm, d), BF)],
        compiler_params=_cparams(("parallel", "arbitrary")),
        name="merge",
    )(x, g, *branches, w_gate_l, w_branch, w_out)


def _merge_cast_kernel(x_ref, g_ref, b0_ref, b1_ref, b2_ref, b3_ref, wgate32_ref, wbr_ref, wo_ref,
                       o_ref, wgate_ref, hn_ref):
    wgate_ref[...] = wgate32_ref[...].astype(BF)
    _merge_kernel(x_ref, g_ref, b0_ref, b1_ref, b2_ref, b3_ref, wgate_ref, wbr_ref, wo_ref, o_ref, hn_ref)


def _merge_cast(x, g, branches, w_gate, w_branch, w_out, l):
    n, d = x.shape
    tn = 256
    once = dict(pipeline_mode=pl.Buffered(1))
    return pl.pallas_call(
        _merge_cast_kernel,
        grid=(1, d // tn),
        in_specs=[
            pl.BlockSpec((n, d), lambda i, j: (0, 0), **once),
            pl.BlockSpec((1, d), lambda i, j: (0, 0)),
        ] + [pl.BlockSpec((n, BRANCH), lambda i, j: (0, 0), **once)] * HEADS + [
            pl.BlockSpec((None, HEADS, d, tn), lambda i, j: (l, 0, 0, j)),
            pl.BlockSpec((None, HEADS, BRANCH, tn), lambda i, j: (l, 0, 0, j)),
            pl.BlockSpec((None, tn, d), lambda i, j: (l, j, 0)),
        ],
        out_specs=[
            pl.BlockSpec((n, d), lambda i, j: (0, 0), **once),
            pl.BlockSpec((HEADS, d, tn), lambda i, j: (0, 0, j)),
        ],
        out_shape=[jax.ShapeDtypeStruct((n, d), F32), jax.ShapeDtypeStruct((HEADS, d, d), BF)],
        scratch_shapes=[pltpu.VMEM((n, d), BF)],
        compiler_params=_cparams(("arbitrary", "arbitrary")),
        name="merge_cast",
    )(x, g, *branches, w_gate, w_branch, w_out)


def _final_norm_kernel(x_ref, g_ref, o_ref):
    o_ref[...] = _rms(x_ref[...], g_ref[...])


def _final_norm(x, g):
    n, d = x.shape
    tm = _pick_tile(n, 512, 8)
    return pl.pallas_call(
        _final_norm_kernel,
        grid=(n // tm,),
        in_specs=[pl.BlockSpec((tm, d), lambda i: (i, 0)), pl.BlockSpec((1, d), lambda i: (0, 0))],
        out_specs=pl.BlockSpec((tm, d), lambda i: (i, 0)),
        out_shape=jax.ShapeDtypeStruct((n, d), F32),
        compiler_params=_cparams(("parallel",)),
        name="final_norm",
    )(x, g)


def _conv4(xs_ref, j_seq, w, tp):
    xs = xs_ref[j_seq]
    y = xs[SUBLANES:] * w[CONV_W - 1:CONV_W, :]
    for back in range(1, CONV_W):
        y = y + pltpu.roll(xs, back, axis=0)[SUBLANES:] * w[CONV_W - 1 - back:CONV_W - back, :]
    return y


class _SeqLayout:
    def __init__(self, nb, tp, ts=None):
        self.nb, self.tp, self.ts = nb, tp, ts

    def load(self, ref, sq, rows=slice(None), cols=slice(None)):
        if self.ts is None:
            return ref[sq, rows, cols]
        x = ref[sq * self.ts:(sq + 1) * self.ts, cols]
        return jnp.concatenate([x, jnp.zeros((self.tp - self.ts, x.shape[1]), x.dtype)], axis=0)

    def store(self, ref, sq, rows, cols, val):
        if self.ts is None:
            ref[sq, rows, cols] = val
        else:
            ref[sq * self.ts:(sq + 1) * self.ts, cols] = val[:self.ts]

    def for_groups(self, n, body):
        if self.ts is None:
            lax.fori_loop(0, n, body, 0)
        else:
            for g in range(n):
                body(g, 0)

    def pcol(self, w, off):
        if self.ts is None:
            return pl.BlockSpec((self.nb, self.tp, w), lambda b, t: (b, t, off // w))
        return pl.BlockSpec((self.nb * self.ts, w), lambda b, t: (b, off // w))

    def dims(self, p):
        return (p.shape[0], p.shape[1]) if self.ts is None else (p.shape[0] // self.ts, self.tp)

    def out_shape(self, p, w):
        return jax.ShapeDtypeStruct(p.shape[:-1] + (w,), F32)


def _make_lru_kernel(lay, gb, tp, t_real, is_prompt):
    nb = lay.nb

    def kern(lx_ref, ly_ref, c0_ref, h0_ref, cw_ref, cb_ref, wa_ref, ba_ref, wx_ref, bx_ref, lam_ref,
             o_ref, cn_ref, hn_ref, xs_scr, a_scr, u_scr, cc_scr, ch_scr):
        tb = pl.program_id(1)

        @pl.when(tb == 0)
        def _():
            cc_scr[...] = c0_ref[...]
            ch_scr[...] = h0_ref[...]

        cw = cw_ref[...]
        neg_sp = -LRU_C * _softplus(-lam_ref[...])
        row = lax.broadcasted_iota(jnp.int32, (tp, 1), 0)

        def per_group(g, carry):
            seqs = [g * gb + j for j in range(gb)]
            for j, sq in enumerate(seqs):
                xs_scr[j, 0:SUBLANES, :] = cc_scr[sq]
                xs_scr[j, SUBLANES:SUBLANES + tp, :] = lay.load(lx_ref, sq)
            for j, sq in enumerate(seqs):
                xc = _conv4(xs_scr, j, cw, tp) + cb_ref[...]
                cc_scr[sq] = xs_scr[j, pl.ds(t_real, SUBLANES), :]
                xcb = xc.astype(BF)
                r_parts, i_parts = [], []
                for n in range(BRANCH // LRU_BD):
                    blk = xcb[:, n * LRU_BD:(n + 1) * LRU_BD]
                    r_parts.append(_dot(blk, wa_ref[n]))
                    i_parts.append(_dot(blk, wx_ref[n]))
                r = jax.nn.sigmoid(jnp.concatenate(r_parts, axis=1) + ba_ref[...])
                ig = jax.nn.sigmoid(jnp.concatenate(i_parts, axis=1) + bx_ref[...])
                log_a = r * neg_sp
                a = jnp.exp(log_a)
                mult = jnp.sqrt(-jnp.tanh(log_a) * (a * a + 1.0))
                if is_prompt:
                    mult = jnp.where(jnp.logical_and(row == 0, tb == 0), 1.0, mult)
                a_scr[j] = a
                u_scr[j] = mult * ig * xc

            def step(t, hs):
                out = []
                for j in range(gb):
                    h = a_scr[j, pl.ds(t, 1), :] * hs[j] + u_scr[j, pl.ds(t, 1), :]
                    a_scr[j, pl.ds(t, 1), :] = h
                    out.append(h)
                return tuple(out)

            h_last = lax.fori_loop(0, t_real, step, tuple(ch_scr[sq] for sq in seqs),
                                   unroll=True if t_real <= SUBLANES else 8)
            for j, sq in enumerate(seqs):
                ch_scr[sq] = h_last[j]
                lay.store(o_ref, sq, slice(None), slice(None), a_scr[j] * _gelu_tanh(lay.load(ly_ref, sq)))
            return carry

        lay.for_groups(nb // gb, per_group)
        cn_ref[...] = cc_scr[...]
        hn_ref[...] = ch_scr[...]

    return kern


def _mixer_specs(lay, l):
    wspec = lambda shape: pl.BlockSpec((None,) + shape, lambda b, t: (l,) + (0,) * len(shape))
    state = lambda *shape: pl.BlockSpec((lay.nb,) + shape, lambda b, t: (b,) + (0,) * len(shape))
    return wspec, lay.pcol, state


class _StateIO:
    def __init__(self, s0, stacked, l, nb):
        self.s0, self.l, self.nb = s0, l, nb
        self.stacked = stacked is not None
        self.prev = stacked[1] if self.stacked else None
        self.depth = stacked[0] if self.stacked else None

    def spec(self):
        shape = self.s0.shape[2:] if self.stacked else self.s0.shape[1:]
        zeros = (0,) * len(shape)
        if self.stacked:
            l = self.l
            return pl.BlockSpec((None, self.nb) + shape, lambda b, t: (l, b) + zeros)
        return pl.BlockSpec((self.nb,) + shape, lambda b, t: (b,) + zeros)

    def out_shape(self):
        return jax.ShapeDtypeStruct(self.s0.shape, F32)

    def extra_inputs(self):
        return [self.prev] if self.prev is not None else []

    def extra_specs(self):
        return [pl.BlockSpec(memory_space=pl.ANY)] if self.prev is not None else []

    def aliases(self, n_in, out_idx):
        return {n_in: out_idx} if self.prev is not None else {}

    def wrap(self, kern, n_in):
        if self.prev is None:
            return kern
        return lambda *refs: kern(*refs[:n_in], *refs[n_in + 1:])


def _lru(p, c0, h0, lw, l, lay, gb, t_real, is_prompt):
    nb, tp = lay.nb, lay.tp
    nseq, tseq = lay.dims(p)
    c = BRANCH
    wspec, pcol, state = _mixer_specs(lay, l)
    return pl.pallas_call(
        _make_lru_kernel(lay, gb, tp, t_real, is_prompt),
        grid=(nseq // nb, tseq // tp),
        in_specs=[
            pcol(c, COL_LX), pcol(c, COL_LY), state(SUBLANES, c), state(1, c),
            wspec((CONV_W, c)), wspec((1, c)),
            wspec((c // LRU_BD, LRU_BD, LRU_BD)), wspec((1, c)),
            wspec((c // LRU_BD, LRU_BD, LRU_BD)), wspec((1, c)),
            wspec((1, c)),
        ],
        out_specs=[pcol(c, 0), state(SUBLANES, c), state(1, c)],
        out_shape=[
            lay.out_shape(p, c),
            jax.ShapeDtypeStruct((nseq, SUBLANES, c), F32),
            jax.ShapeDtypeStruct((nseq, 1, c), F32),
        ],
        scratch_shapes=[
            pltpu.VMEM((gb, tp + SUBLANES, c), F32), pltpu.VMEM((gb, tp, c), F32), pltpu.VMEM((gb, tp, c), F32),
            pltpu.VMEM((nb, SUBLANES, c), F32), pltpu.VMEM((nb, 1, c), F32),
        ],
        compiler_params=_cparams(("parallel", "arbitrary")),
        name="lru_prompt" if is_prompt else "lru_sample",
    )(p, p, c0, h0, lw["conv_w"], lw["conv_b"], lw["wa"], lw["ba"], lw["wx"], lw["bx"], lw["lam"])


def _chunk_consts(c, cl):
    row = lax.broadcasted_iota(jnp.int32, (c, c), 0)
    col = lax.broadcasted_iota(jnp.int32, (c, c), 1)
    tril = row >= col
    lower_ones = jnp.where(tril, 1.0, 0.0).astype(BF)
    real = lax.broadcasted_iota(jnp.int32, (c, 1), 0) < cl
    return row, col, tril, lower_ones, real


def _ks(h):
    return slice(h * DK, (h + 1) * DK)


def _vs(h):
    return slice(h * DV, (h + 1) * DV)


def _chunk_loop(lay, n_groups, nc, c, per_chunk):
    def body(idx, carry):
        g = idx // nc
        per_chunk(g, pl.ds(pl.multiple_of((idx - g * nc) * c, SUBLANES), c))
        return carry

    if lay.ts is None:
        lax.fori_loop(0, n_groups * nc, body, 0)
    else:
        assert nc == 1
        for g in range(n_groups):
            per_chunk(g, slice(None))


def _make_gla_kernel(lay, gb, tp, c, t_real):
    nb = lay.nb
    nc = tp // c
    cl = min(c, t_real)
    G, H = range(gb), range(HEADS)

    def kern(q_ref, k_ref, v_ref, gr_ref, sm_ref, s0_ref, wg_ref, bg_ref, ng_ref, o_ref, sn_ref, s_scr):
        @pl.when(pl.program_id(1) == 0)
        def _():
            s_scr[...] = s0_ref[...]

        _, _, tril, lower_ones, real = _chunk_consts(c, cl)
        ones_real = jnp.where(real, 1.0, 0.0).astype(BF) * jnp.ones((c, DV), BF)
        d_tn = lambda x: lax.dot_general(x, ones_real, (((0,), (0,)), ((), ())), preferred_element_type=F32)
        ng = ng_ref[...]
        wg = wg_ref[...]
        bg = bg_ref[...]

        def per_chunk(g, rows):
            seqs = [g * gb + j for j in G]
            lg = [-_softplus(-(_mm(lay.load(sm_ref, sq, rows), wg) + bg)) * (1.0 / GLA_TAU) for sq in seqs]
            sp = [_split3(x) for x in lg]
            b = [_dot(lower_ones, s[0]) + _dot(lower_ones, s[1]) + _dot(lower_ones, s[2]) for s in sp]
            b_last_col = [d_tn(s[0]) + d_tn(s[1]) + d_tn(s[2]) for s in sp]
            q_t = [lay.load(q_ref, sq, rows) * (DK ** -0.5) * jnp.exp(b[j]) for j, sq in enumerate(seqs)]
            k = [lay.load(k_ref, sq, rows) for sq in seqs]
            k_t = [jnp.where(real, k[j] * jnp.exp(-b[j]), 0.0) for j in G]
            k_d = [jnp.where(real, k[j] * jnp.exp(b[j][cl - 1:cl, :] - b[j]), 0.0) for j in G]
            v = [lay.load(v_ref, sq, rows) for sq in seqs]
            att = [[jnp.where(tril, _mm_nt(q_t[j][:, _ks(h)], k_t[j][:, _ks(h)]), 0.0) for h in H] for j in G]
            s = [[s_scr[seqs[j], h] for h in H] for j in G]
            kv = [[_mm_tn(k_d[j][:, _ks(h)], v[j][:, _vs(h)]) for h in H] for j in G]
            o = [[_mm(att[j][h], v[j][:, _vs(h)]) + _mm(q_t[j][:, _ks(h)], s[j][h]) for h in H] for j in G]
            for j in G:
                for h in H:
                    s_scr[seqs[j], h] = jnp.exp(b_last_col[j][_ks(h), :]) * s[j][h] + kv[j][h]
            for j in G:
                gr = lay.load(gr_ref, seqs[j], rows)
                for h in H:
                    lay.store(o_ref, seqs[j], rows, _vs(h), _rms(o[j][h], ng) * _silu(gr[:, _vs(h)]))

        _chunk_loop(lay, nb // gb, nc, c, per_chunk)
        sn_ref[...] = s_scr[...]

    return kern


def _gla(p, s0, lw, l, lay, gb, c, t_real, name, stacked=None):
    nb, tp = lay.nb, lay.tp
    nseq, tseq = lay.dims(p)
    qk = HEADS * DK
    vw = HEADS * DV
    wspec, pcol, _ = _mixer_specs(lay, l)
    sio = _StateIO(s0, stacked, l, nb)
    in_specs = [
        pcol(qk, COL_GQ), pcol(qk, COL_GK), pcol(vw, COL_GV), pcol(vw, COL_GR), pcol(128, COL_SM),
        sio.spec(), wspec((128, qk)), wspec((1, qk)), wspec((1, DV)),
    ]
    n_in = len(in_specs)
    return pl.pallas_call(
        sio.wrap(_make_gla_kernel(lay, gb, tp, c, t_real), n_in),
        grid=(nseq // nb, tseq // tp),
        in_specs=in_specs + sio.extra_specs(),
        out_specs=[pcol(vw, 0), sio.spec()],
        out_shape=[lay.out_shape(p, vw), sio.out_shape()],
        scratch_shapes=[pltpu.VMEM((nb, HEADS, DK, DV), F32)],
        input_output_aliases=sio.aliases(n_in, 1),
        compiler_params=_cparams(("parallel", "arbitrary")),
        name=name,
    )(p, p, p, p, p, s0, lw["gla_wg"], lw["gla_bg"], lw["gla_ng"], *sio.extra_inputs())


def _make_ret_kernel(lay, gb, tp, c, t_real):
    nb = lay.nb
    nc = tp // c
    cl = min(c, t_real)
    G, H = range(gb), range(HEADS)
    log_gamma = [float(np.log(np.float32(1.0) - np.float32(2.0) ** np.float32(-5.0 - h))) for h in range(HEADS)]

    def kern(q_ref, k_ref, qs_ref, ks_ref, v_ref, g_ref, cos_ref, sin_ref, s0_ref, ng_ref, o_ref, sn_ref, s_scr):
        @pl.when(pl.program_id(1) == 0)
        def _():
            s_scr[...] = s0_ref[...]

        row, col, tril, _, real = _chunk_consts(c, cl)
        diff = (row - col).astype(F32)
        ridx = lax.broadcasted_iota(jnp.int32, (c, 1), 0).astype(F32)
        ng = ng_ref[...]
        dmat = [jnp.where(tril, jnp.exp(jnp.maximum(diff, 0.0) * lgm), 0.0) for lgm in log_gamma]
        q_decay = [jnp.exp((ridx + 1.0) * lgm) for lgm in log_gamma]
        k_decay = [jnp.exp((cl - 1.0 - ridx) * lgm) for lgm in log_gamma]

        def per_chunk(g, rows):
            seqs = [g * gb + j for j in G]
            cos = cos_ref[rows, :]
            sin = sin_ref[rows, :]
            ld = lambda ref, sq: lay.load(ref, sq, rows)
            q = [ld(q_ref, sq) * cos + ld(qs_ref, sq) * sin for sq in seqs]
            k = [jnp.where(real, (ld(k_ref, sq) * cos + ld(ks_ref, sq) * sin) * (DK ** -0.5), 0.0) for sq in seqs]
            v = [ld(v_ref, sq) for sq in seqs]
            att = [[_mm_nt(q[j][:, _ks(h)], k[j][:, _ks(h)]) * dmat[h] for h in H] for j in G]
            s = [[s_scr[seqs[j], h] for h in H] for j in G]
            kv = [[_mm_tn(k[j][:, _ks(h)] * k_decay[h], v[j][:, _vs(h)]) for h in H] for j in G]
            o = [[_mm(att[j][h], v[j][:, _vs(h)]) + _mm(q[j][:, _ks(h)] * q_decay[h], s[j][h]) for h in H]
                 for j in G]
            for j in G:
                for h in H:
                    s_scr[seqs[j], h] = math.exp(cl * log_gamma[h]) * s[j][h] + kv[j][h]
            for j in G:
                gate = ld(g_ref, seqs[j])
                for h in H:
                    oc = o[j][h] - jnp.mean(o[j][h], axis=-1, keepdims=True)
                    on = oc * lax.rsqrt(jnp.mean(oc * oc, axis=-1, keepdims=True) + EPS) * ng
                    lay.store(o_ref, seqs[j], rows, _vs(h), on * _silu(gate[:, _vs(h)]))

        _chunk_loop(lay, nb // gb, nc, c, per_chunk)
        sn_ref[...] = s_scr[...]

    return kern


def _ret(p, s0, cos_t, sin_t, lw, l, lay, gb, c, t_real, name, stacked=None):
    nb, tp = lay.nb, lay.tp
    nseq, tseq = lay.dims(p)
    qk = HEADS * DK
    vw = HEADS * DV
    wspec, pcol, _ = _mixer_specs(lay, l)
    sio = _StateIO(s0, stacked, l, nb)
    in_specs = [
        pcol(qk, COL_RQ), pcol(qk, COL_RK), pcol(qk, COL_RQS), pcol(qk, COL_RKS), pcol(vw, COL_RV),
        pcol(vw, COL_RG),
        pl.BlockSpec((tp, qk), lambda b, t: (t, 0)),
        pl.BlockSpec((tp, qk), lambda b, t: (t, 0)),
        sio.spec(), wspec((1, DV)),
    ]
    n_in = len(in_specs)
    return pl.pallas_call(
        sio.wrap(_make_ret_kernel(lay, gb, tp, c, t_real), n_in),
        grid=(nseq // nb, tseq // tp),
        in_specs=in_specs + sio.extra_specs(),
        out_specs=[pcol(vw, 0), sio.spec()],
        out_shape=[lay.out_shape(p, vw), sio.out_shape()],
        scratch_shapes=[pltpu.VMEM((nb, HEADS, DK, DV), F32)],
        input_output_aliases=sio.aliases(n_in, 1),
        compiler_params=_cparams(("parallel", "arbitrary")),
        name=name,
    )(p, p, p, p, p, p, cos_t, sin_t, s0, lw["ret_ng"], *sio.extra_inputs())


def _unit_lower_inverse_many(ms, c):
    row = lax.broadcasted_iota(jnp.int32, (c, c), 0)
    col = lax.broadcasted_iota(jnp.int32, (c, c), 1)
    eye = jnp.where(row == col, 1.0, 0.0)

    def same_block(bits):
        return lax.shift_right_logical(row, bits) == lax.shift_right_logical(col, bits)

    in8 = same_block(3)
    n1 = [jnp.where(in8, m, 0.0) for m in ms]
    n2 = [_mm_hp(x, x) for x in n1]
    n4 = [_mm_hp(x, x) for x in n2]
    d = [eye - x for x in n1]
    d = [x + _mm_hp(x, y) for x, y in zip(d, n2)]
    d = [x + _mm_hp(x, y) for x, y in zip(d, n4)]
    bits = 3
    while (1 << bits) < c:
        sel = jnp.logical_and(same_block(bits + 1), jnp.logical_not(same_block(bits)))
        ld = [_mm_hp(jnp.where(sel, m, 0.0), x) for m, x in zip(ms, d)]
        d = [x - _mm_hp(x, y) for x, y in zip(d, ld)]
        bits += 1
    return d


def _make_dn_kernel(lay, gb, tp, c, t_real):
    nb = lay.nb
    nc = tp // c
    cl = min(c, t_real)
    G, H = range(gb), range(HEADS)
    GH = [(j, h) for j in G for h in H]

    def kern(x_ref, z_ref, sm_ref, c0_ref, s0_ref, cw_ref, alog_ref, dtb_ref, ng_ref,
             o_ref, cn_ref, sn_ref, xs_scr, qkv_scr, cc_scr, s_scr):
        @pl.when(pl.program_id(1) == 0)
        def _():
            cc_scr[...] = c0_ref[...]
            s_scr[...] = s0_ref[...]

        row, col, tril, lower_ones, real = _chunk_consts(c, cl)
        strict = row > col
        lane = lax.broadcasted_iota(jnp.int32, (c, 128), 1)
        sel = [jnp.where(lane == SM_DA + h, 1.0, 0.0).astype(BF) for h in H]
        d_nt = lambda a, x: lax.dot_general(a, x, (((1,), (1,)), ((), ())), preferred_element_type=F32)
        ng = ng_ref[...]
        cw = cw_ref[...]
        neg_a = -jnp.exp(alog_ref[...])
        dtb = dtb_ref[...]

        def per_group(g, carry):
            seqs = [g * gb + j for j in G]
            for j, sq in enumerate(seqs):
                xs_scr[j, 0:SUBLANES, :] = cc_scr[sq]
                xs_scr[j, SUBLANES:SUBLANES + tp, :] = lay.load(x_ref, sq)
            for j, sq in enumerate(seqs):
                qkv_scr[j] = _silu(_conv4(xs_scr, j, cw, tp))
                cc_scr[sq] = xs_scr[j, pl.ds(t_real, SUBLANES), :]

            def per_chunk(ci, carry2):
                rows = pl.ds(pl.multiple_of(ci * c, SUBLANES), c)
                sm = [lay.load(sm_ref, sq, rows) for sq in seqs]
                gcum = [_cumsum_rows(neg_a * _softplus(x + dtb), lower_ones) for x in sm]
                g3 = [_split3(x) for x in gcum]
                beta_all = [jnp.where(real, jax.nn.sigmoid(x), 0.0) for x in sm]

                def head_in(j, h, part):
                    x = qkv_scr[j, rows, part * BRANCH + h * DV:part * BRANCH + (h + 1) * DV]
                    return x

                qh = {jh: head_in(*jh, 0) for jh in GH}
                kh = {jh: head_in(*jh, 1) for jh in GH}
                vh = {jh: head_in(*jh, 2) for jh in GH}
                qh = {jh: x * lax.rsqrt(jnp.sum(x * x, axis=-1, keepdims=True) + EPS) * (DV ** -0.5)
                      for jh, x in qh.items()}
                kh = {jh: jnp.where(real, x * lax.rsqrt(jnp.sum(x * x, axis=-1, keepdims=True) + EPS), 0.0)
                      for jh, x in kh.items()}
                beta = {(j, h): beta_all[j][:, SM_DB + h:SM_DB + h + 1] for j, h in GH}
                gcol = {(j, h): gcum[j][:, SM_DA + h:SM_DA + h + 1] for j, h in GH}
                grow = {(j, h): d_nt(sel[h], g3[j][0]) + d_nt(sel[h], g3[j][1]) + d_nt(sel[h], g3[j][2])
                        for j, h in GH}
                decay = {jh: jnp.where(tril, jnp.exp(jnp.where(tril, gcol[jh] - grow[jh], 0.0)), 0.0) for jh in GH}
                e_g = {jh: jnp.exp(gcol[jh]) for jh in GH}
                kb = {jh: kh[jh] * beta[jh] for jh in GH}
                kk = {jh: _mm_nt(kb[jh], kh[jh]) for jh in GH}
                qk = {jh: _mm_nt(qh[jh], kh[jh]) for jh in GH}
                tm = _unit_lower_inverse_many([jnp.where(strict, kk[jh] * decay[jh], 0.0) for jh in GH], c)
                tm = dict(zip(GH, tm))
                u = {jh: _mm(tm[jh], vh[jh] * beta[jh]) for jh in GH}
                w = {jh: _mm(tm[jh], kb[jh] * e_g[jh]) for jh in GH}
                s = {(j, h): s_scr[seqs[j], h] for j, h in GH}
                ws = {jh: _mm(w[jh], s[jh]) for jh in GH}
                qs = {jh: _mm(qh[jh] * e_g[jh], s[jh]) for jh in GH}
                v_new = {jh: u[jh] - ws[jh] for jh in GH}
                o = {jh: qs[jh] + _mm(qk[jh] * decay[jh], v_new[jh]) for jh in GH}
                for j, h in GH:
                    g_last = gcol[(j, h)][cl - 1:cl, :]
                    k_d = kh[(j, h)] * jnp.exp(g_last - gcol[(j, h)])
                    s_scr[seqs[j], h] = jnp.exp(g_last) * s[(j, h)] + _mm_tn(k_d, v_new[(j, h)])
                for j in G:
                    z = lay.load(z_ref, seqs[j], rows)
                    for h in H:
                        lay.store(o_ref, seqs[j], rows, _vs(h), _rms(o[(j, h)], ng) * _silu(z[:, _vs(h)]))
                return carry2

            lax.fori_loop(0, nc, per_chunk, 0)
            return carry

        lay.for_groups(nb // gb, per_group)
        cn_ref[...] = cc_scr[...]
        sn_ref[...] = s_scr[...]

    return kern


def _dn(p, c0, s0, lw, l, lay, gb, c, t_real, name, stacked=None):
    nb, tp = lay.nb, lay.tp
    nseq, tseq = lay.dims(p)
    cw3 = 3 * BRANCH
    wspec, pcol, state = _mixer_specs(lay, l)
    sio = _StateIO(s0, stacked, l, nb)
    in_specs = [
        pcol(cw3, COL_DQKV), pcol(BRANCH, COL_DZ), pcol(128, COL_SM),
        state(SUBLANES, cw3), sio.spec(),
        wspec((CONV_W, cw3)), wspec((1, 128)), wspec((1, 128)), wspec((1, DV)),
    ]
    n_in = len(in_specs)
    return pl.pallas_call(
        sio.wrap(_make_dn_kernel(lay, gb, tp, c, t_real), n_in),
        grid=(nseq // nb, tseq // tp),
        in_specs=in_specs + sio.extra_specs(),
        out_specs=[pcol(BRANCH, 0), state(SUBLANES, cw3), sio.spec()],
        out_shape=[
            lay.out_shape(p, BRANCH),
            jax.ShapeDtypeStruct((nseq, SUBLANES, cw3), F32),
            sio.out_shape(),
        ],
        scratch_shapes=[
            pltpu.VMEM((gb, tp + SUBLANES, cw3), F32), pltpu.VMEM((gb, tp, cw3), F32),
            pltpu.VMEM((nb, SUBLANES, cw3), F32), pltpu.VMEM((nb, HEADS, DV, DV), F32),
        ],
        input_output_aliases=sio.aliases(n_in, 2),
        compiler_params=_cparams(("parallel", "arbitrary")),
        name=name,
    )(p, p, p, c0, s0, lw["dn_conv_w"], lw["dn_alog"], lw["dn_dtb"], lw["dn_ng"], *sio.extra_inputs())


IN_SIZES = (BRANCH, BRANCH, HEADS * DK, HEADS * DK, BRANCH, BRANCH, GLA_RANK,
            HEADS * DK, HEADS * DK, BRANCH, BRANCH, BRANCH, BRANCH, BRANCH, BRANCH, HEADS, HEADS)


def _pack_kernel(w_ref, o_ref):
    w = w_ref[...]
    offs = np.concatenate([[0], np.cumsum(IN_SIZES)])
    seg = lambda i: w[:, int(offs[i]):int(offs[i + 1])]
    (lx, ly, gq, gk, gv, gr, glr, rq, rk, rv, rg, dq, dk, dv, dz, db, da) = [seg(i) for i in range(len(IN_SIZES))]

    def swap_halves(x):
        parts = []
        for h in range(HEADS):
            parts += [x[:, h * DK + DK // 2:(h + 1) * DK], x[:, h * DK:h * DK + DK // 2]]
        return jnp.concatenate(parts, axis=1)

    used = COL_SM + GLA_RANK + 2 * HEADS
    tail = jnp.zeros((w.shape[0], N_PACK - used), w.dtype)
    packed = jnp.concatenate([lx, ly, gq, gk, gv, gr, rq, rk, swap_halves(rq), swap_halves(rk), rv, rg,
                              dq, dk, dv, dz, glr, db, da, tail], axis=1)
    o_ref[...] = packed.astype(BF)


def _pack_w_in(w_in):
    depth, d, n_in = w_in.shape
    assert n_in == sum(IN_SIZES)
    td = _pick_tile(d, 256, 16)
    return pl.pallas_call(
        _pack_kernel,
        grid=(depth, d // td),
        in_specs=[pl.BlockSpec((None, td, n_in), lambda l, i: (l, i, 0))],
        out_specs=pl.BlockSpec((None, td, N_PACK), lambda l, i: (l, i, 0)),
        out_shape=jax.ShapeDtypeStruct((depth, d, N_PACK), BF),
        compiler_params=_cparams(("parallel", "parallel")),
        name="pack_w_in",
    )(w_in)


def _rope_tables(pos):
    half = DK // 2
    inv = ROPE_BASE ** (-jnp.arange(half, dtype=F32) / half)
    ang = pos.astype(F32)[:, None] * inv[None, :]
    cos, sin = jnp.cos(ang), jnp.sin(ang)
    cos_t = jnp.tile(jnp.concatenate([cos, cos], axis=1), (1, HEADS))
    sin_t = jnp.tile(jnp.concatenate([-sin, sin], axis=1), (1, HEADS))
    return cos_t, sin_t


def _lane_row(vals, off):
    depth = vals.shape[0]
    return jnp.zeros((depth, 1, 128), F32).at[:, 0, off:off + HEADS].set(vals.astype(F32))


def kernel(x_prompt, x_sample, state_lru_conv, state_lru_h, state_gla, state_ret, state_dn_conv, state_dn,
           norm_g, final_norm_g, w_ff_in, w_ff_out, w_in, w_gate, w_branch, w_out,
           lru_conv_w, lru_conv_b, lru_wa, lru_ba, lru_wx, lru_bx, lru_lambda,
           gla_wg, gla_bg, gla_norm_g, ret_norm_g, dn_conv_w, dn_a_log, dn_dt_bias, dn_norm_g):
    bp, tpr, d = x_prompt.shape
    bs, ts, _ = x_sample.shape
    depth = w_in.shape[0]
    n_p, n_s = bp * tpr, bs * ts
    assert d == D_MODEL and ts <= SAMPLE_PAD and tpr % CHUNK == 0

    w_pack = _pack_w_in(w_in)
    w_branch_b = w_branch.astype(BF)
    w_out_b = w_out.astype(BF)
    wg_pad = jnp.zeros((depth, 128, HEADS * DK), F32).at[:, SM_GLR:SM_GLR + GLA_RANK, :].set(gla_wg).astype(BF)
    r3 = lambda a: a.reshape(a.shape[0], 1, a.shape[1])
    lw = {
        "conv_w": lru_conv_w, "conv_b": r3(lru_conv_b), "wa": lru_wa.astype(BF), "ba": r3(lru_ba),
        "wx": lru_wx.astype(BF), "bx": r3(lru_bx), "lam": r3(lru_lambda),
        "gla_wg": wg_pad, "gla_bg": r3(gla_bg), "gla_ng": r3(gla_norm_g), "ret_ng": r3(ret_norm_g),
        "dn_conv_w": dn_conv_w, "dn_alog": _lane_row(dn_a_log, SM_DA), "dn_dtb": _lane_row(dn_dt_bias, SM_DA),
        "dn_ng": r3(dn_norm_g),
    }
    norm_g4 = norm_g.reshape(depth, 3, 1, d)

    cos_p, sin_p = _rope_tables(jnp.arange(tpr))
    pos_s = jnp.where(jnp.arange(SAMPLE_PAD) < ts, PAST_LEN + jnp.arange(SAMPLE_PAD), 0)
    cos_s, sin_s = _rope_tables(pos_s)

    pad_conv = lambda a: jnp.pad(a, ((0, 0), (SUBLANES - (CONV_W - 1), 0), (0, 0)))
    zeros = lambda *s: jnp.zeros(s, F32)

    tp_p = _pick_tile(tpr, 256, CHUNK)
    a_p = dict(lay=_SeqLayout(bp, tp_p), gb=bp)
    nb_s = _pick_tile(bs, 8, 2)
    assert (nb_s * ts) % SUBLANES == 0
    a_s = dict(lay=_SeqLayout(nb_s, SAMPLE_PAD, ts), gb=_pick_tile(nb_s, 4, 1))

    xp = x_prompt.reshape(n_p, d)
    xs = x_sample.reshape(n_s, d)
    new_p = [[] for _ in range(6)]
    new_s = [[] for _ in range(6)]
    s_gla_s = s_ret_s = s_dn_s = None
    for l in range(depth):
        xs, *w_ffn = _ffn_cast(xs, norm_g4[l, 0], w_ff_in, w_ff_out, l, 0)
        xp = _ffn(xp, norm_g4[l, 0], *w_ffn)
        p_p = _inproj(xp, norm_g4[l, 1], w_pack, l).reshape(bp, tpr, N_PACK)
        p_s = _inproj(xs, norm_g4[l, 1], w_pack, l)

        o_lru_p, c_lru_p, h_lru_p = _lru(p_p, zeros(bp, SUBLANES, BRANCH), zeros(bp, 1, BRANCH), lw, l,
                                         t_real=tp_p, is_prompt=True, **a_p)
        o_gla_p, s_gla_p = _gla(p_p, zeros(bp, HEADS, DK, DV), lw, l, c=CHUNK, t_real=tp_p, name="gla_prompt", **a_p)
        o_ret_p, s_ret_p = _ret(p_p, zeros(bp, HEADS, DK, DV), cos_p, sin_p, lw, l, c=CHUNK, t_real=tp_p,
                                name="ret_prompt", **a_p)
        o_dn_p, c_dn_p, s_dn_p = _dn(p_p, zeros(bp, SUBLANES, 3 * BRANCH), zeros(bp, HEADS, DV, DV), lw, l,
                                     c=CHUNK, t_real=tp_p, name="dn_prompt", **a_p)

        o_lru_s, c_lru_s, h_lru_s = _lru(p_s, pad_conv(state_lru_conv[l]), state_lru_h[l][:, None, :], lw, l,
                                         t_real=ts, is_prompt=False, **a_s)
        o_gla_s, s_gla_s = _gla(p_s, state_gla, lw, l, c=SAMPLE_PAD, t_real=ts, name="gla_sample",
                                stacked=(depth, s_gla_s), **a_s)
        o_ret_s, s_ret_s = _ret(p_s, state_ret, cos_s, sin_s, lw, l, c=SAMPLE_PAD, t_real=ts, name="ret_sample",
                                stacked=(depth, s_ret_s), **a_s)
        o_dn_s, c_dn_s, s_dn_s = _dn(p_s, pad_conv(state_dn_conv[l]), state_dn, lw, l, c=SAMPLE_PAD, t_real=ts,
                                     name="dn_sample", stacked=(depth, s_dn_s), **a_s)

        flat_p = lambda o: o.reshape(n_p, BRANCH)
        xs, w_gate_l = _merge_cast(xs, norm_g4[l, 1], [o_lru_s, o_gla_s, o_ret_s, o_dn_s],
                                   w_gate, w_branch_b, w_out_b, l)
        xp = _merge(xp, norm_g4[l, 1], [flat_p(o) for o in (o_lru_p, o_gla_p, o_ret_p, o_dn_p)],
                    w_gate_l, w_branch_b, w_out_b, l)
        xs, *w_ffn = _ffn_cast(xs, norm_g4[l, 2], w_ff_in, w_ff_out, l, 1)
        xp = _ffn(xp, norm_g4[l, 2], *w_ffn)

        tail3 = lambda cwin: cwin[:, SUBLANES - (CONV_W - 1):, :]
        for i, v in enumerate((tail3(c_lru_p), h_lru_p[:, 0], s_gla_p, s_ret_p, tail3(c_dn_p), s_dn_p)):
            new_p[i].append(v)
        for i, v in ((0, tail3(c_lru_s)), (1, h_lru_s[:, 0]), (4, tail3(c_dn_s))):
            new_s[i].append(v)

    fg = final_norm_g.reshape(1, d)
    y_prompt = _final_norm(xp, fg).reshape(bp, tpr, d)
    y_sample = _final_norm(xs, fg).reshape(bs, ts, d)
    sp = [jnp.stack(v) for v in new_p]
    s_small = {i: jnp.stack(new_s[i]) for i in (0, 1, 4)}
    return tuple([y_prompt, y_sample] + sp + [s_small[0], s_small[1], s_gla_s, s_ret_s, s_small[4], s_dn_s])
```

```python
import functools
import math

import numpy as np
import jax
import jax.numpy as jnp
from jax import lax
from jax.experimental import pallas as pl
from jax.experimental.pallas import tpu as pltpu

F32 = jnp.float32
BF = jnp.bfloat16
EPS = 1e-6

D_MODEL = 2048
BRANCH = 512
CONV_W = 4
HEADS = 4
DK = 64
DV = 128
GLA_RANK = 16
GLA_TAU = 16.0
LRU_C = 8.0
LRU_BD = 128
ROPE_BASE = 10000.0
CHUNK = 64
PAST_LEN = 16384

V7X_VMEM_BYTES = 64 * 1024 * 1024
VMEM_LIMIT = V7X_VMEM_BYTES - 8 * 1024 * 1024
SUBLANES = 8
SAMPLE_PAD = 8

COL_LX, COL_LY = 0, 512
COL_GQ, COL_GK, COL_GV, COL_GR = 1024, 1280, 1536, 2048
COL_RQ, COL_RK, COL_RQS, COL_RKS, COL_RV, COL_RG = 2560, 2816, 3072, 3328, 3584, 4096
COL_DQKV, COL_DZ = 4608, 6144
COL_SM = 6656
SM_GLR, SM_DB, SM_DA = 0, 16, 20
N_PACK = 6912


def _dot(a, b):
    return jnp.dot(a, b, preferred_element_type=F32)


def _mm(a, b):
    return _dot(a.astype(BF), b.astype(BF))


def _mm_nt(a, b):
    return lax.dot_general(a.astype(BF), b.astype(BF), (((1,), (1,)), ((), ())), preferred_element_type=F32)


def _mm_tn(a, b):
    return lax.dot_general(a.astype(BF), b.astype(BF), (((0,), (0,)), ((), ())), preferred_element_type=F32)


def _split3(x):
    hi = x.astype(BF)
    r = x - hi.astype(F32)
    mid = r.astype(BF)
    lo = (r - mid.astype(F32)).astype(BF)
    return hi, mid, lo


def _split2(x):
    hi = x.astype(BF)
    return hi, (x - hi.astype(F32)).astype(BF)


def _mm_hp(a, b):
    ah, al = _split2(a)
    bh, bl = _split2(b)
    return _dot(ah, bh) + _dot(ah, bl) + _dot(al, bh)


def _cumsum_rows(x, lower_ones):
    hi, mid, lo = _split3(x)
    return _dot(lower_ones, hi) + _dot(lower_ones, mid) + _dot(lower_ones, lo)


def _softplus(x):
    return jnp.maximum(x, 0.0) + jnp.log1p(jnp.exp(-jnp.abs(x)))


def _silu(x):
    return x * jax.nn.sigmoid(x)


def _gelu_tanh(x):
    return x * (0.5 * (1.0 + jnp.tanh(math.sqrt(2.0 / math.pi) * (x + 0.044715 * (x * x * x)))))


def _rms(x, g):
    return x * lax.rsqrt(jnp.mean(x * x, axis=-1, keepdims=True) + EPS) * g


def _pick_tile(n, target, align):
    best = None
    for t in range(align, min(n, target) + 1, align):
        if n % t == 0:
            best = t
    assert best is not None, (n, target, align)
    return best


def _cparams(sem):
    return pltpu.CompilerParams(dimension_semantics=sem, vmem_limit_bytes=VMEM_LIMIT)


def _ffn_kernel(x_ref, g_ref, wg_ref, wu_ref, wo_ref, o_ref, hn_ref):
    @pl.when(pl.program_id(1) == 0)
    def _():
        x = x_ref[...]
        hn_ref[...] = _rms(x, g_ref[...]).astype(BF)
        o_ref[...] = x

    hn = hn_ref[...]
    gate = _dot(hn, wg_ref[...])
    up = _dot(hn, wu_ref[...])
    act = (_silu(gate) * up * 0.5).astype(BF)
    o_ref[...] += _dot(act, wo_ref[...])


def _ffn_final_kernel(x_ref, g_ref, wg_ref, wu_ref, wo_ref, fg_ref, o_ref, hn_ref):
    _ffn_kernel(x_ref, g_ref, wg_ref, wu_ref, wo_ref, o_ref, hn_ref)

    @pl.when(pl.program_id(1) == pl.num_programs(1) - 1)
    def _():
        o_ref[...] = _rms(o_ref[...], fg_ref[...])


def _ffn(x, g, wg, wu, wo, final_g=None):
    n, d = x.shape
    f = wo.shape[0]
    tm = _pick_tile(n, 1024, 16)
    tf = _pick_tile(f, 512, 128)
    final = [] if final_g is None else [final_g]
    return pl.pallas_call(
        _ffn_kernel if final_g is None else _ffn_final_kernel,
        grid=(n // tm, f // tf),
        in_specs=[
            pl.BlockSpec((tm, d), lambda i, j: (i, 0)),
            pl.BlockSpec((1, d), lambda i, j: (0, 0)),
            pl.BlockSpec((d, tf), lambda i, j: (0, j)),
            pl.BlockSpec((d, tf), lambda i, j: (0, j)),
            pl.BlockSpec((tf, d), lambda i, j: (j, 0)),
        ] + [pl.BlockSpec((1, d), lambda i, j: (0, 0))] * len(final),
        out_specs=pl.BlockSpec((tm, d), lambda i, j: (i, 0)),
        out_shape=jax.ShapeDtypeStruct((n, d), F32),
        scratch_shapes=[pltpu.VMEM((tm, d), BF)],
        compiler_params=_cparams(("parallel", "arbitrary")),
        name="ffn",
    )(x, g, wg, wu, wo, *final)


def _ffn_cast_kernel(x_ref, g_ref, wg32_ref, wu32_ref, wo32_ref, o_ref, wg_ref, wu_ref, wo_ref, hn_ref):
    @pl.when(pl.program_id(1) == 0)
    def _():
        x = x_ref[...]
        hn_ref[...] = _rms(x, g_ref[...]).astype(BF)
        o_ref[...] = x

    wg_ref[...] = wg32_ref[...].astype(BF)
    wu_ref[...] = wu32_ref[...].astype(BF)
    wo_ref[...] = wo32_ref[...].astype(BF)
    hn = hn_ref[...]
    act = (_silu(_dot(hn, wg_ref[...])) * _dot(hn, wu_ref[...]) * 0.5).astype(BF)
    o_ref[...] += _dot(act, wo_ref[...])


def _ffn_cast(x, g, w_in, w_out, l, s):
    n, d = x.shape
    f = w_out.shape[2]
    tf = _pick_tile(f, 512, 128)
    nf = f // tf
    once = dict(pipeline_mode=pl.Buffered(1))
    return pl.pallas_call(
        _ffn_cast_kernel,
        grid=(1, nf),
        in_specs=[
            pl.BlockSpec((n, d), lambda i, j: (0, 0), **once),
            pl.BlockSpec((1, d), lambda i, j: (0, 0)),
            pl.BlockSpec((None, None, d, tf), lambda i, j: (l, s, 0, j)),
            pl.BlockSpec((None, None, d, tf), lambda i, j: (l, s, 0, j + nf)),
            pl.BlockSpec((None, None, tf, d), lambda i, j: (l, s, j, 0)),
        ],
        out_specs=[
            pl.BlockSpec((n, d), lambda i, j: (0, 0), **once),
            pl.BlockSpec((d, tf), lambda i, j: (0, j)),
            pl.BlockSpec((d, tf), lambda i, j: (0, j)),
            pl.BlockSpec((tf, d), lambda i, j: (j, 0)),
        ],
        out_shape=[
            jax.ShapeDtypeStruct((n, d), F32),
            jax.ShapeDtypeStruct((d, f), BF), jax.ShapeDtypeStruct((d, f), BF), jax.ShapeDtypeStruct((f, d), BF),
        ],
        scratch_shapes=[pltpu.VMEM((n, d), BF)],
        compiler_params=_cparams(("arbitrary", "arbitrary")),
        name="ffn_cast",
    )(x, g, w_in, w_in, w_out)


def _inproj_kernel(x_ref, g_ref, w_ref, o_ref, hn_ref):
    @pl.when(pl.program_id(1) == 0)
    def _():
        hn_ref[...] = _rms(x_ref[...], g_ref[...]).astype(BF)

    o_ref[...] = _dot(hn_ref[...], w_ref[...])


def _inproj(x, g, w_pack, l):
    n, d = x.shape
    npk = w_pack.shape[2]
    tm = _pick_tile(n, 1024, 16)
    tn = _pick_tile(npk, 1024, 128)
    return pl.pallas_call(
        _inproj_kernel,
        grid=(n // tm, npk // tn),
        in_specs=[
            pl.BlockSpec((tm, d), lambda i, j: (i, 0)),
            pl.BlockSpec((1, d), lambda i, j: (0, 0)),
            pl.BlockSpec((None, d, tn), lambda i, j: (l, 0, j)),
        ],
        out_specs=pl.BlockSpec((tm, tn), lambda i, j: (i, j)),
        out_shape=jax.ShapeDtypeStruct((n, npk), F32),
        scratch_shapes=[pltpu.VMEM((tm, d), BF)],
        compiler_params=_cparams(("parallel", "arbitrary")),
        name="inproj",
    )(x, g, w_pack)


def _merge_kernel(x_ref, g_ref, b0_ref, b1_ref, b2_ref, b3_ref, wgate_ref, wbr_ref, wo_ref, o_ref, hn_ref):
    @pl.when(pl.program_id(1) == 0)
    def _():
        x = x_ref[...]
        hn_ref[...] = _rms(x, g_ref[...]).astype(BF)
        o_ref[...] = x

    hn = hn_ref[...]
    m = None
    for n, b_ref in enumerate((b0_ref, b1_ref, b2_ref, b3_ref)):
        gate = jax.nn.sigmoid(_dot(hn, wgate_ref[n]))
        br = _dot(b_ref[...].astype(BF), wbr_ref[n])
        m = gate * br if m is None else m + gate * br
    o_ref[...] += _dot(m.astype(BF), wo_ref[...])


def _gate_kernel(x_ref, g_ref, b0_ref, b1_ref, b2_ref, b3_ref, wgate_ref, wbr_ref, m_ref, hn_ref):
    @pl.when(pl.program_id(1) == 0)
    def _():
        hn_ref[...] = _rms(x_ref[...], g_ref[...]).astype(BF)

    hn = hn_ref[...]
    m = None
    for n, b_ref in enumerate((b0_ref, b1_ref, b2_ref, b3_ref)):
        gate = jax.nn.sigmoid(_dot(hn, wgate_ref[n]))
        br = _dot(b_ref[...].astype(BF), wbr_ref[n])
        m = gate * br if m is None else m + gate * br
    m_ref[...] = m.astype(BF)


def _outproj_kernel(m_ref, w_ref, x_ref, o_ref):
    o_ref[...] = x_ref[...] + _dot(m_ref[...], w_ref[...])


def _merge_split(x, g, branches, w_gate_l, w_branch, w_out, l):
    n, d = x.shape
    tm = _pick_tile(n, 1024, 16)
    tn = 256
    m = pl.pallas_call(
        _gate_kernel,
        grid=(n // tm, d // tn),
        in_specs=[
            pl.BlockSpec((tm, d), lambda i, j: (i, 0)),
            pl.BlockSpec((1, d), lambda i, j: (0, 0)),
        ] + [pl.BlockSpec((tm, BRANCH), lambda i, j: (i, 0))] * HEADS + [
            pl.BlockSpec((HEADS, d, tn), lambda i, j: (0, 0, j)),
            pl.BlockSpec((None, HEADS, BRANCH, tn), lambda i, j: (l, 0, 0, j)),
        ],
        out_specs=pl.BlockSpec((tm, tn), lambda i, j: (i, j)),
        out_shape=jax.ShapeDtypeStruct((n, d), BF),
        scratch_shapes=[pltpu.VMEM((tm, d), BF)],
        compiler_params=_cparams(("parallel", "arbitrary")),
        name="gate",
    )(x, g, *branches, w_gate_l, w_branch)
    to = 512
    return pl.pallas_call(
        _outproj_kernel,
        grid=(n // tm, d // to),
        in_specs=[
            pl.BlockSpec((tm, d), lambda i, j: (i, 0)),
            pl.BlockSpec((None, d, to), lambda i, j: (l, 0, j)),
            pl.BlockSpec((tm, to), lambda i, j: (i, j)),
        ],
        out_specs=pl.BlockSpec((tm, to), lambda i, j: (i, j)),
        out_shape=jax.ShapeDtypeStruct((n, d), F32),
        compiler_params=_cparams(("parallel", "arbitrary")),
        name="outproj",
    )(m, w_out, x)


def _merge_cast_kernel(x_ref, g_ref, b0_ref, b1_ref, b2_ref, b3_ref, wgate32_ref, wbr_ref, wo_ref,
                       o_ref, wgate_ref, hn_ref):
    wgate_ref[...] = wgate32_ref[...].astype(BF)
    _merge_kernel(x_ref, g_ref, b0_ref, b1_ref, b2_ref, b3_ref, wgate_ref, wbr_ref, wo_ref, o_ref, hn_ref)


def _merge_cast(x, g, branches, w_gate, w_branch, w_out, l):
    n, d = x.shape
    tn = 256
    once = dict(pipeline_mode=pl.Buffered(1))
    return pl.pallas_call(
        _merge_cast_kernel,
        grid=(1, d // tn),
        in_specs=[
            pl.BlockSpec((n, d), lambda i, j: (0, 0), **once),
            pl.BlockSpec((1, d), lambda i, j: (0, 0)),
        ] + [pl.BlockSpec((n, BRANCH), lambda i, j: (0, 0), **once)] * HEADS + [
            pl.BlockSpec((None, HEADS, d, tn), lambda i, j: (l, 0, 0, j)),
            pl.BlockSpec((None, HEADS, BRANCH, tn), lambda i, j: (l, 0, 0, j)),
            pl.BlockSpec((None, tn, d), lambda i, j: (l, j, 0)),
        ],
        out_specs=[
            pl.BlockSpec((n, d), lambda i, j: (0, 0), **once),
            pl.BlockSpec((HEADS, d, tn), lambda i, j: (0, 0, j)),
        ],
        out_shape=[jax.ShapeDtypeStruct((n, d), F32), jax.ShapeDtypeStruct((HEADS, d, d), BF)],
        scratch_shapes=[pltpu.VMEM((n, d), BF)],
        compiler_params=_cparams(("arbitrary", "arbitrary")),
        name="merge_cast",
    )(x, g, *branches, w_gate, w_branch, w_out)


def _final_norm_kernel(x_ref, g_ref, o_ref):
    o_ref[...] = _rms(x_ref[...], g_ref[...])


def _final_norm(x, g):
    n, d = x.shape
    tm = _pick_tile(n, 512, 8)
    return pl.pallas_call(
        _final_norm_kernel,
        grid=(n // tm,),
        in_specs=[pl.BlockSpec((tm, d), lambda i: (i, 0)), pl.BlockSpec((1, d), lambda i: (0, 0))],
        out_specs=pl.BlockSpec((tm, d), lambda i: (i, 0)),
        out_shape=jax.ShapeDtypeStruct((n, d), F32),
        compiler_params=_cparams(("parallel",)),
        name="final_norm",
    )(x, g)


def _conv4(xs_ref, j_seq, w, tp):
    xs = xs_ref[j_seq]
    y = xs[SUBLANES:] * w[CONV_W - 1:CONV_W, :]
    for back in range(1, CONV_W):
        y = y + pltpu.roll(xs, back, axis=0)[SUBLANES:] * w[CONV_W - 1 - back:CONV_W - back, :]
    return y


class _SeqLayout:
    def __init__(self, nb, tp, ts=None):
        self.nb, self.tp, self.ts = nb, tp, ts

    def load(self, ref, sq, rows=slice(None), cols=slice(None)):
        if self.ts is None:
            return ref[sq, rows, cols]
        x = ref[sq * self.ts:(sq + 1) * self.ts, cols]
        return jnp.concatenate([x, jnp.zeros((self.tp - self.ts, x.shape[1]), x.dtype)], axis=0)

    def store(self, ref, sq, rows, cols, val):
        if self.ts is None:
            ref[sq, rows, cols] = val
        else:
            ref[sq * self.ts:(sq + 1) * self.ts, cols] = val[:self.ts]

    def for_groups(self, n, body):
        if self.ts is None:
            lax.fori_loop(0, n, body, 0)
        else:
            for g in range(n):
                body(g, 0)

    def pcol(self, w, off):
        if self.ts is None:
            return pl.BlockSpec((self.nb, self.tp, w), lambda b, t: (b, t, off // w))
        return pl.BlockSpec((self.nb * self.ts, w), lambda b, t: (b, off // w))

    def dims(self, p):
        return (p.shape[0], p.shape[1]) if self.ts is None else (p.shape[0] // self.ts, self.tp)

    def out_shape(self, p, w):
        return jax.ShapeDtypeStruct(p.shape[:-1] + (w,), F32)


def _make_lru_kernel(lay, gb, tp, t_real, is_prompt):
    nb = lay.nb

    def kern(lx_ref, ly_ref, c0_ref, h0_ref, cw_ref, cb_ref, wa_ref, ba_ref, wx_ref, bx_ref, lam_ref,
             o_ref, cn_ref, hn_ref, xs_scr, a_scr, u_scr, cc_scr, ch_scr):
        tb = pl.program_id(1)

        @pl.when(tb == 0)
        def _():
            cc_scr[...] = c0_ref[...]
            ch_scr[...] = h0_ref[...]

        cw = cw_ref[...]
        neg_sp = -LRU_C * _softplus(-lam_ref[...])
        row = lax.broadcasted_iota(jnp.int32, (tp, 1), 0)

        def per_group(g, carry):
            seqs = [g * gb + j for j in range(gb)]
            for j, sq in enumerate(seqs):
                xs_scr[j, 0:SUBLANES, :] = cc_scr[sq]
                xs_scr[j, SUBLANES:SUBLANES + tp, :] = lay.load(lx_ref, sq)
            for j, sq in enumerate(seqs):
                xc = _conv4(xs_scr, j, cw, tp) + cb_ref[...]
                cc_scr[sq] = xs_scr[j, pl.ds(t_real, SUBLANES), :]
                xcb = xc.astype(BF)
                r_parts, i_parts = [], []
                for n in range(BRANCH // LRU_BD):
                    blk = xcb[:, n * LRU_BD:(n + 1) * LRU_BD]
                    r_parts.append(_dot(blk, wa_ref[n]))
                    i_parts.append(_dot(blk, wx_ref[n]))
                r = jax.nn.sigmoid(jnp.concatenate(r_parts, axis=1) + ba_ref[...])
                ig = jax.nn.sigmoid(jnp.concatenate(i_parts, axis=1) + bx_ref[...])
                log_a = r * neg_sp
                a = jnp.exp(log_a)
                mult = jnp.sqrt(-jnp.tanh(log_a) * (a * a + 1.0))
                if is_prompt:
                    mult = jnp.where(jnp.logical_and(row == 0, tb == 0), 1.0, mult)
                a_scr[j] = a
                u_scr[j] = mult * ig * xc

            def step(t, hs):
                out = []
                for j in range(gb):
                    h = a_scr[j, pl.ds(t, 1), :] * hs[j] + u_scr[j, pl.ds(t, 1), :]
                    a_scr[j, pl.ds(t, 1), :] = h
                    out.append(h)
                return tuple(out)

            h_last = lax.fori_loop(0, t_real, step, tuple(ch_scr[sq] for sq in seqs),
                                   unroll=True if t_real <= SUBLANES else 8)
            for j, sq in enumerate(seqs):
                ch_scr[sq] = h_last[j]
                lay.store(o_ref, sq, slice(None), slice(None), a_scr[j] * _gelu_tanh(lay.load(ly_ref, sq)))
            return carry

        lay.for_groups(nb // gb, per_group)
        cn_ref[...] = cc_scr[...]
        hn_ref[...] = ch_scr[...]

    return kern


def _mixer_specs(lay, l):
    wspec = lambda shape: pl.BlockSpec((None,) + shape, lambda b, t: (l,) + (0,) * len(shape))
    state = lambda *shape: pl.BlockSpec((lay.nb,) + shape, lambda b, t: (b,) + (0,) * len(shape))
    return wspec, lay.pcol, state


class _StateIO:
    def __init__(self, s0, stacked, l, nb):
        self.s0, self.l, self.nb = s0, l, nb
        self.stacked = stacked is not None
        self.prev = stacked[1] if self.stacked else None
        self.depth = stacked[0] if self.stacked else None

    def spec(self):
        shape = self.s0.shape[2:] if self.stacked else self.s0.shape[1:]
        zeros = (0,) * len(shape)
        if self.stacked:
            l = self.l
            return pl.BlockSpec((None, self.nb) + shape, lambda b, t: (l, b) + zeros)
        return pl.BlockSpec((self.nb,) + shape, lambda b, t: (b,) + zeros)

    def out_shape(self):
        return jax.ShapeDtypeStruct(self.s0.shape, F32)

    def extra_inputs(self):
        return [self.prev] if self.prev is not None else []

    def extra_specs(self):
        return [pl.BlockSpec(memory_space=pl.ANY)] if self.prev is not None else []

    def aliases(self, n_in, out_idx):
        return {n_in: out_idx} if self.prev is not None else {}

    def wrap(self, kern, n_in):
        if self.prev is None:
            return kern
        return lambda *refs: kern(*refs[:n_in], *refs[n_in + 1:])


def _lru(p, c0, h0, lw, l, lay, gb, t_real, is_prompt):
    nb, tp = lay.nb, lay.tp
    nseq, tseq = lay.dims(p)
    c = BRANCH
    wspec, pcol, state = _mixer_specs(lay, l)
    return pl.pallas_call(
        _make_lru_kernel(lay, gb, tp, t_real, is_prompt),
        grid=(nseq // nb, tseq // tp),
        in_specs=[
            pcol(c, COL_LX), pcol(c, COL_LY), state(SUBLANES, c), state(1, c),
            wspec((CONV_W, c)), wspec((1, c)),
            wspec((c // LRU_BD, LRU_BD, LRU_BD)), wspec((1, c)),
            wspec((c // LRU_BD, LRU_BD, LRU_BD)), wspec((1, c)),
            wspec((1, c)),
        ],
        out_specs=[pcol(c, 0), state(SUBLANES, c), state(1, c)],
        out_shape=[
            lay.out_shape(p, c),
            jax.ShapeDtypeStruct((nseq, SUBLANES, c), F32),
            jax.ShapeDtypeStruct((nseq, 1, c), F32),
        ],
        scratch_shapes=[
            pltpu.VMEM((gb, tp + SUBLANES, c), F32), pltpu.VMEM((gb, tp, c), F32), pltpu.VMEM((gb, tp, c), F32),
            pltpu.VMEM((nb, SUBLANES, c), F32), pltpu.VMEM((nb, 1, c), F32),
        ],
        compiler_params=_cparams(("parallel", "arbitrary")),
        name="lru_prompt" if is_prompt else "lru_sample",
    )(p, p, c0, h0, lw["conv_w"], lw["conv_b"], lw["wa"], lw["ba"], lw["wx"], lw["bx"], lw["lam"])


def _chunk_consts(c, cl):
    row = lax.broadcasted_iota(jnp.int32, (c, c), 0)
    col = lax.broadcasted_iota(jnp.int32, (c, c), 1)
    tril = row >= col
    lower_ones = jnp.where(tril, 1.0, 0.0).astype(BF)
    real = lax.broadcasted_iota(jnp.int32, (c, 1), 0) < cl
    return row, col, tril, lower_ones, real


def _ks(h):
    return slice(h * DK, (h + 1) * DK)


def _vs(h):
    return slice(h * DV, (h + 1) * DV)


def _chunk_loop(lay, n_groups, nc, c, per_chunk):
    def body(idx, carry):
        g = idx // nc
        per_chunk(g, pl.ds(pl.multiple_of((idx - g * nc) * c, SUBLANES), c))
        return carry

    if lay.ts is None:
        lax.fori_loop(0, n_groups * nc, body, 0)
    else:
        assert nc == 1
        for g in range(n_groups):
            per_chunk(g, slice(None))


def _make_gla_kernel(lay, gb, tp, c, t_real):
    nb = lay.nb
    nc = tp // c
    cl = min(c, t_real)
    G, H = range(gb), range(HEADS)

    def kern(q_ref, k_ref, v_ref, gr_ref, sm_ref, s0_ref, wg_ref, bg_ref, ng_ref, o_ref, sn_ref, s_scr):
        @pl.when(pl.program_id(1) == 0)
        def _():
            s_scr[...] = s0_ref[...]

        _, _, tril, lower_ones, real = _chunk_consts(c, cl)
        ones_real = jnp.where(real, 1.0, 0.0).astype(BF) * jnp.ones((c, DV), BF)
        d_tn = lambda x: lax.dot_general(x, ones_real, (((0,), (0,)), ((), ())), preferred_element_type=F32)
        ng = ng_ref[...]
        wg = wg_ref[...]
        bg = bg_ref[...]

        def per_chunk(g, rows):
            seqs = [g * gb + j for j in G]
            lg = [-_softplus(-(_mm(lay.load(sm_ref, sq, rows), wg) + bg)) * (1.0 / GLA_TAU) for sq in seqs]
            sp = [_split3(x) for x in lg]
            b = [_dot(lower_ones, s[0]) + _dot(lower_ones, s[1]) + _dot(lower_ones, s[2]) for s in sp]
            b_last_col = [d_tn(s[0]) + d_tn(s[1]) + d_tn(s[2]) for s in sp]
            q_t = [lay.load(q_ref, sq, rows) * (DK ** -0.5) * jnp.exp(b[j]) for j, sq in enumerate(seqs)]
            k = [lay.load(k_ref, sq, rows) for sq in seqs]
            k_t = [jnp.where(real, k[j] * jnp.exp(-b[j]), 0.0) for j in G]
            k_d = [jnp.where(real, k[j] * jnp.exp(b[j][cl - 1:cl, :] - b[j]), 0.0) for j in G]
            v = [lay.load(v_ref, sq, rows) for sq in seqs]
            att = [[jnp.where(tril, _mm_nt(q_t[j][:, _ks(h)], k_t[j][:, _ks(h)]), 0.0) for h in H] for j in G]
            s = [[s_scr[seqs[j], h] for h in H] for j in G]
            kv = [[_mm_tn(k_d[j][:, _ks(h)], v[j][:, _vs(h)]) for h in H] for j in G]
            o = [[_mm(att[j][h], v[j][:, _vs(h)]) + _mm(q_t[j][:, _ks(h)], s[j][h]) for h in H] for j in G]
            for j in G:
                for h in H:
                    s_scr[seqs[j], h] = jnp.exp(b_last_col[j][_ks(h), :]) * s[j][h] + kv[j][h]
            for j in G:
                gr = lay.load(gr_ref, seqs[j], rows)
                for h in H:
                    lay.store(o_ref, seqs[j], rows, _vs(h), _rms(o[j][h], ng) * _silu(gr[:, _vs(h)]))

        _chunk_loop(lay, nb // gb, nc, c, per_chunk)
        sn_ref[...] = s_scr[...]

    return kern


def _gla(p, s0, lw, l, lay, gb, c, t_real, name, stacked=None):
    nb, tp = lay.nb, lay.tp
    nseq, tseq = lay.dims(p)
    qk = HEADS * DK
    vw = HEADS * DV
    wspec, pcol, _ = _mixer_specs(lay, l)
    sio = _StateIO(s0, stacked, l, nb)
    in_specs = [
        pcol(qk, COL_GQ), pcol(qk, COL_GK), pcol(vw, COL_GV), pcol(vw, COL_GR), pcol(128, COL_SM),
        sio.spec(), wspec((128, qk)), wspec((1, qk)), wspec((1, DV)),
    ]
    n_in = len(in_specs)
    return pl.pallas_call(
        sio.wrap(_make_gla_kernel(lay, gb, tp, c, t_real), n_in),
        grid=(nseq // nb, tseq // tp),
        in_specs=in_specs + sio.extra_specs(),
        out_specs=[pcol(vw, 0), sio.spec()],
        out_shape=[lay.out_shape(p, vw), sio.out_shape()],
        scratch_shapes=[pltpu.VMEM((nb, HEADS, DK, DV), F32)],
        input_output_aliases=sio.aliases(n_in, 1),
        compiler_params=_cparams(("parallel", "arbitrary")),
        name=name,
    )(p, p, p, p, p, s0, lw["gla_wg"], lw["gla_bg"], lw["gla_ng"], *sio.extra_inputs())


def _make_ret_kernel(lay, gb, tp, c, t_real):
    nb = lay.nb
    nc = tp // c
    cl = min(c, t_real)
    G, H = range(gb), range(HEADS)
    log_gamma = [float(np.log(np.float32(1.0) - np.float32(2.0) ** np.float32(-5.0 - h))) for h in range(HEADS)]

    def kern(q_ref, k_ref, qs_ref, ks_ref, v_ref, g_ref, cos_ref, sin_ref, s0_ref, ng_ref, o_ref, sn_ref, s_scr):
        @pl.when(pl.program_id(1) == 0)
        def _():
            s_scr[...] = s0_ref[...]

        row, col, tril, _, real = _chunk_consts(c, cl)
        diff = (row - col).astype(F32)
        ridx = lax.broadcasted_iota(jnp.int32, (c, 1), 0).astype(F32)
        ng = ng_ref[...]
        dmat = [jnp.where(tril, jnp.exp(jnp.maximum(diff, 0.0) * lgm), 0.0) for lgm in log_gamma]
        q_decay = [jnp.exp((ridx + 1.0) * lgm) for lgm in log_gamma]
        k_decay = [jnp.exp((cl - 1.0 - ridx) * lgm) for lgm in log_gamma]

        def per_chunk(g, rows):
            seqs = [g * gb + j for j in G]
            cos = cos_ref[rows, :]
            sin = sin_ref[rows, :]
            ld = lambda ref, sq: lay.load(ref, sq, rows)
            q = [ld(q_ref, sq) * cos + ld(qs_ref, sq) * sin for sq in seqs]
            k = [jnp.where(real, (ld(k_ref, sq) * cos + ld(ks_ref, sq) * sin) * (DK ** -0.5), 0.0) for sq in seqs]
            v = [ld(v_ref, sq) for sq in seqs]
            att = [[_mm_nt(q[j][:, _ks(h)], k[j][:, _ks(h)]) * dmat[h] for h in H] for j in G]
            s = [[s_scr[seqs[j], h] for h in H] for j in G]
            kv = [[_mm_tn(k[j][:, _ks(h)] * k_decay[h], v[j][:, _vs(h)]) for h in H] for j in G]
            o = [[_mm(att[j][h], v[j][:, _vs(h)]) + _mm(q[j][:, _ks(h)] * q_decay[h], s[j][h]) for h in H]
                 for j in G]
            for j in G:
                for h in H:
                    s_scr[seqs[j], h] = math.exp(cl * log_gamma[h]) * s[j][h] + kv[j][h]
            for j in G:
                gate = ld(g_ref, seqs[j])
                for h in H:
                    oc = o[j][h] - jnp.mean(o[j][h], axis=-1, keepdims=True)
                    on = oc * lax.rsqrt(jnp.mean(oc * oc, axis=-1, keepdims=True) + EPS) * ng
                    lay.store(o_ref, seqs[j], rows, _vs(h), on * _silu(gate[:, _vs(h)]))

        _chunk_loop(lay, nb // gb, nc, c, per_chunk)
        sn_ref[...] = s_scr[...]

    return kern


def _ret(p, s0, cos_t, sin_t, lw, l, lay, gb, c, t_real, name, stacked=None):
    nb, tp = lay.nb, lay.tp
    nseq, tseq = lay.dims(p)
    qk = HEADS * DK
    vw = HEADS * DV
    wspec, pcol, _ = _mixer_specs(lay, l)
    sio = _StateIO(s0, stacked, l, nb)
    in_specs = [
        pcol(qk, COL_RQ), pcol(qk, COL_RK), pcol(qk, COL_RQS), pcol(qk, COL_RKS), pcol(vw, COL_RV),
        pcol(vw, COL_RG),
        pl.BlockSpec((tp, qk), lambda b, t: (t, 0)),
        pl.BlockSpec((tp, qk), lambda b, t: (t, 0)),
        sio.spec(), wspec((1, DV)),
    ]
    n_in = len(in_specs)
    return pl.pallas_call(
        sio.wrap(_make_ret_kernel(lay, gb, tp, c, t_real), n_in),
        grid=(nseq // nb, tseq // tp),
        in_specs=in_specs + sio.extra_specs(),
        out_specs=[pcol(vw, 0), sio.spec()],
        out_shape=[lay.out_shape(p, vw), sio.out_shape()],
        scratch_shapes=[pltpu.VMEM((nb, HEADS, DK, DV), F32)],
        input_output_aliases=sio.aliases(n_in, 1),
        compiler_params=_cparams(("parallel", "arbitrary")),
        name=name,
    )(p, p, p, p, p, p, cos_t, sin_t, s0, lw["ret_ng"], *sio.extra_inputs())


def _unit_lower_inverse_many(ms, c):
    row = lax.broadcasted_iota(jnp.int32, (c, c), 0)
    col = lax.broadcasted_iota(jnp.int32, (c, c), 1)
    eye = jnp.where(row == col, 1.0, 0.0)

    def same_block(bits):
        return lax.shift_right_logical(row, bits) == lax.shift_right_logical(col, bits)

    in8 = same_block(3)
    n1 = [jnp.where(in8, m, 0.0) for m in ms]
    n2 = [_mm_hp(x, x) for x in n1]
    n4 = [_mm_hp(x, x) for x in n2]
    d = [eye - x for x in n1]
    d = [x + _mm_hp(x, y) for x, y in zip(d, n2)]
    d = [x + _mm_hp(x, y) for x, y in zip(d, n4)]
    bits = 3
    while (1 << bits) < c:
        sel = jnp.logical_and(same_block(bits + 1), jnp.logical_not(same_block(bits)))
        ld = [_mm(jnp.where(sel, m, 0.0), x) for m, x in zip(ms, d)]
        d = [x - _mm(x, y) for x, y in zip(d, ld)]
        bits += 1
    return d


def _make_dn_kernel(lay, gb, tp, c, t_real):
    nb = lay.nb
    nc = tp // c
    cl = min(c, t_real)
    G, H = range(gb), range(HEADS)
    GH = [(j, h) for j in G for h in H]

    def kern(x_ref, z_ref, sm_ref, c0_ref, s0_ref, cw_ref, alog_ref, dtb_ref, ng_ref,
             o_ref, cn_ref, sn_ref, xs_scr, qkv_scr, cc_scr, s_scr):
        @pl.when(pl.program_id(1) == 0)
        def _():
            cc_scr[...] = c0_ref[...]
            s_scr[...] = s0_ref[...]

        row, col, tril, lower_ones, real = _chunk_consts(c, cl)
        strict = row > col
        lane = lax.broadcasted_iota(jnp.int32, (c, 128), 1)
        sel = [jnp.where(lane == SM_DA + h, 1.0, 0.0).astype(BF) for h in H]
        d_nt = lambda a, x: lax.dot_general(a, x, (((1,), (1,)), ((), ())), preferred_element_type=F32)
        ng = ng_ref[...]
        cw = cw_ref[...]
        neg_a = -jnp.exp(alog_ref[...])
        dtb = dtb_ref[...]

        def per_group(g, carry):
            seqs = [g * gb + j for j in G]
            for j, sq in enumerate(seqs):
                xs_scr[j, 0:SUBLANES, :] = cc_scr[sq]
                xs_scr[j, SUBLANES:SUBLANES + tp, :] = lay.load(x_ref, sq)
            for j, sq in enumerate(seqs):
                qkv_scr[j] = _silu(_conv4(xs_scr, j, cw, tp))
                cc_scr[sq] = xs_scr[j, pl.ds(t_real, SUBLANES), :]

            def per_chunk(ci, carry2):
                rows = pl.ds(pl.multiple_of(ci * c, SUBLANES), c)
                sm = [lay.load(sm_ref, sq, rows) for sq in seqs]
                gcum = [_cumsum_rows(neg_a * _softplus(x + dtb), lower_ones) for x in sm]
                g3 = [_split3(x) for x in gcum]
                beta_all = [jnp.where(real, jax.nn.sigmoid(x), 0.0) for x in sm]

                def head_in(j, h, part):
                    x = qkv_scr[j, rows, part * BRANCH + h * DV:part * BRANCH + (h + 1) * DV]
                    return x

                qh = {jh: head_in(*jh, 0) for jh in GH}
                kh = {jh: head_in(*jh, 1) for jh in GH}
                vh = {jh: head_in(*jh, 2) for jh in GH}
                qh = {jh: x * lax.rsqrt(jnp.sum(x * x, axis=-1, keepdims=True) + EPS) * (DV ** -0.5)
                      for jh, x in qh.items()}
                kh = {jh: jnp.where(real, x * lax.rsqrt(jnp.sum(x * x, axis=-1, keepdims=True) + EPS), 0.0)
                      for jh, x in kh.items()}
                beta = {(j, h): beta_all[j][:, SM_DB + h:SM_DB + h + 1] for j, h in GH}
                gcol = {(j, h): gcum[j][:, SM_DA + h:SM_DA + h + 1] for j, h in GH}
                grow = {(j, h): d_nt(sel[h], g3[j][0]) + d_nt(sel[h], g3[j][1]) + d_nt(sel[h], g3[j][2])
                        for j, h in GH}
                decay = {jh: jnp.where(tril, jnp.exp(jnp.where(tril, gcol[jh] - grow[jh], 0.0)), 0.0) for jh in GH}
                e_g = {jh: jnp.exp(gcol[jh]) for jh in GH}
                kb = {jh: kh[jh] * beta[jh] for jh in GH}
                kq = {jh: _mm_nt(jnp.concatenate([kb[jh], qh[jh]], axis=0), kh[jh]) for jh in GH}
                tm = _unit_lower_inverse_many([jnp.where(strict, kq[jh][:c] * decay[jh], 0.0) for jh in GH], c)
                tm = dict(zip(GH, tm))
                uw = {jh: _mm(tm[jh], jnp.concatenate([vh[jh] * beta[jh], kb[jh] * e_g[jh]], axis=1)) for jh in GH}
                s = {(j, h): s_scr[seqs[j], h] for j, h in GH}
                wq = {jh: _mm(jnp.concatenate([uw[jh][:, DV:], qh[jh] * e_g[jh]], axis=0), s[jh]) for jh in GH}
                v_new = {jh: uw[jh][:, :DV] - wq[jh][:c] for jh in GH}
                o = {jh: wq[jh][c:] + _mm(kq[jh][c:] * decay[jh], v_new[jh]) for jh in GH}
                for j, h in GH:
                    g_last = gcol[(j, h)][cl - 1:cl, :]
                    k_d = kh[(j, h)] * jnp.exp(g_last - gcol[(j, h)])
                    s_scr[seqs[j], h] = jnp.exp(g_last) * s[(j, h)] + _mm_tn(k_d, v_new[(j, h)])
                for j in G:
                    z = lay.load(z_ref, seqs[j], rows)
                    for h in H:
                        lay.store(o_ref, seqs[j], rows, _vs(h), _rms(o[(j, h)], ng) * _silu(z[:, _vs(h)]))
                return carry2

            lax.fori_loop(0, nc, per_chunk, 0)
            return carry

        lay.for_groups(nb // gb, per_group)
        cn_ref[...] = cc_scr[...]
        sn_ref[...] = s_scr[...]

    return kern


def _dn(p, c0, s0, lw, l, lay, gb, c, t_real, name, stacked=None):
    nb, tp = lay.nb, lay.tp
    nseq, tseq = lay.dims(p)
    cw3 = 3 * BRANCH
    wspec, pcol, state = _mixer_specs(lay, l)
    sio = _StateIO(s0, stacked, l, nb)
    in_specs = [
        pcol(cw3, COL_DQKV), pcol(BRANCH, COL_DZ), pcol(128, COL_SM),
        state(SUBLANES, cw3), sio.spec(),
        wspec((CONV_W, cw3)), wspec((1, 128)), wspec((1, 128)), wspec((1, DV)),
    ]
    n_in = len(in_specs)
    return pl.pallas_call(
        sio.wrap(_make_dn_kernel(lay, gb, tp, c, t_real), n_in),
        grid=(nseq // nb, tseq // tp),
        in_specs=in_specs + sio.extra_specs(),
        out_specs=[pcol(BRANCH, 0), state(SUBLANES, cw3), sio.spec()],
        out_shape=[
            lay.out_shape(p, BRANCH),
            jax.ShapeDtypeStruct((nseq, SUBLANES, cw3), F32),
            sio.out_shape(),
        ],
        scratch_shapes=[
            pltpu.VMEM((gb, tp + SUBLANES, cw3), F32), pltpu.VMEM((gb, tp, cw3), F32),
            pltpu.VMEM((nb, SUBLANES, cw3), F32), pltpu.VMEM((nb, HEADS, DV, DV), F32),
        ],
        input_output_aliases=sio.aliases(n_in, 2),
        compiler_params=_cparams(("parallel", "arbitrary")),
        name=name,
    )(p, p, p, c0, s0, lw["dn_conv_w"], lw["dn_alog"], lw["dn_dtb"], lw["dn_ng"], *sio.extra_inputs())


IN_SIZES = (BRANCH, BRANCH, HEADS * DK, HEADS * DK, BRANCH, BRANCH, GLA_RANK,
            HEADS * DK, HEADS * DK, BRANCH, BRANCH, BRANCH, BRANCH, BRANCH, BRANCH, HEADS, HEADS)


def _pack_kernel(w_ref, o_ref):
    w = w_ref[...]
    offs = np.concatenate([[0], np.cumsum(IN_SIZES)])
    seg = lambda i: w[:, int(offs[i]):int(offs[i + 1])]
    (lx, ly, gq, gk, gv, gr, glr, rq, rk, rv, rg, dq, dk, dv, dz, db, da) = [seg(i) for i in range(len(IN_SIZES))]

    def swap_halves(x):
        parts = []
        for h in range(HEADS):
            parts += [x[:, h * DK + DK // 2:(h + 1) * DK], x[:, h * DK:h * DK + DK // 2]]
        return jnp.concatenate(parts, axis=1)

    used = COL_SM + GLA_RANK + 2 * HEADS
    tail = jnp.zeros((w.shape[0], N_PACK - used), w.dtype)
    packed = jnp.concatenate([lx, ly, gq, gk, gv, gr, rq, rk, swap_halves(rq), swap_halves(rk), rv, rg,
                              dq, dk, dv, dz, glr, db, da, tail], axis=1)
    o_ref[...] = packed.astype(BF)


def _pack_w_in(w_in):
    depth, d, n_in = w_in.shape
    assert n_in == sum(IN_SIZES)
    td = _pick_tile(d, 256, 16)
    return pl.pallas_call(
        _pack_kernel,
        grid=(depth, d // td),
        in_specs=[pl.BlockSpec((None, td, n_in), lambda l, i: (l, i, 0))],
        out_specs=pl.BlockSpec((None, td, N_PACK), lambda l, i: (l, i, 0)),
        out_shape=jax.ShapeDtypeStruct((depth, d, N_PACK), BF),
        compiler_params=_cparams(("parallel", "parallel")),
        name="pack_w_in",
    )(w_in)


def _rope_tables(pos):
    half = DK // 2
    inv = ROPE_BASE ** (-jnp.arange(half, dtype=F32) / half)
    ang = pos.astype(F32)[:, None] * inv[None, :]
    cos, sin = jnp.cos(ang), jnp.sin(ang)
    cos_t = jnp.tile(jnp.concatenate([cos, cos], axis=1), (1, HEADS))
    sin_t = jnp.tile(jnp.concatenate([-sin, sin], axis=1), (1, HEADS))
    return cos_t, sin_t


def _lane_row(vals, off):
    depth = vals.shape[0]
    return jnp.zeros((depth, 1, 128), F32).at[:, 0, off:off + HEADS].set(vals.astype(F32))


def kernel(x_prompt, x_sample, state_lru_conv, state_lru_h, state_gla, state_ret, state_dn_conv, state_dn,
           norm_g, final_norm_g, w_ff_in, w_ff_out, w_in, w_gate, w_branch, w_out,
           lru_conv_w, lru_conv_b, lru_wa, lru_ba, lru_wx, lru_bx, lru_lambda,
           gla_wg, gla_bg, gla_norm_g, ret_norm_g, dn_conv_w, dn_a_log, dn_dt_bias, dn_norm_g):
    bp, tpr, d = x_prompt.shape
    bs, ts, _ = x_sample.shape
    depth = w_in.shape[0]
    n_p, n_s = bp * tpr, bs * ts
    assert d == D_MODEL and ts <= SAMPLE_PAD and tpr % CHUNK == 0

    w_pack = _pack_w_in(w_in)
    w_branch_b = w_branch.astype(BF)
    w_out_b = w_out.astype(BF)
    wg_pad = jnp.zeros((depth, 128, HEADS * DK), F32).at[:, SM_GLR:SM_GLR + GLA_RANK, :].set(gla_wg).astype(BF)
    r3 = lambda a: a.reshape(a.shape[0], 1, a.shape[1])
    lw = {
        "conv_w": lru_conv_w, "conv_b": r3(lru_conv_b), "wa": lru_wa.astype(BF), "ba": r3(lru_ba),
        "wx": lru_wx.astype(BF), "bx": r3(lru_bx), "lam": r3(lru_lambda),
        "gla_wg": wg_pad, "gla_bg": r3(gla_bg), "gla_ng": r3(gla_norm_g), "ret_ng": r3(ret_norm_g),
        "dn_conv_w": dn_conv_w, "dn_alog": _lane_row(dn_a_log, SM_DA), "dn_dtb": _lane_row(dn_dt_bias, SM_DA),
        "dn_ng": r3(dn_norm_g),
    }
    norm_g4 = norm_g.reshape(depth, 3, 1, d)

    cos_p, sin_p = _rope_tables(jnp.arange(tpr))
    pos_s = jnp.where(jnp.arange(SAMPLE_PAD) < ts, PAST_LEN + jnp.arange(SAMPLE_PAD), 0)
    cos_s, sin_s = _rope_tables(pos_s)

    pad_conv = lambda a: jnp.pad(a, ((0, 0), (SUBLANES - (CONV_W - 1), 0), (0, 0)))
    zeros = lambda *s: jnp.zeros(s, F32)

    tp_p = _pick_tile(tpr, 256, CHUNK)
    a_p = dict(lay=_SeqLayout(bp, tp_p), gb=bp)
    nb_s = _pick_tile(bs, 8, 2)
    assert (nb_s * ts) % SUBLANES == 0
    a_s = dict(lay=_SeqLayout(nb_s, SAMPLE_PAD, ts), gb=_pick_tile(nb_s, 4, 1))

    fg = final_norm_g.reshape(1, d)
    xp = x_prompt.reshape(n_p, d)
    xs = x_sample.reshape(n_s, d)
    new_p = [[] for _ in range(6)]
    new_s = [[] for _ in range(6)]
    s_gla_s = s_ret_s = s_dn_s = None
    for l in range(depth):
        xs, *w_ffn = _ffn_cast(xs, norm_g4[l, 0], w_ff_in, w_ff_out, l, 0)
        xp = _ffn(xp, norm_g4[l, 0], *w_ffn)
        p_p = _inproj(xp, norm_g4[l, 1], w_pack, l).reshape(bp, tpr, N_PACK)
        p_s = _inproj(xs, norm_g4[l, 1], w_pack, l)

        o_lru_p, c_lru_p, h_lru_p = _lru(p_p, zeros(bp, SUBLANES, BRANCH), zeros(bp, 1, BRANCH), lw, l,
                                         t_real=tp_p, is_prompt=True, **a_p)
        o_gla_p, s_gla_p = _gla(p_p, zeros(bp, HEADS, DK, DV), lw, l, c=CHUNK, t_real=tp_p, name="gla_prompt", **a_p)
        o_ret_p, s_ret_p = _ret(p_p, zeros(bp, HEADS, DK, DV), cos_p, sin_p, lw, l, c=CHUNK, t_real=tp_p,
                                name="ret_prompt", **a_p)
        o_dn_p, c_dn_p, s_dn_p = _dn(p_p, zeros(bp, SUBLANES, 3 * BRANCH), zeros(bp, HEADS, DV, DV), lw, l,
                                     c=CHUNK, t_real=tp_p, name="dn_prompt", **a_p)

        o_lru_s, c_lru_s, h_lru_s = _lru(p_s, pad_conv(state_lru_conv[l]), state_lru_h[l][:, None, :], lw, l,
                                         t_real=ts, is_prompt=False, **a_s)
        o_gla_s, s_gla_s = _gla(p_s, state_gla, lw, l, c=SAMPLE_PAD, t_real=ts, name="gla_sample",
                                stacked=(depth, s_gla_s), **a_s)
        o_ret_s, s_ret_s = _ret(p_s, state_ret, cos_s, sin_s, lw, l, c=SAMPLE_PAD, t_real=ts, name="ret_sample",
                                stacked=(depth, s_ret_s), **a_s)
        o_dn_s, c_dn_s, s_dn_s = _dn(p_s, pad_conv(state_dn_conv[l]), state_dn, lw, l, c=SAMPLE_PAD, t_real=ts,
                                     name="dn_sample", stacked=(depth, s_dn_s), **a_s)

        flat_p = lambda o: o.reshape(n_p, BRANCH)
        xs, w_gate_l = _merge_cast(xs, norm_g4[l, 1], [o_lru_s, o_gla_s, o_ret_s, o_dn_s],
                                   w_gate, w_branch_b, w_out_b, l)
        xp = _merge_split(xp, norm_g4[l, 1], [flat_p(o) for o in (o_lru_p, o_gla_p, o_ret_p, o_dn_p)],
                          w_gate_l, w_branch_b, w_out_b, l)
        xs, *w_ffn = _ffn_cast(xs, norm_g4[l, 2], w_ff_in, w_ff_out, l, 1)
        xp = _ffn(xp, norm_g4[l, 2], *w_ffn, final_g=fg if l == depth - 1 else None)

        tail3 = lambda cwin: cwin[:, SUBLANES - (CONV_W - 1):, :]
        for i, v in enumerate((tail3(c_lru_p), h_lru_p[:, 0], s_gla_p, s_ret_p, tail3(c_dn_p), s_dn_p)):
            new_p[i].append(v)
        for i, v in ((0, tail3(c_lru_s)), (1, h_lru_s[:, 0]), (4, tail3(c_dn_s))):
            new_s[i].append(v)

    y_prompt = xp.reshape(bp, tpr, d)
    y_sample = _final_norm(xs, fg).reshape(bs, ts, d)
    sp = [jnp.stack(v) for v in new_p]
    s_small = {i: jnp.stack(new_s[i]) for i in (0, 1, 4)}
    return tuple([y_prompt, y_sample] + sp + [s_small[0], s_small[1], s_gla_s, s_ret_s, s_small[4], s_dn_s])
```

```python
import functools
import math

import numpy as np
import jax
import jax.numpy as jnp
from jax import lax
from jax.experimental import pallas as pl
from jax.experimental.pallas import tpu as pltpu

F32 = jnp.float32
BF = jnp.bfloat16
EPS = 1e-6

D_MODEL = 2048
BRANCH = 512
CONV_W = 4
HEADS = 4
DK = 64
DV = 128
GLA_RANK = 16
GLA_TAU = 16.0
LRU_C = 8.0
LRU_BD = 128
ROPE_BASE = 10000.0
CHUNK = 64
PAST_LEN = 16384

V7X_VMEM_BYTES = 64 * 1024 * 1024
VMEM_LIMIT = V7X_VMEM_BYTES - 8 * 1024 * 1024
SUBLANES = 8
SAMPLE_PAD = 8

COL_LX, COL_LY = 0, 512
COL_GQ, COL_GK, COL_GV, COL_GR = 1024, 1280, 1536, 2048
COL_RQ, COL_RK, COL_RQS, COL_RKS, COL_RV, COL_RG = 2560, 2816, 3072, 3328, 3584, 4096
COL_DQKV, COL_DZ = 4608, 6144
COL_SM = 6656
SM_GLR, SM_DB, SM_DA = 0, 16, 20
N_PACK = 6912


def _dot(a, b):
    return jnp.dot(a, b, preferred_element_type=F32)


def _mm(a, b):
    return _dot(a.astype(BF), b.astype(BF))


def _mm_nt(a, b):
    return lax.dot_general(a.astype(BF), b.astype(BF), (((1,), (1,)), ((), ())), preferred_element_type=F32)


def _mm_tn(a, b):
    return lax.dot_general(a.astype(BF), b.astype(BF), (((0,), (0,)), ((), ())), preferred_element_type=F32)


def _split3(x):
    hi = x.astype(BF)
    r = x - hi.astype(F32)
    mid = r.astype(BF)
    lo = (r - mid.astype(F32)).astype(BF)
    return hi, mid, lo


def _split2(x):
    hi = x.astype(BF)
    return hi, (x - hi.astype(F32)).astype(BF)


def _mm_hp(a, b):
    ah, al = _split2(a)
    bh, bl = _split2(b)
    return _dot(ah, bh) + _dot(ah, bl) + _dot(al, bh)


def _cumsum_rows(x, lower_ones):
    hi, mid, lo = _split3(x)
    return _dot(lower_ones, hi) + _dot(lower_ones, mid) + _dot(lower_ones, lo)


def _softplus(x):
    return jnp.maximum(x, 0.0) + jnp.log1p(jnp.exp(-jnp.abs(x)))


def _silu(x):
    return x * jax.nn.sigmoid(x)


def _gelu_tanh(x):
    return x * (0.5 * (1.0 + jnp.tanh(math.sqrt(2.0 / math.pi) * (x + 0.044715 * (x * x * x)))))


def _rms(x, g):
    return x * lax.rsqrt(jnp.mean(x * x, axis=-1, keepdims=True) + EPS) * g


def _pick_tile(n, target, align):
    best = None
    for t in range(align, min(n, target) + 1, align):
        if n % t == 0:
            best = t
    assert best is not None, (n, target, align)
    return best


def _cparams(sem):
    return pltpu.CompilerParams(dimension_semantics=sem, vmem_limit_bytes=VMEM_LIMIT)


def _ffn_kernel(x_ref, g_ref, wg_ref, wu_ref, wo_ref, o_ref, hn_ref):
    @pl.when(pl.program_id(1) == 0)
    def _():
        x = x_ref[...]
        hn_ref[...] = _rms(x, g_ref[...]).astype(BF)
        o_ref[...] = x

    hn = hn_ref[...]
    gate = _dot(hn, wg_ref[...])
    up = _dot(hn, wu_ref[...])
    act = (_silu(gate) * up * 0.5).astype(BF)
    o_ref[...] += _dot(act, wo_ref[...])


def _ffn_final_kernel(x_ref, g_ref, wg_ref, wu_ref, wo_ref, fg_ref, o_ref, hn_ref):
    _ffn_kernel(x_ref, g_ref, wg_ref, wu_ref, wo_ref, o_ref, hn_ref)

    @pl.when(pl.program_id(1) == pl.num_programs(1) - 1)
    def _():
        o_ref[...] = _rms(o_ref[...], fg_ref[...])


def _ffn(x, g, wg, wu, wo, final_g=None):
    n, d = x.shape
    f = wo.shape[0]
    tm = _pick_tile(n, 1024, 16)
    tf = _pick_tile(f, 512, 128)
    final = [] if final_g is None else [final_g]
    return pl.pallas_call(
        _ffn_kernel if final_g is None else _ffn_final_kernel,
        grid=(n // tm, f // tf),
        in_specs=[
            pl.BlockSpec((tm, d), lambda i, j: (i, 0)),
            pl.BlockSpec((1, d), lambda i, j: (0, 0)),
            pl.BlockSpec((d, tf), lambda i, j: (0, j)),
            pl.BlockSpec((d, tf), lambda i, j: (0, j)),
            pl.BlockSpec((tf, d), lambda i, j: (j, 0)),
        ] + [pl.BlockSpec((1, d), lambda i, j: (0, 0))] * len(final),
        out_specs=pl.BlockSpec((tm, d), lambda i, j: (i, 0)),
        out_shape=jax.ShapeDtypeStruct((n, d), F32),
        scratch_shapes=[pltpu.VMEM((tm, d), BF)],
        compiler_params=_cparams(("parallel", "arbitrary")),
        name="ffn",
    )(x, g, wg, wu, wo, *final)


def _ffn_cast_kernel(x_ref, g_ref, wg32_ref, wu32_ref, wo32_ref, o_ref, wg_ref, wu_ref, wo_ref, hn_ref):
    @pl.when(pl.program_id(1) == 0)
    def _():
        x = x_ref[...]
        hn_ref[...] = _rms(x, g_ref[...]).astype(BF)
        o_ref[...] = x

    wg_ref[...] = wg32_ref[...].astype(BF)
    wu_ref[...] = wu32_ref[...].astype(BF)
    wo_ref[...] = wo32_ref[...].astype(BF)
    hn = hn_ref[...]
    act = (_silu(_dot(hn, wg_ref[...])) * _dot(hn, wu_ref[...]) * 0.5).astype(BF)
    o_ref[...] += _dot(act, wo_ref[...])


def _ffn_cast(x, g, w_in, w_out, l, s):
    n, d = x.shape
    f = w_out.shape[2]
    tf = _pick_tile(f, 512, 128)
    nf = f // tf
    once = dict(pipeline_mode=pl.Buffered(1))
    return pl.pallas_call(
        _ffn_cast_kernel,
        grid=(1, nf),
        in_specs=[
            pl.BlockSpec((n, d), lambda i, j: (0, 0), **once),
            pl.BlockSpec((1, d), lambda i, j: (0, 0)),
            pl.BlockSpec((None, None, d, tf), lambda i, j: (l, s, 0, j)),
            pl.BlockSpec((None, None, d, tf), lambda i, j: (l, s, 0, j + nf)),
            pl.BlockSpec((None, None, tf, d), lambda i, j: (l, s, j, 0)),
        ],
        out_specs=[
            pl.BlockSpec((n, d), lambda i, j: (0, 0), **once),
            pl.BlockSpec((d, tf), lambda i, j: (0, j)),
            pl.BlockSpec((d, tf), lambda i, j: (0, j)),
            pl.BlockSpec((tf, d), lambda i, j: (j, 0)),
        ],
        out_shape=[
            jax.ShapeDtypeStruct((n, d), F32),
            jax.ShapeDtypeStruct((d, f), BF), jax.ShapeDtypeStruct((d, f), BF), jax.ShapeDtypeStruct((f, d), BF),
        ],
        scratch_shapes=[pltpu.VMEM((n, d), BF)],
        compiler_params=_cparams(("arbitrary", "arbitrary")),
        name="ffn_cast",
    )(x, g, w_in, w_in, w_out)


def _inproj_kernel(x_ref, g_ref, w_ref, o_ref, hn_ref):
    @pl.when(pl.program_id(1) == 0)
    def _():
        hn_ref[...] = _rms(x_ref[...], g_ref[...]).astype(BF)

    o_ref[...] = _mm_nt(hn_ref[...], w_ref[...])


def _inproj(x, g, w_pack, l):
    n, d = x.shape
    npk = w_pack.shape[1]
    tm = _pick_tile(n, 1024, 16)
    tn = _pick_tile(npk, 1024, 128)
    return pl.pallas_call(
        _inproj_kernel,
        grid=(n // tm, npk // tn),
        in_specs=[
            pl.BlockSpec((tm, d), lambda i, j: (i, 0)),
            pl.BlockSpec((1, d), lambda i, j: (0, 0)),
            pl.BlockSpec((None, tn, d), lambda i, j: (l, j, 0)),
        ],
        out_specs=pl.BlockSpec((tm, tn), lambda i, j: (i, j)),
        out_shape=jax.ShapeDtypeStruct((n, npk), F32),
        scratch_shapes=[pltpu.VMEM((tm, d), BF)],
        compiler_params=_cparams(("parallel", "arbitrary")),
        name="inproj",
    )(x, g, w_pack)


def _merge_kernel(x_ref, g_ref, b0_ref, b1_ref, b2_ref, b3_ref, wgate_ref, wbr_ref, wo_ref, o_ref, hn_ref):
    @pl.when(pl.program_id(1) == 0)
    def _():
        x = x_ref[...]
        hn_ref[...] = _rms(x, g_ref[...]).astype(BF)
        o_ref[...] = x

    hn = hn_ref[...]
    m = None
    for n, b_ref in enumerate((b0_ref, b1_ref, b2_ref, b3_ref)):
        gate = jax.nn.sigmoid(_dot(hn, wgate_ref[n]))
        br = _dot(b_ref[...].astype(BF), wbr_ref[n])
        m = gate * br if m is None else m + gate * br
    o_ref[...] += _dot(m.astype(BF), wo_ref[...])


def _gate_kernel(x_ref, g_ref, b0_ref, b1_ref, b2_ref, b3_ref, wgate_ref, wbr_ref, m_ref, hn_ref):
    @pl.when(pl.program_id(1) == 0)
    def _():
        hn_ref[...] = _rms(x_ref[...], g_ref[...]).astype(BF)

    hn = hn_ref[...]
    m = None
    for n, b_ref in enumerate((b0_ref, b1_ref, b2_ref, b3_ref)):
        gate = jax.nn.sigmoid(_dot(hn, wgate_ref[n]))
        br = _dot(b_ref[...].astype(BF), wbr_ref[n])
        m = gate * br if m is None else m + gate * br
    m_ref[...] = m.astype(BF)


def _outproj_kernel(m_ref, w_ref, x_ref, o_ref):
    o_ref[...] = x_ref[...] + _dot(m_ref[...], w_ref[...])


def _merge_split(x, g, branches, w_gate_l, w_branch, w_out, l):
    n, d = x.shape
    tm = _pick_tile(n, 1024, 16)
    tn = 256
    m = pl.pallas_call(
        _gate_kernel,
        grid=(n // tm, d // tn),
        in_specs=[
            pl.BlockSpec((tm, d), lambda i, j: (i, 0)),
            pl.BlockSpec((1, d), lambda i, j: (0, 0)),
        ] + [pl.BlockSpec((tm, BRANCH), lambda i, j: (i, 0))] * HEADS + [
            pl.BlockSpec((HEADS, d, tn), lambda i, j: (0, 0, j)),
            pl.BlockSpec((None, HEADS, BRANCH, tn), lambda i, j: (l, 0, 0, j)),
        ],
        out_specs=pl.BlockSpec((tm, tn), lambda i, j: (i, j)),
        out_shape=jax.ShapeDtypeStruct((n, d), BF),
        scratch_shapes=[pltpu.VMEM((tm, d), BF)],
        compiler_params=_cparams(("parallel", "arbitrary")),
        name="gate",
    )(x, g, *branches, w_gate_l, w_branch)
    to = 512
    return pl.pallas_call(
        _outproj_kernel,
        grid=(n // tm, d // to),
        in_specs=[
            pl.BlockSpec((tm, d), lambda i, j: (i, 0)),
            pl.BlockSpec((None, d, to), lambda i, j: (l, 0, j)),
            pl.BlockSpec((tm, to), lambda i, j: (i, j)),
        ],
        out_specs=pl.BlockSpec((tm, to), lambda i, j: (i, j)),
        out_shape=jax.ShapeDtypeStruct((n, d), F32),
        compiler_params=_cparams(("parallel", "arbitrary")),
        name="outproj",
    )(m, w_out, x)


def _merge_cast_kernel(x_ref, g_ref, b0_ref, b1_ref, b2_ref, b3_ref, wgate32_ref, wbr_ref, wo_ref,
                       o_ref, wgate_ref, hn_ref):
    wgate_ref[...] = wgate32_ref[...].astype(BF)
    _merge_kernel(x_ref, g_ref, b0_ref, b1_ref, b2_ref, b3_ref, wgate_ref, wbr_ref, wo_ref, o_ref, hn_ref)


def _merge_cast(x, g, branches, w_gate, w_branch, w_out, l):
    n, d = x.shape
    tn = 256
    once = dict(pipeline_mode=pl.Buffered(1))
    return pl.pallas_call(
        _merge_cast_kernel,
        grid=(1, d // tn),
        in_specs=[
            pl.BlockSpec((n, d), lambda i, j: (0, 0), **once),
            pl.BlockSpec((1, d), lambda i, j: (0, 0)),
        ] + [pl.BlockSpec((n, BRANCH), lambda i, j: (0, 0), **once)] * HEADS + [
            pl.BlockSpec((None, HEADS, d, tn), lambda i, j: (l, 0, 0, j)),
            pl.BlockSpec((None, HEADS, BRANCH, tn), lambda i, j: (l, 0, 0, j)),
            pl.BlockSpec((None, tn, d), lambda i, j: (l, j, 0)),
        ],
        out_specs=[
            pl.BlockSpec((n, d), lambda i, j: (0, 0), **once),
            pl.BlockSpec((HEADS, d, tn), lambda i, j: (0, 0, j)),
        ],
        out_shape=[jax.ShapeDtypeStruct((n, d), F32), jax.ShapeDtypeStruct((HEADS, d, d), BF)],
        scratch_shapes=[pltpu.VMEM((n, d), BF)],
        compiler_params=_cparams(("arbitrary", "arbitrary")),
        name="merge_cast",
    )(x, g, *branches, w_gate, w_branch, w_out)


def _final_norm_kernel(x_ref, g_ref, o_ref):
    o_ref[...] = _rms(x_ref[...], g_ref[...])


def _final_norm(x, g):
    n, d = x.shape
    tm = _pick_tile(n, 512, 8)
    return pl.pallas_call(
        _final_norm_kernel,
        grid=(n // tm,),
        in_specs=[pl.BlockSpec((tm, d), lambda i: (i, 0)), pl.BlockSpec((1, d), lambda i: (0, 0))],
        out_specs=pl.BlockSpec((tm, d), lambda i: (i, 0)),
        out_shape=jax.ShapeDtypeStruct((n, d), F32),
        compiler_params=_cparams(("parallel",)),
        name="final_norm",
    )(x, g)


def _conv4(xs_ref, j_seq, w, tp):
    xs = xs_ref[j_seq]
    y = xs[SUBLANES:] * w[CONV_W - 1:CONV_W, :]
    for back in range(1, CONV_W):
        y = y + pltpu.roll(xs, back, axis=0)[SUBLANES:] * w[CONV_W - 1 - back:CONV_W - back, :]
    return y


class _SeqLayout:
    def __init__(self, nb, tp, ts=None):
        self.nb, self.tp, self.ts = nb, tp, ts

    def load(self, ref, sq, rows=slice(None), cols=slice(None)):
        if self.ts is None:
            return ref[sq, rows, cols]
        x = ref[sq * self.ts:(sq + 1) * self.ts, cols]
        return jnp.concatenate([x, jnp.zeros((self.tp - self.ts, x.shape[1]), x.dtype)], axis=0)

    def store(self, ref, sq, rows, cols, val):
        if self.ts is None:
            ref[sq, rows, cols] = val
        else:
            ref[sq * self.ts:(sq + 1) * self.ts, cols] = val[:self.ts]

    def for_groups(self, n, body):
        if self.ts is None:
            lax.fori_loop(0, n, body, 0)
        else:
            for g in range(n):
                body(g, 0)

    def pcol(self, w, off):
        if self.ts is None:
            return pl.BlockSpec((self.nb, self.tp, w), lambda b, t: (b, t, off // w))
        return pl.BlockSpec((self.nb * self.ts, w), lambda b, t: (b, off // w))

    def dims(self, p):
        return (p.shape[0], p.shape[1]) if self.ts is None else (p.shape[0] // self.ts, self.tp)

    def out_shape(self, p, w):
        return jax.ShapeDtypeStruct(p.shape[:-1] + (w,), F32)


def _make_lru_kernel(lay, gb, tp, t_real, is_prompt):
    nb = lay.nb

    def kern(lx_ref, ly_ref, c0_ref, h0_ref, cw_ref, cb_ref, wa_ref, ba_ref, wx_ref, bx_ref, lam_ref,
             o_ref, cn_ref, hn_ref, xs_scr, a_scr, u_scr, cc_scr, ch_scr):
        tb = pl.program_id(1)

        @pl.when(tb == 0)
        def _():
            cc_scr[...] = c0_ref[...]
            ch_scr[...] = h0_ref[...]

        cw = cw_ref[...]
        neg_sp = -LRU_C * _softplus(-lam_ref[...])
        row = lax.broadcasted_iota(jnp.int32, (tp, 1), 0)

        def per_group(g, carry):
            seqs = [g * gb + j for j in range(gb)]
            for j, sq in enumerate(seqs):
                xs_scr[j, 0:SUBLANES, :] = cc_scr[sq]
                xs_scr[j, SUBLANES:SUBLANES + tp, :] = lay.load(lx_ref, sq)
            for j, sq in enumerate(seqs):
                xc = _conv4(xs_scr, j, cw, tp) + cb_ref[...]
                cc_scr[sq] = xs_scr[j, pl.ds(t_real, SUBLANES), :]
                xcb = xc.astype(BF)
                r_parts, i_parts = [], []
                for n in range(BRANCH // LRU_BD):
                    blk = xcb[:, n * LRU_BD:(n + 1) * LRU_BD]
                    r_parts.append(_dot(blk, wa_ref[n]))
                    i_parts.append(_dot(blk, wx_ref[n]))
                r = jax.nn.sigmoid(jnp.concatenate(r_parts, axis=1) + ba_ref[...])
                ig = jax.nn.sigmoid(jnp.concatenate(i_parts, axis=1) + bx_ref[...])
                log_a = r * neg_sp
                a = jnp.exp(log_a)
                mult = jnp.sqrt(-jnp.tanh(log_a) * (a * a + 1.0))
                if is_prompt:
                    mult = jnp.where(jnp.logical_and(row == 0, tb == 0), 1.0, mult)
                a_scr[j] = a
                u_scr[j] = mult * ig * xc

            def step(t, hs):
                out = []
                for j in range(gb):
                    h = a_scr[j, pl.ds(t, 1), :] * hs[j] + u_scr[j, pl.ds(t, 1), :]
                    a_scr[j, pl.ds(t, 1), :] = h
                    out.append(h)
                return tuple(out)

            h_last = lax.fori_loop(0, t_real, step, tuple(ch_scr[sq] for sq in seqs),
                                   unroll=True if t_real <= SUBLANES else 8)
            for j, sq in enumerate(seqs):
                ch_scr[sq] = h_last[j]
                lay.store(o_ref, sq, slice(None), slice(None), a_scr[j] * _gelu_tanh(lay.load(ly_ref, sq)))
            return carry

        lay.for_groups(nb // gb, per_group)
        cn_ref[...] = cc_scr[...]
        hn_ref[...] = ch_scr[...]

    return kern


def _mixer_specs(lay, l):
    wspec = lambda shape: pl.BlockSpec((None,) + shape, lambda b, t: (l,) + (0,) * len(shape))
    state = lambda *shape: pl.BlockSpec((lay.nb,) + shape, lambda b, t: (b,) + (0,) * len(shape))
    return wspec, lay.pcol, state


class _StateIO:
    def __init__(self, s0, stacked, l, nb):
        self.s0, self.l, self.nb = s0, l, nb
        self.stacked = stacked is not None
        self.prev = stacked[1] if self.stacked else None
        self.depth = stacked[0] if self.stacked else None

    def spec(self):
        shape = self.s0.shape[2:] if self.stacked else self.s0.shape[1:]
        zeros = (0,) * len(shape)
        if self.stacked:
            l = self.l
            return pl.BlockSpec((None, self.nb) + shape, lambda b, t: (l, b) + zeros)
        return pl.BlockSpec((self.nb,) + shape, lambda b, t: (b,) + zeros)

    def out_shape(self):
        return jax.ShapeDtypeStruct(self.s0.shape, F32)

    def extra_inputs(self):
        return [self.prev] if self.prev is not None else []

    def extra_specs(self):
        return [pl.BlockSpec(memory_space=pl.ANY)] if self.prev is not None else []

    def aliases(self, n_in, out_idx):
        return {n_in: out_idx} if self.prev is not None else {}

    def wrap(self, kern, n_in):
        if self.prev is None:
            return kern
        return lambda *refs: kern(*refs[:n_in], *refs[n_in + 1:])


def _lru(p, c0, h0, lw, l, lay, gb, t_real, is_prompt):
    nb, tp = lay.nb, lay.tp
    nseq, tseq = lay.dims(p)
    c = BRANCH
    wspec, pcol, state = _mixer_specs(lay, l)
    return pl.pallas_call(
        _make_lru_kernel(lay, gb, tp, t_real, is_prompt),
        grid=(nseq // nb, tseq // tp),
        in_specs=[
            pcol(c, COL_LX), pcol(c, COL_LY), state(SUBLANES, c), state(1, c),
            wspec((CONV_W, c)), wspec((1, c)),
            wspec((c // LRU_BD, LRU_BD, LRU_BD)), wspec((1, c)),
            wspec((c // LRU_BD, LRU_BD, LRU_BD)), wspec((1, c)),
            wspec((1, c)),
        ],
        out_specs=[pcol(c, 0), state(SUBLANES, c), state(1, c)],
        out_shape=[
            lay.out_shape(p, c),
            jax.ShapeDtypeStruct((nseq, SUBLANES, c), F32),
            jax.ShapeDtypeStruct((nseq, 1, c), F32),
        ],
        scratch_shapes=[
            pltpu.VMEM((gb, tp + SUBLANES, c), F32), pltpu.VMEM((gb, tp, c), F32), pltpu.VMEM((gb, tp, c), F32),
            pltpu.VMEM((nb, SUBLANES, c), F32), pltpu.VMEM((nb, 1, c), F32),
        ],
        compiler_params=_cparams(("parallel", "arbitrary")),
        name="lru_prompt" if is_prompt else "lru_sample",
    )(p, p, c0, h0, lw["conv_w"], lw["conv_b"], lw["wa"], lw["ba"], lw["wx"], lw["bx"], lw["lam"])


def _chunk_consts(c, cl):
    row = lax.broadcasted_iota(jnp.int32, (c, c), 0)
    col = lax.broadcasted_iota(jnp.int32, (c, c), 1)
    tril = row >= col
    lower_ones = jnp.where(tril, 1.0, 0.0).astype(BF)
    real = lax.broadcasted_iota(jnp.int32, (c, 1), 0) < cl
    return row, col, tril, lower_ones, real


def _ks(h):
    return slice(h * DK, (h + 1) * DK)


def _vs(h):
    return slice(h * DV, (h + 1) * DV)


def _chunk_loop(lay, n_groups, nc, c, per_chunk):
    def body(idx, carry):
        g = idx // nc
        per_chunk(g, pl.ds(pl.multiple_of((idx - g * nc) * c, SUBLANES), c))
        return carry

    if lay.ts is None:
        lax.fori_loop(0, n_groups * nc, body, 0)
    else:
        assert nc == 1
        for g in range(n_groups):
            per_chunk(g, slice(None))


def _make_gla_kernel(lay, gb, tp, c, t_real):
    nb = lay.nb
    nc = tp // c
    cl = min(c, t_real)
    G, H = range(gb), range(HEADS)

    def kern(q_ref, k_ref, v_ref, gr_ref, sm_ref, s0_ref, wg_ref, bg_ref, ng_ref, o_ref, sn_ref, s_scr):
        @pl.when(pl.program_id(1) == 0)
        def _():
            s_scr[...] = s0_ref[...]

        _, _, tril, lower_ones, real = _chunk_consts(c, cl)
        ones_real = jnp.where(real, 1.0, 0.0).astype(BF) * jnp.ones((c, DV), BF)
        d_tn = lambda x: lax.dot_general(x, ones_real, (((0,), (0,)), ((), ())), preferred_element_type=F32)
        ng = ng_ref[...]
        wg = wg_ref[...]
        bg = bg_ref[...]

        def per_chunk(g, rows):
            seqs = [g * gb + j for j in G]
            lg = [-_softplus(-(_mm(lay.load(sm_ref, sq, rows), wg) + bg)) * (1.0 / GLA_TAU) for sq in seqs]
            sp = [_split3(x) for x in lg]
            b = [_dot(lower_ones, s[0]) + _dot(lower_ones, s[1]) + _dot(lower_ones, s[2]) for s in sp]
            b_last_col = [d_tn(s[0]) + d_tn(s[1]) + d_tn(s[2]) for s in sp]
            q_t = [lay.load(q_ref, sq, rows) * (DK ** -0.5) * jnp.exp(b[j]) for j, sq in enumerate(seqs)]
            k = [lay.load(k_ref, sq, rows) for sq in seqs]
            k_t = [jnp.where(real, k[j] * jnp.exp(-b[j]), 0.0) for j in G]
            k_d = [jnp.where(real, k[j] * jnp.exp(b[j][cl - 1:cl, :] - b[j]), 0.0) for j in G]
            v = [lay.load(v_ref, sq, rows) for sq in seqs]
            att = [[jnp.where(tril, _mm_nt(q_t[j][:, _ks(h)], k_t[j][:, _ks(h)]), 0.0) for h in H] for j in G]
            s = [[s_scr[seqs[j], h] for h in H] for j in G]
            kv = [[_mm_tn(k_d[j][:, _ks(h)], v[j][:, _vs(h)]) for h in H] for j in G]
            o = [[_mm(att[j][h], v[j][:, _vs(h)]) + _mm(q_t[j][:, _ks(h)], s[j][h]) for h in H] for j in G]
            for j in G:
                for h in H:
                    s_scr[seqs[j], h] = jnp.exp(b_last_col[j][_ks(h), :]) * s[j][h] + kv[j][h]
            for j in G:
                gr = lay.load(gr_ref, seqs[j], rows)
                for h in H:
                    lay.store(o_ref, seqs[j], rows, _vs(h), _rms(o[j][h], ng) * _silu(gr[:, _vs(h)]))

        _chunk_loop(lay, nb // gb, nc, c, per_chunk)
        sn_ref[...] = s_scr[...]

    return kern


def _gla(p, s0, lw, l, lay, gb, c, t_real, name, stacked=None):
    nb, tp = lay.nb, lay.tp
    nseq, tseq = lay.dims(p)
    qk = HEADS * DK
    vw = HEADS * DV
    wspec, pcol, _ = _mixer_specs(lay, l)
    sio = _StateIO(s0, stacked, l, nb)
    in_specs = [
        pcol(qk, COL_GQ), pcol(qk, COL_GK), pcol(vw, COL_GV), pcol(vw, COL_GR), pcol(128, COL_SM),
        sio.spec(), wspec((128, qk)), wspec((1, qk)), wspec((1, DV)),
    ]
    n_in = len(in_specs)
    return pl.pallas_call(
        sio.wrap(_make_gla_kernel(lay, gb, tp, c, t_real), n_in),
        grid=(nseq // nb, tseq // tp),
        in_specs=in_specs + sio.extra_specs(),
        out_specs=[pcol(vw, 0), sio.spec()],
        out_shape=[lay.out_shape(p, vw), sio.out_shape()],
        scratch_shapes=[pltpu.VMEM((nb, HEADS, DK, DV), F32)],
        input_output_aliases=sio.aliases(n_in, 1),
        compiler_params=_cparams(("parallel", "arbitrary")),
        name=name,
    )(p, p, p, p, p, s0, lw["gla_wg"], lw["gla_bg"], lw["gla_ng"], *sio.extra_inputs())


def _make_ret_kernel(lay, gb, tp, c, t_real):
    nb = lay.nb
    nc = tp // c
    cl = min(c, t_real)
    G, H = range(gb), range(HEADS)
    log_gamma = [float(np.log(np.float32(1.0) - np.float32(2.0) ** np.float32(-5.0 - h))) for h in range(HEADS)]

    def kern(q_ref, k_ref, qs_ref, ks_ref, v_ref, g_ref, cos_ref, sin_ref, s0_ref, ng_ref, o_ref, sn_ref, s_scr):
        @pl.when(pl.program_id(1) == 0)
        def _():
            s_scr[...] = s0_ref[...]

        row, col, tril, _, real = _chunk_consts(c, cl)
        diff = (row - col).astype(F32)
        ridx = lax.broadcasted_iota(jnp.int32, (c, 1), 0).astype(F32)
        ng = ng_ref[...]
        dmat = [jnp.where(tril, jnp.exp(jnp.maximum(diff, 0.0) * lgm), 0.0) for lgm in log_gamma]
        q_decay = [jnp.exp((ridx + 1.0) * lgm) for lgm in log_gamma]
        k_decay = [jnp.exp((cl - 1.0 - ridx) * lgm) for lgm in log_gamma]

        def per_chunk(g, rows):
            seqs = [g * gb + j for j in G]
            cos = cos_ref[rows, :]
            sin = sin_ref[rows, :]
            ld = lambda ref, sq: lay.load(ref, sq, rows)
            q = [ld(q_ref, sq) * cos + ld(qs_ref, sq) * sin for sq in seqs]
            k = [jnp.where(real, (ld(k_ref, sq) * cos + ld(ks_ref, sq) * sin) * (DK ** -0.5), 0.0) for sq in seqs]
            v = [ld(v_ref, sq) for sq in seqs]
            att = [[_mm_nt(q[j][:, _ks(h)], k[j][:, _ks(h)]) * dmat[h] for h in H] for j in G]
            s = [[s_scr[seqs[j], h] for h in H] for j in G]
            kv = [[_mm_tn(k[j][:, _ks(h)] * k_decay[h], v[j][:, _vs(h)]) for h in H] for j in G]
            o = [[_mm(att[j][h], v[j][:, _vs(h)]) + _mm(q[j][:, _ks(h)] * q_decay[h], s[j][h]) for h in H]
                 for j in G]
            for j in G:
                for h in H:
                    s_scr[seqs[j], h] = math.exp(cl * log_gamma[h]) * s[j][h] + kv[j][h]
            for j in G:
                gate = ld(g_ref, seqs[j])
                for h in H:
                    oc = o[j][h] - jnp.mean(o[j][h], axis=-1, keepdims=True)
                    on = oc * lax.rsqrt(jnp.mean(oc * oc, axis=-1, keepdims=True) + EPS) * ng
                    lay.store(o_ref, seqs[j], rows, _vs(h), on * _silu(gate[:, _vs(h)]))

        _chunk_loop(lay, nb // gb, nc, c, per_chunk)
        sn_ref[...] = s_scr[...]

    return kern


def _ret(p, s0, cos_t, sin_t, lw, l, lay, gb, c, t_real, name, stacked=None):
    nb, tp = lay.nb, lay.tp
    nseq, tseq = lay.dims(p)
    qk = HEADS * DK
    vw = HEADS * DV
    wspec, pcol, _ = _mixer_specs(lay, l)
    sio = _StateIO(s0, stacked, l, nb)
    in_specs = [
        pcol(qk, COL_RQ), pcol(qk, COL_RK), pcol(qk, COL_RQS), pcol(qk, COL_RKS), pcol(vw, COL_RV),
        pcol(vw, COL_RG),
        pl.BlockSpec((tp, qk), lambda b, t: (t, 0)),
        pl.BlockSpec((tp, qk), lambda b, t: (t, 0)),
        sio.spec(), wspec((1, DV)),
    ]
    n_in = len(in_specs)
    return pl.pallas_call(
        sio.wrap(_make_ret_kernel(lay, gb, tp, c, t_real), n_in),
        grid=(nseq // nb, tseq // tp),
        in_specs=in_specs + sio.extra_specs(),
        out_specs=[pcol(vw, 0), sio.spec()],
        out_shape=[lay.out_shape(p, vw), sio.out_shape()],
        scratch_shapes=[pltpu.VMEM((nb, HEADS, DK, DV), F32)],
        input_output_aliases=sio.aliases(n_in, 1),
        compiler_params=_cparams(("parallel", "arbitrary")),
        name=name,
    )(p, p, p, p, p, p, cos_t, sin_t, s0, lw["ret_ng"], *sio.extra_inputs())


def _unit_lower_inverse_many(ms, c):
    row = lax.broadcasted_iota(jnp.int32, (c, c), 0)
    col = lax.broadcasted_iota(jnp.int32, (c, c), 1)
    eye = jnp.where(row == col, 1.0, 0.0)

    def same_block(bits):
        return lax.shift_right_logical(row, bits) == lax.shift_right_logical(col, bits)

    in8 = same_block(3)
    n1 = [jnp.where(in8, m, 0.0) for m in ms]
    n2 = [_mm_hp(x, x) for x in n1]
    n4 = [_mm_hp(x, x) for x in n2]
    d = [eye - x for x in n1]
    d = [x + _mm_hp(x, y) for x, y in zip(d, n2)]
    d = [x + _mm_hp(x, y) for x, y in zip(d, n4)]
    bits = 3
    while (1 << bits) < c:
        sel = jnp.logical_and(same_block(bits + 1), jnp.logical_not(same_block(bits)))
        ld = [_mm(jnp.where(sel, m, 0.0), x) for m, x in zip(ms, d)]
        d = [x - _mm(x, y) for x, y in zip(d, ld)]
        bits += 1
    return d


def _make_dn_kernel(lay, gb, tp, c, t_real):
    nb = lay.nb
    nc = tp // c
    cl = min(c, t_real)
    G, H = range(gb), range(HEADS)
    GH = [(j, h) for j in G for h in H]

    def kern(x_ref, z_ref, sm_ref, c0_ref, s0_ref, cw_ref, alog_ref, dtb_ref, ng_ref,
             o_ref, cn_ref, sn_ref, xs_scr, qkv_scr, cc_scr, s_scr):
        @pl.when(pl.program_id(1) == 0)
        def _():
            cc_scr[...] = c0_ref[...]
            s_scr[...] = s0_ref[...]

        row, col, tril, lower_ones, real = _chunk_consts(c, cl)
        strict = row > col
        lane = lax.broadcasted_iota(jnp.int32, (c, 128), 1)
        sel = [jnp.where(lane == SM_DA + h, 1.0, 0.0).astype(BF) for h in H]
        d_nt = lambda a, x: lax.dot_general(a, x, (((1,), (1,)), ((), ())), preferred_element_type=F32)
        ng = ng_ref[...]
        cw = cw_ref[...]
        neg_a = -jnp.exp(alog_ref[...])
        dtb = dtb_ref[...]

        def per_group(g, carry):
            seqs = [g * gb + j for j in G]
            for j, sq in enumerate(seqs):
                xs_scr[j, 0:SUBLANES, :] = cc_scr[sq]
                xs_scr[j, SUBLANES:SUBLANES + tp, :] = lay.load(x_ref, sq)
            for j, sq in enumerate(seqs):
                qkv_scr[j] = _silu(_conv4(xs_scr, j, cw, tp))
                cc_scr[sq] = xs_scr[j, pl.ds(t_real, SUBLANES), :]

            def per_chunk(ci, carry2):
                rows = pl.ds(pl.multiple_of(ci * c, SUBLANES), c)
                sm = [lay.load(sm_ref, sq, rows) for sq in seqs]
                gcum = [_cumsum_rows(neg_a * _softplus(x + dtb), lower_ones) for x in sm]
                g3 = [_split3(x) for x in gcum]
                beta_all = [jnp.where(real, jax.nn.sigmoid(x), 0.0) for x in sm]

                def head_in(j, h, part):
                    x = qkv_scr[j, rows, part * BRANCH + h * DV:part * BRANCH + (h + 1) * DV]
                    return x

                qh = {jh: head_in(*jh, 0) for jh in GH}
                kh = {jh: head_in(*jh, 1) for jh in GH}
                vh = {jh: head_in(*jh, 2) for jh in GH}
                qh = {jh: x * lax.rsqrt(jnp.sum(x * x, axis=-1, keepdims=True) + EPS) * (DV ** -0.5)
                      for jh, x in qh.items()}
                kh = {jh: jnp.where(real, x * lax.rsqrt(jnp.sum(x * x, axis=-1, keepdims=True) + EPS), 0.0)
                      for jh, x in kh.items()}
                beta = {(j, h): beta_all[j][:, SM_DB + h:SM_DB + h + 1] for j, h in GH}
                gcol = {(j, h): gcum[j][:, SM_DA + h:SM_DA + h + 1] for j, h in GH}
                grow = {(j, h): d_nt(sel[h], g3[j][0]) + d_nt(sel[h], g3[j][1]) + d_nt(sel[h], g3[j][2])
                        for j, h in GH}
                decay = {jh: jnp.where(tril, jnp.exp(jnp.where(tril, gcol[jh] - grow[jh], 0.0)), 0.0) for jh in GH}
                e_g = {jh: jnp.exp(gcol[jh]) for jh in GH}
                kb = {jh: kh[jh] * beta[jh] for jh in GH}
                kq = {jh: _mm_nt(jnp.concatenate([kb[jh], qh[jh]], axis=0), kh[jh]) for jh in GH}
                tm = _unit_lower_inverse_many([jnp.where(strict, kq[jh][:c] * decay[jh], 0.0) for jh in GH], c)
                tm = dict(zip(GH, tm))
                uw = {jh: _mm(tm[jh], jnp.concatenate([vh[jh] * beta[jh], kb[jh] * e_g[jh]], axis=1)) for jh in GH}
                s = {(j, h): s_scr[seqs[j], h] for j, h in GH}
                wq = {jh: _mm(jnp.concatenate([uw[jh][:, DV:], qh[jh] * e_g[jh]], axis=0), s[jh]) for jh in GH}
                v_new = {jh: uw[jh][:, :DV] - wq[jh][:c] for jh in GH}
                o = {jh: wq[jh][c:] + _mm(kq[jh][c:] * decay[jh], v_new[jh]) for jh in GH}
                for j, h in GH:
                    g_last = gcol[(j, h)][cl - 1:cl, :]
                    k_d = kh[(j, h)] * jnp.exp(g_last - gcol[(j, h)])
                    s_scr[seqs[j], h] = jnp.exp(g_last) * s[(j, h)] + _mm_tn(k_d, v_new[(j, h)])
                for j in G:
                    z = lay.load(z_ref, seqs[j], rows)
                    for h in H:
                        lay.store(o_ref, seqs[j], rows, _vs(h), _rms(o[(j, h)], ng) * _silu(z[:, _vs(h)]))
                return carry2

            lax.fori_loop(0, nc, per_chunk, 0)
            return carry

        lay.for_groups(nb // gb, per_group)
        cn_ref[...] = cc_scr[...]
        sn_ref[...] = s_scr[...]

    return kern


def _dn(p, c0, s0, lw, l, lay, gb, c, t_real, name, stacked=None):
    nb, tp = lay.nb, lay.tp
    nseq, tseq = lay.dims(p)
    cw3 = 3 * BRANCH
    wspec, pcol, state = _mixer_specs(lay, l)
    sio = _StateIO(s0, stacked, l, nb)
    in_specs = [
        pcol(cw3, COL_DQKV), pcol(BRANCH, COL_DZ), pcol(128, COL_SM),
        state(SUBLANES, cw3), sio.spec(),
        wspec((CONV_W, cw3)), wspec((1, 128)), wspec((1, 128)), wspec((1, DV)),
    ]
    n_in = len(in_specs)
    return pl.pallas_call(
        sio.wrap(_make_dn_kernel(lay, gb, tp, c, t_real), n_in),
        grid=(nseq // nb, tseq // tp),
        in_specs=in_specs + sio.extra_specs(),
        out_specs=[pcol(BRANCH, 0), state(SUBLANES, cw3), sio.spec()],
        out_shape=[
            lay.out_shape(p, BRANCH),
            jax.ShapeDtypeStruct((nseq, SUBLANES, cw3), F32),
            sio.out_shape(),
        ],
        scratch_shapes=[
            pltpu.VMEM((gb, tp + SUBLANES, cw3), F32), pltpu.VMEM((gb, tp, cw3), F32),
            pltpu.VMEM((nb, SUBLANES, cw3), F32), pltpu.VMEM((nb, HEADS, DV, DV), F32),
        ],
        input_output_aliases=sio.aliases(n_in, 2),
        compiler_params=_cparams(("parallel", "arbitrary")),
        name=name,
    )(p, p, p, c0, s0, lw["dn_conv_w"], lw["dn_alog"], lw["dn_dtb"], lw["dn_ng"], *sio.extra_inputs())


IN_SIZES = (BRANCH, BRANCH, HEADS * DK, HEADS * DK, BRANCH, BRANCH, GLA_RANK,
            HEADS * DK, HEADS * DK, BRANCH, BRANCH, BRANCH, BRANCH, BRANCH, BRANCH, HEADS, HEADS)


IN_OFFS = tuple(int(v) for v in np.concatenate([[0], np.cumsum(IN_SIZES)]))
PACK_ROWS = 256


def _pack_src_row(r):
    t_rq, t_swap, t_rv, t_small = (c // PACK_ROWS for c in (COL_RQ, COL_RQS, COL_RV, COL_SM))
    o_rq, o_rv = IN_OFFS[7], IN_OFFS[9]
    row = jnp.where(r < t_rq, r * PACK_ROWS,
                    jnp.where(r < t_swap, o_rq + (r - t_rq) * PACK_ROWS,
                              jnp.where(r < t_rv, o_rq + (r - t_swap) * PACK_ROWS,
                                        jnp.where(r < t_small, o_rv + (r - t_rv) * PACK_ROWS, 0))))
    return pl.multiple_of(row, 2 * SUBLANES)


def _pack_kernel(w_ref, glr_ref, dba_ref, o_ref):
    r = pl.program_id(1)
    t_swap, t_rv, t_small = (c // PACK_ROWS for c in (COL_RQS, COL_RV, COL_SM))
    w = w_ref[0]

    @pl.when(jnp.logical_and(jnp.logical_or(r < t_swap, r >= t_rv), r < t_small))
    def _():
        o_ref[...] = w.astype(BF)

    @pl.when(jnp.logical_and(r >= t_swap, r < t_rv))
    def _():
        parts = []
        for h in range(PACK_ROWS // DK):
            parts += [w[h * DK + DK // 2:(h + 1) * DK], w[h * DK:h * DK + DK // 2]]
        o_ref[...] = jnp.concatenate(parts, axis=0).astype(BF)

    @pl.when(r == t_small)
    def _():
        pad = jnp.zeros((PACK_ROWS - GLA_RANK - 2 * HEADS, w.shape[1]), w.dtype)
        o_ref[...] = jnp.concatenate([glr_ref[0], dba_ref[0], pad], axis=0).astype(BF)


def _pack_w_in(w_in_t):
    depth, n_in, d = w_in_t.shape
    assert n_in == IN_OFFS[-1] and N_PACK % PACK_ROWS == 0 and COL_SM // PACK_ROWS == N_PACK // PACK_ROWS - 1
    rows = lambda n, start: pl.BlockSpec((pl.Element(1), pl.Element(n), pl.Element(d)),
                                         lambda l, r: (l, start(r), 0))
    return pl.pallas_call(
        _pack_kernel,
        grid=(depth, N_PACK // PACK_ROWS),
        in_specs=[rows(PACK_ROWS, _pack_src_row), rows(GLA_RANK, lambda r: IN_OFFS[6]),
                  rows(2 * HEADS, lambda r: IN_OFFS[15])],
        out_specs=pl.BlockSpec((None, PACK_ROWS, d), lambda l, r: (l, r, 0)),
        out_shape=jax.ShapeDtypeStruct((depth, N_PACK, d), BF),
        compiler_params=_cparams(("parallel", "parallel")),
        name="pack_w_in",
    )(w_in_t, w_in_t, w_in_t)


def _rope_tables(pos):
    half = DK // 2
    inv = ROPE_BASE ** (-jnp.arange(half, dtype=F32) / half)
    ang = pos.astype(F32)[:, None] * inv[None, :]
    cos, sin = jnp.cos(ang), jnp.sin(ang)
    cos_t = jnp.tile(jnp.concatenate([cos, cos], axis=1), (1, HEADS))
    sin_t = jnp.tile(jnp.concatenate([-sin, sin], axis=1), (1, HEADS))
    return cos_t, sin_t


def _lane_row(vals, off):
    depth = vals.shape[0]
    return jnp.zeros((depth, 1, 128), F32).at[:, 0, off:off + HEADS].set(vals.astype(F32))


def kernel(x_prompt, x_sample, state_lru_conv, state_lru_h, state_gla, state_ret, state_dn_conv, state_dn,
           norm_g, final_norm_g, w_ff_in, w_ff_out, w_in, w_gate, w_branch, w_out,
           lru_conv_w, lru_conv_b, lru_wa, lru_ba, lru_wx, lru_bx, lru_lambda,
           gla_wg, gla_bg, gla_norm_g, ret_norm_g, dn_conv_w, dn_a_log, dn_dt_bias, dn_norm_g):
    bp, tpr, d = x_prompt.shape
    bs, ts, _ = x_sample.shape
    depth = w_in.shape[0]
    n_p, n_s = bp * tpr, bs * ts
    assert d == D_MODEL and ts <= SAMPLE_PAD and tpr % CHUNK == 0

    w_pack = _pack_w_in(jnp.swapaxes(w_in, 1, 2))
    w_branch_b = w_branch.astype(BF)
    w_out_b = w_out.astype(BF)
    wg_pad = jnp.zeros((depth, 128, HEADS * DK), F32).at[:, SM_GLR:SM_GLR + GLA_RANK, :].set(gla_wg).astype(BF)
    r3 = lambda a: a.reshape(a.shape[0], 1, a.shape[1])
    lw = {
        "conv_w": lru_conv_w, "conv_b": r3(lru_conv_b), "wa": lru_wa.astype(BF), "ba": r3(lru_ba),
        "wx": lru_wx.astype(BF), "bx": r3(lru_bx), "lam": r3(lru_lambda),
        "gla_wg": wg_pad, "gla_bg": r3(gla_bg), "gla_ng": r3(gla_norm_g), "ret_ng": r3(ret_norm_g),
        "dn_conv_w": dn_conv_w, "dn_alog": _lane_row(dn_a_log, SM_DA), "dn_dtb": _lane_row(dn_dt_bias, SM_DA),
        "dn_ng": r3(dn_norm_g),
    }
    norm_g4 = norm_g.reshape(depth, 3, 1, d)

    cos_p, sin_p = _rope_tables(jnp.arange(tpr))
    pos_s = jnp.where(jnp.arange(SAMPLE_PAD) < ts, PAST_LEN + jnp.arange(SAMPLE_PAD), 0)
    cos_s, sin_s = _rope_tables(pos_s)

    pad_conv = lambda a: jnp.pad(a, ((0, 0), (SUBLANES - (CONV_W - 1), 0), (0, 0)))
    zeros = lambda *s: jnp.zeros(s, F32)

    tp_p = _pick_tile(tpr, 256, CHUNK)
    a_p = dict(lay=_SeqLayout(bp, tp_p), gb=bp)
    nb_s = _pick_tile(bs, 8, 2)
    assert (nb_s * ts) % SUBLANES == 0
    a_s = dict(lay=_SeqLayout(nb_s, SAMPLE_PAD, ts), gb=_pick_tile(nb_s, 4, 1))

    fg = final_norm_g.reshape(1, d)
    xp = x_prompt.reshape(n_p, d)
    xs = x_sample.reshape(n_s, d)
    new_p = [[] for _ in range(6)]
    new_s = [[] for _ in range(6)]
    s_gla_s = s_ret_s = s_dn_s = None
    for l in range(depth):
        xs, *w_ffn = _ffn_cast(xs, norm_g4[l, 0], w_ff_in, w_ff_out, l, 0)
        xp = _ffn(xp, norm_g4[l, 0], *w_ffn)
        p_p = _inproj(xp, norm_g4[l, 1], w_pack, l).reshape(bp, tpr, N_PACK)
        p_s = _inproj(xs, norm_g4[l, 1], w_pack, l)

        o_lru_p, c_lru_p, h_lru_p = _lru(p_p, zeros(bp, SUBLANES, BRANCH), zeros(bp, 1, BRANCH), lw, l,
                                         t_real=tp_p, is_prompt=True, **a_p)
        o_gla_p, s_gla_p = _gla(p_p, zeros(bp, HEADS, DK, DV), lw, l, c=CHUNK, t_real=tp_p, name="gla_prompt", **a_p)
        o_ret_p, s_ret_p = _ret(p_p, zeros(bp, HEADS, DK, DV), cos_p, sin_p, lw, l, c=CHUNK, t_real=tp_p,
                                name="ret_prompt", **a_p)
        o_dn_p, c_dn_p, s_dn_p = _dn(p_p, zeros(bp, SUBLANES, 3 * BRANCH), zeros(bp, HEADS, DV, DV), lw, l,
                                     c=CHUNK, t_real=tp_p, name="dn_prompt", **a_p)

        o_lru_s, c_lru_s, h_lru_s = _lru(p_s, pad_conv(state_lru_conv[l]), state_lru_h[l][:, None, :], lw, l,
                                         t_real=ts, is_prompt=False, **a_s)
        o_gla_s, s_gla_s = _gla(p_s, state_gla, lw, l, c=SAMPLE_PAD, t_real=ts, name="gla_sample",
                                stacked=(depth, s_gla_s), **a_s)
        o_ret_s, s_ret_s = _ret(p_s, state_ret, cos_s, sin_s, lw, l, c=SAMPLE_PAD, t_real=ts, name="ret_sample",
                                stacked=(depth, s_ret_s), **a_s)
        o_dn_s, c_dn_s, s_dn_s = _dn(p_s, pad_conv(state_dn_conv[l]), state_dn, lw, l, c=SAMPLE_PAD, t_real=ts,
                                     name="dn_sample", stacked=(depth, s_dn_s), **a_s)

        flat_p = lambda o: o.reshape(n_p, BRANCH)
        xs, w_gate_l = _merge_cast(xs, norm_g4[l, 1], [o_lru_s, o_gla_s, o_ret_s, o_dn_s],
                                   w_gate, w_branch_b, w_out_b, l)
        xp = _merge_split(xp, norm_g4[l, 1], [flat_p(o) for o in (o_lru_p, o_gla_p, o_ret_p, o_dn_p)],
                          w_gate_l, w_branch_b, w_out_b, l)
        xs, *w_ffn = _ffn_cast(xs, norm_g4[l, 2], w_ff_in, w_ff_out, l, 1)
        xp = _ffn(xp, norm_g4[l, 2], *w_ffn, final_g=fg if l == depth - 1 else None)

        tail3 = lambda cwin: cwin[:, SUBLANES - (CONV_W - 1):, :]
        for i, v in enumerate((tail3(c_lru_p), h_lru_p[:, 0], s_gla_p, s_ret_p, tail3(c_dn_p), s_dn_p)):
            new_p[i].append(v)
        for i, v in ((0, tail3(c_lru_s)), (1, h_lru_s[:, 0]), (4, tail3(c_dn_s))):
            new_s[i].append(v)

    y_prompt = xp.reshape(bp, tpr, d)
    y_sample = _final_norm(xs, fg).reshape(bs, ts, d)
    sp = [jnp.stack(v) for v in new_p]
    s_small = {i: jnp.stack(new_s[i]) for i in (0, 1, 4)}
    return tuple([y_prompt, y_sample] + sp + [s_small[0], s_small[1], s_gla_s, s_ret_s, s_small[4], s_dn_s])
```

```python
import functools
import math

import numpy as np
import jax
import jax.numpy as jnp
from jax import lax
from jax.experimental import pallas as pl
from jax.experimental.pallas import tpu as pltpu

F32 = jnp.float32
BF = jnp.bfloat16
EPS = 1e-6

D_MODEL = 2048
BRANCH = 512
CONV_W = 4
HEADS = 4
DK = 64
DV = 128
GLA_RANK = 16
GLA_TAU = 16.0
LRU_C = 8.0
LRU_BD = 128
ROPE_BASE = 10000.0
CHUNK = 64
PAST_LEN = 16384

V7X_VMEM_BYTES = 64 * 1024 * 1024
VMEM_LIMIT = V7X_VMEM_BYTES - 8 * 1024 * 1024
SUBLANES = 8
SAMPLE_PAD = 8

COL_LX, COL_LY = 0, 512
COL_GQ, COL_GK, COL_GV, COL_GR = 1024, 1280, 1536, 2048
COL_RQ, COL_RK, COL_RQS, COL_RKS, COL_RV, COL_RG = 2560, 2816, 3072, 3328, 3584, 4096
COL_DQKV, COL_DZ = 4608, 6144
COL_SM = 6656
SM_GLR, SM_DB, SM_DA = 0, 16, 20
N_PACK = 6912


def _dot(a, b):
    return jnp.dot(a, b, preferred_element_type=F32)


def _mm(a, b):
    return _dot(a.astype(BF), b.astype(BF))


def _mm_nt(a, b):
    return lax.dot_general(a.astype(BF), b.astype(BF), (((1,), (1,)), ((), ())), preferred_element_type=F32)


def _mm_tn(a, b):
    return lax.dot_general(a.astype(BF), b.astype(BF), (((0,), (0,)), ((), ())), preferred_element_type=F32)


def _split3(x):
    hi = x.astype(BF)
    r = x - hi.astype(F32)
    mid = r.astype(BF)
    lo = (r - mid.astype(F32)).astype(BF)
    return hi, mid, lo


def _split2(x):
    hi = x.astype(BF)
    return hi, (x - hi.astype(F32)).astype(BF)


def _mm_hp(a, b):
    ah, al = _split2(a)
    bh, bl = _split2(b)
    return _dot(ah, bh) + _dot(ah, bl) + _dot(al, bh)


def _cumsum_rows(x, lower_ones):
    hi, mid, lo = _split3(x)
    return _dot(lower_ones, hi) + _dot(lower_ones, mid) + _dot(lower_ones, lo)


def _softplus(x):
    return jnp.maximum(x, 0.0) + jnp.log1p(jnp.exp(-jnp.abs(x)))


def _silu(x):
    return x * jax.nn.sigmoid(x)


def _gelu_tanh(x):
    return x * (0.5 * (1.0 + jnp.tanh(math.sqrt(2.0 / math.pi) * (x + 0.044715 * (x * x * x)))))


def _rms(x, g):
    return x * lax.rsqrt(jnp.mean(x * x, axis=-1, keepdims=True) + EPS) * g


def _pick_tile(n, target, align):
    best = None
    for t in range(align, min(n, target) + 1, align):
        if n % t == 0:
            best = t
    assert best is not None, (n, target, align)
    return best


def _cparams(sem):
    return pltpu.CompilerParams(dimension_semantics=sem, vmem_limit_bytes=VMEM_LIMIT)


def _ffn_kernel(x_ref, g_ref, wg_ref, wu_ref, wo_ref, o_ref, hn_ref):
    @pl.when(pl.program_id(1) == 0)
    def _():
        x = x_ref[...]
        hn_ref[...] = _rms(x, g_ref[...]).astype(BF)
        o_ref[...] = x

    hn = hn_ref[...]
    gate = _dot(hn, wg_ref[...])
    up = _dot(hn, wu_ref[...])
    act = (_silu(gate) * up * 0.5).astype(BF)
    o_ref[...] += _dot(act, wo_ref[...])


def _ffn_final_kernel(x_ref, g_ref, wg_ref, wu_ref, wo_ref, fg_ref, o_ref, hn_ref):
    _ffn_kernel(x_ref, g_ref, wg_ref, wu_ref, wo_ref, o_ref, hn_ref)

    @pl.when(pl.program_id(1) == pl.num_programs(1) - 1)
    def _():
        o_ref[...] = _rms(o_ref[...], fg_ref[...])


def _ffn(x, g, wg, wu, wo, final_g=None):
    n, d = x.shape
    f = wo.shape[0]
    tm = _pick_tile(n, 1024, 16)
    tf = _pick_tile(f, 512, 128)
    final = [] if final_g is None else [final_g]
    return pl.pallas_call(
        _ffn_kernel if final_g is None else _ffn_final_kernel,
        grid=(n // tm, f // tf),
        in_specs=[
            pl.BlockSpec((tm, d), lambda i, j: (i, 0)),
            pl.BlockSpec((1, d), lambda i, j: (0, 0)),
            pl.BlockSpec((d, tf), lambda i, j: (0, j)),
            pl.BlockSpec((d, tf), lambda i, j: (0, j)),
            pl.BlockSpec((tf, d), lambda i, j: (j, 0)),
        ] + [pl.BlockSpec((1, d), lambda i, j: (0, 0))] * len(final),
        out_specs=pl.BlockSpec((tm, d), lambda i, j: (i, 0)),
        out_shape=jax.ShapeDtypeStruct((n, d), F32),
        scratch_shapes=[pltpu.VMEM((tm, d), BF)],
        compiler_params=_cparams(("parallel", "arbitrary")),
        name="ffn",
    )(x, g, wg, wu, wo, *final)


def _ffn_cast_kernel(x_ref, g_ref, wg32_ref, wu32_ref, wo32_ref, o_ref, wg_ref, wu_ref, wo_ref, hn_ref):
    @pl.when(pl.program_id(1) == 0)
    def _():
        x = x_ref[...]
        hn_ref[...] = _rms(x, g_ref[...]).astype(BF)
        o_ref[...] = x

    wg_ref[...] = wg32_ref[...].astype(BF)
    wu_ref[...] = wu32_ref[...].astype(BF)
    wo_ref[...] = wo32_ref[...].astype(BF)
    hn = hn_ref[...]
    act = (_silu(_dot(hn, wg_ref[...])) * _dot(hn, wu_ref[...]) * 0.5).astype(BF)
    o_ref[...] += _dot(act, wo_ref[...])


def _ffn_cast(x, g, w_in, w_out, l, s):
    n, d = x.shape
    f = w_out.shape[2]
    tf = _pick_tile(f, 512, 128)
    nf = f // tf
    once = dict(pipeline_mode=pl.Buffered(1))
    return pl.pallas_call(
        _ffn_cast_kernel,
        grid=(1, nf),
        in_specs=[
            pl.BlockSpec((n, d), lambda i, j: (0, 0), **once),
            pl.BlockSpec((1, d), lambda i, j: (0, 0)),
            pl.BlockSpec((None, None, d, tf), lambda i, j: (l, s, 0, j)),
            pl.BlockSpec((None, None, d, tf), lambda i, j: (l, s, 0, j + nf)),
            pl.BlockSpec((None, None, tf, d), lambda i, j: (l, s, j, 0)),
        ],
        out_specs=[
            pl.BlockSpec((n, d), lambda i, j: (0, 0), **once),
            pl.BlockSpec((d, tf), lambda i, j: (0, j)),
            pl.BlockSpec((d, tf), lambda i, j: (0, j)),
            pl.BlockSpec((tf, d), lambda i, j: (j, 0)),
        ],
        out_shape=[
            jax.ShapeDtypeStruct((n, d), F32),
            jax.ShapeDtypeStruct((d, f), BF), jax.ShapeDtypeStruct((d, f), BF), jax.ShapeDtypeStruct((f, d), BF),
        ],
        scratch_shapes=[pltpu.VMEM((n, d), BF)],
        compiler_params=_cparams(("arbitrary", "arbitrary")),
        name="ffn_cast",
    )(x, g, w_in, w_in, w_out)


def _inproj_kernel(x_ref, g_ref, w_ref, o_ref, hn_ref):
    @pl.when(pl.program_id(1) == 0)
    def _():
        hn_ref[...] = _rms(x_ref[...], g_ref[...]).astype(BF)

    o_ref[...] = _mm_nt(hn_ref[...], w_ref[...])


def _inproj(x, g, w_pack, l):
    n, d = x.shape
    npk = w_pack.shape[1]
    tm = _pick_tile(n, 1024, 16)
    tn = _pick_tile(npk, 1024, 128)
    return pl.pallas_call(
        _inproj_kernel,
        grid=(n // tm, npk // tn),
        in_specs=[
            pl.BlockSpec((tm, d), lambda i, j: (i, 0)),
            pl.BlockSpec((1, d), lambda i, j: (0, 0)),
            pl.BlockSpec((None, tn, d), lambda i, j: (l, j, 0)),
        ],
        out_specs=pl.BlockSpec((tm, tn), lambda i, j: (i, j)),
        out_shape=jax.ShapeDtypeStruct((n, npk), F32),
        scratch_shapes=[pltpu.VMEM((tm, d), BF)],
        compiler_params=_cparams(("parallel", "arbitrary")),
        name="inproj",
    )(x, g, w_pack)


def _merge_kernel(x_ref, g_ref, b0_ref, b1_ref, b2_ref, b3_ref, wgate_ref, wbr_ref, wo_ref, o_ref, hn_ref):
    @pl.when(pl.program_id(1) == 0)
    def _():
        x = x_ref[...]
        hn_ref[...] = _rms(x, g_ref[...]).astype(BF)
        o_ref[...] = x

    hn = hn_ref[...]
    m = None
    for n, b_ref in enumerate((b0_ref, b1_ref, b2_ref, b3_ref)):
        gate = jax.nn.sigmoid(_dot(hn, wgate_ref[n]))
        br = _dot(b_ref[...].astype(BF), wbr_ref[n])
        m = gate * br if m is None else m + gate * br
    o_ref[...] += _dot(m.astype(BF), wo_ref[...])


def _gate_kernel(x_ref, g_ref, b0_ref, b1_ref, b2_ref, b3_ref, wgate_ref, wbr_ref, m_ref, hn_ref):
    @pl.when(pl.program_id(1) == 0)
    def _():
        hn_ref[...] = _rms(x_ref[...], g_ref[...]).astype(BF)

    hn = hn_ref[...]
    m = None
    for n, b_ref in enumerate((b0_ref, b1_ref, b2_ref, b3_ref)):
        gate = jax.nn.sigmoid(_dot(hn, wgate_ref[n]))
        br = _dot(b_ref[...].astype(BF), wbr_ref[n])
        m = gate * br if m is None else m + gate * br
    m_ref[...] = m.astype(BF)


def _outproj_kernel(m_ref, w_ref, x_ref, o_ref):
    o_ref[...] = x_ref[...] + _dot(m_ref[...], w_ref[...])


def _merge_split(x, g, branches, w_gate_l, w_branch_l, w_out_l):
    n, d = x.shape
    tm = _pick_tile(n, 1024, 16)
    tn = 256
    m = pl.pallas_call(
        _gate_kernel,
        grid=(n // tm, d // tn),
        in_specs=[
            pl.BlockSpec((tm, d), lambda i, j: (i, 0)),
            pl.BlockSpec((1, d), lambda i, j: (0, 0)),
        ] + [pl.BlockSpec((tm, BRANCH), lambda i, j: (i, 0))] * HEADS + [
            pl.BlockSpec((HEADS, d, tn), lambda i, j: (0, 0, j)),
            pl.BlockSpec((HEADS, BRANCH, tn), lambda i, j: (0, 0, j)),
        ],
        out_specs=pl.BlockSpec((tm, tn), lambda i, j: (i, j)),
        out_shape=jax.ShapeDtypeStruct((n, d), BF),
        scratch_shapes=[pltpu.VMEM((tm, d), BF)],
        compiler_params=_cparams(("parallel", "arbitrary")),
        name="gate",
    )(x, g, *branches, w_gate_l, w_branch_l)
    to = _pick_tile(n, 512, 16)
    return pl.pallas_call(
        _outproj_kernel,
        grid=(n // to,),
        in_specs=[
            pl.BlockSpec((to, d), lambda i: (i, 0)),
            pl.BlockSpec((d, d), lambda i: (0, 0), pipeline_mode=pl.Buffered(1)),
            pl.BlockSpec((to, d), lambda i: (i, 0)),
        ],
        out_specs=pl.BlockSpec((to, d), lambda i: (i, 0)),
        out_shape=jax.ShapeDtypeStruct((n, d), F32),
        compiler_params=_cparams(("parallel",)),
        name="outproj",
    )(m, w_out_l, x)


def _merge_cast_kernel(x_ref, g_ref, b0_ref, b1_ref, b2_ref, b3_ref, wgate32_ref, wbr32_ref, wo32_ref,
                       o_ref, wgate_ref, wbr_ref, wo_ref, hn_ref):
    wgate_ref[...] = wgate32_ref[...].astype(BF)
    wbr_ref[...] = wbr32_ref[...].astype(BF)
    wo_ref[...] = wo32_ref[...].astype(BF)
    _merge_kernel(x_ref, g_ref, b0_ref, b1_ref, b2_ref, b3_ref, wgate_ref, wbr_ref, wo_ref, o_ref, hn_ref)


def _merge_cast(x, g, branches, w_gate, w_branch, w_out, l):
    n, d = x.shape
    tn = 256
    once = dict(pipeline_mode=pl.Buffered(1))
    return pl.pallas_call(
        _merge_cast_kernel,
        grid=(1, d // tn),
        in_specs=[
            pl.BlockSpec((n, d), lambda i, j: (0, 0), **once),
            pl.BlockSpec((1, d), lambda i, j: (0, 0)),
        ] + [pl.BlockSpec((n, BRANCH), lambda i, j: (0, 0), **once)] * HEADS + [
            pl.BlockSpec((None, HEADS, d, tn), lambda i, j: (l, 0, 0, j)),
            pl.BlockSpec((None, HEADS, BRANCH, tn), lambda i, j: (l, 0, 0, j)),
            pl.BlockSpec((None, tn, d), lambda i, j: (l, j, 0)),
        ],
        out_specs=[
            pl.BlockSpec((n, d), lambda i, j: (0, 0), **once),
            pl.BlockSpec((HEADS, d, tn), lambda i, j: (0, 0, j)),
            pl.BlockSpec((HEADS, BRANCH, tn), lambda i, j: (0, 0, j)),
            pl.BlockSpec((tn, d), lambda i, j: (j, 0)),
        ],
        out_shape=[
            jax.ShapeDtypeStruct((n, d), F32), jax.ShapeDtypeStruct((HEADS, d, d), BF),
            jax.ShapeDtypeStruct((HEADS, BRANCH, d), BF), jax.ShapeDtypeStruct((d, d), BF),
        ],
        scratch_shapes=[pltpu.VMEM((n, d), BF)],
        compiler_params=_cparams(("arbitrary", "arbitrary")),
        name="merge_cast",
    )(x, g, *branches, w_gate, w_branch, w_out)


def _final_norm_kernel(x_ref, g_ref, o_ref):
    o_ref[...] = _rms(x_ref[...], g_ref[...])


def _final_norm(x, g):
    n, d = x.shape
    tm = _pick_tile(n, 512, 8)
    return pl.pallas_call(
        _final_norm_kernel,
        grid=(n // tm,),
        in_specs=[pl.BlockSpec((tm, d), lambda i: (i, 0)), pl.BlockSpec((1, d), lambda i: (0, 0))],
        out_specs=pl.BlockSpec((tm, d), lambda i: (i, 0)),
        out_shape=jax.ShapeDtypeStruct((n, d), F32),
        compiler_params=_cparams(("parallel",)),
        name="final_norm",
    )(x, g)


def _conv4(xs_ref, j_seq, w, tp):
    xs = xs_ref[j_seq]
    y = xs[SUBLANES:] * w[CONV_W - 1:CONV_W, :]
    for back in range(1, CONV_W):
        y = y + pltpu.roll(xs, back, axis=0)[SUBLANES:] * w[CONV_W - 1 - back:CONV_W - back, :]
    return y


class _SeqLayout:
    def __init__(self, nb, tp, ts=None):
        self.nb, self.tp, self.ts = nb, tp, ts

    def load(self, ref, sq, rows=slice(None), cols=slice(None)):
        if self.ts is None:
            return ref[sq, rows, cols]
        x = ref[sq * self.ts:(sq + 1) * self.ts, cols]
        return jnp.concatenate([x, jnp.zeros((self.tp - self.ts, x.shape[1]), x.dtype)], axis=0)

    def store(self, ref, sq, rows, cols, val):
        if self.ts is None:
            ref[sq, rows, cols] = val
        else:
            ref[sq * self.ts:(sq + 1) * self.ts, cols] = val[:self.ts]

    def for_groups(self, n, body):
        if self.ts is None:
            lax.fori_loop(0, n, body, 0)
        else:
            for g in range(n):
                body(g, 0)

    def pcol(self, w, off):
        if self.ts is None:
            return pl.BlockSpec((self.nb, self.tp, w), lambda b, t: (b, t, off // w))
        return pl.BlockSpec((self.nb * self.ts, w), lambda b, t: (b, off // w))

    def dims(self, p):
        return (p.shape[0], p.shape[1]) if self.ts is None else (p.shape[0] // self.ts, self.tp)

    def out_shape(self, p, w):
        return jax.ShapeDtypeStruct(p.shape[:-1] + (w,), F32)


def _make_lru_kernel(lay, gb, tp, t_real, is_prompt):
    nb = lay.nb

    def kern(lx_ref, ly_ref, c0_ref, h0_ref, cw_ref, cb_ref, wa_ref, ba_ref, wx_ref, bx_ref, lam_ref,
             o_ref, cn_ref, hn_ref, xs_scr, a_scr, u_scr, cc_scr, ch_scr):
        tb = pl.program_id(1)

        @pl.when(tb == 0)
        def _():
            cc_scr[...] = c0_ref[...]
            ch_scr[...] = h0_ref[...]

        cw = cw_ref[...]
        neg_sp = -LRU_C * _softplus(-lam_ref[...])
        row = lax.broadcasted_iota(jnp.int32, (tp, 1), 0)

        def per_group(g, carry):
            seqs = [g * gb + j for j in range(gb)]
            for j, sq in enumerate(seqs):
                xs_scr[j, 0:SUBLANES, :] = cc_scr[sq]
                xs_scr[j, SUBLANES:SUBLANES + tp, :] = lay.load(lx_ref, sq)
            for j, sq in enumerate(seqs):
                xc = _conv4(xs_scr, j, cw, tp) + cb_ref[...]
                cc_scr[sq] = xs_scr[j, pl.ds(t_real, SUBLANES), :]
                xcb = xc.astype(BF)
                r_parts, i_parts = [], []
                for n in range(BRANCH // LRU_BD):
                    blk = xcb[:, n * LRU_BD:(n + 1) * LRU_BD]
                    r_parts.append(_dot(blk, wa_ref[n]))
                    i_parts.append(_dot(blk, wx_ref[n]))
                r = jax.nn.sigmoid(jnp.concatenate(r_parts, axis=1) + ba_ref[...])
                ig = jax.nn.sigmoid(jnp.concatenate(i_parts, axis=1) + bx_ref[...])
                log_a = r * neg_sp
                a = jnp.exp(log_a)
                mult = jnp.sqrt(-jnp.tanh(log_a) * (a * a + 1.0))
                if is_prompt:
                    mult = jnp.where(jnp.logical_and(row == 0, tb == 0), 1.0, mult)
                a_scr[j] = a
                u_scr[j] = mult * ig * xc

            def step(t, hs):
                out = []
                for j in range(gb):
                    h = a_scr[j, pl.ds(t, 1), :] * hs[j] + u_scr[j, pl.ds(t, 1), :]
                    a_scr[j, pl.ds(t, 1), :] = h
                    out.append(h)
                return tuple(out)

            h_last = lax.fori_loop(0, t_real, step, tuple(ch_scr[sq] for sq in seqs),
                                   unroll=True if t_real <= SUBLANES else 8)
            for j, sq in enumerate(seqs):
                ch_scr[sq] = h_last[j]
                lay.store(o_ref, sq, slice(None), slice(None), a_scr[j] * _gelu_tanh(lay.load(ly_ref, sq)))
            return carry

        lay.for_groups(nb // gb, per_group)
        cn_ref[...] = cc_scr[...]
        hn_ref[...] = ch_scr[...]

    return kern


def _mixer_specs(lay, l):
    wspec = lambda shape: pl.BlockSpec((None,) + shape, lambda b, t: (l,) + (0,) * len(shape))
    state = lambda *shape: pl.BlockSpec((lay.nb,) + shape, lambda b, t: (b,) + (0,) * len(shape))
    return wspec, lay.pcol, state


class _StateIO:
    def __init__(self, s0, stacked, l, nb):
        self.s0, self.l, self.nb = s0, l, nb
        self.stacked = stacked is not None
        self.prev = stacked[1] if self.stacked else None
        self.depth = stacked[0] if self.stacked else None

    def spec(self):
        shape = self.s0.shape[2:] if self.stacked else self.s0.shape[1:]
        zeros = (0,) * len(shape)
        if self.stacked:
            l = self.l
            return pl.BlockSpec((None, self.nb) + shape, lambda b, t: (l, b) + zeros)
        return pl.BlockSpec((self.nb,) + shape, lambda b, t: (b,) + zeros)

    def out_shape(self):
        return jax.ShapeDtypeStruct(self.s0.shape, F32)

    def extra_inputs(self):
        return [self.prev] if self.prev is not None else []

    def extra_specs(self):
        return [pl.BlockSpec(memory_space=pl.ANY)] if self.prev is not None else []

    def aliases(self, n_in, out_idx):
        return {n_in: out_idx} if self.prev is not None else {}

    def wrap(self, kern, n_in):
        if self.prev is None:
            return kern
        return lambda *refs: kern(*refs[:n_in], *refs[n_in + 1:])


def _lru(p, c0, h0, lw, l, lay, gb, t_real, is_prompt):
    nb, tp = lay.nb, lay.tp
    nseq, tseq = lay.dims(p)
    c = BRANCH
    wspec, pcol, state = _mixer_specs(lay, l)
    return pl.pallas_call(
        _make_lru_kernel(lay, gb, tp, t_real, is_prompt),
        grid=(nseq // nb, tseq // tp),
        in_specs=[
            pcol(c, COL_LX), pcol(c, COL_LY), state(SUBLANES, c), state(1, c),
            wspec((CONV_W, c)), wspec((1, c)),
            wspec((c // LRU_BD, LRU_BD, LRU_BD)), wspec((1, c)),
            wspec((c // LRU_BD, LRU_BD, LRU_BD)), wspec((1, c)),
            wspec((1, c)),
        ],
        out_specs=[pcol(c, 0), state(SUBLANES, c), state(1, c)],
        out_shape=[
            lay.out_shape(p, c),
            jax.ShapeDtypeStruct((nseq, SUBLANES, c), F32),
            jax.ShapeDtypeStruct((nseq, 1, c), F32),
        ],
        scratch_shapes=[
            pltpu.VMEM((gb, tp + SUBLANES, c), F32), pltpu.VMEM((gb, tp, c), F32), pltpu.VMEM((gb, tp, c), F32),
            pltpu.VMEM((nb, SUBLANES, c), F32), pltpu.VMEM((nb, 1, c), F32),
        ],
        compiler_params=_cparams(("parallel", "arbitrary")),
        name="lru_prompt" if is_prompt else "lru_sample",
    )(p, p, c0, h0, lw["conv_w"], lw["conv_b"], lw["wa"], lw["ba"], lw["wx"], lw["bx"], lw["lam"])


def _chunk_consts(c, cl):
    row = lax.broadcasted_iota(jnp.int32, (c, c), 0)
    col = lax.broadcasted_iota(jnp.int32, (c, c), 1)
    tril = row >= col
    lower_ones = jnp.where(tril, 1.0, 0.0).astype(BF)
    real = lax.broadcasted_iota(jnp.int32, (c, 1), 0) < cl
    return row, col, tril, lower_ones, real


def _ks(h):
    return slice(h * DK, (h + 1) * DK)


def _vs(h):
    return slice(h * DV, (h + 1) * DV)


def _chunk_loop(lay, n_groups, nc, c, per_chunk):
    def body(idx, carry):
        g = idx // nc
        per_chunk(g, pl.ds(pl.multiple_of((idx - g * nc) * c, SUBLANES), c))
        return carry

    if lay.ts is None:
        lax.fori_loop(0, n_groups * nc, body, 0)
    else:
        assert nc == 1
        for g in range(n_groups):
            per_chunk(g, slice(None))


def _make_gla_kernel(lay, gb, tp, c, t_real):
    nb = lay.nb
    nc = tp // c
    cl = min(c, t_real)
    G, H = range(gb), range(HEADS)

    def kern(q_ref, k_ref, v_ref, gr_ref, sm_ref, s0_ref, wg_ref, bg_ref, ng_ref, o_ref, sn_ref, s_scr):
        @pl.when(pl.program_id(1) == 0)
        def _():
            s_scr[...] = s0_ref[...]

        _, _, tril, lower_ones, real = _chunk_consts(c, cl)
        ones_real = jnp.where(real, 1.0, 0.0).astype(BF) * jnp.ones((c, DV), BF)
        d_tn = lambda x: lax.dot_general(x, ones_real, (((0,), (0,)), ((), ())), preferred_element_type=F32)
        ng = ng_ref[...]
        wg = wg_ref[...]
        bg = bg_ref[...]

        def per_chunk(g, rows):
            seqs = [g * gb + j for j in G]
            lg = [-_softplus(-(_mm(lay.load(sm_ref, sq, rows), wg) + bg)) * (1.0 / GLA_TAU) for sq in seqs]
            sp = [_split3(x) for x in lg]
            b = [_dot(lower_ones, s[0]) + _dot(lower_ones, s[1]) + _dot(lower_ones, s[2]) for s in sp]
            b_last_col = [d_tn(s[0]) + d_tn(s[1]) + d_tn(s[2]) for s in sp]
            q_t = [lay.load(q_ref, sq, rows) * (DK ** -0.5) * jnp.exp(b[j]) for j, sq in enumerate(seqs)]
            k = [lay.load(k_ref, sq, rows) for sq in seqs]
            k_t = [jnp.where(real, k[j] * jnp.exp(-b[j]), 0.0) for j in G]
            k_d = [jnp.where(real, k[j] * jnp.exp(b[j][cl - 1:cl, :] - b[j]), 0.0) for j in G]
            v = [lay.load(v_ref, sq, rows) for sq in seqs]
            att = [[jnp.where(tril, _mm_nt(q_t[j][:, _ks(h)], k_t[j][:, _ks(h)]), 0.0) for h in H] for j in G]
            s = [[s_scr[seqs[j], h] for h in H] for j in G]
            kv = [[_mm_tn(k_d[j][:, _ks(h)], v[j][:, _vs(h)]) for h in H] for j in G]
            o = [[_mm(att[j][h], v[j][:, _vs(h)]) + _mm(q_t[j][:, _ks(h)], s[j][h]) for h in H] for j in G]
            for j in G:
                for h in H:
                    s_scr[seqs[j], h] = jnp.exp(b_last_col[j][_ks(h), :]) * s[j][h] + kv[j][h]
            for j in G:
                gr = lay.load(gr_ref, seqs[j], rows)
                for h in H:
                    lay.store(o_ref, seqs[j], rows, _vs(h), _rms(o[j][h], ng) * _silu(gr[:, _vs(h)]))

        _chunk_loop(lay, nb // gb, nc, c, per_chunk)
        sn_ref[...] = s_scr[...]

    return kern


def _gla(p, s0, lw, l, lay, gb, c, t_real, name, stacked=None):
    nb, tp = lay.nb, lay.tp
    nseq, tseq = lay.dims(p)
    qk = HEADS * DK
    vw = HEADS * DV
    wspec, pcol, _ = _mixer_specs(lay, l)
    sio = _StateIO(s0, stacked, l, nb)
    in_specs = [
        pcol(qk, COL_GQ), pcol(qk, COL_GK), pcol(vw, COL_GV), pcol(vw, COL_GR), pcol(128, COL_SM),
        sio.spec(), wspec((128, qk)), wspec((1, qk)), wspec((1, DV)),
    ]
    n_in = len(in_specs)
    return pl.pallas_call(
        sio.wrap(_make_gla_kernel(lay, gb, tp, c, t_real), n_in),
        grid=(nseq // nb, tseq // tp),
        in_specs=in_specs + sio.extra_specs(),
        out_specs=[pcol(vw, 0), sio.spec()],
        out_shape=[lay.out_shape(p, vw), sio.out_shape()],
        scratch_shapes=[pltpu.VMEM((nb, HEADS, DK, DV), F32)],
        input_output_aliases=sio.aliases(n_in, 1),
        compiler_params=_cparams(("parallel", "arbitrary")),
        name=name,
    )(p, p, p, p, p, s0, lw["gla_wg"], lw["gla_bg"], lw["gla_ng"], *sio.extra_inputs())


def _make_ret_kernel(lay, gb, tp, c, t_real):
    nb = lay.nb
    nc = tp // c
    cl = min(c, t_real)
    G, H = range(gb), range(HEADS)
    log_gamma = [float(np.log(np.float32(1.0) - np.float32(2.0) ** np.float32(-5.0 - h))) for h in range(HEADS)]

    def kern(q_ref, k_ref, qs_ref, ks_ref, v_ref, g_ref, cos_ref, sin_ref, s0_ref, ng_ref, o_ref, sn_ref, s_scr):
        @pl.when(pl.program_id(1) == 0)
        def _():
            s_scr[...] = s0_ref[...]

        row, col, tril, _, real = _chunk_consts(c, cl)
        diff = (row - col).astype(F32)
        ridx = lax.broadcasted_iota(jnp.int32, (c, 1), 0).astype(F32)
        ng = ng_ref[...]
        dmat = [jnp.where(tril, jnp.exp(jnp.maximum(diff, 0.0) * lgm), 0.0) for lgm in log_gamma]
        q_decay = [jnp.exp((ridx + 1.0) * lgm) for lgm in log_gamma]
        k_decay = [jnp.exp((cl - 1.0 - ridx) * lgm) for lgm in log_gamma]

        def per_chunk(g, rows):
            seqs = [g * gb + j for j in G]
            cos = cos_ref[rows, :]
            sin = sin_ref[rows, :]
            ld = lambda ref, sq: lay.load(ref, sq, rows)
            q = [ld(q_ref, sq) * cos + ld(qs_ref, sq) * sin for sq in seqs]
            k = [jnp.where(real, (ld(k_ref, sq) * cos + ld(ks_ref, sq) * sin) * (DK ** -0.5), 0.0) for sq in seqs]
            v = [ld(v_ref, sq) for sq in seqs]
            att = [[_mm_nt(q[j][:, _ks(h)], k[j][:, _ks(h)]) * dmat[h] for h in H] for j in G]
            s = [[s_scr[seqs[j], h] for h in H] for j in G]
            kv = [[_mm_tn(k[j][:, _ks(h)] * k_decay[h], v[j][:, _vs(h)]) for h in H] for j in G]
            o = [[_mm(att[j][h], v[j][:, _vs(h)]) + _mm(q[j][:, _ks(h)] * q_decay[h], s[j][h]) for h in H]
                 for j in G]
            for j in G:
                for h in H:
                    s_scr[seqs[j], h] = math.exp(cl * log_gamma[h]) * s[j][h] + kv[j][h]
            for j in G:
                gate = ld(g_ref, seqs[j])
                for h in H:
                    oc = o[j][h] - jnp.mean(o[j][h], axis=-1, keepdims=True)
                    on = oc * lax.rsqrt(jnp.mean(oc * oc, axis=-1, keepdims=True) + EPS) * ng
                    lay.store(o_ref, seqs[j], rows, _vs(h), on * _silu(gate[:, _vs(h)]))

        _chunk_loop(lay, nb // gb, nc, c, per_chunk)
        sn_ref[...] = s_scr[...]

    return kern


def _ret(p, s0, cos_t, sin_t, lw, l, lay, gb, c, t_real, name, stacked=None):
    nb, tp = lay.nb, lay.tp
    nseq, tseq = lay.dims(p)
    qk = HEADS * DK
    vw = HEADS * DV
    wspec, pcol, _ = _mixer_specs(lay, l)
    sio = _StateIO(s0, stacked, l, nb)
    in_specs = [
        pcol(qk, COL_RQ), pcol(qk, COL_RK), pcol(qk, COL_RQS), pcol(qk, COL_RKS), pcol(vw, COL_RV),
        pcol(vw, COL_RG),
        pl.BlockSpec((tp, qk), lambda b, t: (t, 0)),
        pl.BlockSpec((tp, qk), lambda b, t: (t, 0)),
        sio.spec(), wspec((1, DV)),
    ]
    n_in = len(in_specs)
    return pl.pallas_call(
        sio.wrap(_make_ret_kernel(lay, gb, tp, c, t_real), n_in),
        grid=(nseq // nb, tseq // tp),
        in_specs=in_specs + sio.extra_specs(),
        out_specs=[pcol(vw, 0), sio.spec()],
        out_shape=[lay.out_shape(p, vw), sio.out_shape()],
        scratch_shapes=[pltpu.VMEM((nb, HEADS, DK, DV), F32)],
        input_output_aliases=sio.aliases(n_in, 1),
        compiler_params=_cparams(("parallel", "arbitrary")),
        name=name,
    )(p, p, p, p, p, p, cos_t, sin_t, s0, lw["ret_ng"], *sio.extra_inputs())


def _unit_lower_inverse_many(ms, c):
    row = lax.broadcasted_iota(jnp.int32, (c, c), 0)
    col = lax.broadcasted_iota(jnp.int32, (c, c), 1)
    eye = jnp.where(row == col, 1.0, 0.0)

    def same_block(bits):
        return lax.shift_right_logical(row, bits) == lax.shift_right_logical(col, bits)

    in8 = same_block(3)
    n1 = [jnp.where(in8, m, 0.0) for m in ms]
    n2 = [_mm_hp(x, x) for x in n1]
    n4 = [_mm_hp(x, x) for x in n2]
    d = [eye - x for x in n1]
    d = [x + _mm_hp(x, y) for x, y in zip(d, n2)]
    d = [x + _mm_hp(x, y) for x, y in zip(d, n4)]
    bits = 3
    while (1 << bits) < c:
        sel = jnp.logical_and(same_block(bits + 1), jnp.logical_not(same_block(bits)))
        ld = [_mm(jnp.where(sel, m, 0.0), x) for m, x in zip(ms, d)]
        d = [x - _mm(x, y) for x, y in zip(d, ld)]
        bits += 1
    return d


def _make_dn_kernel(lay, gb, tp, c, t_real):
    nb = lay.nb
    nc = tp // c
    cl = min(c, t_real)
    G, H = range(gb), range(HEADS)
    GH = [(j, h) for j in G for h in H]

    def kern(x_ref, z_ref, sm_ref, c0_ref, s0_ref, cw_ref, alog_ref, dtb_ref, ng_ref,
             o_ref, cn_ref, sn_ref, xs_scr, qkv_scr, cc_scr, s_scr):
        @pl.when(pl.program_id(1) == 0)
        def _():
            cc_scr[...] = c0_ref[...]
            s_scr[...] = s0_ref[...]

        row, col, tril, lower_ones, real = _chunk_consts(c, cl)
        strict = row > col
        lane = lax.broadcasted_iota(jnp.int32, (c, 128), 1)
        sel = [jnp.where(lane == SM_DA + h, 1.0, 0.0).astype(BF) for h in H]
        d_nt = lambda a, x: lax.dot_general(a, x, (((1,), (1,)), ((), ())), preferred_element_type=F32)
        ng = ng_ref[...]
        cw = cw_ref[...]
        neg_a = -jnp.exp(alog_ref[...])
        dtb = dtb_ref[...]

        def per_group(g, carry):
            seqs = [g * gb + j for j in G]
            for j, sq in enumerate(seqs):
                xs_scr[j, 0:SUBLANES, :] = cc_scr[sq]
                xs_scr[j, SUBLANES:SUBLANES + tp, :] = lay.load(x_ref, sq)
            for j, sq in enumerate(seqs):
                qkv_scr[j] = _silu(_conv4(xs_scr, j, cw, tp))
                cc_scr[sq] = xs_scr[j, pl.ds(t_real, SUBLANES), :]

            def per_chunk(ci, carry2):
                rows = pl.ds(pl.multiple_of(ci * c, SUBLANES), c)
                sm = [lay.load(sm_ref, sq, rows) for sq in seqs]
                gcum = [_cumsum_rows(neg_a * _softplus(x + dtb), lower_ones) for x in sm]
                g3 = [_split3(x) for x in gcum]
                beta_all = [jnp.where(real, jax.nn.sigmoid(x), 0.0) for x in sm]

                def head_in(j, h, part):
                    x = qkv_scr[j, rows, part * BRANCH + h * DV:part * BRANCH + (h + 1) * DV]
                    return x

                qh = {jh: head_in(*jh, 0) for jh in GH}
                kh = {jh: head_in(*jh, 1) for jh in GH}
                vh = {jh: head_in(*jh, 2) for jh in GH}
                qh = {jh: x * lax.rsqrt(jnp.sum(x * x, axis=-1, keepdims=True) + EPS) * (DV ** -0.5)
                      for jh, x in qh.items()}
                kh = {jh: jnp.where(real, x * lax.rsqrt(jnp.sum(x * x, axis=-1, keepdims=True) + EPS), 0.0)
                      for jh, x in kh.items()}
                beta = {(j, h): beta_all[j][:, SM_DB + h:SM_DB + h + 1] for j, h in GH}
                gcol = {(j, h): gcum[j][:, SM_DA + h:SM_DA + h + 1] for j, h in GH}
                grow = {(j, h): d_nt(sel[h], g3[j][0]) + d_nt(sel[h], g3[j][1]) + d_nt(sel[h], g3[j][2])
                        for j, h in GH}
                decay = {jh: jnp.where(tril, jnp.exp(jnp.where(tril, gcol[jh] - grow[jh], 0.0)), 0.0) for jh in GH}
                e_g = {jh: jnp.exp(gcol[jh]) for jh in GH}
                kb = {jh: kh[jh] * beta[jh] for jh in GH}
                kq = {jh: _mm_nt(jnp.concatenate([kb[jh], qh[jh]], axis=0), kh[jh]) for jh in GH}
                tm = _unit_lower_inverse_many([jnp.where(strict, kq[jh][:c] * decay[jh], 0.0) for jh in GH], c)
                tm = dict(zip(GH, tm))
                uw = {jh: _mm(tm[jh], jnp.concatenate([vh[jh] * beta[jh], kb[jh] * e_g[jh]], axis=1)) for jh in GH}
                s = {(j, h): s_scr[seqs[j], h] for j, h in GH}
                wq = {jh: _mm(jnp.concatenate([uw[jh][:, DV:], qh[jh] * e_g[jh]], axis=0), s[jh]) for jh in GH}
                v_new = {jh: uw[jh][:, :DV] - wq[jh][:c] for jh in GH}
                o = {jh: wq[jh][c:] + _mm(kq[jh][c:] * decay[jh], v_new[jh]) for jh in GH}
                for j, h in GH:
                    g_last = gcol[(j, h)][cl - 1:cl, :]
                    k_d = kh[(j, h)] * jnp.exp(g_last - gcol[(j, h)])
                    s_scr[seqs[j], h] = jnp.exp(g_last) * s[(j, h)] + _mm_tn(k_d, v_new[(j, h)])
                for j in G:
                    z = lay.load(z_ref, seqs[j], rows)
                    for h in H:
                        lay.store(o_ref, seqs[j], rows, _vs(h), _rms(o[(j, h)], ng) * _silu(z[:, _vs(h)]))
                return carry2

            lax.fori_loop(0, nc, per_chunk, 0)
            return carry

        lay.for_groups(nb // gb, per_group)
        cn_ref[...] = cc_scr[...]
        sn_ref[...] = s_scr[...]

    return kern


def _dn(p, c0, s0, lw, l, lay, gb, c, t_real, name, stacked=None):
    nb, tp = lay.nb, lay.tp
    nseq, tseq = lay.dims(p)
    cw3 = 3 * BRANCH
    wspec, pcol, state = _mixer_specs(lay, l)
    sio = _StateIO(s0, stacked, l, nb)
    in_specs = [
        pcol(cw3, COL_DQKV), pcol(BRANCH, COL_DZ), pcol(128, COL_SM),
        state(SUBLANES, cw3), sio.spec(),
        wspec((CONV_W, cw3)), wspec((1, 128)), wspec((1, 128)), wspec((1, DV)),
    ]
    n_in = len(in_specs)
    return pl.pallas_call(
        sio.wrap(_make_dn_kernel(lay, gb, tp, c, t_real), n_in),
        grid=(nseq // nb, tseq // tp),
        in_specs=in_specs + sio.extra_specs(),
        out_specs=[pcol(BRANCH, 0), state(SUBLANES, cw3), sio.spec()],
        out_shape=[
            lay.out_shape(p, BRANCH),
            jax.ShapeDtypeStruct((nseq, SUBLANES, cw3), F32),
            sio.out_shape(),
        ],
        scratch_shapes=[
            pltpu.VMEM((gb, tp + SUBLANES, cw3), F32), pltpu.VMEM((gb, tp, cw3), F32),
            pltpu.VMEM((nb, SUBLANES, cw3), F32), pltpu.VMEM((nb, HEADS, DV, DV), F32),
        ],
        input_output_aliases=sio.aliases(n_in, 2),
        compiler_params=_cparams(("parallel", "arbitrary")),
        name=name,
    )(p, p, p, c0, s0, lw["dn_conv_w"], lw["dn_alog"], lw["dn_dtb"], lw["dn_ng"], *sio.extra_inputs())


IN_SIZES = (BRANCH, BRANCH, HEADS * DK, HEADS * DK, BRANCH, BRANCH, GLA_RANK,
            HEADS * DK, HEADS * DK, BRANCH, BRANCH, BRANCH, BRANCH, BRANCH, BRANCH, HEADS, HEADS)


IN_OFFS = tuple(int(v) for v in np.concatenate([[0], np.cumsum(IN_SIZES)]))
PACK_ROWS = 256


def _pack_src_row(r):
    t_rq, t_swap, t_rv, t_small = (c // PACK_ROWS for c in (COL_RQ, COL_RQS, COL_RV, COL_SM))
    o_rq, o_rv = IN_OFFS[7], IN_OFFS[9]
    row = jnp.where(r < t_rq, r * PACK_ROWS,
                    jnp.where(r < t_swap, o_rq + (r - t_rq) * PACK_ROWS,
                              jnp.where(r < t_rv, o_rq + (r - t_swap) * PACK_ROWS,
                                        jnp.where(r < t_small, o_rv + (r - t_rv) * PACK_ROWS, 0))))
    return pl.multiple_of(row, 2 * SUBLANES)


def _pack_kernel(w_ref, glr_ref, dba_ref, o_ref):
    r = pl.program_id(1)
    t_swap, t_rv, t_small = (c // PACK_ROWS for c in (COL_RQS, COL_RV, COL_SM))
    w = w_ref[0]

    @pl.when(jnp.logical_and(jnp.logical_or(r < t_swap, r >= t_rv), r < t_small))
    def _():
        o_ref[...] = w.astype(BF)

    @pl.when(jnp.logical_and(r >= t_swap, r < t_rv))
    def _():
        parts = []
        for h in range(PACK_ROWS // DK):
            parts += [w[h * DK + DK // 2:(h + 1) * DK], w[h * DK:h * DK + DK // 2]]
        o_ref[...] = jnp.concatenate(parts, axis=0).astype(BF)

    @pl.when(r == t_small)
    def _():
        pad = jnp.zeros((PACK_ROWS - GLA_RANK - 2 * HEADS, w.shape[1]), w.dtype)
        o_ref[...] = jnp.concatenate([glr_ref[0], dba_ref[0], pad], axis=0).astype(BF)


def _pack_w_in(w_in_t):
    depth, n_in, d = w_in_t.shape
    assert n_in == IN_OFFS[-1] and N_PACK % PACK_ROWS == 0 and COL_SM // PACK_ROWS == N_PACK // PACK_ROWS - 1
    rows = lambda n, start: pl.BlockSpec((pl.Element(1), pl.Element(n), pl.Element(d)),
                                         lambda l, r: (l, start(r), 0))
    return pl.pallas_call(
        _pack_kernel,
        grid=(depth, N_PACK // PACK_ROWS),
        in_specs=[rows(PACK_ROWS, _pack_src_row), rows(GLA_RANK, lambda r: IN_OFFS[6]),
                  rows(2 * HEADS, lambda r: IN_OFFS[15])],
        out_specs=pl.BlockSpec((None, PACK_ROWS, d), lambda l, r: (l, r, 0)),
        out_shape=jax.ShapeDtypeStruct((depth, N_PACK, d), BF),
        compiler_params=_cparams(("parallel", "parallel")),
        name="pack_w_in",
    )(w_in_t, w_in_t, w_in_t)


def _rope_tables(pos):
    half = DK // 2
    inv = ROPE_BASE ** (-jnp.arange(half, dtype=F32) / half)
    ang = pos.astype(F32)[:, None] * inv[None, :]
    cos, sin = jnp.cos(ang), jnp.sin(ang)
    cos_t = jnp.tile(jnp.concatenate([cos, cos], axis=1), (1, HEADS))
    sin_t = jnp.tile(jnp.concatenate([-sin, sin], axis=1), (1, HEADS))
    return cos_t, sin_t


def _lane_row(vals, off):
    depth = vals.shape[0]
    return jnp.zeros((depth, 1, 128), F32).at[:, 0, off:off + HEADS].set(vals.astype(F32))


def kernel(x_prompt, x_sample, state_lru_conv, state_lru_h, state_gla, state_ret, state_dn_conv, state_dn,
           norm_g, final_norm_g, w_ff_in, w_ff_out, w_in, w_gate, w_branch, w_out,
           lru_conv_w, lru_conv_b, lru_wa, lru_ba, lru_wx, lru_bx, lru_lambda,
           gla_wg, gla_bg, gla_norm_g, ret_norm_g, dn_conv_w, dn_a_log, dn_dt_bias, dn_norm_g):
    bp, tpr, d = x_prompt.shape
    bs, ts, _ = x_sample.shape
    depth = w_in.shape[0]
    n_p, n_s = bp * tpr, bs * ts
    assert d == D_MODEL and ts <= SAMPLE_PAD and tpr % CHUNK == 0

    w_pack = _pack_w_in(jnp.swapaxes(w_in, 1, 2))
    wg_pad = jnp.zeros((depth, 128, HEADS * DK), F32).at[:, SM_GLR:SM_GLR + GLA_RANK, :].set(gla_wg).astype(BF)
    r3 = lambda a: a.reshape(a.shape[0], 1, a.shape[1])
    lw = {
        "conv_w": lru_conv_w, "conv_b": r3(lru_conv_b), "wa": lru_wa.astype(BF), "ba": r3(lru_ba),
        "wx": lru_wx.astype(BF), "bx": r3(lru_bx), "lam": r3(lru_lambda),
        "gla_wg": wg_pad, "gla_bg": r3(gla_bg), "gla_ng": r3(gla_norm_g), "ret_ng": r3(ret_norm_g),
        "dn_conv_w": dn_conv_w, "dn_alog": _lane_row(dn_a_log, SM_DA), "dn_dtb": _lane_row(dn_dt_bias, SM_DA),
        "dn_ng": r3(dn_norm_g),
    }
    norm_g4 = norm_g.reshape(depth, 3, 1, d)

    cos_p, sin_p = _rope_tables(jnp.arange(tpr))
    pos_s = jnp.where(jnp.arange(SAMPLE_PAD) < ts, PAST_LEN + jnp.arange(SAMPLE_PAD), 0)
    cos_s, sin_s = _rope_tables(pos_s)

    pad_conv = lambda a: jnp.pad(a, ((0, 0), (SUBLANES - (CONV_W - 1), 0), (0, 0)))
    zeros = lambda *s: jnp.zeros(s, F32)

    tp_p = _pick_tile(tpr, 256, CHUNK)
    a_p = dict(lay=_SeqLayout(bp, tp_p), gb=bp)
    nb_s = _pick_tile(bs, 8, 2)
    assert (nb_s * ts) % SUBLANES == 0
    a_s = dict(lay=_SeqLayout(nb_s, SAMPLE_PAD, ts), gb=_pick_tile(nb_s, 4, 1))

    fg = final_norm_g.reshape(1, d)
    xp = x_prompt.reshape(n_p, d)
    xs = x_sample.reshape(n_s, d)
    new_p = [[] for _ in range(6)]
    new_s = [[] for _ in range(6)]
    s_gla_s = s_ret_s = s_dn_s = None
    for l in range(depth):
        xs, *w_ffn = _ffn_cast(xs, norm_g4[l, 0], w_ff_in, w_ff_out, l, 0)
        xp = _ffn(xp, norm_g4[l, 0], *w_ffn)
        p_p = _inproj(xp, norm_g4[l, 1], w_pack, l).reshape(bp, tpr, N_PACK)
        p_s = _inproj(xs, norm_g4[l, 1], w_pack, l)

        o_lru_p, c_lru_p, h_lru_p = _lru(p_p, zeros(bp, SUBLANES, BRANCH), zeros(bp, 1, BRANCH), lw, l,
                                         t_real=tp_p, is_prompt=True, **a_p)
        o_gla_p, s_gla_p = _gla(p_p, zeros(bp, HEADS, DK, DV), lw, l, c=CHUNK, t_real=tp_p, name="gla_prompt", **a_p)
        o_ret_p, s_ret_p = _ret(p_p, zeros(bp, HEADS, DK, DV), cos_p, sin_p, lw, l, c=CHUNK, t_real=tp_p,
                                name="ret_prompt", **a_p)
        o_dn_p, c_dn_p, s_dn_p = _dn(p_p, zeros(bp, SUBLANES, 3 * BRANCH), zeros(bp, HEADS, DV, DV), lw, l,
                                     c=CHUNK, t_real=tp_p, name="dn_prompt", **a_p)

        o_lru_s, c_lru_s, h_lru_s = _lru(p_s, pad_conv(state_lru_conv[l]), state_lru_h[l][:, None, :], lw, l,
                                         t_real=ts, is_prompt=False, **a_s)
        o_gla_s, s_gla_s = _gla(p_s, state_gla, lw, l, c=SAMPLE_PAD, t_real=ts, name="gla_sample",
                                stacked=(depth, s_gla_s), **a_s)
        o_ret_s, s_ret_s = _ret(p_s, state_ret, cos_s, sin_s, lw, l, c=SAMPLE_PAD, t_real=ts, name="ret_sample",
                                stacked=(depth, s_ret_s), **a_s)
        o_dn_s, c_dn_s, s_dn_s = _dn(p_s, pad_conv(state_dn_conv[l]), state_dn, lw, l, c=SAMPLE_PAD, t_real=ts,
                                     name="dn_sample", stacked=(depth, s_dn_s), **a_s)

        flat_p = lambda o: o.reshape(n_p, BRANCH)
        xs, *w_merge = _merge_cast(xs, norm_g4[l, 1], [o_lru_s, o_gla_s, o_ret_s, o_dn_s], w_gate, w_branch, w_out, l)
        xp = _merge_split(xp, norm_g4[l, 1], [flat_p(o) for o in (o_lru_p, o_gla_p, o_ret_p, o_dn_p)], *w_merge)
        xs, *w_ffn = _ffn_cast(xs, norm_g4[l, 2], w_ff_in, w_ff_out, l, 1)
        xp = _ffn(xp, norm_g4[l, 2], *w_ffn, final_g=fg if l == depth - 1 else None)

        tail3 = lambda cwin: cwin[:, SUBLANES - (CONV_W - 1):, :]
        for i, v in enumerate((tail3(c_lru_p), h_lru_p[:, 0], s_gla_p, s_ret_p, tail3(c_dn_p), s_dn_p)):
            new_p[i].append(v)
        for i, v in ((0, tail3(c_lru_s)), (1, h_lru_s[:, 0]), (4, tail3(c_dn_s))):
            new_s[i].append(v)

    y_prompt = xp.reshape(bp, tpr, d)
    y_sample = _final_norm(xs, fg).reshape(bs, ts, d)
    sp = [jnp.stack(v) for v in new_p]
    s_small = {i: jnp.stack(new_s[i]) for i in (0, 1, 4)}
    return tuple([y_prompt, y_sample] + sp + [s_small[0], s_small[1], s_gla_s, s_ret_s, s_small[4], s_dn_s])
```

```python
import functools
import math

import numpy as np
import jax
import jax.numpy as jnp
from jax import lax
from jax.experimental import pallas as pl
from jax.experimental.pallas import tpu as pltpu

F32 = jnp.float32
BF = jnp.bfloat16
EPS = 1e-6

D_MODEL = 2048
BRANCH = 512
CONV_W = 4
HEADS = 4
DK = 64
DV = 128
GLA_RANK = 16
GLA_TAU = 16.0
LRU_C = 8.0
LRU_BD = 128
ROPE_BASE = 10000.0
CHUNK = 64
PAST_LEN = 16384

V7X_VMEM_BYTES = 64 * 1024 * 1024
VMEM_LIMIT = V7X_VMEM_BYTES - 8 * 1024 * 1024
SUBLANES = 8
SAMPLE_PAD = 8

COL_LX, COL_LY = 0, 512
COL_GQ, COL_GK, COL_GV, COL_GR = 1024, 1280, 1536, 2048
COL_RQ, COL_RK, COL_RQS, COL_RKS, COL_RV, COL_RG = 2560, 2816, 3072, 3328, 3584, 4096
COL_DQKV, COL_DZ = 4608, 6144
COL_SM = 6656
SM_GLR, SM_DB, SM_DA = 0, 16, 20
N_PACK = 6912


def _dot(a, b):
    return jnp.dot(a, b, preferred_element_type=F32)


def _mm(a, b):
    return _dot(a.astype(BF), b.astype(BF))


def _mm_nt(a, b):
    return lax.dot_general(a.astype(BF), b.astype(BF), (((1,), (1,)), ((), ())), preferred_element_type=F32)


def _mm_tn(a, b):
    return lax.dot_general(a.astype(BF), b.astype(BF), (((0,), (0,)), ((), ())), preferred_element_type=F32)


def _split3(x):
    hi = x.astype(BF)
    r = x - hi.astype(F32)
    mid = r.astype(BF)
    lo = (r - mid.astype(F32)).astype(BF)
    return hi, mid, lo


def _split2(x):
    hi = x.astype(BF)
    return hi, (x - hi.astype(F32)).astype(BF)


def _mm_hp(a, b):
    ah, al = _split2(a)
    bh, bl = _split2(b)
    return _dot(ah, bh) + _dot(ah, bl) + _dot(al, bh)


def _cumsum_rows(x, lower_ones):
    hi, mid, lo = _split3(x)
    return _dot(lower_ones, hi) + _dot(lower_ones, mid) + _dot(lower_ones, lo)


def _softplus(x):
    return jnp.maximum(x, 0.0) + jnp.log1p(jnp.exp(-jnp.abs(x)))


def _silu(x):
    return x * jax.nn.sigmoid(x)


def _gelu_tanh(x):
    return x * (0.5 * (1.0 + jnp.tanh(math.sqrt(2.0 / math.pi) * (x + 0.044715 * (x * x * x)))))


def _rms(x, g):
    return x * lax.rsqrt(jnp.mean(x * x, axis=-1, keepdims=True) + EPS) * g


def _pick_tile(n, target, align):
    best = None
    for t in range(align, min(n, target) + 1, align):
        if n % t == 0:
            best = t
    assert best is not None, (n, target, align)
    return best


def _cparams(sem):
    return pltpu.CompilerParams(dimension_semantics=sem, vmem_limit_bytes=VMEM_LIMIT)


def _ffn_kernel(x_ref, g_ref, wg_ref, wu_ref, wo_ref, o_ref, hn_ref):
    @pl.when(pl.program_id(1) == 0)
    def _():
        x = x_ref[...]
        hn_ref[...] = _rms(x, g_ref[...]).astype(BF)
        o_ref[...] = x

    hn = hn_ref[...]
    gate = _dot(hn, wg_ref[...])
    up = _dot(hn, wu_ref[...])
    act = (_silu(gate) * up * 0.5).astype(BF)
    o_ref[...] += _dot(act, wo_ref[...])


def _ffn_final_kernel(x_ref, g_ref, wg_ref, wu_ref, wo_ref, fg_ref, o_ref, hn_ref):
    _ffn_kernel(x_ref, g_ref, wg_ref, wu_ref, wo_ref, o_ref, hn_ref)

    @pl.when(pl.program_id(1) == pl.num_programs(1) - 1)
    def _():
        o_ref[...] = _rms(o_ref[...], fg_ref[...])


def _ffn(x, g, wg, wu, wo, final_g=None):
    n, d = x.shape
    f = wo.shape[0]
    tm = _pick_tile(n, 1024, 16)
    tf = _pick_tile(f, 512, 128)
    final = [] if final_g is None else [final_g]
    return pl.pallas_call(
        _ffn_kernel if final_g is None else _ffn_final_kernel,
        grid=(n // tm, f // tf),
        in_specs=[
            pl.BlockSpec((tm, d), lambda i, j: (i, 0)),
            pl.BlockSpec((1, d), lambda i, j: (0, 0)),
            pl.BlockSpec((d, tf), lambda i, j: (0, j)),
            pl.BlockSpec((d, tf), lambda i, j: (0, j)),
            pl.BlockSpec((tf, d), lambda i, j: (j, 0)),
        ] + [pl.BlockSpec((1, d), lambda i, j: (0, 0))] * len(final),
        out_specs=pl.BlockSpec((tm, d), lambda i, j: (i, 0)),
        out_shape=jax.ShapeDtypeStruct((n, d), F32),
        scratch_shapes=[pltpu.VMEM((tm, d), BF)],
        compiler_params=_cparams(("parallel", "arbitrary")),
        name="ffn",
    )(x, g, wg, wu, wo, *final)


def _ffn_cast_kernel(x_ref, g_ref, wg32_ref, wu32_ref, wo32_ref, o_ref, wg_ref, wu_ref, wo_ref, hn_ref):
    @pl.when(pl.program_id(1) == 0)
    def _():
        x = x_ref[...]
        hn_ref[...] = _rms(x, g_ref[...]).astype(BF)
        o_ref[...] = x

    wg_ref[...] = wg32_ref[...].astype(BF)
    wu_ref[...] = wu32_ref[...].astype(BF)
    wo_ref[...] = wo32_ref[...].astype(BF)
    hn = hn_ref[...]
    act = (_silu(_dot(hn, wg_ref[...])) * _dot(hn, wu_ref[...]) * 0.5).astype(BF)
    o_ref[...] += _dot(act, wo_ref[...])


def _ffn_cast(x, g, w_in, w_out, l, s):
    n, d = x.shape
    f = w_out.shape[2]
    tf = _pick_tile(f, 512, 128)
    nf = f // tf
    once = dict(pipeline_mode=pl.Buffered(1))
    return pl.pallas_call(
        _ffn_cast_kernel,
        grid=(1, nf),
        in_specs=[
            pl.BlockSpec((n, d), lambda i, j: (0, 0), **once),
            pl.BlockSpec((1, d), lambda i, j: (0, 0)),
            pl.BlockSpec((None, None, d, tf), lambda i, j: (l, s, 0, j)),
            pl.BlockSpec((None, None, d, tf), lambda i, j: (l, s, 0, j + nf)),
            pl.BlockSpec((None, None, tf, d), lambda i, j: (l, s, j, 0)),
        ],
        out_specs=[
            pl.BlockSpec((n, d), lambda i, j: (0, 0), **once),
            pl.BlockSpec((d, tf), lambda i, j: (0, j)),
            pl.BlockSpec((d, tf), lambda i, j: (0, j)),
            pl.BlockSpec((tf, d), lambda i, j: (j, 0)),
        ],
        out_shape=[
            jax.ShapeDtypeStruct((n, d), F32),
            jax.ShapeDtypeStruct((d, f), BF), jax.ShapeDtypeStruct((d, f), BF), jax.ShapeDtypeStruct((f, d), BF),
        ],
        scratch_shapes=[pltpu.VMEM((n, d), BF)],
        compiler_params=_cparams(("arbitrary", "arbitrary")),
        name="ffn_cast",
    )(x, g, w_in, w_in, w_out)


def _inproj_kernel(x_ref, g_ref, w_ref, o_ref, hn_ref):
    @pl.when(pl.program_id(1) == 0)
    def _():
        hn_ref[...] = _rms(x_ref[...], g_ref[...]).astype(BF)

    o_ref[...] = _mm_nt(hn_ref[...], w_ref[...])


def _inproj(x, g, w_pack, l):
    n, d = x.shape
    npk = w_pack.shape[1]
    tm = _pick_tile(n, 1024, 16)
    tn = _pick_tile(npk, 1024, 128)
    return pl.pallas_call(
        _inproj_kernel,
        grid=(n // tm, npk // tn),
        in_specs=[
            pl.BlockSpec((tm, d), lambda i, j: (i, 0)),
            pl.BlockSpec((1, d), lambda i, j: (0, 0)),
            pl.BlockSpec((None, tn, d), lambda i, j: (l, j, 0)),
        ],
        out_specs=[pl.BlockSpec((tm, tn), lambda i, j: (i, j)), pl.BlockSpec((tm, d), lambda i, j: (i, 0))],
        out_shape=[jax.ShapeDtypeStruct((n, npk), F32), jax.ShapeDtypeStruct((n, d), BF)],
        compiler_params=_cparams(("parallel", "arbitrary")),
        name="inproj",
    )(x, g, w_pack)


def _merge_kernel(x_ref, g_ref, b0_ref, b1_ref, b2_ref, b3_ref, wgate_ref, wbr_ref, wo_ref, o_ref, hn_ref):
    @pl.when(pl.program_id(1) == 0)
    def _():
        x = x_ref[...]
        hn_ref[...] = _rms(x, g_ref[...]).astype(BF)
        o_ref[...] = x

    hn = hn_ref[...]
    m = None
    for n, b_ref in enumerate((b0_ref, b1_ref, b2_ref, b3_ref)):
        gate = jax.nn.sigmoid(_dot(hn, wgate_ref[n]))
        br = _dot(b_ref[...].astype(BF), wbr_ref[n])
        m = gate * br if m is None else m + gate * br
    o_ref[...] += _dot(m.astype(BF), wo_ref[...])


def _gate_kernel(hn_ref, b0_ref, b1_ref, b2_ref, b3_ref, wgate_ref, wbr_ref, m_ref):
    hn = hn_ref[...]
    m = None
    for n, b_ref in enumerate((b0_ref, b1_ref, b2_ref, b3_ref)):
        gate = jax.nn.sigmoid(_dot(hn, wgate_ref[n]))
        br = _dot(b_ref[...].astype(BF), wbr_ref[n])
        m = gate * br if m is None else m + gate * br
    m_ref[...] = m.astype(BF)


def _outproj_kernel(m_ref, w_ref, x_ref, o_ref):
    o_ref[...] = x_ref[...] + _dot(m_ref[...], w_ref[...])


def _merge_split(x, hn, branches, w_gate_l, w_branch_l, w_out_l):
    n, d = x.shape
    tm = _pick_tile(n, 1024, 16)
    tn = 512
    m = pl.pallas_call(
        _gate_kernel,
        grid=(n // tm, d // tn),
        in_specs=[pl.BlockSpec((tm, d), lambda i, j: (i, 0))]
        + [pl.BlockSpec((tm, BRANCH), lambda i, j: (i, 0))] * HEADS + [
            pl.BlockSpec((HEADS, d, tn), lambda i, j: (0, 0, j)),
            pl.BlockSpec((HEADS, BRANCH, tn), lambda i, j: (0, 0, j)),
        ],
        out_specs=pl.BlockSpec((tm, tn), lambda i, j: (i, j)),
        out_shape=jax.ShapeDtypeStruct((n, d), BF),
        compiler_params=_cparams(("parallel", "arbitrary")),
        name="gate",
    )(hn, *branches, w_gate_l, w_branch_l)
    to = _pick_tile(n, 512, 16)
    return pl.pallas_call(
        _outproj_kernel,
        grid=(n // to,),
        in_specs=[
            pl.BlockSpec((to, d), lambda i: (i, 0)),
            pl.BlockSpec((d, d), lambda i: (0, 0), pipeline_mode=pl.Buffered(1)),
            pl.BlockSpec((to, d), lambda i: (i, 0)),
        ],
        out_specs=pl.BlockSpec((to, d), lambda i: (i, 0)),
        out_shape=jax.ShapeDtypeStruct((n, d), F32),
        compiler_params=_cparams(("parallel",)),
        name="outproj",
    )(m, w_out_l, x)


def _merge_cast_kernel(x_ref, g_ref, b0_ref, b1_ref, b2_ref, b3_ref, wgate32_ref, wbr32_ref, wo32_ref,
                       o_ref, wgate_ref, wbr_ref, wo_ref, hn_ref):
    wgate_ref[...] = wgate32_ref[...].astype(BF)
    wbr_ref[...] = wbr32_ref[...].astype(BF)
    wo_ref[...] = wo32_ref[...].astype(BF)
    _merge_kernel(x_ref, g_ref, b0_ref, b1_ref, b2_ref, b3_ref, wgate_ref, wbr_ref, wo_ref, o_ref, hn_ref)


def _merge_cast(x, g, branches, w_gate, w_branch, w_out, l):
    n, d = x.shape
    tn = 256
    once = dict(pipeline_mode=pl.Buffered(1))
    return pl.pallas_call(
        _merge_cast_kernel,
        grid=(1, d // tn),
        in_specs=[
            pl.BlockSpec((n, d), lambda i, j: (0, 0), **once),
            pl.BlockSpec((1, d), lambda i, j: (0, 0)),
        ] + [pl.BlockSpec((n, BRANCH), lambda i, j: (0, 0), **once)] * HEADS + [
            pl.BlockSpec((None, HEADS, d, tn), lambda i, j: (l, 0, 0, j)),
            pl.BlockSpec((None, HEADS, BRANCH, tn), lambda i, j: (l, 0, 0, j)),
            pl.BlockSpec((None, tn, d), lambda i, j: (l, j, 0)),
        ],
        out_specs=[
            pl.BlockSpec((n, d), lambda i, j: (0, 0), **once),
            pl.BlockSpec((HEADS, d, tn), lambda i, j: (0, 0, j)),
            pl.BlockSpec((HEADS, BRANCH, tn), lambda i, j: (0, 0, j)),
            pl.BlockSpec((tn, d), lambda i, j: (j, 0)),
        ],
        out_shape=[
            jax.ShapeDtypeStruct((n, d), F32), jax.ShapeDtypeStruct((HEADS, d, d), BF),
            jax.ShapeDtypeStruct((HEADS, BRANCH, d), BF), jax.ShapeDtypeStruct((d, d), BF),
        ],
        scratch_shapes=[pltpu.VMEM((n, d), BF)],
        compiler_params=_cparams(("arbitrary", "arbitrary")),
        name="merge_cast",
    )(x, g, *branches, w_gate, w_branch, w_out)


def _final_norm_kernel(x_ref, g_ref, o_ref):
    o_ref[...] = _rms(x_ref[...], g_ref[...])


def _final_norm(x, g):
    n, d = x.shape
    tm = _pick_tile(n, 512, 8)
    return pl.pallas_call(
        _final_norm_kernel,
        grid=(n // tm,),
        in_specs=[pl.BlockSpec((tm, d), lambda i: (i, 0)), pl.BlockSpec((1, d), lambda i: (0, 0))],
        out_specs=pl.BlockSpec((tm, d), lambda i: (i, 0)),
        out_shape=jax.ShapeDtypeStruct((n, d), F32),
        compiler_params=_cparams(("parallel",)),
        name="final_norm",
    )(x, g)


def _conv4(xs_ref, j_seq, w, tp):
    xs = xs_ref[j_seq]
    y = xs[SUBLANES:] * w[CONV_W - 1:CONV_W, :]
    for back in range(1, CONV_W):
        y = y + pltpu.roll(xs, back, axis=0)[SUBLANES:] * w[CONV_W - 1 - back:CONV_W - back, :]
    return y


class _SeqLayout:
    def __init__(self, nb, tp, ts=None):
        self.nb, self.tp, self.ts = nb, tp, ts
        self.out_dtype = BF if ts is None else F32

    def load(self, ref, sq, rows=slice(None), cols=slice(None)):
        if self.ts is None:
            return ref[sq, rows, cols]
        x = ref[sq * self.ts:(sq + 1) * self.ts, cols]
        return jnp.concatenate([x, jnp.zeros((self.tp - self.ts, x.shape[1]), x.dtype)], axis=0)

    def store(self, ref, sq, rows, cols, val):
        if self.ts is None:
            ref[sq, rows, cols] = val.astype(ref.dtype)
        else:
            ref[sq * self.ts:(sq + 1) * self.ts, cols] = val[:self.ts]

    def for_groups(self, n, body):
        if self.ts is None:
            lax.fori_loop(0, n, body, 0)
        else:
            for g in range(n):
                body(g, 0)

    def pcol(self, w, off):
        if self.ts is None:
            return pl.BlockSpec((self.nb, self.tp, w), lambda b, t: (b, t, off // w))
        return pl.BlockSpec((self.nb * self.ts, w), lambda b, t: (b, off // w))

    def dims(self, p):
        return (p.shape[0], p.shape[1]) if self.ts is None else (p.shape[0] // self.ts, self.tp)

    def out_shape(self, p, w):
        return jax.ShapeDtypeStruct(p.shape[:-1] + (w,), self.out_dtype)


def _make_lru_kernel(lay, gb, tp, t_real, is_prompt):
    nb = lay.nb

    def kern(lx_ref, ly_ref, c0_ref, h0_ref, cw_ref, cb_ref, wa_ref, ba_ref, wx_ref, bx_ref, lam_ref,
             o_ref, cn_ref, hn_ref, xs_scr, a_scr, u_scr, cc_scr, ch_scr):
        tb = pl.program_id(1)

        @pl.when(tb == 0)
        def _():
            cc_scr[...] = c0_ref[...]
            ch_scr[...] = h0_ref[...]

        cw = cw_ref[...]
        neg_sp = -LRU_C * _softplus(-lam_ref[...])
        row = lax.broadcasted_iota(jnp.int32, (tp, 1), 0)

        def per_group(g, carry):
            seqs = [g * gb + j for j in range(gb)]
            for j, sq in enumerate(seqs):
                xs_scr[j, 0:SUBLANES, :] = cc_scr[sq]
                xs_scr[j, SUBLANES:SUBLANES + tp, :] = lay.load(lx_ref, sq)
            for j, sq in enumerate(seqs):
                xc = _conv4(xs_scr, j, cw, tp) + cb_ref[...]
                cc_scr[sq] = xs_scr[j, pl.ds(t_real, SUBLANES), :]
                xcb = xc.astype(BF)
                r_parts, i_parts = [], []
                for n in range(BRANCH // LRU_BD):
                    blk = xcb[:, n * LRU_BD:(n + 1) * LRU_BD]
                    r_parts.append(_dot(blk, wa_ref[n]))
                    i_parts.append(_dot(blk, wx_ref[n]))
                r = jax.nn.sigmoid(jnp.concatenate(r_parts, axis=1) + ba_ref[...])
                ig = jax.nn.sigmoid(jnp.concatenate(i_parts, axis=1) + bx_ref[...])
                log_a = r * neg_sp
                a = jnp.exp(log_a)
                mult = jnp.sqrt(-jnp.tanh(log_a) * (a * a + 1.0))
                if is_prompt:
                    mult = jnp.where(jnp.logical_and(row == 0, tb == 0), 1.0, mult)
                a_scr[j] = a
                u_scr[j] = mult * ig * xc

            def step(t, hs):
                out = []
                for j in range(gb):
                    h = a_scr[j, pl.ds(t, 1), :] * hs[j] + u_scr[j, pl.ds(t, 1), :]
                    a_scr[j, pl.ds(t, 1), :] = h
                    out.append(h)
                return tuple(out)

            h_last = lax.fori_loop(0, t_real, step, tuple(ch_scr[sq] for sq in seqs),
                                   unroll=True if t_real <= SUBLANES else 8)
            for j, sq in enumerate(seqs):
                ch_scr[sq] = h_last[j]
                lay.store(o_ref, sq, slice(None), slice(None), a_scr[j] * _gelu_tanh(lay.load(ly_ref, sq)))
            return carry

        lay.for_groups(nb // gb, per_group)
        cn_ref[...] = cc_scr[...]
        hn_ref[...] = ch_scr[...]

    return kern


def _mixer_specs(lay, l):
    wspec = lambda shape: pl.BlockSpec((None,) + shape, lambda b, t: (l,) + (0,) * len(shape))
    state = lambda *shape: pl.BlockSpec((lay.nb,) + shape, lambda b, t: (b,) + (0,) * len(shape))
    return wspec, lay.pcol, state


class _StateIO:
    def __init__(self, s0, stacked, l, nb):
        self.s0, self.l, self.nb = s0, l, nb
        self.stacked = stacked is not None
        self.prev = stacked[1] if self.stacked else None
        self.depth = stacked[0] if self.stacked else None

    def spec(self):
        shape = self.s0.shape[2:] if self.stacked else self.s0.shape[1:]
        zeros = (0,) * len(shape)
        if self.stacked:
            l = self.l
            return pl.BlockSpec((None, self.nb) + shape, lambda b, t: (l, b) + zeros)
        return pl.BlockSpec((self.nb,) + shape, lambda b, t: (b,) + zeros)

    def out_shape(self):
        return jax.ShapeDtypeStruct(self.s0.shape, F32)

    def extra_inputs(self):
        return [self.prev] if self.prev is not None else []

    def extra_specs(self):
        return [pl.BlockSpec(memory_space=pl.ANY)] if self.prev is not None else []

    def aliases(self, n_in, out_idx):
        return {n_in: out_idx} if self.prev is not None else {}

    def wrap(self, kern, n_in):
        if self.prev is None:
            return kern
        return lambda *refs: kern(*refs[:n_in], *refs[n_in + 1:])


def _lru(p, c0, h0, lw, l, lay, gb, t_real, is_prompt):
    nb, tp = lay.nb, lay.tp
    nseq, tseq = lay.dims(p)
    c = BRANCH
    wspec, pcol, state = _mixer_specs(lay, l)
    return pl.pallas_call(
        _make_lru_kernel(lay, gb, tp, t_real, is_prompt),
        grid=(nseq // nb, tseq // tp),
        in_specs=[
            pcol(c, COL_LX), pcol(c, COL_LY), state(SUBLANES, c), state(1, c),
            wspec((CONV_W, c)), wspec((1, c)),
            wspec((c // LRU_BD, LRU_BD, LRU_BD)), wspec((1, c)),
            wspec((c // LRU_BD, LRU_BD, LRU_BD)), wspec((1, c)),
            wspec((1, c)),
        ],
        out_specs=[pcol(c, 0), state(SUBLANES, c), state(1, c)],
        out_shape=[
            lay.out_shape(p, c),
            jax.ShapeDtypeStruct((nseq, SUBLANES, c), F32),
            jax.ShapeDtypeStruct((nseq, 1, c), F32),
        ],
        scratch_shapes=[
            pltpu.VMEM((gb, tp + SUBLANES, c), F32), pltpu.VMEM((gb, tp, c), F32), pltpu.VMEM((gb, tp, c), F32),
            pltpu.VMEM((nb, SUBLANES, c), F32), pltpu.VMEM((nb, 1, c), F32),
        ],
        compiler_params=_cparams(("parallel", "arbitrary")),
        name="lru_prompt" if is_prompt else "lru_sample",
    )(p, p, c0, h0, lw["conv_w"], lw["conv_b"], lw["wa"], lw["ba"], lw["wx"], lw["bx"], lw["lam"])


def _chunk_consts(c, cl):
    row = lax.broadcasted_iota(jnp.int32, (c, c), 0)
    col = lax.broadcasted_iota(jnp.int32, (c, c), 1)
    tril = row >= col
    lower_ones = jnp.where(tril, 1.0, 0.0).astype(BF)
    real = lax.broadcasted_iota(jnp.int32, (c, 1), 0) < cl
    return row, col, tril, lower_ones, real


def _ks(h):
    return slice(h * DK, (h + 1) * DK)


def _vs(h):
    return slice(h * DV, (h + 1) * DV)


def _chunk_loop(lay, n_groups, nc, c, per_chunk):
    def body(idx, carry):
        g = idx // nc
        per_chunk(g, pl.ds(pl.multiple_of((idx - g * nc) * c, SUBLANES), c))
        return carry

    if lay.ts is None:
        lax.fori_loop(0, n_groups * nc, body, 0)
    else:
        assert nc == 1
        for g in range(n_groups):
            per_chunk(g, slice(None))


def _make_gla_kernel(lay, gb, tp, c, t_real):
    nb = lay.nb
    nc = tp // c
    cl = min(c, t_real)
    G, H = range(gb), range(HEADS)

    def kern(q_ref, k_ref, v_ref, gr_ref, sm_ref, s0_ref, wg_ref, bg_ref, ng_ref, o_ref, sn_ref, s_scr):
        @pl.when(pl.program_id(1) == 0)
        def _():
            s_scr[...] = s0_ref[...]

        _, _, tril, lower_ones, real = _chunk_consts(c, cl)
        ones_real = jnp.where(real, 1.0, 0.0).astype(BF) * jnp.ones((c, DV), BF)
        d_tn = lambda x: lax.dot_general(x, ones_real, (((0,), (0,)), ((), ())), preferred_element_type=F32)
        ng = ng_ref[...]
        wg = wg_ref[...]
        bg = bg_ref[...]

        def per_chunk(g, rows):
            seqs = [g * gb + j for j in G]
            lg = [-_softplus(-(_mm(lay.load(sm_ref, sq, rows), wg) + bg)) * (1.0 / GLA_TAU) for sq in seqs]
            sp = [_split3(x) for x in lg]
            b = [_dot(lower_ones, s[0]) + _dot(lower_ones, s[1]) + _dot(lower_ones, s[2]) for s in sp]
            b_last_col = [d_tn(s[0]) + d_tn(s[1]) + d_tn(s[2]) for s in sp]
            q_t = [lay.load(q_ref, sq, rows) * (DK ** -0.5) * jnp.exp(b[j]) for j, sq in enumerate(seqs)]
            k = [lay.load(k_ref, sq, rows) for sq in seqs]
            k_t = [jnp.where(real, k[j] * jnp.exp(-b[j]), 0.0) for j in G]
            k_d = [jnp.where(real, k[j] * jnp.exp(b[j][cl - 1:cl, :] - b[j]), 0.0) for j in G]
            v = [lay.load(v_ref, sq, rows) for sq in seqs]
            att = [[jnp.where(tril, _mm_nt(q_t[j][:, _ks(h)], k_t[j][:, _ks(h)]), 0.0) for h in H] for j in G]
            s = [[s_scr[seqs[j], h] for h in H] for j in G]
            kv = [[_mm_tn(k_d[j][:, _ks(h)], v[j][:, _vs(h)]) for h in H] for j in G]
            o = [[_mm(att[j][h], v[j][:, _vs(h)]) + _mm(q_t[j][:, _ks(h)], s[j][h]) for h in H] for j in G]
            for j in G:
                for h in H:
                    s_scr[seqs[j], h] = jnp.exp(b_last_col[j][_ks(h), :]) * s[j][h] + kv[j][h]
            for j in G:
                gr = lay.load(gr_ref, seqs[j], rows)
                for h in H:
                    lay.store(o_ref, seqs[j], rows, _vs(h), _rms(o[j][h], ng) * _silu(gr[:, _vs(h)]))

        _chunk_loop(lay, nb // gb, nc, c, per_chunk)
        sn_ref[...] = s_scr[...]

    return kern


def _gla(p, s0, lw, l, lay, gb, c, t_real, name, stacked=None):
    nb, tp = lay.nb, lay.tp
    nseq, tseq = lay.dims(p)
    qk = HEADS * DK
    vw = HEADS * DV
    wspec, pcol, _ = _mixer_specs(lay, l)
    sio = _StateIO(s0, stacked, l, nb)
    in_specs = [
        pcol(qk, COL_GQ), pcol(qk, COL_GK), pcol(vw, COL_GV), pcol(vw, COL_GR), pcol(128, COL_SM),
        sio.spec(), wspec((128, qk)), wspec((1, qk)), wspec((1, DV)),
    ]
    n_in = len(in_specs)
    return pl.pallas_call(
        sio.wrap(_make_gla_kernel(lay, gb, tp, c, t_real), n_in),
        grid=(nseq // nb, tseq // tp),
        in_specs=in_specs + sio.extra_specs(),
        out_specs=[pcol(vw, 0), sio.spec()],
        out_shape=[lay.out_shape(p, vw), sio.out_shape()],
        scratch_shapes=[pltpu.VMEM((nb, HEADS, DK, DV), F32)],
        input_output_aliases=sio.aliases(n_in, 1),
        compiler_params=_cparams(("parallel", "arbitrary")),
        name=name,
    )(p, p, p, p, p, s0, lw["gla_wg"], lw["gla_bg"], lw["gla_ng"], *sio.extra_inputs())


def _make_ret_kernel(lay, gb, tp, c, t_real):
    nb = lay.nb
    nc = tp // c
    cl = min(c, t_real)
    G, H = range(gb), range(HEADS)
    log_gamma = [float(np.log(np.float32(1.0) - np.float32(2.0) ** np.float32(-5.0 - h))) for h in range(HEADS)]

    def kern(q_ref, k_ref, qs_ref, ks_ref, v_ref, g_ref, cos_ref, sin_ref, s0_ref, ng_ref, o_ref, sn_ref, s_scr):
        @pl.when(pl.program_id(1) == 0)
        def _():
            s_scr[...] = s0_ref[...]

        row, col, tril, _, real = _chunk_consts(c, cl)
        diff = (row - col).astype(F32)
        ridx = lax.broadcasted_iota(jnp.int32, (c, 1), 0).astype(F32)
        ng = ng_ref[...]
        dmat = [jnp.where(tril, jnp.exp(jnp.maximum(diff, 0.0) * lgm), 0.0) for lgm in log_gamma]
        q_decay = [jnp.exp((ridx + 1.0) * lgm) for lgm in log_gamma]
        k_decay = [jnp.exp((cl - 1.0 - ridx) * lgm) for lgm in log_gamma]

        def per_chunk(g, rows):
            seqs = [g * gb + j for j in G]
            cos = cos_ref[rows, :]
            sin = sin_ref[rows, :]
            ld = lambda ref, sq: lay.load(ref, sq, rows)
            q = [ld(q_ref, sq) * cos + ld(qs_ref, sq) * sin for sq in seqs]
            k = [jnp.where(real, (ld(k_ref, sq) * cos + ld(ks_ref, sq) * sin) * (DK ** -0.5), 0.0) for sq in seqs]
            v = [ld(v_ref, sq) for sq in seqs]
            att = [[_mm_nt(q[j][:, _ks(h)], k[j][:, _ks(h)]) * dmat[h] for h in H] for j in G]
            s = [[s_scr[seqs[j], h] for h in H] for j in G]
            kv = [[_mm_tn(k[j][:, _ks(h)] * k_decay[h], v[j][:, _vs(h)]) for h in H] for j in G]
            o = [[_mm(att[j][h], v[j][:, _vs(h)]) + _mm(q[j][:, _ks(h)] * q_decay[h], s[j][h]) for h in H]
                 for j in G]
            for j in G:
                for h in H:
                    s_scr[seqs[j], h] = math.exp(cl * log_gamma[h]) * s[j][h] + kv[j][h]
            for j in G:
                gate = ld(g_ref, seqs[j])
                for h in H:
                    oc = o[j][h] - jnp.mean(o[j][h], axis=-1, keepdims=True)
                    on = oc * lax.rsqrt(jnp.mean(oc * oc, axis=-1, keepdims=True) + EPS) * ng
                    lay.store(o_ref, seqs[j], rows, _vs(h), on * _silu(gate[:, _vs(h)]))

        _chunk_loop(lay, nb // gb, nc, c, per_chunk)
        sn_ref[...] = s_scr[...]

    return kern


def _ret(p, s0, cos_t, sin_t, lw, l, lay, gb, c, t_real, name, stacked=None):
    nb, tp = lay.nb, lay.tp
    nseq, tseq = lay.dims(p)
    qk = HEADS * DK
    vw = HEADS * DV
    wspec, pcol, _ = _mixer_specs(lay, l)
    sio = _StateIO(s0, stacked, l, nb)
    in_specs = [
        pcol(qk, COL_RQ), pcol(qk, COL_RK), pcol(qk, COL_RQS), pcol(qk, COL_RKS), pcol(vw, COL_RV),
        pcol(vw, COL_RG),
        pl.BlockSpec((tp, qk), lambda b, t: (t, 0)),
        pl.BlockSpec((tp, qk), lambda b, t: (t, 0)),
        sio.spec(), wspec((1, DV)),
    ]
    n_in = len(in_specs)
    return pl.pallas_call(
        sio.wrap(_make_ret_kernel(lay, gb, tp, c, t_real), n_in),
        grid=(nseq // nb, tseq // tp),
        in_specs=in_specs + sio.extra_specs(),
        out_specs=[pcol(vw, 0), sio.spec()],
        out_shape=[lay.out_shape(p, vw), sio.out_shape()],
        scratch_shapes=[pltpu.VMEM((nb, HEADS, DK, DV), F32)],
        input_output_aliases=sio.aliases(n_in, 1),
        compiler_params=_cparams(("parallel", "arbitrary")),
        name=name,
    )(p, p, p, p, p, p, cos_t, sin_t, s0, lw["ret_ng"], *sio.extra_inputs())


def _unit_lower_inverse_many(ms, c):
    row = lax.broadcasted_iota(jnp.int32, (c, c), 0)
    col = lax.broadcasted_iota(jnp.int32, (c, c), 1)
    eye = jnp.where(row == col, 1.0, 0.0)

    def same_block(bits):
        return lax.shift_right_logical(row, bits) == lax.shift_right_logical(col, bits)

    in8 = same_block(3)
    n1 = [jnp.where(in8, m, 0.0) for m in ms]
    n2 = [_mm_hp(x, x) for x in n1]
    n4 = [_mm_hp(x, x) for x in n2]
    d = [eye - x for x in n1]
    d = [x + _mm_hp(x, y) for x, y in zip(d, n2)]
    d = [x + _mm_hp(x, y) for x, y in zip(d, n4)]
    bits = 3
    while (1 << bits) < c:
        sel = jnp.logical_and(same_block(bits + 1), jnp.logical_not(same_block(bits)))
        ld = [_mm(jnp.where(sel, m, 0.0), x) for m, x in zip(ms, d)]
        d = [x - _mm(x, y) for x, y in zip(d, ld)]
        bits += 1
    return d


def _make_dn_kernel(lay, gb, tp, c, t_real):
    nb = lay.nb
    nc = tp // c
    cl = min(c, t_real)
    G, H = range(gb), range(HEADS)
    GH = [(j, h) for j in G for h in H]

    def kern(x_ref, z_ref, sm_ref, c0_ref, s0_ref, cw_ref, alog_ref, dtb_ref, ng_ref,
             o_ref, cn_ref, sn_ref, xs_scr, qkv_scr, cc_scr, s_scr):
        @pl.when(pl.program_id(1) == 0)
        def _():
            cc_scr[...] = c0_ref[...]
            s_scr[...] = s0_ref[...]

        row, col, tril, lower_ones, real = _chunk_consts(c, cl)
        strict = row > col
        lane = lax.broadcasted_iota(jnp.int32, (c, 128), 1)
        sel = [jnp.where(lane == SM_DA + h, 1.0, 0.0).astype(BF) for h in H]
        d_nt = lambda a, x: lax.dot_general(a, x, (((1,), (1,)), ((), ())), preferred_element_type=F32)
        ng = ng_ref[...]
        cw = cw_ref[...]
        neg_a = -jnp.exp(alog_ref[...])
        dtb = dtb_ref[...]

        def per_group(g, carry):
            seqs = [g * gb + j for j in G]
            for j, sq in enumerate(seqs):
                xs_scr[j, 0:SUBLANES, :] = cc_scr[sq]
                xs_scr[j, SUBLANES:SUBLANES + tp, :] = lay.load(x_ref, sq)
            for j, sq in enumerate(seqs):
                qkv_scr[j] = _silu(_conv4(xs_scr, j, cw, tp))
                cc_scr[sq] = xs_scr[j, pl.ds(t_real, SUBLANES), :]

            def per_chunk(ci, carry2):
                rows = pl.ds(pl.multiple_of(ci * c, SUBLANES), c)
                sm = [lay.load(sm_ref, sq, rows) for sq in seqs]
                gcum = [_cumsum_rows(neg_a * _softplus(x + dtb), lower_ones) for x in sm]
                g3 = [_split3(x) for x in gcum]
                beta_all = [jnp.where(real, jax.nn.sigmoid(x), 0.0) for x in sm]

                def head_in(j, h, part):
                    x = qkv_scr[j, rows, part * BRANCH + h * DV:part * BRANCH + (h + 1) * DV]
                    return x

                qh = {jh: head_in(*jh, 0) for jh in GH}
                kh = {jh: head_in(*jh, 1) for jh in GH}
                vh = {jh: head_in(*jh, 2) for jh in GH}
                qh = {jh: x * lax.rsqrt(jnp.sum(x * x, axis=-1, keepdims=True) + EPS) * (DV ** -0.5)
                      for jh, x in qh.items()}
                kh = {jh: jnp.where(real, x * lax.rsqrt(jnp.sum(x * x, axis=-1, keepdims=True) + EPS), 0.0)
                      for jh, x in kh.items()}
                beta = {(j, h): beta_all[j][:, SM_DB + h:SM_DB + h + 1] for j, h in GH}
                gcol = {(j, h): gcum[j][:, SM_DA + h:SM_DA + h + 1] for j, h in GH}
                grow = {(j, h): d_nt(sel[h], g3[j][0]) + d_nt(sel[h], g3[j][1]) + d_nt(sel[h], g3[j][2])
                        for j, h in GH}
                decay = {jh: jnp.where(tril, jnp.exp(jnp.where(tril, gcol[jh] - grow[jh], 0.0)), 0.0) for jh in GH}
                e_g = {jh: jnp.exp(gcol[jh]) for jh in GH}
                kb = {jh: kh[jh] * beta[jh] for jh in GH}
                kq = {jh: _mm_nt(jnp.concatenate([kb[jh], qh[jh]], axis=0), kh[jh]) for jh in GH}
                tm = _unit_lower_inverse_many([jnp.where(strict, kq[jh][:c] * decay[jh], 0.0) for jh in GH], c)
                tm = dict(zip(GH, tm))
                uw = {jh: _mm(tm[jh], jnp.concatenate([vh[jh] * beta[jh], kb[jh] * e_g[jh]], axis=1)) for jh in GH}
                s = {(j, h): s_scr[seqs[j], h] for j, h in GH}
                wq = {jh: _mm(jnp.concatenate([uw[jh][:, DV:], qh[jh] * e_g[jh]], axis=0), s[jh]) for jh in GH}
                v_new = {jh: uw[jh][:, :DV] - wq[jh][:c] for jh in GH}
                o = {jh: wq[jh][c:] + _mm(kq[jh][c:] * decay[jh], v_new[jh]) for jh in GH}
                for j, h in GH:
                    g_last = gcol[(j, h)][cl - 1:cl, :]
                    k_d = kh[(j, h)] * jnp.exp(g_last - gcol[(j, h)])
                    s_scr[seqs[j], h] = jnp.exp(g_last) * s[(j, h)] + _mm_tn(k_d, v_new[(j, h)])
                for j in G:
                    z = lay.load(z_ref, seqs[j], rows)
                    for h in H:
                        lay.store(o_ref, seqs[j], rows, _vs(h), _rms(o[(j, h)], ng) * _silu(z[:, _vs(h)]))
                return carry2

            lax.fori_loop(0, nc, per_chunk, 0)
            return carry

        lay.for_groups(nb // gb, per_group)
        cn_ref[...] = cc_scr[...]
        sn_ref[...] = s_scr[...]

    return kern


def _dn(p, c0, s0, lw, l, lay, gb, c, t_real, name, stacked=None):
    nb, tp = lay.nb, lay.tp
    nseq, tseq = lay.dims(p)
    cw3 = 3 * BRANCH
    wspec, pcol, state = _mixer_specs(lay, l)
    sio = _StateIO(s0, stacked, l, nb)
    in_specs = [
        pcol(cw3, COL_DQKV), pcol(BRANCH, COL_DZ), pcol(128, COL_SM),
        state(SUBLANES, cw3), sio.spec(),
        wspec((CONV_W, cw3)), wspec((1, 128)), wspec((1, 128)), wspec((1, DV)),
    ]
    n_in = len(in_specs)
    return pl.pallas_call(
        sio.wrap(_make_dn_kernel(lay, gb, tp, c, t_real), n_in),
        grid=(nseq // nb, tseq // tp),
        in_specs=in_specs + sio.extra_specs(),
        out_specs=[pcol(BRANCH, 0), state(SUBLANES, cw3), sio.spec()],
        out_shape=[
            lay.out_shape(p, BRANCH),
            jax.ShapeDtypeStruct((nseq, SUBLANES, cw3), F32),
            sio.out_shape(),
        ],
        scratch_shapes=[
            pltpu.VMEM((gb, tp + SUBLANES, cw3), F32), pltpu.VMEM((gb, tp, cw3), F32),
            pltpu.VMEM((nb, SUBLANES, cw3), F32), pltpu.VMEM((nb, HEADS, DV, DV), F32),
        ],
        input_output_aliases=sio.aliases(n_in, 2),
        compiler_params=_cparams(("parallel", "arbitrary")),
        name=name,
    )(p, p, p, c0, s0, lw["dn_conv_w"], lw["dn_alog"], lw["dn_dtb"], lw["dn_ng"], *sio.extra_inputs())


IN_SIZES = (BRANCH, BRANCH, HEADS * DK, HEADS * DK, BRANCH, BRANCH, GLA_RANK,
            HEADS * DK, HEADS * DK, BRANCH, BRANCH, BRANCH, BRANCH, BRANCH, BRANCH, HEADS, HEADS)


IN_OFFS = tuple(int(v) for v in np.concatenate([[0], np.cumsum(IN_SIZES)]))
PACK_ROWS = 256


def _pack_src_row(r):
    t_rq, t_swap, t_rv, t_small = (c // PACK_ROWS for c in (COL_RQ, COL_RQS, COL_RV, COL_SM))
    o_rq, o_rv = IN_OFFS[7], IN_OFFS[9]
    row = jnp.where(r < t_rq, r * PACK_ROWS,
                    jnp.where(r < t_swap, o_rq + (r - t_rq) * PACK_ROWS,
                              jnp.where(r < t_rv, o_rq + (r - t_swap) * PACK_ROWS,
                                        jnp.where(r < t_small, o_rv + (r - t_rv) * PACK_ROWS, 0))))
    return pl.multiple_of(row, 2 * SUBLANES)


def _pack_kernel(w_ref, glr_ref, dba_ref, o_ref):
    r = pl.program_id(1)
    t_swap, t_rv, t_small = (c // PACK_ROWS for c in (COL_RQS, COL_RV, COL_SM))
    w = w_ref[0]

    @pl.when(jnp.logical_and(jnp.logical_or(r < t_swap, r >= t_rv), r < t_small))
    def _():
        o_ref[...] = w.astype(BF)

    @pl.when(jnp.logical_and(r >= t_swap, r < t_rv))
    def _():
        parts = []
        for h in range(PACK_ROWS // DK):
            parts += [w[h * DK + DK // 2:(h + 1) * DK], w[h * DK:h * DK + DK // 2]]
        o_ref[...] = jnp.concatenate(parts, axis=0).astype(BF)

    @pl.when(r == t_small)
    def _():
        pad = jnp.zeros((PACK_ROWS - GLA_RANK - 2 * HEADS, w.shape[1]), w.dtype)
        o_ref[...] = jnp.concatenate([glr_ref[0], dba_ref[0], pad], axis=0).astype(BF)


def _pack_w_in(w_in_t):
    depth, n_in, d = w_in_t.shape
    assert n_in == IN_OFFS[-1] and N_PACK % PACK_ROWS == 0 and COL_SM // PACK_ROWS == N_PACK // PACK_ROWS - 1
    rows = lambda n, start: pl.BlockSpec((pl.Element(1), pl.Element(n), pl.Element(d)),
                                         lambda l, r: (l, start(r), 0))
    return pl.pallas_call(
        _pack_kernel,
        grid=(depth, N_PACK // PACK_ROWS),
        in_specs=[rows(PACK_ROWS, _pack_src_row), rows(GLA_RANK, lambda r: IN_OFFS[6]),
                  rows(2 * HEADS, lambda r: IN_OFFS[15])],
        out_specs=pl.BlockSpec((None, PACK_ROWS, d), lambda l, r: (l, r, 0)),
        out_shape=jax.ShapeDtypeStruct((depth, N_PACK, d), BF),
        compiler_params=_cparams(("parallel", "parallel")),
        name="pack_w_in",
    )(w_in_t, w_in_t, w_in_t)


def _rope_tables(pos):
    half = DK // 2
    inv = ROPE_BASE ** (-jnp.arange(half, dtype=F32) / half)
    ang = pos.astype(F32)[:, None] * inv[None, :]
    cos, sin = jnp.cos(ang), jnp.sin(ang)
    cos_t = jnp.tile(jnp.concatenate([cos, cos], axis=1), (1, HEADS))
    sin_t = jnp.tile(jnp.concatenate([-sin, sin], axis=1), (1, HEADS))
    return cos_t, sin_t


def _lane_row(vals, off):
    depth = vals.shape[0]
    return jnp.zeros((depth, 1, 128), F32).at[:, 0, off:off + HEADS].set(vals.astype(F32))


def kernel(x_prompt, x_sample, state_lru_conv, state_lru_h, state_gla, state_ret, state_dn_conv, state_dn,
           norm_g, final_norm_g, w_ff_in, w_ff_out, w_in, w_gate, w_branch, w_out,
           lru_conv_w, lru_conv_b, lru_wa, lru_ba, lru_wx, lru_bx, lru_lambda,
           gla_wg, gla_bg, gla_norm_g, ret_norm_g, dn_conv_w, dn_a_log, dn_dt_bias, dn_norm_g):
    bp, tpr, d = x_prompt.shape
    bs, ts, _ = x_sample.shape
    depth = w_in.shape[0]
    n_p, n_s = bp * tpr, bs * ts
    assert d == D_MODEL and ts <= SAMPLE_PAD and tpr % CHUNK == 0

    w_pack = _pack_w_in(jnp.swapaxes(w_in, 1, 2))
    wg_pad = jnp.zeros((depth, 128, HEADS * DK), F32).at[:, SM_GLR:SM_GLR + GLA_RANK, :].set(gla_wg).astype(BF)
    r3 = lambda a: a.reshape(a.shape[0], 1, a.shape[1])
    lw = {
        "conv_w": lru_conv_w, "conv_b": r3(lru_conv_b), "wa": lru_wa.astype(BF), "ba": r3(lru_ba),
        "wx": lru_wx.astype(BF), "bx": r3(lru_bx), "lam": r3(lru_lambda),
        "gla_wg": wg_pad, "gla_bg": r3(gla_bg), "gla_ng": r3(gla_norm_g), "ret_ng": r3(ret_norm_g),
        "dn_conv_w": dn_conv_w, "dn_alog": _lane_row(dn_a_log, SM_DA), "dn_dtb": _lane_row(dn_dt_bias, SM_DA),
        "dn_ng": r3(dn_norm_g),
    }
    norm_g4 = norm_g.reshape(depth, 3, 1, d)

    cos_p, sin_p = _rope_tables(jnp.arange(tpr))
    pos_s = jnp.where(jnp.arange(SAMPLE_PAD) < ts, PAST_LEN + jnp.arange(SAMPLE_PAD), 0)
    cos_s, sin_s = _rope_tables(pos_s)

    pad_conv = lambda a: jnp.pad(a, ((0, 0), (SUBLANES - (CONV_W - 1), 0), (0, 0)))
    zeros = lambda *s: jnp.zeros(s, F32)

    tp_p = _pick_tile(tpr, 256, CHUNK)
    a_p = dict(lay=_SeqLayout(bp, tp_p), gb=bp)
    nb_s = _pick_tile(bs, 8, 2)
    assert (nb_s * ts) % SUBLANES == 0
    a_s = dict(lay=_SeqLayout(nb_s, SAMPLE_PAD, ts), gb=_pick_tile(nb_s, 4, 1))

    fg = final_norm_g.reshape(1, d)
    xp = x_prompt.reshape(n_p, d)
    xs = x_sample.reshape(n_s, d)
    new_p = [[] for _ in range(6)]
    new_s = [[] for _ in range(6)]
    s_gla_s = s_ret_s = s_dn_s = None
    for l in range(depth):
        xs, *w_ffn = _ffn_cast(xs, norm_g4[l, 0], w_ff_in, w_ff_out, l, 0)
        xp = _ffn(xp, norm_g4[l, 0], *w_ffn)
        p_p, hn_p = _inproj(xp, norm_g4[l, 1], w_pack, l)
        p_p = p_p.reshape(bp, tpr, N_PACK)
        p_s, _ = _inproj(xs, norm_g4[l, 1], w_pack, l)

        o_lru_p, c_lru_p, h_lru_p = _lru(p_p, zeros(bp, SUBLANES, BRANCH), zeros(bp, 1, BRANCH), lw, l,
                                         t_real=tp_p, is_prompt=True, **a_p)
        o_gla_p, s_gla_p = _gla(p_p, zeros(bp, HEADS, DK, DV), lw, l, c=CHUNK, t_real=tp_p, name="gla_prompt", **a_p)
        o_ret_p, s_ret_p = _ret(p_p, zeros(bp, HEADS, DK, DV), cos_p, sin_p, lw, l, c=CHUNK, t_real=tp_p,
                                name="ret_prompt", **a_p)
        o_dn_p, c_dn_p, s_dn_p = _dn(p_p, zeros(bp, SUBLANES, 3 * BRANCH), zeros(bp, HEADS, DV, DV), lw, l,
                                     c=CHUNK, t_real=tp_p, name="dn_prompt", **a_p)

        o_lru_s, c_lru_s, h_lru_s = _lru(p_s, pad_conv(state_lru_conv[l]), state_lru_h[l][:, None, :], lw, l,
                                         t_real=ts, is_prompt=False, **a_s)
        o_gla_s, s_gla_s = _gla(p_s, state_gla, lw, l, c=SAMPLE_PAD, t_real=ts, name="gla_sample",
                                stacked=(depth, s_gla_s), **a_s)
        o_ret_s, s_ret_s = _ret(p_s, state_ret, cos_s, sin_s, lw, l, c=SAMPLE_PAD, t_real=ts, name="ret_sample",
                                stacked=(depth, s_ret_s), **a_s)
        o_dn_s, c_dn_s, s_dn_s = _dn(p_s, pad_conv(state_dn_conv[l]), state_dn, lw, l, c=SAMPLE_PAD, t_real=ts,
                                     name="dn_sample", stacked=(depth, s_dn_s), **a_s)

        flat_p = lambda o: o.reshape(n_p, BRANCH)
        xs, *w_merge = _merge_cast(xs, norm_g4[l, 1], [o_lru_s, o_gla_s, o_ret_s, o_dn_s], w_gate, w_branch, w_out, l)
        xp = _merge_split(xp, hn_p, [flat_p(o) for o in (o_lru_p, o_gla_p, o_ret_p, o_dn_p)], *w_merge)
        xs, *w_ffn = _ffn_cast(xs, norm_g4[l, 2], w_ff_in, w_ff_out, l, 1)
        xp = _ffn(xp, norm_g4[l, 2], *w_ffn, final_g=fg if l == depth - 1 else None)

        tail3 = lambda cwin: cwin[:, SUBLANES - (CONV_W - 1):, :]
        for i, v in enumerate((tail3(c_lru_p), h_lru_p[:, 0], s_gla_p, s_ret_p, tail3(c_dn_p), s_dn_p)):
            new_p[i].append(v)
        for i, v in ((0, tail3(c_lru_s)), (1, h_lru_s[:, 0]), (4, tail3(c_dn_s))):
            new_s[i].append(v)

    y_prompt = xp.reshape(bp, tpr, d)
    y_sample = _final_norm(xs, fg).reshape(bs, ts, d)
    sp = [jnp.stack(v) for v in new_p]
    s_small = {i: jnp.stack(new_s[i]) for i in (0, 1, 4)}
    return tuple([y_prompt, y_sample] + sp + [s_small[0], s_small[1], s_gla_s, s_ret_s, s_small[4], s_dn_s])
```

```python
import math

import numpy as np
import jax
import jax.numpy as jnp
from jax import lax
from jax.experimental import pallas as pl
from jax.experimental.pallas import tpu as pltpu

F32 = jnp.float32
BF = jnp.bfloat16
EPS = 1e-6

D_MODEL = 2048
BRANCH = 512
CONV_W = 4
HEADS = 4
DK = 64
DV = 128
GLA_RANK = 16
GLA_TAU = 16.0
LRU_C = 8.0
LRU_BD = 128
ROPE_BASE = 10000.0
CHUNK = 64
PAST_LEN = 16384

V7X_VMEM_BYTES = 64 * 1024 * 1024
VMEM_LIMIT = V7X_VMEM_BYTES - 8 * 1024 * 1024
SUBLANES = 8
BF16_ROWS = 16
LANES = 128
ROW_TILE = 1024
ROW_TILE_SMALL = 512
FFN_COL_TILE = 512
PROJ_COL_TILE = 1024
GATE_COL_TILE = 512
MERGE_COL_TILE = 256
MIXER_TIME_TILE = 256
SAMPLE_SEQS = 16
LOCKSTEP_SEQS = 4
SAMPLE_PAD = 8

COL_LX, COL_LY = 0, 512
COL_GQ, COL_GK, COL_GV, COL_GR = 1024, 1280, 1536, 2048
COL_RQ, COL_RK, COL_RQS, COL_RKS, COL_RV, COL_RG = 2560, 2816, 3072, 3328, 3584, 4096
COL_DQKV, COL_DZ = 4608, 6144
COL_SM = 6656
SM_GLR, SM_DB, SM_DA = 0, 16, 20
N_PACK = 6912


def _dot(a, b):
    return jnp.dot(a, b, preferred_element_type=F32)


def _mm(a, b):
    return _dot(a.astype(BF), b.astype(BF))


def _mm_nt(a, b):
    return lax.dot_general(a.astype(BF), b.astype(BF), (((1,), (1,)), ((), ())), preferred_element_type=F32)


def _mm_tn(a, b):
    return lax.dot_general(a.astype(BF), b.astype(BF), (((0,), (0,)), ((), ())), preferred_element_type=F32)


def _split3(x):
    hi = x.astype(BF)
    r = x - hi.astype(F32)
    mid = r.astype(BF)
    lo = (r - mid.astype(F32)).astype(BF)
    return hi, mid, lo


def _split2(x):
    hi = x.astype(BF)
    return hi, (x - hi.astype(F32)).astype(BF)


def _mm_hp(a, b):
    ah, al = _split2(a)
    bh, bl = _split2(b)
    return _dot(ah, bh) + _dot(ah, bl) + _dot(al, bh)


def _cumsum_rows(x, lower_ones):
    hi, mid, lo = _split3(x)
    return _dot(lower_ones, hi) + _dot(lower_ones, mid) + _dot(lower_ones, lo)


def _softplus(x):
    return jnp.maximum(x, 0.0) + jnp.log1p(jnp.exp(-jnp.abs(x)))


def _silu(x):
    return x * jax.nn.sigmoid(x)


def _gelu_tanh(x):
    return x * (0.5 * (1.0 + jnp.tanh(math.sqrt(2.0 / math.pi) * (x + 0.044715 * (x * x * x)))))


def _rms(x, g):
    return x * lax.rsqrt(jnp.mean(x * x, axis=-1, keepdims=True) + EPS) * g


def _pick_tile(n, target, align):
    best = None
    for t in range(align, min(n, target) + 1, align):
        if n % t == 0:
            best = t
    assert best is not None, (n, target, align)
    return best


def _cparams(sem):
    return pltpu.CompilerParams(dimension_semantics=sem, vmem_limit_bytes=VMEM_LIMIT)


def _ffn_kernel(x_ref, g_ref, wg_ref, wu_ref, wo_ref, o_ref, hn_ref):
    @pl.when(pl.program_id(1) == 0)
    def _():
        x = x_ref[...]
        hn_ref[...] = _rms(x, g_ref[...]).astype(BF)
        o_ref[...] = x

    hn = hn_ref[...]
    gate = _dot(hn, wg_ref[...])
    up = _dot(hn, wu_ref[...])
    act = (_silu(gate) * up * 0.5).astype(BF)
    o_ref[...] += _dot(act, wo_ref[...])


def _ffn_final_kernel(x_ref, g_ref, wg_ref, wu_ref, wo_ref, fg_ref, o_ref, hn_ref):
    _ffn_kernel(x_ref, g_ref, wg_ref, wu_ref, wo_ref, o_ref, hn_ref)

    @pl.when(pl.program_id(1) == pl.num_programs(1) - 1)
    def _():
        o_ref[...] = _rms(o_ref[...], fg_ref[...])


def _ffn(x, g, wg, wu, wo, final_g=None):
    n, d = x.shape
    f = wo.shape[0]
    tm = _pick_tile(n, ROW_TILE, BF16_ROWS)
    tf = _pick_tile(f, FFN_COL_TILE, LANES)
    final = [] if final_g is None else [final_g]
    return pl.pallas_call(
        _ffn_kernel if final_g is None else _ffn_final_kernel,
        grid=(n // tm, f // tf),
        in_specs=[
            pl.BlockSpec((tm, d), lambda i, j: (i, 0)),
            pl.BlockSpec((1, d), lambda i, j: (0, 0)),
            pl.BlockSpec((d, tf), lambda i, j: (0, j)),
            pl.BlockSpec((d, tf), lambda i, j: (0, j)),
            pl.BlockSpec((tf, d), lambda i, j: (j, 0)),
        ] + [pl.BlockSpec((1, d), lambda i, j: (0, 0))] * len(final),
        out_specs=pl.BlockSpec((tm, d), lambda i, j: (i, 0)),
        out_shape=jax.ShapeDtypeStruct((n, d), F32),
        scratch_shapes=[pltpu.VMEM((tm, d), BF)],
        compiler_params=_cparams(("parallel", "arbitrary")),
        name="ffn",
    )(x, g, wg, wu, wo, *final)


def _ffn_cast_kernel(x_ref, g_ref, wg32_ref, wu32_ref, wo32_ref, o_ref, wg_ref, wu_ref, wo_ref, hn_ref):
    wg_ref[...] = wg32_ref[...].astype(BF)
    wu_ref[...] = wu32_ref[...].astype(BF)
    wo_ref[...] = wo32_ref[...].astype(BF)
    _ffn_kernel(x_ref, g_ref, wg_ref, wu_ref, wo_ref, o_ref, hn_ref)


def _ffn_cast(x, g, w_in, w_out, l, s):
    n, d = x.shape
    f = w_out.shape[2]
    tf = _pick_tile(f, FFN_COL_TILE, LANES)
    nf = f // tf
    once = dict(pipeline_mode=pl.Buffered(1))
    return pl.pallas_call(
        _ffn_cast_kernel,
        grid=(1, nf),
        in_specs=[
            pl.BlockSpec((n, d), lambda i, j: (0, 0), **once),
            pl.BlockSpec((1, d), lambda i, j: (0, 0)),
            pl.BlockSpec((None, None, d, tf), lambda i, j: (l, s, 0, j)),
            pl.BlockSpec((None, None, d, tf), lambda i, j: (l, s, 0, j + nf)),
            pl.BlockSpec((None, None, tf, d), lambda i, j: (l, s, j, 0)),
        ],
        out_specs=[
            pl.BlockSpec((n, d), lambda i, j: (0, 0), **once),
            pl.BlockSpec((d, tf), lambda i, j: (0, j)),
            pl.BlockSpec((d, tf), lambda i, j: (0, j)),
            pl.BlockSpec((tf, d), lambda i, j: (j, 0)),
        ],
        out_shape=[
            jax.ShapeDtypeStruct((n, d), F32),
            jax.ShapeDtypeStruct((d, f), BF), jax.ShapeDtypeStruct((d, f), BF), jax.ShapeDtypeStruct((f, d), BF),
        ],
        scratch_shapes=[pltpu.VMEM((n, d), BF)],
        compiler_params=_cparams(("arbitrary", "arbitrary")),
        name="ffn_cast",
    )(x, g, w_in, w_in, w_out)


def _inproj_kernel(x_ref, g_ref, w_ref, o_ref, hn_ref):
    @pl.when(pl.program_id(1) == 0)
    def _():
        hn_ref[...] = _rms(x_ref[...], g_ref[...]).astype(BF)

    o_ref[...] = _mm_nt(hn_ref[...], w_ref[...])


def _inproj(x, g, w_pack, l):
    n, d = x.shape
    npk = w_pack.shape[1]
    tm = _pick_tile(n, ROW_TILE, BF16_ROWS)
    tn = _pick_tile(npk, PROJ_COL_TILE, LANES)
    return pl.pallas_call(
        _inproj_kernel,
        grid=(n // tm, npk // tn),
        in_specs=[
            pl.BlockSpec((tm, d), lambda i, j: (i, 0)),
            pl.BlockSpec((1, d), lambda i, j: (0, 0)),
            pl.BlockSpec((None, tn, d), lambda i, j: (l, j, 0)),
        ],
        out_specs=[pl.BlockSpec((tm, tn), lambda i, j: (i, j)), pl.BlockSpec((tm, d), lambda i, j: (i, 0))],
        out_shape=[jax.ShapeDtypeStruct((n, npk), F32), jax.ShapeDtypeStruct((n, d), BF)],
        compiler_params=_cparams(("parallel", "arbitrary")),
        name="inproj",
    )(x, g, w_pack)


def _merge_kernel(x_ref, g_ref, b0_ref, b1_ref, b2_ref, b3_ref, wgate_ref, wbr_ref, wo_ref, o_ref, hn_ref):
    @pl.when(pl.program_id(1) == 0)
    def _():
        x = x_ref[...]
        hn_ref[...] = _rms(x, g_ref[...]).astype(BF)
        o_ref[...] = x

    hn = hn_ref[...]
    m = None
    for n, b_ref in enumerate((b0_ref, b1_ref, b2_ref, b3_ref)):
        gate = jax.nn.sigmoid(_dot(hn, wgate_ref[n]))
        br = _dot(b_ref[...].astype(BF), wbr_ref[n])
        m = gate * br if m is None else m + gate * br
    o_ref[...] += _dot(m.astype(BF), wo_ref[...])


def _gate_kernel(hn_ref, b0_ref, b1_ref, b2_ref, b3_ref, wgate_ref, wbr_ref, m_ref):
    hn = hn_ref[...]
    m = None
    for n, b_ref in enumerate((b0_ref, b1_ref, b2_ref, b3_ref)):
        gate = jax.nn.sigmoid(_dot(hn, wgate_ref[n]))
        br = _dot(b_ref[...].astype(BF), wbr_ref[n])
        m = gate * br if m is None else m + gate * br
    m_ref[...] = m.astype(BF)


def _outproj_kernel(m_ref, w_ref, x_ref, o_ref):
    o_ref[...] = x_ref[...] + _dot(m_ref[...], w_ref[...])


def _merge_split(x, hn, branches, w_gate_l, w_branch_l, w_out_l):
    n, d = x.shape
    tm = _pick_tile(n, ROW_TILE, BF16_ROWS)
    tn = GATE_COL_TILE
    m = pl.pallas_call(
        _gate_kernel,
        grid=(n // tm, d // tn),
        in_specs=[pl.BlockSpec((tm, d), lambda i, j: (i, 0))]
        + [pl.BlockSpec((tm, BRANCH), lambda i, j: (i, 0))] * HEADS + [
            pl.BlockSpec((HEADS, d, tn), lambda i, j: (0, 0, j)),
            pl.BlockSpec((HEADS, BRANCH, tn), lambda i, j: (0, 0, j)),
        ],
        out_specs=pl.BlockSpec((tm, tn), lambda i, j: (i, j)),
        out_shape=jax.ShapeDtypeStruct((n, d), BF),
        compiler_params=_cparams(("parallel", "arbitrary")),
        name="gate",
    )(hn, *branches, w_gate_l, w_branch_l)
    to = _pick_tile(n, ROW_TILE_SMALL, BF16_ROWS)
    return pl.pallas_call(
        _outproj_kernel,
        grid=(n // to,),
        in_specs=[
            pl.BlockSpec((to, d), lambda i: (i, 0)),
            pl.BlockSpec((d, d), lambda i: (0, 0), pipeline_mode=pl.Buffered(1)),
            pl.BlockSpec((to, d), lambda i: (i, 0)),
        ],
        out_specs=pl.BlockSpec((to, d), lambda i: (i, 0)),
        out_shape=jax.ShapeDtypeStruct((n, d), F32),
        compiler_params=_cparams(("parallel",)),
        name="outproj",
    )(m, w_out_l, x)


def _merge_cast_kernel(x_ref, g_ref, b0_ref, b1_ref, b2_ref, b3_ref, wgate32_ref, wbr32_ref, wo32_ref,
                       o_ref, wgate_ref, wbr_ref, wo_ref, hn_ref):
    wgate_ref[...] = wgate32_ref[...].astype(BF)
    wbr_ref[...] = wbr32_ref[...].astype(BF)
    wo_ref[...] = wo32_ref[...].astype(BF)
    _merge_kernel(x_ref, g_ref, b0_ref, b1_ref, b2_ref, b3_ref, wgate_ref, wbr_ref, wo_ref, o_ref, hn_ref)


def _merge_cast(x, g, branches, w_gate, w_branch, w_out, l):
    n, d = x.shape
    tn = MERGE_COL_TILE
    once = dict(pipeline_mode=pl.Buffered(1))
    return pl.pallas_call(
        _merge_cast_kernel,
        grid=(1, d // tn),
        in_specs=[
            pl.BlockSpec((n, d), lambda i, j: (0, 0), **once),
            pl.BlockSpec((1, d), lambda i, j: (0, 0)),
        ] + [pl.BlockSpec((n, BRANCH), lambda i, j: (0, 0), **once)] * HEADS + [
            pl.BlockSpec((None, HEADS, d, tn), lambda i, j: (l, 0, 0, j)),
            pl.BlockSpec((None, HEADS, BRANCH, tn), lambda i, j: (l, 0, 0, j)),
            pl.BlockSpec((None, tn, d), lambda i, j: (l, j, 0)),
        ],
        out_specs=[
            pl.BlockSpec((n, d), lambda i, j: (0, 0), **once),
            pl.BlockSpec((HEADS, d, tn), lambda i, j: (0, 0, j)),
            pl.BlockSpec((HEADS, BRANCH, tn), lambda i, j: (0, 0, j)),
            pl.BlockSpec((tn, d), lambda i, j: (j, 0)),
        ],
        out_shape=[
            jax.ShapeDtypeStruct((n, d), F32), jax.ShapeDtypeStruct((HEADS, d, d), BF),
            jax.ShapeDtypeStruct((HEADS, BRANCH, d), BF), jax.ShapeDtypeStruct((d, d), BF),
        ],
        scratch_shapes=[pltpu.VMEM((n, d), BF)],
        compiler_params=_cparams(("arbitrary", "arbitrary")),
        name="merge_cast",
    )(x, g, *branches, w_gate, w_branch, w_out)


def _final_norm_kernel(x_ref, g_ref, o_ref):
    o_ref[...] = _rms(x_ref[...], g_ref[...])


def _final_norm(x, g):
    n, d = x.shape
    tm = _pick_tile(n, ROW_TILE_SMALL, SUBLANES)
    return pl.pallas_call(
        _final_norm_kernel,
        grid=(n // tm,),
        in_specs=[pl.BlockSpec((tm, d), lambda i: (i, 0)), pl.BlockSpec((1, d), lambda i: (0, 0))],
        out_specs=pl.BlockSpec((tm, d), lambda i: (i, 0)),
        out_shape=jax.ShapeDtypeStruct((n, d), F32),
        compiler_params=_cparams(("parallel",)),
        name="final_norm",
    )(x, g)


def _conv4(xs_ref, j_seq, w, tp):
    xs = xs_ref[j_seq]
    y = xs[SUBLANES:] * w[CONV_W - 1:CONV_W, :]
    for back in range(1, CONV_W):
        y = y + pltpu.roll(xs, back, axis=0)[SUBLANES:] * w[CONV_W - 1 - back:CONV_W - back, :]
    return y


class _SeqLayout:
    def __init__(self, nb, tp, ts=None):
        self.nb, self.tp, self.ts = nb, tp, ts
        self.out_dtype = BF if ts is None else F32

    def load(self, ref, sq, rows=slice(None), cols=slice(None)):
        if self.ts is None:
            return ref[sq, rows, cols]
        x = ref[sq * self.ts:(sq + 1) * self.ts, cols]
        return jnp.concatenate([x, jnp.zeros((self.tp - self.ts, x.shape[1]), x.dtype)], axis=0)

    def store(self, ref, sq, rows, cols, val):
        if self.ts is None:
            ref[sq, rows, cols] = val.astype(ref.dtype)
        else:
            ref[sq * self.ts:(sq + 1) * self.ts, cols] = val[:self.ts]

    def for_groups(self, n, body):
        if self.ts is None:
            lax.fori_loop(0, n, body, 0)
        else:
            for g in range(n):
                body(g, 0)

    def pcol(self, w, off):
        if self.ts is None:
            return pl.BlockSpec((self.nb, self.tp, w), lambda b, t: (b, t, off // w))
        return pl.BlockSpec((self.nb * self.ts, w), lambda b, t: (b, off // w))

    def dims(self, p):
        return (p.shape[0], p.shape[1]) if self.ts is None else (p.shape[0] // self.ts, self.tp)

    def out_shape(self, p, w):
        return jax.ShapeDtypeStruct(p.shape[:-1] + (w,), self.out_dtype)


def _make_lru_kernel(lay, gb, tp, t_real, is_prompt):
    nb = lay.nb

    def kern(lx_ref, ly_ref, c0_ref, h0_ref, cw_ref, cb_ref, wa_ref, ba_ref, wx_ref, bx_ref, lam_ref,
             o_ref, cn_ref, hn_ref, xs_scr, a_scr, u_scr, cc_scr, ch_scr):
        tb = pl.program_id(1)

        @pl.when(tb == 0)
        def _():
            cc_scr[...] = c0_ref[...]
            ch_scr[...] = h0_ref[...]

        cw = cw_ref[...]
        neg_sp = -LRU_C * _softplus(-lam_ref[...])
        row = lax.broadcasted_iota(jnp.int32, (tp, 1), 0)

        def per_group(g, carry):
            seqs = [g * gb + j for j in range(gb)]
            for j, sq in enumerate(seqs):
                xs_scr[j, 0:SUBLANES, :] = cc_scr[sq]
                xs_scr[j, SUBLANES:SUBLANES + tp, :] = lay.load(lx_ref, sq)
            for j, sq in enumerate(seqs):
                xc = _conv4(xs_scr, j, cw, tp) + cb_ref[...]
                cc_scr[sq] = xs_scr[j, pl.ds(t_real, SUBLANES), :]
                xcb = xc.astype(BF)
                r_parts, i_parts = [], []
                for n in range(BRANCH // LRU_BD):
                    blk = xcb[:, n * LRU_BD:(n + 1) * LRU_BD]
                    r_parts.append(_dot(blk, wa_ref[n]))
                    i_parts.append(_dot(blk, wx_ref[n]))
                r = jax.nn.sigmoid(jnp.concatenate(r_parts, axis=1) + ba_ref[...])
                ig = jax.nn.sigmoid(jnp.concatenate(i_parts, axis=1) + bx_ref[...])
                log_a = r * neg_sp
                a = jnp.exp(log_a)
                mult = jnp.sqrt(-jnp.tanh(log_a) * (a * a + 1.0))
                if is_prompt:
                    mult = jnp.where(jnp.logical_and(row == 0, tb == 0), 1.0, mult)
                a_scr[j] = a
                u_scr[j] = mult * ig * xc

            def step(t, hs):
                out = []
                for j in range(gb):
                    h = a_scr[j, pl.ds(t, 1), :] * hs[j] + u_scr[j, pl.ds(t, 1), :]
                    a_scr[j, pl.ds(t, 1), :] = h
                    out.append(h)
                return tuple(out)

            h_last = lax.fori_loop(0, t_real, step, tuple(ch_scr[sq] for sq in seqs),
                                   unroll=True if t_real <= SUBLANES else 8)
            for j, sq in enumerate(seqs):
                ch_scr[sq] = h_last[j]
                lay.store(o_ref, sq, slice(None), slice(None), a_scr[j] * _gelu_tanh(lay.load(ly_ref, sq)))
            return carry

        lay.for_groups(nb // gb, per_group)
        cn_ref[...] = cc_scr[...]
        hn_ref[...] = ch_scr[...]

    return kern


def _mixer_specs(lay, l):
    wspec = lambda shape: pl.BlockSpec((None,) + shape, lambda b, t: (l,) + (0,) * len(shape))
    state = lambda *shape: pl.BlockSpec((lay.nb,) + shape, lambda b, t: (b,) + (0,) * len(shape))
    return wspec, lay.pcol, state


class _StateIO:
    def __init__(self, s0, stacked, l, nb):
        self.s0, self.l, self.nb = s0, l, nb
        self.stacked = stacked is not None
        self.prev = stacked[1] if self.stacked else None

    def spec(self):
        shape = self.s0.shape[2:] if self.stacked else self.s0.shape[1:]
        zeros = (0,) * len(shape)
        if self.stacked:
            l = self.l
            return pl.BlockSpec((None, self.nb) + shape, lambda b, t: (l, b) + zeros)
        return pl.BlockSpec((self.nb,) + shape, lambda b, t: (b,) + zeros)

    def out_shape(self):
        return jax.ShapeDtypeStruct(self.s0.shape, F32)

    def extra_inputs(self):
        return [self.prev] if self.prev is not None else []

    def extra_specs(self):
        return [pl.BlockSpec(memory_space=pl.ANY)] if self.prev is not None else []

    def aliases(self, n_in, out_idx):
        return {n_in: out_idx} if self.prev is not None else {}

    def wrap(self, kern, n_in):
        if self.prev is None:
            return kern
        return lambda *refs: kern(*refs[:n_in], *refs[n_in + 1:])


def _lru(p, c0, h0, lw, l, lay, gb, t_real, is_prompt):
    nb, tp = lay.nb, lay.tp
    nseq, tseq = lay.dims(p)
    c = BRANCH
    wspec, pcol, state = _mixer_specs(lay, l)
    return pl.pallas_call(
        _make_lru_kernel(lay, gb, tp, t_real, is_prompt),
        grid=(nseq // nb, tseq // tp),
        in_specs=[
            pcol(c, COL_LX), pcol(c, COL_LY), state(SUBLANES, c), state(1, c),
            wspec((CONV_W, c)), wspec((1, c)),
            wspec((c // LRU_BD, LRU_BD, LRU_BD)), wspec((1, c)),
            wspec((c // LRU_BD, LRU_BD, LRU_BD)), wspec((1, c)),
            wspec((1, c)),
        ],
        out_specs=[pcol(c, 0), state(SUBLANES, c), state(1, c)],
        out_shape=[
            lay.out_shape(p, c),
            jax.ShapeDtypeStruct((nseq, SUBLANES, c), F32),
            jax.ShapeDtypeStruct((nseq, 1, c), F32),
        ],
        scratch_shapes=[
            pltpu.VMEM((gb, tp + SUBLANES, c), F32), pltpu.VMEM((gb, tp, c), F32), pltpu.VMEM((gb, tp, c), F32),
            pltpu.VMEM((nb, SUBLANES, c), F32), pltpu.VMEM((nb, 1, c), F32),
        ],
        compiler_params=_cparams(("parallel", "arbitrary")),
        name="lru_prompt" if is_prompt else "lru_sample",
    )(p, p, c0, h0, lw["conv_w"], lw["conv_b"], lw["wa"], lw["ba"], lw["wx"], lw["bx"], lw["lam"])


def _chunk_consts(c, cl):
    row = lax.broadcasted_iota(jnp.int32, (c, c), 0)
    col = lax.broadcasted_iota(jnp.int32, (c, c), 1)
    tril = row >= col
    lower_ones = jnp.where(tril, 1.0, 0.0).astype(BF)
    real = lax.broadcasted_iota(jnp.int32, (c, 1), 0) < cl
    return row, col, tril, lower_ones, real


def _ks(h):
    return slice(h * DK, (h + 1) * DK)


def _vs(h):
    return slice(h * DV, (h + 1) * DV)


def _chunk_loop(lay, n_groups, nc, c, per_chunk):
    def body(idx, carry):
        g = idx // nc
        per_chunk(g, pl.ds(pl.multiple_of((idx - g * nc) * c, SUBLANES), c))
        return carry

    if lay.ts is None:
        lax.fori_loop(0, n_groups * nc, body, 0)
    else:
        assert nc == 1
        for g in range(n_groups):
            per_chunk(g, slice(None))


def _make_gla_kernel(lay, gb, tp, c, t_real):
    nb = lay.nb
    nc = tp // c
    cl = min(c, t_real)
    G, H = range(gb), range(HEADS)

    def kern(q_ref, k_ref, v_ref, gr_ref, sm_ref, s0_ref, wg_ref, bg_ref, ng_ref, o_ref, sn_ref, s_scr):
        @pl.when(pl.program_id(1) == 0)
        def _():
            s_scr[...] = s0_ref[...]

        _, _, tril, lower_ones, real = _chunk_consts(c, cl)
        ones_real = jnp.where(real, 1.0, 0.0).astype(BF) * jnp.ones((c, DV), BF)
        d_tn = lambda x: lax.dot_general(x, ones_real, (((0,), (0,)), ((), ())), preferred_element_type=F32)
        ng = ng_ref[...]
        wg = wg_ref[...]
        bg = bg_ref[...]

        def per_chunk(g, rows):
            seqs = [g * gb + j for j in G]
            lg = [-_softplus(-(_mm(lay.load(sm_ref, sq, rows), wg) + bg)) * (1.0 / GLA_TAU) for sq in seqs]
            sp = [_split3(x) for x in lg]
            b = [_dot(lower_ones, s[0]) + _dot(lower_ones, s[1]) + _dot(lower_ones, s[2]) for s in sp]
            b_last_col = [d_tn(s[0]) + d_tn(s[1]) + d_tn(s[2]) for s in sp]
            q_t = [lay.load(q_ref, sq, rows) * (DK ** -0.5) * jnp.exp(b[j]) for j, sq in enumerate(seqs)]
            k = [lay.load(k_ref, sq, rows) for sq in seqs]
            k_t = [jnp.where(real, k[j] * jnp.exp(-b[j]), 0.0) for j in G]
            k_d = [jnp.where(real, k[j] * jnp.exp(b[j][cl - 1:cl, :] - b[j]), 0.0) for j in G]
            v = [lay.load(v_ref, sq, rows) for sq in seqs]
            att = [[jnp.where(tril, _mm_nt(q_t[j][:, _ks(h)], k_t[j][:, _ks(h)]), 0.0) for h in H] for j in G]
            s = [[s_scr[seqs[j], h] for h in H] for j in G]
            kv = [[_mm_tn(k_d[j][:, _ks(h)], v[j][:, _vs(h)]) for h in H] for j in G]
            o = [[_mm(att[j][h], v[j][:, _vs(h)]) + _mm(q_t[j][:, _ks(h)], s[j][h]) for h in H] for j in G]
            for j in G:
                for h in H:
                    s_scr[seqs[j], h] = jnp.exp(b_last_col[j][_ks(h), :]) * s[j][h] + kv[j][h]
            for j in G:
                gr = lay.load(gr_ref, seqs[j], rows)
                for h in H:
                    lay.store(o_ref, seqs[j], rows, _vs(h), _rms(o[j][h], ng) * _silu(gr[:, _vs(h)]))

        _chunk_loop(lay, nb // gb, nc, c, per_chunk)
        sn_ref[...] = s_scr[...]

    return kern


def _gla(p, s0, lw, l, lay, gb, c, t_real, name, stacked=None):
    nb, tp = lay.nb, lay.tp
    nseq, tseq = lay.dims(p)
    qk = HEADS * DK
    vw = HEADS * DV
    wspec, pcol, _ = _mixer_specs(lay, l)
    sio = _StateIO(s0, stacked, l, nb)
    in_specs = [
        pcol(qk, COL_GQ), pcol(qk, COL_GK), pcol(vw, COL_GV), pcol(vw, COL_GR), pcol(128, COL_SM),
        sio.spec(), wspec((128, qk)), wspec((1, qk)), wspec((1, DV)),
    ]
    n_in = len(in_specs)
    return pl.pallas_call(
        sio.wrap(_make_gla_kernel(lay, gb, tp, c, t_real), n_in),
        grid=(nseq // nb, tseq // tp),
        in_specs=in_specs + sio.extra_specs(),
        out_specs=[pcol(vw, 0), sio.spec()],
        out_shape=[lay.out_shape(p, vw), sio.out_shape()],
        scratch_shapes=[pltpu.VMEM((nb, HEADS, DK, DV), F32)],
        input_output_aliases=sio.aliases(n_in, 1),
        compiler_params=_cparams(("parallel", "arbitrary")),
        name=name,
    )(p, p, p, p, p, s0, lw["gla_wg"], lw["gla_bg"], lw["gla_ng"], *sio.extra_inputs())


def _make_ret_kernel(lay, gb, tp, c, t_real):
    nb = lay.nb
    nc = tp // c
    cl = min(c, t_real)
    G, H = range(gb), range(HEADS)
    log_gamma = [float(np.log(np.float32(1.0) - np.float32(2.0) ** np.float32(-5.0 - h))) for h in range(HEADS)]

    def kern(q_ref, k_ref, qs_ref, ks_ref, v_ref, g_ref, cos_ref, sin_ref, s0_ref, ng_ref, o_ref, sn_ref, s_scr):
        @pl.when(pl.program_id(1) == 0)
        def _():
            s_scr[...] = s0_ref[...]

        row, col, tril, _, real = _chunk_consts(c, cl)
        diff = (row - col).astype(F32)
        ridx = lax.broadcasted_iota(jnp.int32, (c, 1), 0).astype(F32)
        ng = ng_ref[...]
        dmat = [jnp.where(tril, jnp.exp(jnp.maximum(diff, 0.0) * lgm), 0.0) for lgm in log_gamma]
        q_decay = [jnp.exp((ridx + 1.0) * lgm) for lgm in log_gamma]
        k_decay = [jnp.exp((cl - 1.0 - ridx) * lgm) for lgm in log_gamma]

        def per_chunk(g, rows):
            seqs = [g * gb + j for j in G]
            cos = cos_ref[rows, :]
            sin = sin_ref[rows, :]
            ld = lambda ref, sq: lay.load(ref, sq, rows)
            q = [ld(q_ref, sq) * cos + ld(qs_ref, sq) * sin for sq in seqs]
            k = [jnp.where(real, (ld(k_ref, sq) * cos + ld(ks_ref, sq) * sin) * (DK ** -0.5), 0.0) for sq in seqs]
            v = [ld(v_ref, sq) for sq in seqs]
            att = [[_mm_nt(q[j][:, _ks(h)], k[j][:, _ks(h)]) * dmat[h] for h in H] for j in G]
            s = [[s_scr[seqs[j], h] for h in H] for j in G]
            kv = [[_mm_tn(k[j][:, _ks(h)] * k_decay[h], v[j][:, _vs(h)]) for h in H] for j in G]
            o = [[_mm(att[j][h], v[j][:, _vs(h)]) + _mm(q[j][:, _ks(h)] * q_decay[h], s[j][h]) for h in H]
                 for j in G]
            for j in G:
                for h in H:
                    s_scr[seqs[j], h] = math.exp(cl * log_gamma[h]) * s[j][h] + kv[j][h]
            for j in G:
                gate = ld(g_ref, seqs[j])
                for h in H:
                    oc = o[j][h] - jnp.mean(o[j][h], axis=-1, keepdims=True)
                    on = oc * lax.rsqrt(jnp.mean(oc * oc, axis=-1, keepdims=True) + EPS) * ng
                    lay.store(o_ref, seqs[j], rows, _vs(h), on * _silu(gate[:, _vs(h)]))

        _chunk_loop(lay, nb // gb, nc, c, per_chunk)
        sn_ref[...] = s_scr[...]

    return kern


def _ret(p, s0, cos_t, sin_t, lw, l, lay, gb, c, t_real, name, stacked=None):
    nb, tp = lay.nb, lay.tp
    nseq, tseq = lay.dims(p)
    qk = HEADS * DK
    vw = HEADS * DV
    wspec, pcol, _ = _mixer_specs(lay, l)
    sio = _StateIO(s0, stacked, l, nb)
    in_specs = [
        pcol(qk, COL_RQ), pcol(qk, COL_RK), pcol(qk, COL_RQS), pcol(qk, COL_RKS), pcol(vw, COL_RV),
        pcol(vw, COL_RG),
        pl.BlockSpec((tp, qk), lambda b, t: (t, 0)),
        pl.BlockSpec((tp, qk), lambda b, t: (t, 0)),
        sio.spec(), wspec((1, DV)),
    ]
    n_in = len(in_specs)
    return pl.pallas_call(
        sio.wrap(_make_ret_kernel(lay, gb, tp, c, t_real), n_in),
        grid=(nseq // nb, tseq // tp),
        in_specs=in_specs + sio.extra_specs(),
        out_specs=[pcol(vw, 0), sio.spec()],
        out_shape=[lay.out_shape(p, vw), sio.out_shape()],
        scratch_shapes=[pltpu.VMEM((nb, HEADS, DK, DV), F32)],
        input_output_aliases=sio.aliases(n_in, 1),
        compiler_params=_cparams(("parallel", "arbitrary")),
        name=name,
    )(p, p, p, p, p, p, cos_t, sin_t, s0, lw["ret_ng"], *sio.extra_inputs())


def _unit_lower_inverse_many(ms, c):
    row = lax.broadcasted_iota(jnp.int32, (c, c), 0)
    col = lax.broadcasted_iota(jnp.int32, (c, c), 1)
    eye = jnp.where(row == col, 1.0, 0.0)

    def same_block(bits):
        return lax.shift_right_logical(row, bits) == lax.shift_right_logical(col, bits)

    in8 = same_block(3)
    n1 = [jnp.where(in8, m, 0.0) for m in ms]
    n2 = [_mm_hp(x, x) for x in n1]
    n4 = [_mm_hp(x, x) for x in n2]
    d = [eye - x for x in n1]
    d = [x + _mm_hp(x, y) for x, y in zip(d, n2)]
    d = [x + _mm_hp(x, y) for x, y in zip(d, n4)]
    bits = 3
    while (1 << bits) < c:
        sel = jnp.logical_and(same_block(bits + 1), jnp.logical_not(same_block(bits)))
        ld = [_mm(jnp.where(sel, m, 0.0), x) for m, x in zip(ms, d)]
        d = [x - _mm(x, y) for x, y in zip(d, ld)]
        bits += 1
    return d


def _make_dn_kernel(lay, gb, tp, c, t_real):
    nb = lay.nb
    nc = tp // c
    cl = min(c, t_real)
    G, H = range(gb), range(HEADS)
    GH = [(j, h) for j in G for h in H]

    def kern(x_ref, z_ref, sm_ref, c0_ref, s0_ref, cw_ref, alog_ref, dtb_ref, ng_ref,
             o_ref, cn_ref, sn_ref, xs_scr, qkv_scr, cc_scr, s_scr):
        @pl.when(pl.program_id(1) == 0)
        def _():
            cc_scr[...] = c0_ref[...]
            s_scr[...] = s0_ref[...]

        row, col, tril, lower_ones, real = _chunk_consts(c, cl)
        strict = row > col
        lane = lax.broadcasted_iota(jnp.int32, (c, 128), 1)
        sel = [jnp.where(lane == SM_DA + h, 1.0, 0.0).astype(BF) for h in H]
        d_nt = lambda a, x: lax.dot_general(a, x, (((1,), (1,)), ((), ())), preferred_element_type=F32)
        ng = ng_ref[...]
        cw = cw_ref[...]
        neg_a = -jnp.exp(alog_ref[...])
        dtb = dtb_ref[...]

        def per_group(g, carry):
            seqs = [g * gb + j for j in G]
            for j, sq in enumerate(seqs):
                xs_scr[j, 0:SUBLANES, :] = cc_scr[sq]
                xs_scr[j, SUBLANES:SUBLANES + tp, :] = lay.load(x_ref, sq)
            for j, sq in enumerate(seqs):
                qkv_scr[j] = _silu(_conv4(xs_scr, j, cw, tp))
                cc_scr[sq] = xs_scr[j, pl.ds(t_real, SUBLANES), :]

            def per_chunk(ci, carry2):
                rows = pl.ds(pl.multiple_of(ci * c, SUBLANES), c)
                sm = [lay.load(sm_ref, sq, rows) for sq in seqs]
                gcum = [_cumsum_rows(neg_a * _softplus(x + dtb), lower_ones) for x in sm]
                g3 = [_split3(x) for x in gcum]
                beta_all = [jnp.where(real, jax.nn.sigmoid(x), 0.0) for x in sm]

                def head_in(j, h, part):
                    x = qkv_scr[j, rows, part * BRANCH + h * DV:part * BRANCH + (h + 1) * DV]
                    return x

                qh = {jh: head_in(*jh, 0) for jh in GH}
                kh = {jh: head_in(*jh, 1) for jh in GH}
                vh = {jh: head_in(*jh, 2) for jh in GH}
                qh = {jh: x * lax.rsqrt(jnp.sum(x * x, axis=-1, keepdims=True) + EPS) * (DV ** -0.5)
                      for jh, x in qh.items()}
                kh = {jh: jnp.where(real, x * lax.rsqrt(jnp.sum(x * x, axis=-1, keepdims=True) + EPS), 0.0)
                      for jh, x in kh.items()}
                beta = {(j, h): beta_all[j][:, SM_DB + h:SM_DB + h + 1] for j, h in GH}
                gcol = {(j, h): gcum[j][:, SM_DA + h:SM_DA + h + 1] for j, h in GH}
                grow = {(j, h): d_nt(sel[h], g3[j][0]) + d_nt(sel[h], g3[j][1]) + d_nt(sel[h], g3[j][2])
                        for j, h in GH}
                decay = {jh: jnp.where(tril, jnp.exp(jnp.where(tril, gcol[jh] - grow[jh], 0.0)), 0.0) for jh in GH}
                e_g = {jh: jnp.exp(gcol[jh]) for jh in GH}
                kb = {jh: kh[jh] * beta[jh] for jh in GH}
                kq = {jh: _mm_nt(jnp.concatenate([kb[jh], qh[jh]], axis=0), kh[jh]) for jh in GH}
                tm = _unit_lower_inverse_many([jnp.where(strict, kq[jh][:c] * decay[jh], 0.0) for jh in GH], c)
                tm = dict(zip(GH, tm))
                uw = {jh: _mm(tm[jh], jnp.concatenate([vh[jh] * beta[jh], kb[jh] * e_g[jh]], axis=1)) for jh in GH}
                s = {(j, h): s_scr[seqs[j], h] for j, h in GH}
                wq = {jh: _mm(jnp.concatenate([uw[jh][:, DV:], qh[jh] * e_g[jh]], axis=0), s[jh]) for jh in GH}
                v_new = {jh: uw[jh][:, :DV] - wq[jh][:c] for jh in GH}
                o = {jh: wq[jh][c:] + _mm(kq[jh][c:] * decay[jh], v_new[jh]) for jh in GH}
                for j, h in GH:
                    g_last = gcol[(j, h)][cl - 1:cl, :]
                    k_d = kh[(j, h)] * jnp.exp(g_last - gcol[(j, h)])
                    s_scr[seqs[j], h] = jnp.exp(g_last) * s[(j, h)] + _mm_tn(k_d, v_new[(j, h)])
                for j in G:
                    z = lay.load(z_ref, seqs[j], rows)
                    for h in H:
                        lay.store(o_ref, seqs[j], rows, _vs(h), _rms(o[(j, h)], ng) * _silu(z[:, _vs(h)]))
                return carry2

            lax.fori_loop(0, nc, per_chunk, 0)
            return carry

        lay.for_groups(nb // gb, per_group)
        cn_ref[...] = cc_scr[...]
        sn_ref[...] = s_scr[...]

    return kern


def _dn(p, c0, s0, lw, l, lay, gb, c, t_real, name, stacked=None):
    nb, tp = lay.nb, lay.tp
    nseq, tseq = lay.dims(p)
    cw3 = 3 * BRANCH
    wspec, pcol, state = _mixer_specs(lay, l)
    sio = _StateIO(s0, stacked, l, nb)
    in_specs = [
        pcol(cw3, COL_DQKV), pcol(BRANCH, COL_DZ), pcol(128, COL_SM),
        state(SUBLANES, cw3), sio.spec(),
        wspec((CONV_W, cw3)), wspec((1, 128)), wspec((1, 128)), wspec((1, DV)),
    ]
    n_in = len(in_specs)
    return pl.pallas_call(
        sio.wrap(_make_dn_kernel(lay, gb, tp, c, t_real), n_in),
        grid=(nseq // nb, tseq // tp),
        in_specs=in_specs + sio.extra_specs(),
        out_specs=[pcol(BRANCH, 0), state(SUBLANES, cw3), sio.spec()],
        out_shape=[
            lay.out_shape(p, BRANCH),
            jax.ShapeDtypeStruct((nseq, SUBLANES, cw3), F32),
            sio.out_shape(),
        ],
        scratch_shapes=[
            pltpu.VMEM((gb, tp + SUBLANES, cw3), F32), pltpu.VMEM((gb, tp, cw3), F32),
            pltpu.VMEM((nb, SUBLANES, cw3), F32), pltpu.VMEM((nb, HEADS, DV, DV), F32),
        ],
        input_output_aliases=sio.aliases(n_in, 2),
        compiler_params=_cparams(("parallel", "arbitrary")),
        name=name,
    )(p, p, p, c0, s0, lw["dn_conv_w"], lw["dn_alog"], lw["dn_dtb"], lw["dn_ng"], *sio.extra_inputs())


IN_SIZES = (BRANCH, BRANCH, HEADS * DK, HEADS * DK, BRANCH, BRANCH, GLA_RANK,
            HEADS * DK, HEADS * DK, BRANCH, BRANCH, BRANCH, BRANCH, BRANCH, BRANCH, HEADS, HEADS)


IN_OFFS = tuple(int(v) for v in np.concatenate([[0], np.cumsum(IN_SIZES)]))
PACK_ROWS = 256


def _pack_src_row(r):
    t_rq, t_swap, t_rv, t_small = (c // PACK_ROWS for c in (COL_RQ, COL_RQS, COL_RV, COL_SM))
    o_rq, o_rv = IN_OFFS[7], IN_OFFS[9]
    row = jnp.where(r < t_rq, r * PACK_ROWS,
                    jnp.where(r < t_swap, o_rq + (r - t_rq) * PACK_ROWS,
                              jnp.where(r < t_rv, o_rq + (r - t_swap) * PACK_ROWS,
                                        jnp.where(r < t_small, o_rv + (r - t_rv) * PACK_ROWS, 0))))
    return pl.multiple_of(row, 2 * SUBLANES)


def _pack_kernel(w_ref, glr_ref, dba_ref, o_ref):
    r = pl.program_id(1)
    t_swap, t_rv, t_small = (c // PACK_ROWS for c in (COL_RQS, COL_RV, COL_SM))
    w = w_ref[0]

    @pl.when(jnp.logical_and(jnp.logical_or(r < t_swap, r >= t_rv), r < t_small))
    def _():
        o_ref[...] = w.astype(BF)

    @pl.when(jnp.logical_and(r >= t_swap, r < t_rv))
    def _():
        parts = []
        for h in range(PACK_ROWS // DK):
            parts += [w[h * DK + DK // 2:(h + 1) * DK], w[h * DK:h * DK + DK // 2]]
        o_ref[...] = jnp.concatenate(parts, axis=0).astype(BF)

    @pl.when(r == t_small)
    def _():
        pad = jnp.zeros((PACK_ROWS - GLA_RANK - 2 * HEADS, w.shape[1]), w.dtype)
        o_ref[...] = jnp.concatenate([glr_ref[0], dba_ref[0], pad], axis=0).astype(BF)


def _pack_w_in(w_in_t):
    depth, n_in, d = w_in_t.shape
    assert n_in == IN_OFFS[-1] and N_PACK % PACK_ROWS == 0 and COL_SM // PACK_ROWS == N_PACK // PACK_ROWS - 1
    rows = lambda n, start: pl.BlockSpec((pl.Element(1), pl.Element(n), pl.Element(d)),
                                         lambda l, r: (l, start(r), 0))
    return pl.pallas_call(
        _pack_kernel,
        grid=(depth, N_PACK // PACK_ROWS),
        in_specs=[rows(PACK_ROWS, _pack_src_row), rows(GLA_RANK, lambda r: IN_OFFS[6]),
                  rows(2 * HEADS, lambda r: IN_OFFS[15])],
        out_specs=pl.BlockSpec((None, PACK_ROWS, d), lambda l, r: (l, r, 0)),
        out_shape=jax.ShapeDtypeStruct((depth, N_PACK, d), BF),
        compiler_params=_cparams(("parallel", "parallel")),
        name="pack_w_in",
    )(w_in_t, w_in_t, w_in_t)


def _rope_tables(pos):
    half = DK // 2
    inv = ROPE_BASE ** (-jnp.arange(half, dtype=F32) / half)
    ang = pos.astype(F32)[:, None] * inv[None, :]
    cos, sin = jnp.cos(ang), jnp.sin(ang)
    cos_t = jnp.tile(jnp.concatenate([cos, cos], axis=1), (1, HEADS))
    sin_t = jnp.tile(jnp.concatenate([-sin, sin], axis=1), (1, HEADS))
    return cos_t, sin_t


def _lane_row(vals, off):
    depth = vals.shape[0]
    return jnp.zeros((depth, 1, 128), F32).at[:, 0, off:off + HEADS].set(vals.astype(F32))


def kernel(x_prompt, x_sample, state_lru_conv, state_lru_h, state_gla, state_ret, state_dn_conv, state_dn,
           norm_g, final_norm_g, w_ff_in, w_ff_out, w_in, w_gate, w_branch, w_out,
           lru_conv_w, lru_conv_b, lru_wa, lru_ba, lru_wx, lru_bx, lru_lambda,
           gla_wg, gla_bg, gla_norm_g, ret_norm_g, dn_conv_w, dn_a_log, dn_dt_bias, dn_norm_g):
    bp, tpr, d = x_prompt.shape
    bs, ts, _ = x_sample.shape
    depth = w_in.shape[0]
    n_p, n_s = bp * tpr, bs * ts
    assert d == D_MODEL and ts <= SAMPLE_PAD and tpr % CHUNK == 0

    w_pack = _pack_w_in(jnp.swapaxes(w_in, 1, 2))
    wg_pad = jnp.zeros((depth, 128, HEADS * DK), F32).at[:, SM_GLR:SM_GLR + GLA_RANK, :].set(gla_wg).astype(BF)
    r3 = lambda a: a.reshape(a.shape[0], 1, a.shape[1])
    lw = {
        "conv_w": lru_conv_w, "conv_b": r3(lru_conv_b), "wa": lru_wa.astype(BF), "ba": r3(lru_ba),
        "wx": lru_wx.astype(BF), "bx": r3(lru_bx), "lam": r3(lru_lambda),
        "gla_wg": wg_pad, "gla_bg": r3(gla_bg), "gla_ng": r3(gla_norm_g), "ret_ng": r3(ret_norm_g),
        "dn_conv_w": dn_conv_w, "dn_alog": _lane_row(dn_a_log, SM_DA), "dn_dtb": _lane_row(dn_dt_bias, SM_DA),
        "dn_ng": r3(dn_norm_g),
    }
    norm_g4 = norm_g.reshape(depth, 3, 1, d)

    cos_p, sin_p = _rope_tables(jnp.arange(tpr))
    pos_s = jnp.where(jnp.arange(SAMPLE_PAD) < ts, PAST_LEN + jnp.arange(SAMPLE_PAD), 0)
    cos_s, sin_s = _rope_tables(pos_s)

    pad_conv = lambda a: jnp.pad(a, ((0, 0), (SUBLANES - (CONV_W - 1), 0), (0, 0)))
    zeros = lambda *s: jnp.zeros(s, F32)

    tp_p = _pick_tile(tpr, MIXER_TIME_TILE, CHUNK)
    a_p = dict(lay=_SeqLayout(bp, tp_p), gb=bp)
    nb_s = _pick_tile(bs, SAMPLE_SEQS, 2)
    assert (nb_s * ts) % SUBLANES == 0
    a_s = dict(lay=_SeqLayout(nb_s, SAMPLE_PAD, ts), gb=_pick_tile(nb_s, LOCKSTEP_SEQS, 1))

    fg = final_norm_g.reshape(1, d)
    xp = x_prompt.reshape(n_p, d)
    xs = x_sample.reshape(n_s, d)
    new_p = [[] for _ in range(6)]
    new_s = [[] for _ in range(6)]
    s_gla_s = s_ret_s = s_dn_s = None
    for l in range(depth):
        xs, *w_ffn = _ffn_cast(xs, norm_g4[l, 0], w_ff_in, w_ff_out, l, 0)
        xp = _ffn(xp, norm_g4[l, 0], *w_ffn)
        p_p, hn_p = _inproj(xp, norm_g4[l, 1], w_pack, l)
        p_p = p_p.reshape(bp, tpr, N_PACK)
        p_s, _ = _inproj(xs, norm_g4[l, 1], w_pack, l)

        o_lru_p, c_lru_p, h_lru_p = _lru(p_p, zeros(bp, SUBLANES, BRANCH), zeros(bp, 1, BRANCH), lw, l,
                                         t_real=tp_p, is_prompt=True, **a_p)
        o_gla_p, s_gla_p = _gla(p_p, zeros(bp, HEADS, DK, DV), lw, l, c=CHUNK, t_real=tp_p, name="gla_prompt", **a_p)
        o_ret_p, s_ret_p = _ret(p_p, zeros(bp, HEADS, DK, DV), cos_p, sin_p, lw, l, c=CHUNK, t_real=tp_p,
                                name="ret_prompt", **a_p)
        o_dn_p, c_dn_p, s_dn_p = _dn(p_p, zeros(bp, SUBLANES, 3 * BRANCH), zeros(bp, HEADS, DV, DV), lw, l,
                                     c=CHUNK, t_real=tp_p, name="dn_prompt", **a_p)

        o_lru_s, c_lru_s, h_lru_s = _lru(p_s, pad_conv(state_lru_conv[l]), state_lru_h[l][:, None, :], lw, l,
                                         t_real=ts, is_prompt=False, **a_s)
        o_gla_s, s_gla_s = _gla(p_s, state_gla, lw, l, c=SAMPLE_PAD, t_real=ts, name="gla_sample",
                                stacked=(depth, s_gla_s), **a_s)
        o_ret_s, s_ret_s = _ret(p_s, state_ret, cos_s, sin_s, lw, l, c=SAMPLE_PAD, t_real=ts, name="ret_sample",
                                stacked=(depth, s_ret_s), **a_s)
        o_dn_s, c_dn_s, s_dn_s = _dn(p_s, pad_conv(state_dn_conv[l]), state_dn, lw, l, c=SAMPLE_PAD, t_real=ts,
                                     name="dn_sample", stacked=(depth, s_dn_s), **a_s)

        flat_p = lambda o: o.reshape(n_p, BRANCH)
        xs, *w_merge = _merge_cast(xs, norm_g4[l, 1], [o_lru_s, o_gla_s, o_ret_s, o_dn_s], w_gate, w_branch, w_out, l)
        xp = _merge_split(xp, hn_p, [flat_p(o) for o in (o_lru_p, o_gla_p, o_ret_p, o_dn_p)], *w_merge)
        xs, *w_ffn = _ffn_cast(xs, norm_g4[l, 2], w_ff_in, w_ff_out, l, 1)
        xp = _ffn(xp, norm_g4[l, 2], *w_ffn, final_g=fg if l == depth - 1 else None)

        tail3 = lambda cwin: cwin[:, SUBLANES - (CONV_W - 1):, :]
        for i, v in enumerate((tail3(c_lru_p), h_lru_p[:, 0], s_gla_p, s_ret_p, tail3(c_dn_p), s_dn_p)):
            new_p[i].append(v)
        for i, v in ((0, tail3(c_lru_s)), (1, h_lru_s[:, 0]), (4, tail3(c_dn_s))):
            new_s[i].append(v)

    y_prompt = xp.reshape(bp, tpr, d)
    y_sample = _final_norm(xs, fg).reshape(bs, ts, d)
    sp = [jnp.stack(v) for v in new_p]
    s_small = {i: jnp.stack(new_s[i]) for i in (0, 1, 4)}
    return tuple([y_prompt, y_sample] + sp + [s_small[0], s_small[1], s_gla_s, s_ret_s, s_small[4], s_dn_s])
```

```python
import math

import numpy as np
import jax
import jax.numpy as jnp
from jax import lax
from jax.experimental import pallas as pl
from jax.experimental.pallas import tpu as pltpu

F32 = jnp.float32
BF = jnp.bfloat16
EPS = 1e-6

D_MODEL = 2048
BRANCH = 512
CONV_W = 4
HEADS = 4
DK = 64
DV = 128
GLA_RANK = 16
GLA_TAU = 16.0
LRU_C = 8.0
LRU_BD = 128
ROPE_BASE = 10000.0
CHUNK = 64
PAST_LEN = 16384

V7X_VMEM_BYTES = 64 * 1024 * 1024
VMEM_LIMIT = V7X_VMEM_BYTES - 8 * 1024 * 1024
SUBLANES = 8
BF16_ROWS = 16
LANES = 128
ROW_TILE = 1024
ROW_TILE_SMALL = 512
FFN_COL_TILE = 512
PROJ_COL_TILE = 1024
GATE_COL_TILE = 512
MERGE_COL_TILE = 256
MIXER_TIME_TILE = 256
SAMPLE_SEQS = 16
LOCKSTEP_SEQS = 4
SAMPLE_PAD = 8

COL_LX, COL_LY = 0, 512
COL_GQ, COL_GK, COL_GV, COL_GR = 1024, 1280, 1536, 2048
COL_RQ, COL_RK, COL_RQS, COL_RKS, COL_RV, COL_RG = 2560, 2816, 3072, 3328, 3584, 4096
COL_DQKV, COL_DZ = 4608, 6144
COL_SM = 6656
SM_GLR, SM_DB, SM_DA = 0, 16, 20
N_PACK = 6912


def _dot(a, b):
    return jnp.dot(a, b, preferred_element_type=F32)


def _mm(a, b):
    return _dot(a.astype(BF), b.astype(BF))


def _mm_nt(a, b):
    return lax.dot_general(a.astype(BF), b.astype(BF), (((1,), (1,)), ((), ())), preferred_element_type=F32)


def _mm_tn(a, b):
    return lax.dot_general(a.astype(BF), b.astype(BF), (((0,), (0,)), ((), ())), preferred_element_type=F32)


def _split3(x):
    hi = x.astype(BF)
    r = x - hi.astype(F32)
    mid = r.astype(BF)
    lo = (r - mid.astype(F32)).astype(BF)
    return hi, mid, lo


def _split2(x):
    hi = x.astype(BF)
    return hi, (x - hi.astype(F32)).astype(BF)


def _mm_hp(a, b):
    ah, al = _split2(a)
    bh, bl = _split2(b)
    return _dot(ah, bh) + _dot(ah, bl) + _dot(al, bh)


def _cumsum_rows(x, lower_ones):
    hi, mid, lo = _split3(x)
    return _dot(lower_ones, hi) + _dot(lower_ones, mid) + _dot(lower_ones, lo)


def _softplus(x):
    return jnp.maximum(x, 0.0) + jnp.log1p(jnp.exp(-jnp.abs(x)))


def _silu(x):
    return x * jax.nn.sigmoid(x)


def _gelu_tanh(x):
    return x * (0.5 * (1.0 + jnp.tanh(math.sqrt(2.0 / math.pi) * (x + 0.044715 * (x * x * x)))))


def _rms(x, g):
    return x * lax.rsqrt(jnp.mean(x * x, axis=-1, keepdims=True) + EPS) * g


def _pick_tile(n, target, align):
    best = None
    for t in range(align, min(n, target) + 1, align):
        if n % t == 0:
            best = t
    assert best is not None, (n, target, align)
    return best


def _cparams(sem):
    return pltpu.CompilerParams(dimension_semantics=sem, vmem_limit_bytes=VMEM_LIMIT)


def _ffn_kernel(x_ref, g_ref, wg_ref, wu_ref, wo_ref, o_ref, hn_ref):
    @pl.when(pl.program_id(1) == 0)
    def _():
        x = x_ref[...]
        hn_ref[...] = _rms(x, g_ref[...]).astype(BF)
        o_ref[...] = x

    hn = hn_ref[...]
    gate = _dot(hn, wg_ref[...])
    up = _dot(hn, wu_ref[...])
    act = (_silu(gate) * up * 0.5).astype(BF)
    o_ref[...] += _dot(act, wo_ref[...])


def _ffn_final_kernel(x_ref, g_ref, wg_ref, wu_ref, wo_ref, fg_ref, o_ref, hn_ref):
    _ffn_kernel(x_ref, g_ref, wg_ref, wu_ref, wo_ref, o_ref, hn_ref)

    @pl.when(pl.program_id(1) == pl.num_programs(1) - 1)
    def _():
        o_ref[...] = _rms(o_ref[...], fg_ref[...])


def _ffn(x, g, wg, wu, wo, final_g=None):
    n, d = x.shape
    f = wo.shape[0]
    tm = _pick_tile(n, ROW_TILE, BF16_ROWS)
    tf = _pick_tile(f, FFN_COL_TILE, LANES)
    final = [] if final_g is None else [final_g]
    return pl.pallas_call(
        _ffn_kernel if final_g is None else _ffn_final_kernel,
        grid=(n // tm, f // tf),
        in_specs=[
            pl.BlockSpec((tm, d), lambda i, j: (i, 0)),
            pl.BlockSpec((1, d), lambda i, j: (0, 0)),
            pl.BlockSpec((d, tf), lambda i, j: (0, j)),
            pl.BlockSpec((d, tf), lambda i, j: (0, j)),
            pl.BlockSpec((tf, d), lambda i, j: (j, 0)),
        ] + [pl.BlockSpec((1, d), lambda i, j: (0, 0))] * len(final),
        out_specs=pl.BlockSpec((tm, d), lambda i, j: (i, 0)),
        out_shape=jax.ShapeDtypeStruct((n, d), F32),
        scratch_shapes=[pltpu.VMEM((tm, d), BF)],
        compiler_params=_cparams(("parallel", "arbitrary")),
        name="ffn",
    )(x, g, wg, wu, wo, *final)


def _ffn_cast_kernel(x_ref, g_ref, wg32_ref, wu32_ref, wo32_ref, o_ref, wg_ref, wu_ref, wo_ref, hn_ref):
    wg_ref[...] = wg32_ref[...].astype(BF)
    wu_ref[...] = wu32_ref[...].astype(BF)
    wo_ref[...] = wo32_ref[...].astype(BF)
    _ffn_kernel(x_ref, g_ref, wg_ref, wu_ref, wo_ref, o_ref, hn_ref)


def _ffn_cast(x, g, w_in, w_out, l, s):
    n, d = x.shape
    f = w_out.shape[2]
    tf = _pick_tile(f, FFN_COL_TILE, LANES)
    nf = f // tf
    once = dict(pipeline_mode=pl.Buffered(1))
    return pl.pallas_call(
        _ffn_cast_kernel,
        grid=(1, nf),
        in_specs=[
            pl.BlockSpec((n, d), lambda i, j: (0, 0), **once),
            pl.BlockSpec((1, d), lambda i, j: (0, 0)),
            pl.BlockSpec((None, None, d, tf), lambda i, j: (l, s, 0, j)),
            pl.BlockSpec((None, None, d, tf), lambda i, j: (l, s, 0, j + nf)),
            pl.BlockSpec((None, None, tf, d), lambda i, j: (l, s, j, 0)),
        ],
        out_specs=[
            pl.BlockSpec((n, d), lambda i, j: (0, 0), **once),
            pl.BlockSpec((d, tf), lambda i, j: (0, j)),
            pl.BlockSpec((d, tf), lambda i, j: (0, j)),
            pl.BlockSpec((tf, d), lambda i, j: (j, 0)),
        ],
        out_shape=[
            jax.ShapeDtypeStruct((n, d), F32),
            jax.ShapeDtypeStruct((d, f), BF), jax.ShapeDtypeStruct((d, f), BF), jax.ShapeDtypeStruct((f, d), BF),
        ],
        scratch_shapes=[pltpu.VMEM((n, d), BF)],
        compiler_params=_cparams(("arbitrary", "arbitrary")),
        name="ffn_cast",
    )(x, g, w_in, w_in, w_out)


def _inproj_kernel(x_ref, g_ref, w_ref, o_ref, hn_ref):
    @pl.when(pl.program_id(1) == 0)
    def _():
        hn_ref[...] = _rms(x_ref[...], g_ref[...]).astype(BF)

    o_ref[...] = _mm_nt(hn_ref[...], w_ref[...])


def _inproj(x, g, w_pack, l):
    n, d = x.shape
    npk = w_pack.shape[1]
    tm = _pick_tile(n, ROW_TILE, BF16_ROWS)
    tn = _pick_tile(npk, PROJ_COL_TILE, LANES)
    return pl.pallas_call(
        _inproj_kernel,
        grid=(n // tm, npk // tn),
        in_specs=[
            pl.BlockSpec((tm, d), lambda i, j: (i, 0)),
            pl.BlockSpec((1, d), lambda i, j: (0, 0)),
            pl.BlockSpec((None, tn, d), lambda i, j: (l, j, 0)),
        ],
        out_specs=[pl.BlockSpec((tm, tn), lambda i, j: (i, j)), pl.BlockSpec((tm, d), lambda i, j: (i, 0))],
        out_shape=[jax.ShapeDtypeStruct((n, npk), F32), jax.ShapeDtypeStruct((n, d), BF)],
        compiler_params=_cparams(("parallel", "arbitrary")),
        name="inproj",
    )(x, g, w_pack)


def _merge_kernel(x_ref, g_ref, b0_ref, b1_ref, b2_ref, b3_ref, wgate_ref, wbr_ref, wo_ref, o_ref, hn_ref):
    @pl.when(pl.program_id(1) == 0)
    def _():
        x = x_ref[...]
        hn_ref[...] = _rms(x, g_ref[...]).astype(BF)
        o_ref[...] = x

    hn = hn_ref[...]
    m = None
    for n, b_ref in enumerate((b0_ref, b1_ref, b2_ref, b3_ref)):
        gate = jax.nn.sigmoid(_dot(hn, wgate_ref[n]))
        br = _dot(b_ref[...].astype(BF), wbr_ref[n])
        m = gate * br if m is None else m + gate * br
    o_ref[...] += _dot(m.astype(BF), wo_ref[...])


def _gate_kernel(hn_ref, b0_ref, b1_ref, b2_ref, b3_ref, wgate_ref, wbr_ref, m_ref):
    hn = hn_ref[...]
    m = None
    for n, b_ref in enumerate((b0_ref, b1_ref, b2_ref, b3_ref)):
        gate = jax.nn.sigmoid(_dot(hn, wgate_ref[n]))
        br = _dot(b_ref[...].astype(BF), wbr_ref[n])
        m = gate * br if m is None else m + gate * br
    m_ref[...] = m.astype(BF)


def _outproj_kernel(m_ref, w_ref, x_ref, o_ref):
    o_ref[...] = x_ref[...] + _dot(m_ref[...], w_ref[...])


def _merge_split(x, hn, branches, w_gate_l, w_branch_l, w_out_l):
    n, d = x.shape
    tm = _pick_tile(n, ROW_TILE, BF16_ROWS)
    tn = GATE_COL_TILE
    m = pl.pallas_call(
        _gate_kernel,
        grid=(n // tm, d // tn),
        in_specs=[pl.BlockSpec((tm, d), lambda i, j: (i, 0))]
        + [pl.BlockSpec((tm, BRANCH), lambda i, j: (i, 0))] * HEADS + [
            pl.BlockSpec((HEADS, d, tn), lambda i, j: (0, 0, j)),
            pl.BlockSpec((HEADS, BRANCH, tn), lambda i, j: (0, 0, j)),
        ],
        out_specs=pl.BlockSpec((tm, tn), lambda i, j: (i, j)),
        out_shape=jax.ShapeDtypeStruct((n, d), BF),
        compiler_params=_cparams(("parallel", "arbitrary")),
        name="gate",
    )(hn, *branches, w_gate_l, w_branch_l)
    to = _pick_tile(n, ROW_TILE_SMALL, BF16_ROWS)
    return pl.pallas_call(
        _outproj_kernel,
        grid=(n // to,),
        in_specs=[
            pl.BlockSpec((to, d), lambda i: (i, 0)),
            pl.BlockSpec((d, d), lambda i: (0, 0), pipeline_mode=pl.Buffered(1)),
            pl.BlockSpec((to, d), lambda i: (i, 0)),
        ],
        out_specs=pl.BlockSpec((to, d), lambda i: (i, 0)),
        out_shape=jax.ShapeDtypeStruct((n, d), F32),
        compiler_params=_cparams(("parallel",)),
        name="outproj",
    )(m, w_out_l, x)


def _merge_cast_kernel(x_ref, g_ref, b0_ref, b1_ref, b2_ref, b3_ref, wgate32_ref, wbr32_ref, wo32_ref,
                       o_ref, wgate_ref, wbr_ref, wo_ref, hn_ref):
    wgate_ref[...] = wgate32_ref[...].astype(BF)
    wbr_ref[...] = wbr32_ref[...].astype(BF)
    wo_ref[...] = wo32_ref[...].astype(BF)
    _merge_kernel(x_ref, g_ref, b0_ref, b1_ref, b2_ref, b3_ref, wgate_ref, wbr_ref, wo_ref, o_ref, hn_ref)


def _merge_cast(x, g, branches, w_gate, w_branch, w_out, l):
    n, d = x.shape
    tn = MERGE_COL_TILE
    once = dict(pipeline_mode=pl.Buffered(1))
    return pl.pallas_call(
        _merge_cast_kernel,
        grid=(1, d // tn),
        in_specs=[
            pl.BlockSpec((n, d), lambda i, j: (0, 0), **once),
            pl.BlockSpec((1, d), lambda i, j: (0, 0)),
        ] + [pl.BlockSpec((n, BRANCH), lambda i, j: (0, 0), **once)] * HEADS + [
            pl.BlockSpec((None, HEADS, d, tn), lambda i, j: (l, 0, 0, j)),
            pl.BlockSpec((None, HEADS, BRANCH, tn), lambda i, j: (l, 0, 0, j)),
            pl.BlockSpec((None, tn, d), lambda i, j: (l, j, 0)),
        ],
        out_specs=[
            pl.BlockSpec((n, d), lambda i, j: (0, 0), **once),
            pl.BlockSpec((HEADS, d, tn), lambda i, j: (0, 0, j)),
            pl.BlockSpec((HEADS, BRANCH, tn), lambda i, j: (0, 0, j)),
            pl.BlockSpec((tn, d), lambda i, j: (j, 0)),
        ],
        out_shape=[
            jax.ShapeDtypeStruct((n, d), F32), jax.ShapeDtypeStruct((HEADS, d, d), BF),
            jax.ShapeDtypeStruct((HEADS, BRANCH, d), BF), jax.ShapeDtypeStruct((d, d), BF),
        ],
        scratch_shapes=[pltpu.VMEM((n, d), BF)],
        compiler_params=_cparams(("arbitrary", "arbitrary")),
        name="merge_cast",
    )(x, g, *branches, w_gate, w_branch, w_out)


def _final_norm_kernel(x_ref, g_ref, o_ref):
    o_ref[...] = _rms(x_ref[...], g_ref[...])


def _final_norm(x, g):
    n, d = x.shape
    tm = _pick_tile(n, ROW_TILE_SMALL, SUBLANES)
    return pl.pallas_call(
        _final_norm_kernel,
        grid=(n // tm,),
        in_specs=[pl.BlockSpec((tm, d), lambda i: (i, 0)), pl.BlockSpec((1, d), lambda i: (0, 0))],
        out_specs=pl.BlockSpec((tm, d), lambda i: (i, 0)),
        out_shape=jax.ShapeDtypeStruct((n, d), F32),
        compiler_params=_cparams(("parallel",)),
        name="final_norm",
    )(x, g)


def _conv4(xs_ref, j_seq, w, tp):
    xs = xs_ref[j_seq]
    y = xs[SUBLANES:] * w[CONV_W - 1:CONV_W, :]
    for back in range(1, CONV_W):
        y = y + pltpu.roll(xs, back, axis=0)[SUBLANES:] * w[CONV_W - 1 - back:CONV_W - back, :]
    return y


class _SeqLayout:
    def __init__(self, nb, tp, ts=None):
        self.nb, self.tp, self.ts = nb, tp, ts
        self.out_dtype = BF if ts is None else F32

    def load(self, ref, sq, rows=slice(None), cols=slice(None)):
        if self.ts is None:
            return ref[sq, rows, cols]
        x = ref[sq * self.ts:(sq + 1) * self.ts, cols]
        return jnp.concatenate([x, jnp.zeros((self.tp - self.ts, x.shape[1]), x.dtype)], axis=0)

    def store(self, ref, sq, rows, cols, val):
        if self.ts is None:
            ref[sq, rows, cols] = val.astype(ref.dtype)
        else:
            ref[sq * self.ts:(sq + 1) * self.ts, cols] = val[:self.ts]

    def for_groups(self, n, body):
        if self.ts is None:
            lax.fori_loop(0, n, body, 0)
        else:
            for g in range(n):
                body(g, 0)

    def pcol(self, w, off):
        if self.ts is None:
            return pl.BlockSpec((self.nb, self.tp, w), lambda b, t: (b, t, off // w))
        return pl.BlockSpec((self.nb * self.ts, w), lambda b, t: (b, off // w))

    def dims(self, p):
        return (p.shape[0], p.shape[1]) if self.ts is None else (p.shape[0] // self.ts, self.tp)

    def out_shape(self, p, w):
        return jax.ShapeDtypeStruct(p.shape[:-1] + (w,), self.out_dtype)


def _make_lru_kernel(lay, gb, tp, t_real, is_prompt):
    nb = lay.nb

    def kern(lx_ref, ly_ref, c0_ref, h0_ref, cw_ref, cb_ref, wa_ref, ba_ref, wx_ref, bx_ref, lam_ref,
             o_ref, cn_ref, hn_ref, xs_scr, a_scr, u_scr, cc_scr, ch_scr):
        tb = pl.program_id(1)

        @pl.when(tb == 0)
        def _():
            cc_scr[...] = c0_ref[...]
            ch_scr[...] = h0_ref[...]

        cw = cw_ref[...]
        neg_sp = -LRU_C * _softplus(-lam_ref[...])
        row = lax.broadcasted_iota(jnp.int32, (tp, 1), 0)

        def per_group(g, carry):
            seqs = [g * gb + j for j in range(gb)]
            for j, sq in enumerate(seqs):
                xs_scr[j, 0:SUBLANES, :] = cc_scr[sq]
                xs_scr[j, SUBLANES:SUBLANES + tp, :] = lay.load(lx_ref, sq)
            for j, sq in enumerate(seqs):
                xc = _conv4(xs_scr, j, cw, tp) + cb_ref[...]
                cc_scr[sq] = xs_scr[j, pl.ds(t_real, SUBLANES), :]
                xcb = xc.astype(BF)
                r_parts, i_parts = [], []
                for n in range(BRANCH // LRU_BD):
                    blk = xcb[:, n * LRU_BD:(n + 1) * LRU_BD]
                    r_parts.append(_dot(blk, wa_ref[n]))
                    i_parts.append(_dot(blk, wx_ref[n]))
                r = jax.nn.sigmoid(jnp.concatenate(r_parts, axis=1) + ba_ref[...])
                ig = jax.nn.sigmoid(jnp.concatenate(i_parts, axis=1) + bx_ref[...])
                log_a = r * neg_sp
                a = jnp.exp(log_a)
                mult = jnp.sqrt(-jnp.tanh(log_a) * (a * a + 1.0))
                if is_prompt:
                    mult = jnp.where(jnp.logical_and(row == 0, tb == 0), 1.0, mult)
                a_scr[j] = a
                u_scr[j] = mult * ig * xc

            def step(t, hs):
                out = []
                for j in range(gb):
                    h = a_scr[j, pl.ds(t, 1), :] * hs[j] + u_scr[j, pl.ds(t, 1), :]
                    a_scr[j, pl.ds(t, 1), :] = h
                    out.append(h)
                return tuple(out)

            h_last = lax.fori_loop(0, t_real, step, tuple(ch_scr[sq] for sq in seqs),
                                   unroll=True if t_real <= SUBLANES else 8)
            for j, sq in enumerate(seqs):
                ch_scr[sq] = h_last[j]
                lay.store(o_ref, sq, slice(None), slice(None), a_scr[j] * _gelu_tanh(lay.load(ly_ref, sq)))
            return carry

        lay.for_groups(nb // gb, per_group)
        cn_ref[...] = cc_scr[...]
        hn_ref[...] = ch_scr[...]

    return kern


def _mixer_specs(lay, l):
    wspec = lambda shape: pl.BlockSpec((None,) + shape, lambda b, t: (l,) + (0,) * len(shape))
    state = lambda *shape: pl.BlockSpec((lay.nb,) + shape, lambda b, t: (b,) + (0,) * len(shape))
    return wspec, lay.pcol, state


class _StateIO:
    def __init__(self, s0, stacked, l, nb):
        self.s0, self.l, self.nb = s0, l, nb
        self.stacked = stacked is not None
        self.prev = stacked[1] if self.stacked else None

    def spec(self):
        shape = self.s0.shape[2:] if self.stacked else self.s0.shape[1:]
        zeros = (0,) * len(shape)
        if self.stacked:
            l = self.l
            return pl.BlockSpec((None, self.nb) + shape, lambda b, t: (l, b) + zeros)
        return pl.BlockSpec((self.nb,) + shape, lambda b, t: (b,) + zeros)

    def out_shape(self):
        return jax.ShapeDtypeStruct(self.s0.shape, F32)

    def extra_inputs(self):
        return [self.prev] if self.prev is not None else []

    def extra_specs(self):
        return [pl.BlockSpec(memory_space=pl.ANY)] if self.prev is not None else []

    def aliases(self, n_in, out_idx):
        return {n_in: out_idx} if self.prev is not None else {}

    def wrap(self, kern, n_in):
        if self.prev is None:
            return kern
        return lambda *refs: kern(*refs[:n_in], *refs[n_in + 1:])


def _lru(p, c0, h0, lw, l, lay, gb, t_real, is_prompt):
    nb, tp = lay.nb, lay.tp
    nseq, tseq = lay.dims(p)
    c = BRANCH
    wspec, pcol, state = _mixer_specs(lay, l)
    return pl.pallas_call(
        _make_lru_kernel(lay, gb, tp, t_real, is_prompt),
        grid=(nseq // nb, tseq // tp),
        in_specs=[
            pcol(c, COL_LX), pcol(c, COL_LY), state(SUBLANES, c), state(1, c),
            wspec((CONV_W, c)), wspec((1, c)),
            wspec((c // LRU_BD, LRU_BD, LRU_BD)), wspec((1, c)),
            wspec((c // LRU_BD, LRU_BD, LRU_BD)), wspec((1, c)),
            wspec((1, c)),
        ],
        out_specs=[pcol(c, 0), state(SUBLANES, c), state(1, c)],
        out_shape=[
            lay.out_shape(p, c),
            jax.ShapeDtypeStruct((nseq, SUBLANES, c), F32),
            jax.ShapeDtypeStruct((nseq, 1, c), F32),
        ],
        scratch_shapes=[
            pltpu.VMEM((gb, tp + SUBLANES, c), F32), pltpu.VMEM((gb, tp, c), F32), pltpu.VMEM((gb, tp, c), F32),
            pltpu.VMEM((nb, SUBLANES, c), F32), pltpu.VMEM((nb, 1, c), F32),
        ],
        compiler_params=_cparams(("parallel", "arbitrary")),
        name="lru_prompt" if is_prompt else "lru_sample",
    )(p, p, c0, h0, lw["conv_w"], lw["conv_b"], lw["wa"], lw["ba"], lw["wx"], lw["bx"], lw["lam"])


def _chunk_consts(c, cl):
    row = lax.broadcasted_iota(jnp.int32, (c, c), 0)
    col = lax.broadcasted_iota(jnp.int32, (c, c), 1)
    tril = row >= col
    lower_ones = jnp.where(tril, 1.0, 0.0).astype(BF)
    real = lax.broadcasted_iota(jnp.int32, (c, 1), 0) < cl
    return row, col, tril, lower_ones, real


def _ks(h):
    return slice(h * DK, (h + 1) * DK)


def _vs(h):
    return slice(h * DV, (h + 1) * DV)


def _chunk_loop(lay, n_groups, nc, c, per_chunk):
    def body(idx, carry):
        g = idx // nc
        per_chunk(g, pl.ds(pl.multiple_of((idx - g * nc) * c, SUBLANES), c))
        return carry

    if lay.ts is None:
        lax.fori_loop(0, n_groups * nc, body, 0)
    else:
        assert nc == 1
        for g in range(n_groups):
            per_chunk(g, slice(None))


def _make_gla_kernel(lay, gb, tp, c, t_real):
    nb = lay.nb
    nc = tp // c
    cl = min(c, t_real)
    G, H = range(gb), range(HEADS)

    def kern(q_ref, k_ref, v_ref, gr_ref, sm_ref, s0_ref, wg_ref, bg_ref, ng_ref, o_ref, sn_ref, s_scr):
        @pl.when(pl.program_id(1) == 0)
        def _():
            s_scr[...] = s0_ref[...]

        _, _, tril, lower_ones, real = _chunk_consts(c, cl)
        ones_real = jnp.where(real, 1.0, 0.0).astype(BF) * jnp.ones((c, DV), BF)
        d_tn = lambda x: lax.dot_general(x, ones_real, (((0,), (0,)), ((), ())), preferred_element_type=F32)
        ng = ng_ref[...]
        wg = wg_ref[...]
        bg = bg_ref[...]

        def per_chunk(g, rows):
            seqs = [g * gb + j for j in G]
            lg = [-_softplus(-(_mm(lay.load(sm_ref, sq, rows), wg) + bg)) * (1.0 / GLA_TAU) for sq in seqs]
            sp = [_split3(x) for x in lg]
            b = [_dot(lower_ones, s[0]) + _dot(lower_ones, s[1]) + _dot(lower_ones, s[2]) for s in sp]
            b_last_col = [d_tn(s[0]) + d_tn(s[1]) + d_tn(s[2]) for s in sp]
            q_t = [lay.load(q_ref, sq, rows) * (DK ** -0.5) * jnp.exp(b[j]) for j, sq in enumerate(seqs)]
            k = [lay.load(k_ref, sq, rows) for sq in seqs]
            k_t = [jnp.where(real, k[j] * jnp.exp(-b[j]), 0.0) for j in G]
            k_d = [jnp.where(real, k[j] * jnp.exp(b[j][cl - 1:cl, :] - b[j]), 0.0) for j in G]
            v = [lay.load(v_ref, sq, rows) for sq in seqs]
            att = [[jnp.where(tril, _mm_nt(q_t[j][:, _ks(h)], k_t[j][:, _ks(h)]), 0.0) for h in H] for j in G]
            s = [[s_scr[seqs[j], h] for h in H] for j in G]
            kv = [[_mm_tn(k_d[j][:, _ks(h)], v[j][:, _vs(h)]) for h in H] for j in G]
            o = [[_mm(att[j][h], v[j][:, _vs(h)]) + _mm(q_t[j][:, _ks(h)], s[j][h]) for h in H] for j in G]
            for j in G:
                for h in H:
                    s_scr[seqs[j], h] = jnp.exp(b_last_col[j][_ks(h), :]) * s[j][h] + kv[j][h]
            for j in G:
                gr = lay.load(gr_ref, seqs[j], rows)
                for h in H:
                    lay.store(o_ref, seqs[j], rows, _vs(h), _rms(o[j][h], ng) * _silu(gr[:, _vs(h)]))

        _chunk_loop(lay, nb // gb, nc, c, per_chunk)
        sn_ref[...] = s_scr[...]

    return kern


def _gla(p, s0, lw, l, lay, gb, c, t_real, name, stacked=None):
    nb, tp = lay.nb, lay.tp
    nseq, tseq = lay.dims(p)
    qk = HEADS * DK
    vw = HEADS * DV
    wspec, pcol, _ = _mixer_specs(lay, l)
    sio = _StateIO(s0, stacked, l, nb)
    in_specs = [
        pcol(qk, COL_GQ), pcol(qk, COL_GK), pcol(vw, COL_GV), pcol(vw, COL_GR), pcol(128, COL_SM),
        sio.spec(), wspec((128, qk)), wspec((1, qk)), wspec((1, DV)),
    ]
    n_in = len(in_specs)
    return pl.pallas_call(
        sio.wrap(_make_gla_kernel(lay, gb, tp, c, t_real), n_in),
        grid=(nseq // nb, tseq // tp),
        in_specs=in_specs + sio.extra_specs(),
        out_specs=[pcol(vw, 0), sio.spec()],
        out_shape=[lay.out_shape(p, vw), sio.out_shape()],
        scratch_shapes=[pltpu.VMEM((nb, HEADS, DK, DV), F32)],
        input_output_aliases=sio.aliases(n_in, 1),
        compiler_params=_cparams(("parallel", "arbitrary")),
        name=name,
    )(p, p, p, p, p, s0, lw["gla_wg"], lw["gla_bg"], lw["gla_ng"], *sio.extra_inputs())


def _make_ret_kernel(lay, gb, tp, c, t_real):
    nb = lay.nb
    nc = tp // c
    cl = min(c, t_real)
    G, H = range(gb), range(HEADS)
    log_gamma = [float(np.log(np.float32(1.0) - np.float32(2.0) ** np.float32(-5.0 - h))) for h in range(HEADS)]

    def kern(q_ref, k_ref, qs_ref, ks_ref, v_ref, g_ref, cos_ref, sin_ref, s0_ref, ng_ref, o_ref, sn_ref, s_scr):
        @pl.when(pl.program_id(1) == 0)
        def _():
            s_scr[...] = s0_ref[...]

        row, col, tril, _, real = _chunk_consts(c, cl)
        diff = (row - col).astype(F32)
        ridx = lax.broadcasted_iota(jnp.int32, (c, 1), 0).astype(F32)
        ng = ng_ref[...]
        dmat = [jnp.where(tril, jnp.exp(jnp.maximum(diff, 0.0) * lgm), 0.0) for lgm in log_gamma]
        q_decay = [jnp.exp((ridx + 1.0) * lgm) for lgm in log_gamma]
        k_decay = [jnp.exp((cl - 1.0 - ridx) * lgm) for lgm in log_gamma]

        def per_chunk(g, rows):
            seqs = [g * gb + j for j in G]
            cos = cos_ref[rows, :]
            sin = sin_ref[rows, :]
            ld = lambda ref, sq: lay.load(ref, sq, rows)
            q = [ld(q_ref, sq) * cos + ld(qs_ref, sq) * sin for sq in seqs]
            k = [jnp.where(real, (ld(k_ref, sq) * cos + ld(ks_ref, sq) * sin) * (DK ** -0.5), 0.0) for sq in seqs]
            v = [ld(v_ref, sq) for sq in seqs]
            att = [[_mm_nt(q[j][:, _ks(h)], k[j][:, _ks(h)]) * dmat[h] for h in H] for j in G]
            s = [[s_scr[seqs[j], h] for h in H] for j in G]
            kv = [[_mm_tn(k[j][:, _ks(h)] * k_decay[h], v[j][:, _vs(h)]) for h in H] for j in G]
            o = [[_mm(att[j][h], v[j][:, _vs(h)]) + _mm(q[j][:, _ks(h)] * q_decay[h], s[j][h]) for h in H]
                 for j in G]
            for j in G:
                for h in H:
                    s_scr[seqs[j], h] = math.exp(cl * log_gamma[h]) * s[j][h] + kv[j][h]
            for j in G:
                gate = ld(g_ref, seqs[j])
                for h in H:
                    oc = o[j][h] - jnp.mean(o[j][h], axis=-1, keepdims=True)
                    on = oc * lax.rsqrt(jnp.mean(oc * oc, axis=-1, keepdims=True) + EPS) * ng
                    lay.store(o_ref, seqs[j], rows, _vs(h), on * _silu(gate[:, _vs(h)]))

        _chunk_loop(lay, nb // gb, nc, c, per_chunk)
        sn_ref[...] = s_scr[...]

    return kern


def _ret(p, s0, cos_t, sin_t, lw, l, lay, gb, c, t_real, name, stacked=None):
    nb, tp = lay.nb, lay.tp
    nseq, tseq = lay.dims(p)
    qk = HEADS * DK
    vw = HEADS * DV
    wspec, pcol, _ = _mixer_specs(lay, l)
    sio = _StateIO(s0, stacked, l, nb)
    in_specs = [
        pcol(qk, COL_RQ), pcol(qk, COL_RK), pcol(qk, COL_RQS), pcol(qk, COL_RKS), pcol(vw, COL_RV),
        pcol(vw, COL_RG),
        pl.BlockSpec((tp, qk), lambda b, t: (t, 0)),
        pl.BlockSpec((tp, qk), lambda b, t: (t, 0)),
        sio.spec(), wspec((1, DV)),
    ]
    n_in = len(in_specs)
    return pl.pallas_call(
        sio.wrap(_make_ret_kernel(lay, gb, tp, c, t_real), n_in),
        grid=(nseq // nb, tseq // tp),
        in_specs=in_specs + sio.extra_specs(),
        out_specs=[pcol(vw, 0), sio.spec()],
        out_shape=[lay.out_shape(p, vw), sio.out_shape()],
        scratch_shapes=[pltpu.VMEM((nb, HEADS, DK, DV), F32)],
        input_output_aliases=sio.aliases(n_in, 1),
        compiler_params=_cparams(("parallel", "arbitrary")),
        name=name,
    )(p, p, p, p, p, p, cos_t, sin_t, s0, lw["ret_ng"], *sio.extra_inputs())


def _unit_lower_inverse_many(ms, c, n_real):
    row = lax.broadcasted_iota(jnp.int32, (c, c), 0)
    col = lax.broadcasted_iota(jnp.int32, (c, c), 1)
    eye = jnp.where(row == col, 1.0, 0.0)

    def same_block(bits):
        return lax.shift_right_logical(row, bits) == lax.shift_right_logical(col, bits)

    in8 = same_block(3)
    n1 = [jnp.where(in8, m, 0.0) for m in ms]
    n2 = [_mm_hp(x, x) for x in n1]
    d = [eye - x for x in n1]
    d = [x + _mm_hp(x, y) for x, y in zip(d, n2)]
    if n_real > 4:
        n4 = [_mm_hp(x, x) for x in n2]
        d = [x + _mm_hp(x, y) for x, y in zip(d, n4)]
    bits = 3
    while (1 << bits) < c:
        sel = jnp.logical_and(same_block(bits + 1), jnp.logical_not(same_block(bits)))
        ld = [_mm(jnp.where(sel, m, 0.0), x) for m, x in zip(ms, d)]
        d = [x - _mm(x, y) for x, y in zip(d, ld)]
        bits += 1
    return d


def _make_dn_kernel(lay, gb, tp, c, t_real):
    nb = lay.nb
    nc = tp // c
    cl = min(c, t_real)
    G, H = range(gb), range(HEADS)
    GH = [(j, h) for j in G for h in H]

    def kern(x_ref, z_ref, sm_ref, c0_ref, s0_ref, cw_ref, alog_ref, dtb_ref, ng_ref,
             o_ref, cn_ref, sn_ref, xs_scr, qkv_scr, cc_scr, s_scr):
        @pl.when(pl.program_id(1) == 0)
        def _():
            cc_scr[...] = c0_ref[...]
            s_scr[...] = s0_ref[...]

        row, col, tril, lower_ones, real = _chunk_consts(c, cl)
        strict = row > col
        lane = lax.broadcasted_iota(jnp.int32, (c, 128), 1)
        sel = [jnp.where(lane == SM_DA + h, 1.0, 0.0).astype(BF) for h in H]
        d_nt = lambda a, x: lax.dot_general(a, x, (((1,), (1,)), ((), ())), preferred_element_type=F32)
        ng = ng_ref[...]
        cw = cw_ref[...]
        neg_a = -jnp.exp(alog_ref[...])
        dtb = dtb_ref[...]

        def per_group(g, carry):
            seqs = [g * gb + j for j in G]
            for j, sq in enumerate(seqs):
                xs_scr[j, 0:SUBLANES, :] = cc_scr[sq]
                xs_scr[j, SUBLANES:SUBLANES + tp, :] = lay.load(x_ref, sq)
            for j, sq in enumerate(seqs):
                qkv_scr[j] = _silu(_conv4(xs_scr, j, cw, tp))
                cc_scr[sq] = xs_scr[j, pl.ds(t_real, SUBLANES), :]

            def per_chunk(ci, carry2):
                rows = pl.ds(pl.multiple_of(ci * c, SUBLANES), c)
                sm = [lay.load(sm_ref, sq, rows) for sq in seqs]
                gcum = [_cumsum_rows(neg_a * _softplus(x + dtb), lower_ones) for x in sm]
                g3 = [_split3(x) for x in gcum]
                beta_all = [jnp.where(real, jax.nn.sigmoid(x), 0.0) for x in sm]

                def head_in(j, h, part):
                    x = qkv_scr[j, rows, part * BRANCH + h * DV:part * BRANCH + (h + 1) * DV]
                    return x

                qh = {jh: head_in(*jh, 0) for jh in GH}
                kh = {jh: head_in(*jh, 1) for jh in GH}
                vh = {jh: head_in(*jh, 2) for jh in GH}
                qh = {jh: x * lax.rsqrt(jnp.sum(x * x, axis=-1, keepdims=True) + EPS) * (DV ** -0.5)
                      for jh, x in qh.items()}
                kh = {jh: jnp.where(real, x * lax.rsqrt(jnp.sum(x * x, axis=-1, keepdims=True) + EPS), 0.0)
                      for jh, x in kh.items()}
                beta = {(j, h): beta_all[j][:, SM_DB + h:SM_DB + h + 1] for j, h in GH}
                gcol = {(j, h): gcum[j][:, SM_DA + h:SM_DA + h + 1] for j, h in GH}
                grow = {(j, h): d_nt(sel[h], g3[j][0]) + d_nt(sel[h], g3[j][1]) + d_nt(sel[h], g3[j][2])
                        for j, h in GH}
                decay = {jh: jnp.where(tril, jnp.exp(jnp.where(tril, gcol[jh] - grow[jh], 0.0)), 0.0) for jh in GH}
                e_g = {jh: jnp.exp(gcol[jh]) for jh in GH}
                kb = {jh: kh[jh] * beta[jh] for jh in GH}
                kq = {jh: _mm_nt(jnp.concatenate([kb[jh], qh[jh]], axis=0), kh[jh]) for jh in GH}
                tm = _unit_lower_inverse_many([jnp.where(strict, kq[jh][:c] * decay[jh], 0.0) for jh in GH], c, cl)
                tm = dict(zip(GH, tm))
                uw = {jh: _mm(tm[jh], jnp.concatenate([vh[jh] * beta[jh], kb[jh] * e_g[jh]], axis=1)) for jh in GH}
                s = {(j, h): s_scr[seqs[j], h] for j, h in GH}
                wq = {jh: _mm(jnp.concatenate([uw[jh][:, DV:], qh[jh] * e_g[jh]], axis=0), s[jh]) for jh in GH}
                v_new = {jh: uw[jh][:, :DV] - wq[jh][:c] for jh in GH}
                o = {jh: wq[jh][c:] + _mm(kq[jh][c:] * decay[jh], v_new[jh]) for jh in GH}
                for j, h in GH:
                    g_last = gcol[(j, h)][cl - 1:cl, :]
                    k_d = kh[(j, h)] * jnp.exp(g_last - gcol[(j, h)])
                    s_scr[seqs[j], h] = jnp.exp(g_last) * s[(j, h)] + _mm_tn(k_d, v_new[(j, h)])
                for j in G:
                    z = lay.load(z_ref, seqs[j], rows)
                    for h in H:
                        lay.store(o_ref, seqs[j], rows, _vs(h), _rms(o[(j, h)], ng) * _silu(z[:, _vs(h)]))
                return carry2

            lax.fori_loop(0, nc, per_chunk, 0)
            return carry

        lay.for_groups(nb // gb, per_group)
        cn_ref[...] = cc_scr[...]
        sn_ref[...] = s_scr[...]

    return kern


def _dn(p, c0, s0, lw, l, lay, gb, c, t_real, name, stacked=None):
    nb, tp = lay.nb, lay.tp
    nseq, tseq = lay.dims(p)
    cw3 = 3 * BRANCH
    wspec, pcol, state = _mixer_specs(lay, l)
    sio = _StateIO(s0, stacked, l, nb)
    in_specs = [
        pcol(cw3, COL_DQKV), pcol(BRANCH, COL_DZ), pcol(128, COL_SM),
        state(SUBLANES, cw3), sio.spec(),
        wspec((CONV_W, cw3)), wspec((1, 128)), wspec((1, 128)), wspec((1, DV)),
    ]
    n_in = len(in_specs)
    return pl.pallas_call(
        sio.wrap(_make_dn_kernel(lay, gb, tp, c, t_real), n_in),
        grid=(nseq // nb, tseq // tp),
        in_specs=in_specs + sio.extra_specs(),
        out_specs=[pcol(BRANCH, 0), state(SUBLANES, cw3), sio.spec()],
        out_shape=[
            lay.out_shape(p, BRANCH),
            jax.ShapeDtypeStruct((nseq, SUBLANES, cw3), F32),
            sio.out_shape(),
        ],
        scratch_shapes=[
            pltpu.VMEM((gb, tp + SUBLANES, cw3), F32), pltpu.VMEM((gb, tp, cw3), F32),
            pltpu.VMEM((nb, SUBLANES, cw3), F32), pltpu.VMEM((nb, HEADS, DV, DV), F32),
        ],
        input_output_aliases=sio.aliases(n_in, 2),
        compiler_params=_cparams(("parallel", "arbitrary")),
        name=name,
    )(p, p, p, c0, s0, lw["dn_conv_w"], lw["dn_alog"], lw["dn_dtb"], lw["dn_ng"], *sio.extra_inputs())


IN_SIZES = (BRANCH, BRANCH, HEADS * DK, HEADS * DK, BRANCH, BRANCH, GLA_RANK,
            HEADS * DK, HEADS * DK, BRANCH, BRANCH, BRANCH, BRANCH, BRANCH, BRANCH, HEADS, HEADS)


IN_OFFS = tuple(int(v) for v in np.concatenate([[0], np.cumsum(IN_SIZES)]))
PACK_ROWS = 256


def _pack_src_row(r):
    t_rq, t_swap, t_rv, t_small = (c // PACK_ROWS for c in (COL_RQ, COL_RQS, COL_RV, COL_SM))
    o_rq, o_rv = IN_OFFS[7], IN_OFFS[9]
    row = jnp.where(r < t_rq, r * PACK_ROWS,
                    jnp.where(r < t_swap, o_rq + (r - t_rq) * PACK_ROWS,
                              jnp.where(r < t_rv, o_rq + (r - t_swap) * PACK_ROWS,
                                        jnp.where(r < t_small, o_rv + (r - t_rv) * PACK_ROWS, 0))))
    return pl.multiple_of(row, 2 * SUBLANES)


def _pack_kernel(w_ref, glr_ref, dba_ref, o_ref):
    r = pl.program_id(1)
    t_swap, t_rv, t_small = (c // PACK_ROWS for c in (COL_RQS, COL_RV, COL_SM))
    w = w_ref[0]

    @pl.when(jnp.logical_and(jnp.logical_or(r < t_swap, r >= t_rv), r < t_small))
    def _():
        o_ref[...] = w.astype(BF)

    @pl.when(jnp.logical_and(r >= t_swap, r < t_rv))
    def _():
        parts = []
        for h in range(PACK_ROWS // DK):
            parts += [w[h * DK + DK // 2:(h + 1) * DK], w[h * DK:h * DK + DK // 2]]
        o_ref[...] = jnp.concatenate(parts, axis=0).astype(BF)

    @pl.when(r == t_small)
    def _():
        pad = jnp.zeros((PACK_ROWS - GLA_RANK - 2 * HEADS, w.shape[1]), w.dtype)
        o_ref[...] = jnp.concatenate([glr_ref[0], dba_ref[0], pad], axis=0).astype(BF)


def _pack_w_in(w_in_t):
    depth, n_in, d = w_in_t.shape
    assert n_in == IN_OFFS[-1] and N_PACK % PACK_ROWS == 0 and COL_SM // PACK_ROWS == N_PACK // PACK_ROWS - 1
    rows = lambda n, start: pl.BlockSpec((pl.Element(1), pl.Element(n), pl.Element(d)),
                                         lambda l, r: (l, start(r), 0))
    return pl.pallas_call(
        _pack_kernel,
        grid=(depth, N_PACK // PACK_ROWS),
        in_specs=[rows(PACK_ROWS, _pack_src_row), rows(GLA_RANK, lambda r: IN_OFFS[6]),
                  rows(2 * HEADS, lambda r: IN_OFFS[15])],
        out_specs=pl.BlockSpec((None, PACK_ROWS, d), lambda l, r: (l, r, 0)),
        out_shape=jax.ShapeDtypeStruct((depth, N_PACK, d), BF),
        compiler_params=_cparams(("parallel", "parallel")),
        name="pack_w_in",
    )(w_in_t, w_in_t, w_in_t)


def _rope_tables(pos):
    half = DK // 2
    inv = ROPE_BASE ** (-jnp.arange(half, dtype=F32) / half)
    ang = pos.astype(F32)[:, None] * inv[None, :]
    cos, sin = jnp.cos(ang), jnp.sin(ang)
    cos_t = jnp.tile(jnp.concatenate([cos, cos], axis=1), (1, HEADS))
    sin_t = jnp.tile(jnp.concatenate([-sin, sin], axis=1), (1, HEADS))
    return cos_t, sin_t


def _lane_row(vals, off):
    depth = vals.shape[0]
    return jnp.zeros((depth, 1, 128), F32).at[:, 0, off:off + HEADS].set(vals.astype(F32))


def kernel(x_prompt, x_sample, state_lru_conv, state_lru_h, state_gla, state_ret, state_dn_conv, state_dn,
           norm_g, final_norm_g, w_ff_in, w_ff_out, w_in, w_gate, w_branch, w_out,
           lru_conv_w, lru_conv_b, lru_wa, lru_ba, lru_wx, lru_bx, lru_lambda,
           gla_wg, gla_bg, gla_norm_g, ret_norm_g, dn_conv_w, dn_a_log, dn_dt_bias, dn_norm_g):
    bp, tpr, d = x_prompt.shape
    bs, ts, _ = x_sample.shape
    depth = w_in.shape[0]
    n_p, n_s = bp * tpr, bs * ts
    assert d == D_MODEL and ts <= SAMPLE_PAD and tpr % CHUNK == 0

    w_pack = _pack_w_in(jnp.swapaxes(w_in, 1, 2))
    wg_pad = jnp.zeros((depth, 128, HEADS * DK), F32).at[:, SM_GLR:SM_GLR + GLA_RANK, :].set(gla_wg).astype(BF)
    r3 = lambda a: a.reshape(a.shape[0], 1, a.shape[1])
    lw = {
        "conv_w": lru_conv_w, "conv_b": r3(lru_conv_b), "wa": lru_wa.astype(BF), "ba": r3(lru_ba),
        "wx": lru_wx.astype(BF), "bx": r3(lru_bx), "lam": r3(lru_lambda),
        "gla_wg": wg_pad, "gla_bg": r3(gla_bg), "gla_ng": r3(gla_norm_g), "ret_ng": r3(ret_norm_g),
        "dn_conv_w": dn_conv_w, "dn_alog": _lane_row(dn_a_log, SM_DA), "dn_dtb": _lane_row(dn_dt_bias, SM_DA),
        "dn_ng": r3(dn_norm_g),
    }
    norm_g4 = norm_g.reshape(depth, 3, 1, d)

    cos_p, sin_p = _rope_tables(jnp.arange(tpr))
    pos_s = jnp.where(jnp.arange(SAMPLE_PAD) < ts, PAST_LEN + jnp.arange(SAMPLE_PAD), 0)
    cos_s, sin_s = _rope_tables(pos_s)

    pad_conv = lambda a: jnp.pad(a, ((0, 0), (SUBLANES - (CONV_W - 1), 0), (0, 0)))
    zeros = lambda *s: jnp.zeros(s, F32)

    tp_p = _pick_tile(tpr, MIXER_TIME_TILE, CHUNK)
    a_p = dict(lay=_SeqLayout(bp, tp_p), gb=bp)
    nb_s = _pick_tile(bs, SAMPLE_SEQS, 2)
    assert (nb_s * ts) % SUBLANES == 0
    a_s = dict(lay=_SeqLayout(nb_s, SAMPLE_PAD, ts), gb=_pick_tile(nb_s, LOCKSTEP_SEQS, 1))

    fg = final_norm_g.reshape(1, d)
    xp = x_prompt.reshape(n_p, d)
    xs = x_sample.reshape(n_s, d)
    new_p = [[] for _ in range(6)]
    new_s = [[] for _ in range(6)]
    s_gla_s = s_ret_s = s_dn_s = None
    for l in range(depth):
        xs, *w_ffn = _ffn_cast(xs, norm_g4[l, 0], w_ff_in, w_ff_out, l, 0)
        xp = _ffn(xp, norm_g4[l, 0], *w_ffn)
        p_p, hn_p = _inproj(xp, norm_g4[l, 1], w_pack, l)
        p_p = p_p.reshape(bp, tpr, N_PACK)
        p_s, _ = _inproj(xs, norm_g4[l, 1], w_pack, l)

        o_lru_p, c_lru_p, h_lru_p = _lru(p_p, zeros(bp, SUBLANES, BRANCH), zeros(bp, 1, BRANCH), lw, l,
                                         t_real=tp_p, is_prompt=True, **a_p)
        o_gla_p, s_gla_p = _gla(p_p, zeros(bp, HEADS, DK, DV), lw, l, c=CHUNK, t_real=tp_p, name="gla_prompt", **a_p)
        o_ret_p, s_ret_p = _ret(p_p, zeros(bp, HEADS, DK, DV), cos_p, sin_p, lw, l, c=CHUNK, t_real=tp_p,
                                name="ret_prompt", **a_p)
        o_dn_p, c_dn_p, s_dn_p = _dn(p_p, zeros(bp, SUBLANES, 3 * BRANCH), zeros(bp, HEADS, DV, DV), lw, l,
                                     c=CHUNK, t_real=tp_p, name="dn_prompt", **a_p)

        o_lru_s, c_lru_s, h_lru_s = _lru(p_s, pad_conv(state_lru_conv[l]), state_lru_h[l][:, None, :], lw, l,
                                         t_real=ts, is_prompt=False, **a_s)
        o_gla_s, s_gla_s = _gla(p_s, state_gla, lw, l, c=SAMPLE_PAD, t_real=ts, name="gla_sample",
                                stacked=(depth, s_gla_s), **a_s)
        o_ret_s, s_ret_s = _ret(p_s, state_ret, cos_s, sin_s, lw, l, c=SAMPLE_PAD, t_real=ts, name="ret_sample",
                                stacked=(depth, s_ret_s), **a_s)
        o_dn_s, c_dn_s, s_dn_s = _dn(p_s, pad_conv(state_dn_conv[l]), state_dn, lw, l, c=SAMPLE_PAD, t_real=ts,
                                     name="dn_sample", stacked=(depth, s_dn_s), **a_s)

        flat_p = lambda o: o.reshape(n_p, BRANCH)
        xs, *w_merge = _merge_cast(xs, norm_g4[l, 1], [o_lru_s, o_gla_s, o_ret_s, o_dn_s], w_gate, w_branch, w_out, l)
        xp = _merge_split(xp, hn_p, [flat_p(o) for o in (o_lru_p, o_gla_p, o_ret_p, o_dn_p)], *w_merge)
        xs, *w_ffn = _ffn_cast(xs, norm_g4[l, 2], w_ff_in, w_ff_out, l, 1)
        xp = _ffn(xp, norm_g4[l, 2], *w_ffn, final_g=fg if l == depth - 1 else None)

        tail3 = lambda cwin: cwin[:, SUBLANES - (CONV_W - 1):, :]
        for i, v in enumerate((tail3(c_lru_p), h_lru_p[:, 0], s_gla_p, s_ret_p, tail3(c_dn_p), s_dn_p)):
            new_p[i].append(v)
        for i, v in ((0, tail3(c_lru_s)), (1, h_lru_s[:, 0]), (4, tail3(c_dn_s))):
            new_s[i].append(v)

    y_prompt = xp.reshape(bp, tpr, d)
    y_sample = _final_norm(xs, fg).reshape(bs, ts, d)
    sp = [jnp.stack(v) for v in new_p]
    s_small = {i: jnp.stack(new_s[i]) for i in (0, 1, 4)}
    return tuple([y_prompt, y_sample] + sp + [s_small[0], s_small[1], s_gla_s, s_ret_s, s_small[4], s_dn_s])
```

```python
import math

import numpy as np
import jax
import jax.numpy as jnp
from jax import lax
from jax.experimental import pallas as pl
from jax.experimental.pallas import tpu as pltpu

F32 = jnp.float32
BF = jnp.bfloat16
EPS = 1e-6

D_MODEL = 2048
BRANCH = 512
CONV_W = 4
HEADS = 4
DK = 64
DV = 128
GLA_RANK = 16
GLA_TAU = 16.0
LRU_C = 8.0
LRU_BD = 128
ROPE_BASE = 10000.0
CHUNK = 64
PAST_LEN = 16384

V7X_VMEM_BYTES = 64 * 1024 * 1024
VMEM_LIMIT = V7X_VMEM_BYTES - 8 * 1024 * 1024
SUBLANES = 8
BF16_ROWS = 16
LANES = 128
ROW_TILE = 1024
ROW_TILE_SMALL = 512
FFN_COL_TILE = 512
PROJ_COL_TILE = 1024
GATE_COL_TILE = 512
MERGE_COL_TILE = 256
MIXER_TIME_TILE = 256
SAMPLE_SEQS = 16
LOCKSTEP_SEQS = 4
SAMPLE_PAD = 8

COL_LX, COL_LY = 0, 512
COL_GQ, COL_GK, COL_GV, COL_GR = 1024, 1280, 1536, 2048
COL_RQ, COL_RK, COL_RQS, COL_RKS, COL_RV, COL_RG = 2560, 2816, 3072, 3328, 3584, 4096
COL_DQKV, COL_DZ = 4608, 6144
COL_SM = 6656
SM_GLR, SM_DB, SM_DA = 0, 16, 20
N_PACK = 6912


def _dot(a, b):
    return jnp.dot(a, b, preferred_element_type=F32)


def _mm(a, b):
    return _dot(a.astype(BF), b.astype(BF))


def _mm_nt(a, b):
    return lax.dot_general(a.astype(BF), b.astype(BF), (((1,), (1,)), ((), ())), preferred_element_type=F32)


def _mm_tn(a, b):
    return lax.dot_general(a.astype(BF), b.astype(BF), (((0,), (0,)), ((), ())), preferred_element_type=F32)


def _split3(x):
    hi = x.astype(BF)
    r = x - hi.astype(F32)
    mid = r.astype(BF)
    lo = (r - mid.astype(F32)).astype(BF)
    return hi, mid, lo


def _split2(x):
    hi = x.astype(BF)
    return hi, (x - hi.astype(F32)).astype(BF)


def _mm_hp(a, b):
    ah, al = _split2(a)
    bh, bl = _split2(b)
    return _dot(ah, bh) + _dot(ah, bl) + _dot(al, bh)


def _cumsum_rows(x, lower_ones):
    hi, mid, lo = _split3(x)
    return _dot(lower_ones, hi) + _dot(lower_ones, mid) + _dot(lower_ones, lo)


def _softplus(x):
    return jnp.maximum(x, 0.0) + jnp.log1p(jnp.exp(-jnp.abs(x)))


def _silu(x):
    return x * jax.nn.sigmoid(x)


def _gelu_tanh(x):
    return x * (0.5 * (1.0 + jnp.tanh(math.sqrt(2.0 / math.pi) * (x + 0.044715 * (x * x * x)))))


def _rms(x, g):
    return x * lax.rsqrt(jnp.mean(x * x, axis=-1, keepdims=True) + EPS) * g


def _pick_tile(n, target, align):
    best = None
    for t in range(align, min(n, target) + 1, align):
        if n % t == 0:
            best = t
    assert best is not None, (n, target, align)
    return best


def _cparams(sem):
    return pltpu.CompilerParams(dimension_semantics=sem, vmem_limit_bytes=VMEM_LIMIT)


def _ffn_kernel(x_ref, g_ref, wg_ref, wu_ref, wo_ref, o_ref, hn_ref):
    @pl.when(pl.program_id(1) == 0)
    def _():
        x = x_ref[...]
        hn_ref[...] = _rms(x, g_ref[...]).astype(BF)
        o_ref[...] = x

    hn = hn_ref[...]
    gate = _dot(hn, wg_ref[...])
    up = _dot(hn, wu_ref[...])
    act = (_silu(gate) * up * 0.5).astype(BF)
    o_ref[...] += _dot(act, wo_ref[...])


def _ffn_final_kernel(x_ref, g_ref, wg_ref, wu_ref, wo_ref, fg_ref, o_ref, hn_ref):
    _ffn_kernel(x_ref, g_ref, wg_ref, wu_ref, wo_ref, o_ref, hn_ref)

    @pl.when(pl.program_id(1) == pl.num_programs(1) - 1)
    def _():
        o_ref[...] = _rms(o_ref[...], fg_ref[...])


def _ffn(x, g, wg, wu, wo, final_g=None):
    n, d = x.shape
    f = wo.shape[0]
    tm = _pick_tile(n, ROW_TILE, BF16_ROWS)
    tf = _pick_tile(f, FFN_COL_TILE, LANES)
    final = [] if final_g is None else [final_g]
    return pl.pallas_call(
        _ffn_kernel if final_g is None else _ffn_final_kernel,
        grid=(n // tm, f // tf),
        in_specs=[
            pl.BlockSpec((tm, d), lambda i, j: (i, 0)),
            pl.BlockSpec((1, d), lambda i, j: (0, 0)),
            pl.BlockSpec((d, tf), lambda i, j: (0, j)),
            pl.BlockSpec((d, tf), lambda i, j: (0, j)),
            pl.BlockSpec((tf, d), lambda i, j: (j, 0)),
        ] + [pl.BlockSpec((1, d), lambda i, j: (0, 0))] * len(final),
        out_specs=pl.BlockSpec((tm, d), lambda i, j: (i, 0)),
        out_shape=jax.ShapeDtypeStruct((n, d), F32),
        scratch_shapes=[pltpu.VMEM((tm, d), BF)],
        compiler_params=_cparams(("parallel", "arbitrary")),
        name="ffn",
    )(x, g, wg, wu, wo, *final)


def _ffn_cast_kernel(x_ref, g_ref, wg32_ref, wu32_ref, wo32_ref, o_ref, wg_ref, wu_ref, wo_ref, hn_ref):
    wg_ref[...] = wg32_ref[...].astype(BF)
    wu_ref[...] = wu32_ref[...].astype(BF)
    wo_ref[...] = wo32_ref[...].astype(BF)
    _ffn_kernel(x_ref, g_ref, wg_ref, wu_ref, wo_ref, o_ref, hn_ref)


def _ffn_cast(x, g, w_in, w_out, l, s):
    n, d = x.shape
    f = w_out.shape[2]
    tf = _pick_tile(f, FFN_COL_TILE, LANES)
    nf = f // tf
    once = dict(pipeline_mode=pl.Buffered(1))
    return pl.pallas_call(
        _ffn_cast_kernel,
        grid=(1, nf),
        in_specs=[
            pl.BlockSpec((n, d), lambda i, j: (0, 0), **once),
            pl.BlockSpec((1, d), lambda i, j: (0, 0)),
            pl.BlockSpec((None, None, d, tf), lambda i, j: (l, s, 0, j)),
            pl.BlockSpec((None, None, d, tf), lambda i, j: (l, s, 0, j + nf)),
            pl.BlockSpec((None, None, tf, d), lambda i, j: (l, s, j, 0)),
        ],
        out_specs=[
            pl.BlockSpec((n, d), lambda i, j: (0, 0), **once),
            pl.BlockSpec((d, tf), lambda i, j: (0, j)),
            pl.BlockSpec((d, tf), lambda i, j: (0, j)),
            pl.BlockSpec((tf, d), lambda i, j: (j, 0)),
        ],
        out_shape=[
            jax.ShapeDtypeStruct((n, d), F32),
            jax.ShapeDtypeStruct((d, f), BF), jax.ShapeDtypeStruct((d, f), BF), jax.ShapeDtypeStruct((f, d), BF),
        ],
        scratch_shapes=[pltpu.VMEM((n, d), BF)],
        compiler_params=_cparams(("arbitrary", "arbitrary")),
        name="ffn_cast",
    )(x, g, w_in, w_in, w_out)


def _inproj_kernel(x_ref, g_ref, w_ref, o_ref, hn_ref):
    @pl.when(pl.program_id(1) == 0)
    def _():
        hn_ref[...] = _rms(x_ref[...], g_ref[...]).astype(BF)

    o_ref[...] = _mm_nt(hn_ref[...], w_ref[...])


def _inproj(x, g, w_pack, l):
    n, d = x.shape
    npk = w_pack.shape[1]
    tm = _pick_tile(n, ROW_TILE, BF16_ROWS)
    tn = _pick_tile(npk, PROJ_COL_TILE, LANES)
    return pl.pallas_call(
        _inproj_kernel,
        grid=(n // tm, npk // tn),
        in_specs=[
            pl.BlockSpec((tm, d), lambda i, j: (i, 0)),
            pl.BlockSpec((1, d), lambda i, j: (0, 0)),
            pl.BlockSpec((None, tn, d), lambda i, j: (l, j, 0)),
        ],
        out_specs=[pl.BlockSpec((tm, tn), lambda i, j: (i, j)), pl.BlockSpec((tm, d), lambda i, j: (i, 0))],
        out_shape=[jax.ShapeDtypeStruct((n, npk), F32), jax.ShapeDtypeStruct((n, d), BF)],
        compiler_params=_cparams(("parallel", "arbitrary")),
        name="inproj",
    )(x, g, w_pack)


def _merge_kernel(x_ref, g_ref, b0_ref, b1_ref, b2_ref, b3_ref, wgate_ref, wbr_ref, wo_ref, o_ref, hn_ref):
    @pl.when(pl.program_id(1) == 0)
    def _():
        x = x_ref[...]
        hn_ref[...] = _rms(x, g_ref[...]).astype(BF)
        o_ref[...] = x

    hn = hn_ref[...]
    m = None
    for n, b_ref in enumerate((b0_ref, b1_ref, b2_ref, b3_ref)):
        gate = jax.nn.sigmoid(_dot(hn, wgate_ref[n]))
        br = _dot(b_ref[...].astype(BF), wbr_ref[n])
        m = gate * br if m is None else m + gate * br
    o_ref[...] += _dot(m.astype(BF), wo_ref[...])


def _gate_kernel(hn_ref, b0_ref, b1_ref, b2_ref, b3_ref, wgate_ref, wbr_ref, m_ref):
    hn = hn_ref[...]
    m = None
    for n, b_ref in enumerate((b0_ref, b1_ref, b2_ref, b3_ref)):
        gate = jax.nn.sigmoid(_dot(hn, wgate_ref[n]))
        br = _dot(b_ref[...].astype(BF), wbr_ref[n])
        m = gate * br if m is None else m + gate * br
    m_ref[...] = m.astype(BF)


def _outproj_kernel(m_ref, w_ref, x_ref, o_ref):
    o_ref[...] = x_ref[...] + _dot(m_ref[...], w_ref[...])


def _merge_split(x, hn, branches, w_gate_l, w_branch_l, w_out_l):
    n, d = x.shape
    tm = _pick_tile(n, ROW_TILE, BF16_ROWS)
    tn = GATE_COL_TILE
    m = pl.pallas_call(
        _gate_kernel,
        grid=(n // tm, d // tn),
        in_specs=[pl.BlockSpec((tm, d), lambda i, j: (i, 0))]
        + [pl.BlockSpec((tm, BRANCH), lambda i, j: (i, 0))] * HEADS + [
            pl.BlockSpec((HEADS, d, tn), lambda i, j: (0, 0, j)),
            pl.BlockSpec((HEADS, BRANCH, tn), lambda i, j: (0, 0, j)),
        ],
        out_specs=pl.BlockSpec((tm, tn), lambda i, j: (i, j)),
        out_shape=jax.ShapeDtypeStruct((n, d), BF),
        compiler_params=_cparams(("parallel", "arbitrary")),
        name="gate",
    )(hn, *branches, w_gate_l, w_branch_l)
    to = _pick_tile(n, ROW_TILE_SMALL, BF16_ROWS)
    return pl.pallas_call(
        _outproj_kernel,
        grid=(n // to,),
        in_specs=[
            pl.BlockSpec((to, d), lambda i: (i, 0)),
            pl.BlockSpec((d, d), lambda i: (0, 0), pipeline_mode=pl.Buffered(1)),
            pl.BlockSpec((to, d), lambda i: (i, 0)),
        ],
        out_specs=pl.BlockSpec((to, d), lambda i: (i, 0)),
        out_shape=jax.ShapeDtypeStruct((n, d), F32),
        compiler_params=_cparams(("parallel",)),
        name="outproj",
    )(m, w_out_l, x)


def _merge_cast_kernel(x_ref, g_ref, b0_ref, b1_ref, b2_ref, b3_ref, wgate32_ref, wbr32_ref, wo32_ref,
                       o_ref, wgate_ref, wbr_ref, wo_ref, hn_ref):
    wgate_ref[...] = wgate32_ref[...].astype(BF)
    wbr_ref[...] = wbr32_ref[...].astype(BF)
    wo_ref[...] = wo32_ref[...].astype(BF)
    _merge_kernel(x_ref, g_ref, b0_ref, b1_ref, b2_ref, b3_ref, wgate_ref, wbr_ref, wo_ref, o_ref, hn_ref)


def _merge_cast(x, g, branches, w_gate, w_branch, w_out, l):
    n, d = x.shape
    tn = MERGE_COL_TILE
    once = dict(pipeline_mode=pl.Buffered(1))
    return pl.pallas_call(
        _merge_cast_kernel,
        grid=(1, d // tn),
        in_specs=[
            pl.BlockSpec((n, d), lambda i, j: (0, 0), **once),
            pl.BlockSpec((1, d), lambda i, j: (0, 0)),
        ] + [pl.BlockSpec((n, BRANCH), lambda i, j: (0, 0), **once)] * HEADS + [
            pl.BlockSpec((None, HEADS, d, tn), lambda i, j: (l, 0, 0, j)),
            pl.BlockSpec((None, HEADS, BRANCH, tn), lambda i, j: (l, 0, 0, j)),
            pl.BlockSpec((None, tn, d), lambda i, j: (l, j, 0)),
        ],
        out_specs=[
            pl.BlockSpec((n, d), lambda i, j: (0, 0), **once),
            pl.BlockSpec((HEADS, d, tn), lambda i, j: (0, 0, j)),
            pl.BlockSpec((HEADS, BRANCH, tn), lambda i, j: (0, 0, j)),
            pl.BlockSpec((tn, d), lambda i, j: (j, 0)),
        ],
        out_shape=[
            jax.ShapeDtypeStruct((n, d), F32), jax.ShapeDtypeStruct((HEADS, d, d), BF),
            jax.ShapeDtypeStruct((HEADS, BRANCH, d), BF), jax.ShapeDtypeStruct((d, d), BF),
        ],
        scratch_shapes=[pltpu.VMEM((n, d), BF)],
        compiler_params=_cparams(("arbitrary", "arbitrary")),
        name="merge_cast",
    )(x, g, *branches, w_gate, w_branch, w_out)


def _final_norm_kernel(x_ref, g_ref, o_ref):
    o_ref[...] = _rms(x_ref[...], g_ref[...])


def _final_norm(x, g):
    n, d = x.shape
    tm = _pick_tile(n, ROW_TILE_SMALL, SUBLANES)
    return pl.pallas_call(
        _final_norm_kernel,
        grid=(n // tm,),
        in_specs=[pl.BlockSpec((tm, d), lambda i: (i, 0)), pl.BlockSpec((1, d), lambda i: (0, 0))],
        out_specs=pl.BlockSpec((tm, d), lambda i: (i, 0)),
        out_shape=jax.ShapeDtypeStruct((n, d), F32),
        compiler_params=_cparams(("parallel",)),
        name="final_norm",
    )(x, g)


def _conv4(xs_ref, j_seq, w, tp):
    xs = xs_ref[j_seq]
    y = xs[SUBLANES:] * w[CONV_W - 1:CONV_W, :]
    for back in range(1, CONV_W):
        y = y + pltpu.roll(xs, back, axis=0)[SUBLANES:] * w[CONV_W - 1 - back:CONV_W - back, :]
    return y


class _SeqLayout:
    def __init__(self, nb, tp, ts=None):
        self.nb, self.tp, self.ts = nb, tp, ts
        self.out_dtype = BF if ts is None else F32

    def load(self, ref, sq, rows=slice(None), cols=slice(None)):
        if self.ts is None:
            return ref[sq, rows, cols]
        x = ref[sq * self.ts:(sq + 1) * self.ts, cols]
        return jnp.concatenate([x, jnp.zeros((self.tp - self.ts, x.shape[1]), x.dtype)], axis=0)

    def store(self, ref, sq, rows, cols, val):
        if self.ts is None:
            ref[sq, rows, cols] = val.astype(ref.dtype)
        else:
            ref[sq * self.ts:(sq + 1) * self.ts, cols] = val[:self.ts]

    def for_groups(self, n, body):
        if self.ts is None:
            lax.fori_loop(0, n, body, 0)
        else:
            for g in range(n):
                body(g, 0)

    def pcol(self, w, off):
        if self.ts is None:
            return pl.BlockSpec((self.nb, self.tp, w), lambda b, t: (b, t, off // w))
        return pl.BlockSpec((self.nb * self.ts, w), lambda b, t: (b, off // w))

    def dims(self, p):
        return (p.shape[0], p.shape[1]) if self.ts is None else (p.shape[0] // self.ts, self.tp)

    def out_shape(self, p, w):
        return jax.ShapeDtypeStruct(p.shape[:-1] + (w,), self.out_dtype)


def _make_lru_kernel(lay, gb, tp, t_real, is_prompt):
    nb = lay.nb

    def kern(lx_ref, ly_ref, c0_ref, h0_ref, cw_ref, cb_ref, wa_ref, ba_ref, wx_ref, bx_ref, lam_ref,
             o_ref, cn_ref, hn_ref, xs_scr, a_scr, u_scr, cc_scr, ch_scr):
        tb = pl.program_id(1)

        @pl.when(tb == 0)
        def _():
            cc_scr[...] = c0_ref[...]
            ch_scr[...] = h0_ref[...]

        cw = cw_ref[...]
        neg_sp = -LRU_C * _softplus(-lam_ref[...])
        row = lax.broadcasted_iota(jnp.int32, (tp, 1), 0)

        def per_group(g, carry):
            seqs = [g * gb + j for j in range(gb)]
            for j, sq in enumerate(seqs):
                xs_scr[j, 0:SUBLANES, :] = cc_scr[sq]
                xs_scr[j, SUBLANES:SUBLANES + tp, :] = lay.load(lx_ref, sq)
            for j, sq in enumerate(seqs):
                xc = _conv4(xs_scr, j, cw, tp) + cb_ref[...]
                cc_scr[sq] = xs_scr[j, pl.ds(t_real, SUBLANES), :]
                xcb = xc.astype(BF)
                r_parts, i_parts = [], []
                for n in range(BRANCH // LRU_BD):
                    blk = xcb[:, n * LRU_BD:(n + 1) * LRU_BD]
                    r_parts.append(_dot(blk, wa_ref[n]))
                    i_parts.append(_dot(blk, wx_ref[n]))
                r = jax.nn.sigmoid(jnp.concatenate(r_parts, axis=1) + ba_ref[...])
                ig = jax.nn.sigmoid(jnp.concatenate(i_parts, axis=1) + bx_ref[...])
                log_a = r * neg_sp
                a = jnp.exp(log_a)
                mult = jnp.sqrt(-jnp.tanh(log_a) * (a * a + 1.0))
                if is_prompt:
                    mult = jnp.where(jnp.logical_and(row == 0, tb == 0), 1.0, mult)
                a_scr[j] = a
                u_scr[j] = mult * ig * xc

            def step(t, hs):
                out = []
                for j in range(gb):
                    h = a_scr[j, pl.ds(t, 1), :] * hs[j] + u_scr[j, pl.ds(t, 1), :]
                    a_scr[j, pl.ds(t, 1), :] = h
                    out.append(h)
                return tuple(out)

            h_last = lax.fori_loop(0, t_real, step, tuple(ch_scr[sq] for sq in seqs),
                                   unroll=True if t_real <= SUBLANES else 8)
            for j, sq in enumerate(seqs):
                ch_scr[sq] = h_last[j]
                lay.store(o_ref, sq, slice(None), slice(None), a_scr[j] * _gelu_tanh(lay.load(ly_ref, sq)))
            return carry

        lay.for_groups(nb // gb, per_group)
        cn_ref[...] = cc_scr[...]
        hn_ref[...] = ch_scr[...]

    return kern


def _mixer_specs(lay, l):
    wspec = lambda shape: pl.BlockSpec((None,) + shape, lambda b, t: (l,) + (0,) * len(shape))
    state = lambda *shape: pl.BlockSpec((lay.nb,) + shape, lambda b, t: (b,) + (0,) * len(shape))
    return wspec, lay.pcol, state


class _StateIO:
    def __init__(self, s0, stacked, l, nb):
        self.s0, self.l, self.nb = s0, l, nb
        self.stacked = stacked is not None
        self.prev = stacked[1] if self.stacked else None

    def spec(self):
        shape = self.s0.shape[2:] if self.stacked else self.s0.shape[1:]
        zeros = (0,) * len(shape)
        if self.stacked:
            l = self.l
            return pl.BlockSpec((None, self.nb) + shape, lambda b, t: (l, b) + zeros)
        return pl.BlockSpec((self.nb,) + shape, lambda b, t: (b,) + zeros)

    def out_shape(self):
        return jax.ShapeDtypeStruct(self.s0.shape, F32)

    def extra_inputs(self):
        return [self.prev] if self.prev is not None else []

    def extra_specs(self):
        return [pl.BlockSpec(memory_space=pl.ANY)] if self.prev is not None else []

    def aliases(self, n_in, out_idx):
        return {n_in: out_idx} if self.prev is not None else {}

    def wrap(self, kern, n_in):
        if self.prev is None:
            return kern
        return lambda *refs: kern(*refs[:n_in], *refs[n_in + 1:])


class _Call:
    def __init__(self, kernel, grid, in_specs, out_specs, out_shape, scratch_shapes, args, name, aliases=None):
        self.kernel, self.grid, self.in_specs, self.out_specs = kernel, grid, in_specs, out_specs
        self.out_shape, self.scratch_shapes, self.args, self.name = out_shape, scratch_shapes, args, name
        self.aliases = aliases or {}

    def run(self):
        return _run_fused([self], self.name)[0]


def _run_fused(calls, name):
    assert all(c.grid == calls[0].grid for c in calls)
    n_in = [len(c.in_specs) for c in calls]
    n_out = [len(c.out_specs) for c in calls]
    n_scr = [len(c.scratch_shapes) for c in calls]
    in_off = [sum(n_in[:i]) for i in range(len(calls))]
    out_off = [sum(n_out[:i]) for i in range(len(calls))]

    def kern(*refs):
        ins, outs, scr = refs[:sum(n_in)], refs[sum(n_in):sum(n_in) + sum(n_out)], refs[sum(n_in) + sum(n_out):]
        s = 0
        for i, c in enumerate(calls):
            c.kernel(*ins[in_off[i]:in_off[i] + n_in[i]], *outs[out_off[i]:out_off[i] + n_out[i]],
                     *scr[s:s + n_scr[i]])
            s += n_scr[i]

    aliases = {in_off[i] + a: out_off[i] + b for i, c in enumerate(calls) for a, b in c.aliases.items()}
    outs = pl.pallas_call(
        kern,
        grid=calls[0].grid,
        in_specs=[sp for c in calls for sp in c.in_specs],
        out_specs=[sp for c in calls for sp in c.out_specs],
        out_shape=[sh for c in calls for sh in c.out_shape],
        scratch_shapes=[sh for c in calls for sh in c.scratch_shapes],
        input_output_aliases=aliases,
        compiler_params=_cparams(("parallel", "arbitrary")),
        name=name,
    )(*[a for c in calls for a in c.args])
    return [list(outs[out_off[i]:out_off[i] + n_out[i]]) for i in range(len(calls))]


def _lru(p, c0, h0, lw, l, lay, gb, t_real, is_prompt, defer=False):
    nb, tp = lay.nb, lay.tp
    nseq, tseq = lay.dims(p)
    c = BRANCH
    wspec, pcol, state = _mixer_specs(lay, l)
    call = _Call(
        _make_lru_kernel(lay, gb, tp, t_real, is_prompt),
        grid=(nseq // nb, tseq // tp),
        in_specs=[
            pcol(c, COL_LX), pcol(c, COL_LY), state(SUBLANES, c), state(1, c),
            wspec((CONV_W, c)), wspec((1, c)),
            wspec((c // LRU_BD, LRU_BD, LRU_BD)), wspec((1, c)),
            wspec((c // LRU_BD, LRU_BD, LRU_BD)), wspec((1, c)),
            wspec((1, c)),
        ],
        out_specs=[pcol(c, 0), state(SUBLANES, c), state(1, c)],
        out_shape=[
            lay.out_shape(p, c),
            jax.ShapeDtypeStruct((nseq, SUBLANES, c), F32),
            jax.ShapeDtypeStruct((nseq, 1, c), F32),
        ],
        scratch_shapes=[
            pltpu.VMEM((gb, tp + SUBLANES, c), F32), pltpu.VMEM((gb, tp, c), F32), pltpu.VMEM((gb, tp, c), F32),
            pltpu.VMEM((nb, SUBLANES, c), F32), pltpu.VMEM((nb, 1, c), F32),
        ],
        args=(p, p, c0, h0, lw["conv_w"], lw["conv_b"], lw["wa"], lw["ba"], lw["wx"], lw["bx"], lw["lam"]),
        name="lru_prompt" if is_prompt else "lru_sample",
    )
    return call if defer else call.run()


def _chunk_consts(c, cl):
    row = lax.broadcasted_iota(jnp.int32, (c, c), 0)
    col = lax.broadcasted_iota(jnp.int32, (c, c), 1)
    tril = row >= col
    lower_ones = jnp.where(tril, 1.0, 0.0).astype(BF)
    real = lax.broadcasted_iota(jnp.int32, (c, 1), 0) < cl
    return row, col, tril, lower_ones, real


def _ks(h):
    return slice(h * DK, (h + 1) * DK)


def _vs(h):
    return slice(h * DV, (h + 1) * DV)


def _chunk_loop(lay, n_groups, nc, c, per_chunk):
    def body(idx, carry):
        g = idx // nc
        per_chunk(g, pl.ds(pl.multiple_of((idx - g * nc) * c, SUBLANES), c))
        return carry

    if lay.ts is None:
        lax.fori_loop(0, n_groups * nc, body, 0)
    else:
        assert nc == 1
        for g in range(n_groups):
            per_chunk(g, slice(None))


def _make_gla_kernel(lay, gb, tp, c, t_real):
    nb = lay.nb
    nc = tp // c
    cl = min(c, t_real)
    G, H = range(gb), range(HEADS)

    def kern(q_ref, k_ref, v_ref, gr_ref, sm_ref, s0_ref, wg_ref, bg_ref, ng_ref, o_ref, sn_ref, s_scr):
        @pl.when(pl.program_id(1) == 0)
        def _():
            s_scr[...] = s0_ref[...]

        _, _, tril, lower_ones, real = _chunk_consts(c, cl)
        ones_real = jnp.where(real, 1.0, 0.0).astype(BF) * jnp.ones((c, DV), BF)
        d_tn = lambda x: lax.dot_general(x, ones_real, (((0,), (0,)), ((), ())), preferred_element_type=F32)
        ng = ng_ref[...]
        wg = wg_ref[...]
        bg = bg_ref[...]

        def per_chunk(g, rows):
            seqs = [g * gb + j for j in G]
            lg = [-_softplus(-(_mm(lay.load(sm_ref, sq, rows), wg) + bg)) * (1.0 / GLA_TAU) for sq in seqs]
            sp = [_split3(x) for x in lg]
            b = [_dot(lower_ones, s[0]) + _dot(lower_ones, s[1]) + _dot(lower_ones, s[2]) for s in sp]
            b_last_col = [d_tn(s[0]) + d_tn(s[1]) + d_tn(s[2]) for s in sp]
            q_t = [lay.load(q_ref, sq, rows) * (DK ** -0.5) * jnp.exp(b[j]) for j, sq in enumerate(seqs)]
            k = [lay.load(k_ref, sq, rows) for sq in seqs]
            k_t = [jnp.where(real, k[j] * jnp.exp(-b[j]), 0.0) for j in G]
            k_d = [jnp.where(real, k[j] * jnp.exp(b[j][cl - 1:cl, :] - b[j]), 0.0) for j in G]
            v = [lay.load(v_ref, sq, rows) for sq in seqs]
            att = [[jnp.where(tril, _mm_nt(q_t[j][:, _ks(h)], k_t[j][:, _ks(h)]), 0.0) for h in H] for j in G]
            s = [[s_scr[seqs[j], h] for h in H] for j in G]
            kv = [[_mm_tn(k_d[j][:, _ks(h)], v[j][:, _vs(h)]) for h in H] for j in G]
            o = [[_mm(att[j][h], v[j][:, _vs(h)]) + _mm(q_t[j][:, _ks(h)], s[j][h]) for h in H] for j in G]
            for j in G:
                for h in H:
                    s_scr[seqs[j], h] = jnp.exp(b_last_col[j][_ks(h), :]) * s[j][h] + kv[j][h]
            for j in G:
                gr = lay.load(gr_ref, seqs[j], rows)
                for h in H:
                    lay.store(o_ref, seqs[j], rows, _vs(h), _rms(o[j][h], ng) * _silu(gr[:, _vs(h)]))

        _chunk_loop(lay, nb // gb, nc, c, per_chunk)
        sn_ref[...] = s_scr[...]

    return kern


def _gla(p, s0, lw, l, lay, gb, c, t_real, name, stacked=None, defer=False):
    nb, tp = lay.nb, lay.tp
    nseq, tseq = lay.dims(p)
    qk = HEADS * DK
    vw = HEADS * DV
    wspec, pcol, _ = _mixer_specs(lay, l)
    sio = _StateIO(s0, stacked, l, nb)
    in_specs = [
        pcol(qk, COL_GQ), pcol(qk, COL_GK), pcol(vw, COL_GV), pcol(vw, COL_GR), pcol(128, COL_SM),
        sio.spec(), wspec((128, qk)), wspec((1, qk)), wspec((1, DV)),
    ]
    n_in = len(in_specs)
    call = _Call(
        sio.wrap(_make_gla_kernel(lay, gb, tp, c, t_real), n_in),
        grid=(nseq // nb, tseq // tp),
        in_specs=in_specs + sio.extra_specs(),
        out_specs=[pcol(vw, 0), sio.spec()],
        out_shape=[lay.out_shape(p, vw), sio.out_shape()],
        scratch_shapes=[pltpu.VMEM((nb, HEADS, DK, DV), F32)],
        args=(p, p, p, p, p, s0, lw["gla_wg"], lw["gla_bg"], lw["gla_ng"], *sio.extra_inputs()),
        name=name,
        aliases=sio.aliases(n_in, 1),
    )
    return call if defer else call.run()


def _make_ret_kernel(lay, gb, tp, c, t_real):
    nb = lay.nb
    nc = tp // c
    cl = min(c, t_real)
    G, H = range(gb), range(HEADS)
    log_gamma = [float(np.log(np.float32(1.0) - np.float32(2.0) ** np.float32(-5.0 - h))) for h in range(HEADS)]

    def kern(q_ref, k_ref, qs_ref, ks_ref, v_ref, g_ref, cos_ref, sin_ref, s0_ref, ng_ref, o_ref, sn_ref, s_scr):
        @pl.when(pl.program_id(1) == 0)
        def _():
            s_scr[...] = s0_ref[...]

        row, col, tril, _, real = _chunk_consts(c, cl)
        diff = (row - col).astype(F32)
        ridx = lax.broadcasted_iota(jnp.int32, (c, 1), 0).astype(F32)
        ng = ng_ref[...]
        dmat = [jnp.where(tril, jnp.exp(jnp.maximum(diff, 0.0) * lgm), 0.0) for lgm in log_gamma]
        q_decay = [jnp.exp((ridx + 1.0) * lgm) for lgm in log_gamma]
        k_decay = [jnp.exp((cl - 1.0 - ridx) * lgm) for lgm in log_gamma]

        def per_chunk(g, rows):
            seqs = [g * gb + j for j in G]
            cos = cos_ref[rows, :]
            sin = sin_ref[rows, :]
            ld = lambda ref, sq: lay.load(ref, sq, rows)
            q = [ld(q_ref, sq) * cos + ld(qs_ref, sq) * sin for sq in seqs]
            k = [jnp.where(real, (ld(k_ref, sq) * cos + ld(ks_ref, sq) * sin) * (DK ** -0.5), 0.0) for sq in seqs]
            v = [ld(v_ref, sq) for sq in seqs]
            att = [[_mm_nt(q[j][:, _ks(h)], k[j][:, _ks(h)]) * dmat[h] for h in H] for j in G]
            s = [[s_scr[seqs[j], h] for h in H] for j in G]
            kv = [[_mm_tn(k[j][:, _ks(h)] * k_decay[h], v[j][:, _vs(h)]) for h in H] for j in G]
            o = [[_mm(att[j][h], v[j][:, _vs(h)]) + _mm(q[j][:, _ks(h)] * q_decay[h], s[j][h]) for h in H]
                 for j in G]
            for j in G:
                for h in H:
                    s_scr[seqs[j], h] = math.exp(cl * log_gamma[h]) * s[j][h] + kv[j][h]
            for j in G:
                gate = ld(g_ref, seqs[j])
                for h in H:
                    oc = o[j][h] - jnp.mean(o[j][h], axis=-1, keepdims=True)
                    on = oc * lax.rsqrt(jnp.mean(oc * oc, axis=-1, keepdims=True) + EPS) * ng
                    lay.store(o_ref, seqs[j], rows, _vs(h), on * _silu(gate[:, _vs(h)]))

        _chunk_loop(lay, nb // gb, nc, c, per_chunk)
        sn_ref[...] = s_scr[...]

    return kern


def _ret(p, s0, cos_t, sin_t, lw, l, lay, gb, c, t_real, name, stacked=None, defer=False):
    nb, tp = lay.nb, lay.tp
    nseq, tseq = lay.dims(p)
    qk = HEADS * DK
    vw = HEADS * DV
    wspec, pcol, _ = _mixer_specs(lay, l)
    sio = _StateIO(s0, stacked, l, nb)
    in_specs = [
        pcol(qk, COL_RQ), pcol(qk, COL_RK), pcol(qk, COL_RQS), pcol(qk, COL_RKS), pcol(vw, COL_RV),
        pcol(vw, COL_RG),
        pl.BlockSpec((tp, qk), lambda b, t: (t, 0)),
        pl.BlockSpec((tp, qk), lambda b, t: (t, 0)),
        sio.spec(), wspec((1, DV)),
    ]
    n_in = len(in_specs)
    call = _Call(
        sio.wrap(_make_ret_kernel(lay, gb, tp, c, t_real), n_in),
        grid=(nseq // nb, tseq // tp),
        in_specs=in_specs + sio.extra_specs(),
        out_specs=[pcol(vw, 0), sio.spec()],
        out_shape=[lay.out_shape(p, vw), sio.out_shape()],
        scratch_shapes=[pltpu.VMEM((nb, HEADS, DK, DV), F32)],
        args=(p, p, p, p, p, p, cos_t, sin_t, s0, lw["ret_ng"], *sio.extra_inputs()),
        name=name,
        aliases=sio.aliases(n_in, 1),
    )
    return call if defer else call.run()


def _unit_lower_inverse_many(ms, c, n_real):
    row = lax.broadcasted_iota(jnp.int32, (c, c), 0)
    col = lax.broadcasted_iota(jnp.int32, (c, c), 1)
    eye = jnp.where(row == col, 1.0, 0.0)

    def same_block(bits):
        return lax.shift_right_logical(row, bits) == lax.shift_right_logical(col, bits)

    in8 = same_block(3)
    n1 = [jnp.where(in8, m, 0.0) for m in ms]
    n2 = [_mm_hp(x, x) for x in n1]
    d = [eye - x for x in n1]
    d = [x + _mm_hp(x, y) for x, y in zip(d, n2)]
    if n_real > 4:
        n4 = [_mm_hp(x, x) for x in n2]
        d = [x + _mm_hp(x, y) for x, y in zip(d, n4)]
    bits = 3
    while (1 << bits) < c:
        sel = jnp.logical_and(same_block(bits + 1), jnp.logical_not(same_block(bits)))
        ld = [_mm(jnp.where(sel, m, 0.0), x) for m, x in zip(ms, d)]
        d = [x - _mm(x, y) for x, y in zip(d, ld)]
        bits += 1
    return d


def _make_dn_kernel(lay, gb, tp, c, t_real):
    nb = lay.nb
    nc = tp // c
    cl = min(c, t_real)
    G, H = range(gb), range(HEADS)
    GH = [(j, h) for j in G for h in H]

    def kern(x_ref, z_ref, sm_ref, c0_ref, s0_ref, cw_ref, alog_ref, dtb_ref, ng_ref,
             o_ref, cn_ref, sn_ref, xs_scr, qkv_scr, cc_scr, s_scr):
        @pl.when(pl.program_id(1) == 0)
        def _():
            cc_scr[...] = c0_ref[...]
            s_scr[...] = s0_ref[...]

        row, col, tril, lower_ones, real = _chunk_consts(c, cl)
        strict = row > col
        lane = lax.broadcasted_iota(jnp.int32, (c, 128), 1)
        sel = [jnp.where(lane == SM_DA + h, 1.0, 0.0).astype(BF) for h in H]
        d_nt = lambda a, x: lax.dot_general(a, x, (((1,), (1,)), ((), ())), preferred_element_type=F32)
        ng = ng_ref[...]
        cw = cw_ref[...]
        neg_a = -jnp.exp(alog_ref[...])
        dtb = dtb_ref[...]

        def per_group(g, carry):
            seqs = [g * gb + j for j in G]
            for j, sq in enumerate(seqs):
                xs_scr[j, 0:SUBLANES, :] = cc_scr[sq]
                xs_scr[j, SUBLANES:SUBLANES + tp, :] = lay.load(x_ref, sq)
            for j, sq in enumerate(seqs):
                qkv_scr[j] = _silu(_conv4(xs_scr, j, cw, tp))
                cc_scr[sq] = xs_scr[j, pl.ds(t_real, SUBLANES), :]

            def per_chunk(ci, carry2):
                rows = pl.ds(pl.multiple_of(ci * c, SUBLANES), c)
                sm = [lay.load(sm_ref, sq, rows) for sq in seqs]
                gcum = [_cumsum_rows(neg_a * _softplus(x + dtb), lower_ones) for x in sm]
                g3 = [_split3(x) for x in gcum]
                beta_all = [jnp.where(real, jax.nn.sigmoid(x), 0.0) for x in sm]

                def head_in(j, h, part):
                    x = qkv_scr[j, rows, part * BRANCH + h * DV:part * BRANCH + (h + 1) * DV]
                    return x

                qh = {jh: head_in(*jh, 0) for jh in GH}
                kh = {jh: head_in(*jh, 1) for jh in GH}
                vh = {jh: head_in(*jh, 2) for jh in GH}
                qh = {jh: x * lax.rsqrt(jnp.sum(x * x, axis=-1, keepdims=True) + EPS) * (DV ** -0.5)
                      for jh, x in qh.items()}
                kh = {jh: jnp.where(real, x * lax.rsqrt(jnp.sum(x * x, axis=-1, keepdims=True) + EPS), 0.0)
                      for jh, x in kh.items()}
                beta = {(j, h): beta_all[j][:, SM_DB + h:SM_DB + h + 1] for j, h in GH}
                gcol = {(j, h): gcum[j][:, SM_DA + h:SM_DA + h + 1] for j, h in GH}
                grow = {(j, h): d_nt(sel[h], g3[j][0]) + d_nt(sel[h], g3[j][1]) + d_nt(sel[h], g3[j][2])
                        for j, h in GH}
                decay = {jh: jnp.where(tril, jnp.exp(jnp.where(tril, gcol[jh] - grow[jh], 0.0)), 0.0) for jh in GH}
                e_g = {jh: jnp.exp(gcol[jh]) for jh in GH}
                kb = {jh: kh[jh] * beta[jh] for jh in GH}
                kq = {jh: _mm_nt(jnp.concatenate([kb[jh], qh[jh]], axis=0), kh[jh]) for jh in GH}
                tm = _unit_lower_inverse_many([jnp.where(strict, kq[jh][:c] * decay[jh], 0.0) for jh in GH], c, cl)
                tm = dict(zip(GH, tm))
                uw = {jh: _mm(tm[jh], jnp.concatenate([vh[jh] * beta[jh], kb[jh] * e_g[jh]], axis=1)) for jh in GH}
                s = {(j, h): s_scr[seqs[j], h] for j, h in GH}
                wq = {jh: _mm(jnp.concatenate([uw[jh][:, DV:], qh[jh] * e_g[jh]], axis=0), s[jh]) for jh in GH}
                v_new = {jh: uw[jh][:, :DV] - wq[jh][:c] for jh in GH}
                o = {jh: wq[jh][c:] + _mm(kq[jh][c:] * decay[jh], v_new[jh]) for jh in GH}
                for j, h in GH:
                    g_last = gcol[(j, h)][cl - 1:cl, :]
                    k_d = kh[(j, h)] * jnp.exp(g_last - gcol[(j, h)])
                    s_scr[seqs[j], h] = jnp.exp(g_last) * s[(j, h)] + _mm_tn(k_d, v_new[(j, h)])
                for j in G:
                    z = lay.load(z_ref, seqs[j], rows)
                    for h in H:
                        lay.store(o_ref, seqs[j], rows, _vs(h), _rms(o[(j, h)], ng) * _silu(z[:, _vs(h)]))
                return carry2

            lax.fori_loop(0, nc, per_chunk, 0)
            return carry

        lay.for_groups(nb // gb, per_group)
        cn_ref[...] = cc_scr[...]
        sn_ref[...] = s_scr[...]

    return kern


def _dn(p, c0, s0, lw, l, lay, gb, c, t_real, name, stacked=None, defer=False):
    nb, tp = lay.nb, lay.tp
    nseq, tseq = lay.dims(p)
    cw3 = 3 * BRANCH
    wspec, pcol, state = _mixer_specs(lay, l)
    sio = _StateIO(s0, stacked, l, nb)
    in_specs = [
        pcol(cw3, COL_DQKV), pcol(BRANCH, COL_DZ), pcol(128, COL_SM),
        state(SUBLANES, cw3), sio.spec(),
        wspec((CONV_W, cw3)), wspec((1, 128)), wspec((1, 128)), wspec((1, DV)),
    ]
    n_in = len(in_specs)
    call = _Call(
        sio.wrap(_make_dn_kernel(lay, gb, tp, c, t_real), n_in),
        grid=(nseq // nb, tseq // tp),
        in_specs=in_specs + sio.extra_specs(),
        out_specs=[pcol(BRANCH, 0), state(SUBLANES, cw3), sio.spec()],
        out_shape=[
            lay.out_shape(p, BRANCH),
            jax.ShapeDtypeStruct((nseq, SUBLANES, cw3), F32),
            sio.out_shape(),
        ],
        scratch_shapes=[
            pltpu.VMEM((gb, tp + SUBLANES, cw3), F32), pltpu.VMEM((gb, tp, cw3), F32),
            pltpu.VMEM((nb, SUBLANES, cw3), F32), pltpu.VMEM((nb, HEADS, DV, DV), F32),
        ],
        args=(p, p, p, c0, s0, lw["dn_conv_w"], lw["dn_alog"], lw["dn_dtb"], lw["dn_ng"], *sio.extra_inputs()),
        name=name,
        aliases=sio.aliases(n_in, 2),
    )
    return call if defer else call.run()


IN_SIZES = (BRANCH, BRANCH, HEADS * DK, HEADS * DK, BRANCH, BRANCH, GLA_RANK,
            HEADS * DK, HEADS * DK, BRANCH, BRANCH, BRANCH, BRANCH, BRANCH, BRANCH, HEADS, HEADS)


IN_OFFS = tuple(int(v) for v in np.concatenate([[0], np.cumsum(IN_SIZES)]))
PACK_ROWS = 256


def _pack_src_row(r):
    t_rq, t_swap, t_rv, t_small = (c // PACK_ROWS for c in (COL_RQ, COL_RQS, COL_RV, COL_SM))
    o_rq, o_rv = IN_OFFS[7], IN_OFFS[9]
    row = jnp.where(r < t_rq, r * PACK_ROWS,
                    jnp.where(r < t_swap, o_rq + (r - t_rq) * PACK_ROWS,
                              jnp.where(r < t_rv, o_rq + (r - t_swap) * PACK_ROWS,
                                        jnp.where(r < t_small, o_rv + (r - t_rv) * PACK_ROWS, 0))))
    return pl.multiple_of(row, 2 * SUBLANES)


def _pack_kernel(w_ref, glr_ref, dba_ref, o_ref):
    r = pl.program_id(1)
    t_swap, t_rv, t_small = (c // PACK_ROWS for c in (COL_RQS, COL_RV, COL_SM))
    w = w_ref[0]

    @pl.when(jnp.logical_and(jnp.logical_or(r < t_swap, r >= t_rv), r < t_small))
    def _():
        o_ref[...] = w.astype(BF)

    @pl.when(jnp.logical_and(r >= t_swap, r < t_rv))
    def _():
        parts = []
        for h in range(PACK_ROWS // DK):
            parts += [w[h * DK + DK // 2:(h + 1) * DK], w[h * DK:h * DK + DK // 2]]
        o_ref[...] = jnp.concatenate(parts, axis=0).astype(BF)

    @pl.when(r == t_small)
    def _():
        pad = jnp.zeros((PACK_ROWS - GLA_RANK - 2 * HEADS, w.shape[1]), w.dtype)
        o_ref[...] = jnp.concatenate([glr_ref[0], dba_ref[0], pad], axis=0).astype(BF)


def _pack_w_in(w_in_t):
    depth, n_in, d = w_in_t.shape
    assert n_in == IN_OFFS[-1] and N_PACK % PACK_ROWS == 0 and COL_SM // PACK_ROWS == N_PACK // PACK_ROWS - 1
    rows = lambda n, start: pl.BlockSpec((pl.Element(1), pl.Element(n), pl.Element(d)),
                                         lambda l, r: (l, start(r), 0))
    return pl.pallas_call(
        _pack_kernel,
        grid=(depth, N_PACK // PACK_ROWS),
        in_specs=[rows(PACK_ROWS, _pack_src_row), rows(GLA_RANK, lambda r: IN_OFFS[6]),
                  rows(2 * HEADS, lambda r: IN_OFFS[15])],
        out_specs=pl.BlockSpec((None, PACK_ROWS, d), lambda l, r: (l, r, 0)),
        out_shape=jax.ShapeDtypeStruct((depth, N_PACK, d), BF),
        compiler_params=_cparams(("parallel", "parallel")),
        name="pack_w_in",
    )(w_in_t, w_in_t, w_in_t)


def _rope_tables(pos):
    half = DK // 2
    inv = ROPE_BASE ** (-jnp.arange(half, dtype=F32) / half)
    ang = pos.astype(F32)[:, None] * inv[None, :]
    cos, sin = jnp.cos(ang), jnp.sin(ang)
    cos_t = jnp.tile(jnp.concatenate([cos, cos], axis=1), (1, HEADS))
    sin_t = jnp.tile(jnp.concatenate([-sin, sin], axis=1), (1, HEADS))
    return cos_t, sin_t


def _lane_row(vals, off):
    depth = vals.shape[0]
    return jnp.zeros((depth, 1, 128), F32).at[:, 0, off:off + HEADS].set(vals.astype(F32))


def kernel(x_prompt, x_sample, state_lru_conv, state_lru_h, state_gla, state_ret, state_dn_conv, state_dn,
           norm_g, final_norm_g, w_ff_in, w_ff_out, w_in, w_gate, w_branch, w_out,
           lru_conv_w, lru_conv_b, lru_wa, lru_ba, lru_wx, lru_bx, lru_lambda,
           gla_wg, gla_bg, gla_norm_g, ret_norm_g, dn_conv_w, dn_a_log, dn_dt_bias, dn_norm_g):
    bp, tpr, d = x_prompt.shape
    bs, ts, _ = x_sample.shape
    depth = w_in.shape[0]
    n_p, n_s = bp * tpr, bs * ts
    assert d == D_MODEL and ts <= SAMPLE_PAD and tpr % CHUNK == 0

    w_pack = _pack_w_in(jnp.swapaxes(w_in, 1, 2))
    wg_pad = jnp.zeros((depth, 128, HEADS * DK), F32).at[:, SM_GLR:SM_GLR + GLA_RANK, :].set(gla_wg).astype(BF)
    r3 = lambda a: a.reshape(a.shape[0], 1, a.shape[1])
    lw = {
        "conv_w": lru_conv_w, "conv_b": r3(lru_conv_b), "wa": lru_wa.astype(BF), "ba": r3(lru_ba),
        "wx": lru_wx.astype(BF), "bx": r3(lru_bx), "lam": r3(lru_lambda),
        "gla_wg": wg_pad, "gla_bg": r3(gla_bg), "gla_ng": r3(gla_norm_g), "ret_ng": r3(ret_norm_g),
        "dn_conv_w": dn_conv_w, "dn_alog": _lane_row(dn_a_log, SM_DA), "dn_dtb": _lane_row(dn_dt_bias, SM_DA),
        "dn_ng": r3(dn_norm_g),
    }
    norm_g4 = norm_g.reshape(depth, 3, 1, d)

    cos_p, sin_p = _rope_tables(jnp.arange(tpr))
    pos_s = jnp.where(jnp.arange(SAMPLE_PAD) < ts, PAST_LEN + jnp.arange(SAMPLE_PAD), 0)
    cos_s, sin_s = _rope_tables(pos_s)

    pad_conv = lambda a: jnp.pad(a, ((0, 0), (SUBLANES - (CONV_W - 1), 0), (0, 0)))
    zeros = lambda *s: jnp.zeros(s, F32)

    tp_p = _pick_tile(tpr, MIXER_TIME_TILE, CHUNK)
    a_p = dict(lay=_SeqLayout(bp, tp_p), gb=bp)
    nb_s = _pick_tile(bs, SAMPLE_SEQS, 2)
    assert (nb_s * ts) % SUBLANES == 0
    a_s = dict(lay=_SeqLayout(nb_s, SAMPLE_PAD, ts), gb=_pick_tile(nb_s, LOCKSTEP_SEQS, 1))

    fg = final_norm_g.reshape(1, d)
    xp = x_prompt.reshape(n_p, d)
    xs = x_sample.reshape(n_s, d)
    new_p = [[] for _ in range(6)]
    new_s = [[] for _ in range(6)]
    s_gla_s = s_ret_s = s_dn_s = None
    for l in range(depth):
        xs, *w_ffn = _ffn_cast(xs, norm_g4[l, 0], w_ff_in, w_ff_out, l, 0)
        xp = _ffn(xp, norm_g4[l, 0], *w_ffn)
        p_p, hn_p = _inproj(xp, norm_g4[l, 1], w_pack, l)
        p_p = p_p.reshape(bp, tpr, N_PACK)
        p_s, _ = _inproj(xs, norm_g4[l, 1], w_pack, l)

        prompt_calls = [
            _lru(p_p, zeros(bp, SUBLANES, BRANCH), zeros(bp, 1, BRANCH), lw, l, t_real=tp_p, is_prompt=True,
                 defer=True, **a_p),
            _gla(p_p, zeros(bp, HEADS, DK, DV), lw, l, c=CHUNK, t_real=tp_p, name="gla_prompt", defer=True, **a_p),
            _ret(p_p, zeros(bp, HEADS, DK, DV), cos_p, sin_p, lw, l, c=CHUNK, t_real=tp_p, name="ret_prompt",
                 defer=True, **a_p),
        ]
        (o_lru_p, c_lru_p, h_lru_p), (o_gla_p, s_gla_p), (o_ret_p, s_ret_p) = _run_fused(prompt_calls,
                                                                                         "mixers_prompt")
        o_dn_p, c_dn_p, s_dn_p = _dn(p_p, zeros(bp, SUBLANES, 3 * BRANCH), zeros(bp, HEADS, DV, DV), lw, l,
                                     c=CHUNK, t_real=tp_p, name="dn_prompt", **a_p)

        sample_calls = [
            _lru(p_s, pad_conv(state_lru_conv[l]), state_lru_h[l][:, None, :], lw, l, t_real=ts, is_prompt=False,
                 defer=True, **a_s),
            _gla(p_s, state_gla, lw, l, c=SAMPLE_PAD, t_real=ts, name="gla_sample", stacked=(depth, s_gla_s),
                 defer=True, **a_s),
            _ret(p_s, state_ret, cos_s, sin_s, lw, l, c=SAMPLE_PAD, t_real=ts, name="ret_sample",
                 stacked=(depth, s_ret_s), defer=True, **a_s),
            _dn(p_s, pad_conv(state_dn_conv[l]), state_dn, lw, l, c=SAMPLE_PAD, t_real=ts, name="dn_sample",
                stacked=(depth, s_dn_s), defer=True, **a_s),
        ]
        ((o_lru_s, c_lru_s, h_lru_s), (o_gla_s, s_gla_s), (o_ret_s, s_ret_s),
         (o_dn_s, c_dn_s, s_dn_s)) = _run_fused(sample_calls, "mixers_sample")

        flat_p = lambda o: o.reshape(n_p, BRANCH)
        xs, *w_merge = _merge_cast(xs, norm_g4[l, 1], [o_lru_s, o_gla_s, o_ret_s, o_dn_s], w_gate, w_branch, w_out, l)
        xp = _merge_split(xp, hn_p, [flat_p(o) for o in (o_lru_p, o_gla_p, o_ret_p, o_dn_p)], *w_merge)
        xs, *w_ffn = _ffn_cast(xs, norm_g4[l, 2], w_ff_in, w_ff_out, l, 1)
        xp = _ffn(xp, norm_g4[l, 2], *w_ffn, final_g=fg if l == depth - 1 else None)

        tail3 = lambda cwin: cwin[:, SUBLANES - (CONV_W - 1):, :]
        for i, v in enumerate((tail3(c_lru_p), h_lru_p[:, 0], s_gla_p, s_ret_p, tail3(c_dn_p), s_dn_p)):
            new_p[i].append(v)
        for i, v in ((0, tail3(c_lru_s)), (1, h_lru_s[:, 0]), (4, tail3(c_dn_s))):
            new_s[i].append(v)

    y_prompt = xp.reshape(bp, tpr, d)
    y_sample = _final_norm(xs, fg).reshape(bs, ts, d)
    sp = [jnp.stack(v) for v in new_p]
    s_small = {i: jnp.stack(new_s[i]) for i in (0, 1, 4)}
    return tuple([y_prompt, y_sample] + sp + [s_small[0], s_small[1], s_gla_s, s_ret_s, s_small[4], s_dn_s])
```
